```python
import math
import jax, jax.numpy as jnp
from jax import lax
import numpy as np

D_MODEL = 1024
BATCH = 2
SEQ = 8192
DEPTH = 4

DEEPNORM_ALPHA = (2 * DEPTH) ** 0.25
DEEPNORM_BETA = (8 * DEPTH) ** -0.25
LN_EPS = 1e-5
N_BRANCHES = 3

SSD_INNER = 2 * D_MODEL
SSD_HEAD_DIM = 64
SSD_HEADS = SSD_INNER // SSD_HEAD_DIM
SSD_GROUPS = 8
SSD_HPG = SSD_HEADS // SSD_GROUPS
SSD_STATE = 128
SSD_CONV = 5
SSD_CHUNK = 128
SSD_CONV_CH = SSD_INNER + 2 * SSD_GROUPS * SSD_STATE

ATT_HEAD_DIM = 64
ATT_PATTERNS = ((128, 1), (512, 4), (2048, 16))
ATT_GROUPS = len(ATT_PATTERNS)
ATT_HPG = 6
ATT_HEADS = ATT_GROUPS * ATT_HPG
ATT_WIDTH = ATT_HEADS * ATT_HEAD_DIM
ATT_OUT = ATT_HPG * ATT_HEAD_DIM
ATT_NEG = -1e30

S5_CH = 16
S5_STATE = 64
S5_WIDTH = 9 * D_MODEL // 8
S5_GROUPS = S5_WIDTH // S5_CH

IN_SIZES = (N_BRANCHES * D_MODEL, SSD_INNER, SSD_CONV_CH, 2 * SSD_HEADS, 3 * ATT_WIDTH, S5_WIDTH)
N_IN = sum(IN_SIZES)

MOE_EXPERTS = 32
MOE_TOP_K = 4
MOE_FF = D_MODEL
SWIGLU_LIMIT = 7.0
SWIGLU_ALPHA = 1.702
MOE_BLOCK = 128

kernel_name = "hybrid_ssd_dilattn_s5_moe_deepnorm"


def layer_norm(x, g, b):
    xf = x.astype(jnp.float32)
    mu = jnp.mean(xf, axis=-1, keepdims=True)
    var = jnp.mean(jnp.square(xf - mu), axis=-1, keepdims=True)
    return ((xf - mu) * lax.rsqrt(var + LN_EPS)).astype(x.dtype) * g + b


def rms_norm(x, w):
    xf = x.astype(jnp.float32)
    return (xf * lax.rsqrt(jnp.mean(xf * xf, axis=-1, keepdims=True) + LN_EPS)).astype(x.dtype) * w


def split_columns(proj):
    idx, acc = [], 0
    for s in IN_SIZES[:-1]:
        acc += s
        idx.append(acc)
    return jnp.split(proj, idx, axis=-1)


def centred_dwconv(x, w, b):
    k_taps = w.shape[0]
    half = k_taps // 2
    seqlen = x.shape[1]
    xp = jnp.pad(x, ((0, 0), (half, half), (0, 0)))
    out = b + w[0] * xp[:, 0:seqlen]
    for k in range(1, k_taps):
        out = out + w[k] * xp[:, k:k + seqlen]
    return out


def ssd_chunked(xin, a_dt, bm, cm):
    bsz, seqlen, ng, nr, hd = xin.shape
    nc = seqlen // SSD_CHUNK
    xc = xin.reshape(bsz, nc, SSD_CHUNK, ng, nr, hd)
    bc = bm.reshape(bsz, nc, SSD_CHUNK, ng, SSD_STATE)
    cc = cm.reshape(bsz, nc, SSD_CHUNK, ng, SSD_STATE)
    acs = jnp.cumsum(a_dt.astype(jnp.float32).reshape(bsz, nc, SSD_CHUNK, ng, nr).transpose(0, 3, 4, 1, 2), axis=-1)
    tri = jnp.tril(jnp.ones((SSD_CHUNK, SSD_CHUNK), bool))
    seg = acs[..., :, None] - acs[..., None, :]
    decay_in = jnp.where(tri, jnp.exp(jnp.where(tri, seg, 0.0)), 0.0)
    cb = jnp.einsum('bclgn,bcsgn->bgcls', cc, bc)
    y_diag = jnp.einsum('bgcls,bgrcls,bcsgrp->bclgrp', cb, decay_in, xc)
    decay_to_end = jnp.exp(acs[..., -1:] - acs)
    states = jnp.einsum('bclgn,bgrcl,bclgrp->bcgrpn', bc, decay_to_end, xc)
    totals = jnp.pad(acs[..., -1], ((0, 0), (0, 0), (0, 0), (1, 0)))
    tcs = jnp.cumsum(totals, axis=-1)
    tri_c = jnp.tril(jnp.ones((nc + 1, nc + 1), bool))
    seg_c = tcs[..., :, None] - tcs[..., None, :]
    decay_chunks = jnp.where(tri_c, jnp.exp(jnp.where(tri_c, seg_c, 0.0)), 0.0)
    states = jnp.pad(states, ((0, 0), (1, 0), (0, 0), (0, 0), (0, 0), (0, 0)))
    carried = jnp.einsum('bgrzc,bcgrpn->bzgrpn', decay_chunks, states)[:, :-1]
    y_off = jnp.einsum('bclgn,bcgrpn,bgrcl->bclgrp', cc, carried, jnp.exp(acs))
    return (y_diag + y_off).reshape(bsz, seqlen, ng, nr, hd)


def ssd_branch(z, xbc, dt, conv_w, conv_b, a_log, dt_bias, d_skip, norm_w):
    bsz, seqlen, _ = z.shape
    xbc = jax.nn.silu(centred_dwconv(xbc, conv_w, conv_b))
    xs, bm, cm = jnp.split(xbc, [SSD_INNER, SSD_INNER + SSD_GROUPS * SSD_STATE], axis=-1)
    xs = xs.reshape(bsz, seqlen, SSD_GROUPS, SSD_HPG, SSD_HEAD_DIM)
    bm = bm.reshape(bsz, seqlen, SSD_GROUPS, SSD_STATE)
    cm = cm.reshape(bsz, seqlen, SSD_GROUPS, SSD_STATE)
    dt = dt.astype(jnp.float32).reshape(bsz, seqlen, 2, SSD_GROUPS, SSD_HPG)
    y = d_skip.reshape(SSD_GROUPS, SSD_HPG)[:, :, None] * xs
    for direction in range(2):
        delta = jax.nn.softplus(dt[:, :, direction] + dt_bias[direction].astype(jnp.float32).reshape(SSD_GROUPS, SSD_HPG))
        a_dt = -jnp.exp(a_log[direction].astype(jnp.float32).reshape(SSD_GROUPS, SSD_HPG)) * delta
        xin = xs * delta[..., None]
        if direction == 1:
            yd = jnp.flip(ssd_chunked(jnp.flip(xin, 1), jnp.flip(a_dt, 1), jnp.flip(bm, 1), jnp.flip(cm, 1)), 1)
        else:
            yd = ssd_chunked(xin, a_dt, bm, cm)
        y = y + yd
    y = y.reshape(bsz, seqlen, SSD_INNER).astype(z.dtype) * jax.nn.silu(z)
    return rms_norm(y, norm_w)


def dilated_window_attention(q, k, v, dil, radius, slopes):
    bsz, seqlen, nh, hd = q.shape
    ls = seqlen // dil
    blk = radius
    nb = -(-ls // blk)
    lp = nb * blk

    def by_class(a):
        return a.reshape(bsz, ls, dil, nh, hd).transpose(0, 2, 3, 1, 4)

    qc = jnp.pad(by_class(q), ((0, 0), (0, 0), (0, 0), (0, lp - ls), (0, 0)))
    kc = jnp.pad(by_class(k), ((0, 0), (0, 0), (0, 0), (blk, lp - ls + blk), (0, 0)))
    vc = jnp.pad(by_class(v), ((0, 0), (0, 0), (0, 0), (blk, lp - ls + blk), (0, 0)))
    qb = qc.reshape(bsz, dil, nh, nb, blk, hd)

    def windows(a):
        return jnp.concatenate([a[:, :, :, o * blk:o * blk + lp].reshape(bsz, dil, nh, nb, blk, hd) for o in range(3)], axis=4)

    kb, vb = windows(kc), windows(vc)
    qpos = jnp.arange(lp).reshape(nb, blk)
    kpos = (jnp.arange(nb)[:, None] - 1) * blk + jnp.arange(3 * blk)[None, :]
    dist = jnp.abs(qpos[:, :, None] - kpos[:, None, :])
    valid = (dist <= radius) & (kpos[:, None, :] >= 0) & (kpos[:, None, :] < ls)
    alibi = -slopes.astype(jnp.float32)[:, None, None, None] * (dil * dist).astype(jnp.float32)
    s = jnp.einsum('bdhnqe,bdhnke->bdhnqk', qb, kb, preferred_element_type=jnp.float32) * (ATT_HEAD_DIM ** -0.5) + alibi[None, None]
    s = jnp.where(valid, s, ATT_NEG)
    m = jnp.max(s, axis=-1, keepdims=True)
    p = jnp.exp(s - m)
    zsum = jnp.sum(p, axis=-1, keepdims=True)
    o = jnp.einsum('bdhnqk,bdhnke->bdhnqe', p, vb.astype(jnp.float32)) / zsum
    lse = (m + jnp.log(zsum))[..., 0]
    o = o.reshape(bsz, dil, nh, lp, hd)[:, :, :, :ls].transpose(0, 3, 1, 2, 4).reshape(bsz, seqlen, nh, hd)
    lse = lse.reshape(bsz, dil, nh, lp)[..., :ls].transpose(0, 3, 1, 2).reshape(bsz, seqlen, nh)
    return o, lse


def attention_branch(qkv):
    bsz, seqlen, _ = qkv.shape
    q, k, v = [a.reshape(bsz, seqlen, ATT_GROUPS, ATT_HPG, ATT_HEAD_DIM) for a in jnp.split(qkv, 3, axis=-1)]
    slopes = (2.0 ** (-8.0 * jnp.arange(1, ATT_HEADS + 1, dtype=jnp.float32) / ATT_HEADS)).reshape(ATT_HPG, ATT_GROUPS)
    outs, lses = [], []
    for g, (window, dil) in enumerate(ATT_PATTERNS):
        o, lse = dilated_window_attention(q[:, :, g], k[:, :, g], v[:, :, g], dil, (window // 2) // dil, slopes[:, g])
        outs.append(o)
        lses.append(lse)
    wts = jax.nn.softmax(jnp.stack(lses, axis=0), axis=0)
    out = wts[0][..., None] * outs[0]
    for g in range(1, ATT_GROUPS):
        out = out + wts[g][..., None] * outs[g]
    return out.reshape(bsz, seqlen, ATT_OUT).astype(qkv.dtype)


def complex_affine_combine(left, right):
    ar_l, ai_l, br_l, bi_l = left
    ar_r, ai_r, br_r, bi_r = right
    return (ar_r * ar_l - ai_r * ai_l,
            ar_r * ai_l + ai_r * ar_l,
            ar_r * br_l - ai_r * bi_l + br_r,
            ar_r * bi_l + ai_r * br_l + bi_r)


def s5_branch(u, a_re, a_im, log_step, b_re, b_im, c_re, c_im, d_skip, glu_w1, glu_w2):
    bsz, seqlen, _ = u.shape
    f32 = jnp.float32
    uf = u.astype(f32)
    ug = uf.reshape(bsz, seqlen, S5_GROUPS, S5_CH)
    y = (uf * d_skip.astype(f32)).reshape(bsz, seqlen, S5_GROUPS, S5_CH)
    br, bi = b_re.astype(f32), b_im.astype(f32)
    for direction in range(2):
        ar, ai = a_re[direction].astype(f32), a_im[direction].astype(f32)
        step = jnp.exp(log_step[direction].astype(f32))[:, None]
        mag = jnp.exp(step * ar)
        abr, abi = mag * jnp.cos(step * ai), mag * jnp.sin(step * ai)
        den = ar * ar + ai * ai
        fr = ((abr - 1.0) * ar + abi * ai) / den
        fi = (abi * ar - (abr - 1.0) * ai) / den
        bbr = fr[..., None] * br - fi[..., None] * bi
        bbi = fr[..., None] * bi + fi[..., None] * br
        u_dir = ug if direction == 0 else jnp.flip(ug, axis=1)
        bu_r = jnp.einsum('blgc,gnc->lbgn', u_dir, bbr)
        bu_i = jnp.einsum('blgc,gnc->lbgn', u_dir, bbi)
        a_r = jnp.broadcast_to(abr[None, None], (seqlen, 1, S5_GROUPS, S5_STATE))
        a_i = jnp.broadcast_to(abi[None, None], (seqlen, 1, S5_GROUPS, S5_STATE))
        _, _, st_r, st_i = lax.associative_scan(complex_affine_combine, (a_r, a_i, bu_r, bu_i), axis=0)
        if direction == 1:
            st_r, st_i = jnp.flip(st_r, axis=0), jnp.flip(st_i, axis=0)
        y = y + jnp.einsum('lbgn,gcn->blgc', st_r, c_re[direction].astype(f32)) \
              - jnp.einsum('lbgn,gcn->blgc', st_i, c_im[direction].astype(f32))
    h = jax.nn.gelu(y.reshape(bsz, seqlen, S5_WIDTH)).astype(u.dtype)
    return (h @ glu_w1) * jax.nn.sigmoid(h @ glu_w2)


def hybrid_mixer(x, w_in, b_gate, ssd_conv_w, ssd_conv_b, ssd_a_log, ssd_dt_bias, ssd_d, ssd_norm_w,
                 s5_a_re, s5_a_im, s5_log_step, s5_b_re, s5_b_im, s5_c_re, s5_c_im, s5_d, s5_glu_w1, s5_glu_w2,
                 w_br_ssd, w_br_attn, w_br_s5, w_out):
    bsz, seqlen, _ = x.shape
    proj = x @ w_in
    gates, z, xbc, dt, qkv, u = split_columns(proj)
    y_ssd = ssd_branch(z, xbc, dt, ssd_conv_w, ssd_conv_b, ssd_a_log, ssd_dt_bias, ssd_d, ssd_norm_w)
    y_att = attention_branch(qkv)
    y_s5 = s5_branch(u, s5_a_re, s5_a_im, s5_log_step, s5_b_re, s5_b_im, s5_c_re, s5_c_im, s5_d, s5_glu_w1, s5_glu_w2)
    g = jax.nn.sigmoid(gates.reshape(bsz, seqlen, N_BRANCHES, D_MODEL) + b_gate)
    merged = g[:, :, 0] * (y_ssd @ w_br_ssd) + g[:, :, 1] * (y_att @ w_br_attn) + g[:, :, 2] * (y_s5 @ w_br_s5)
    return merged @ w_out


def moe_ffn(x, router_w, router_b, w_gu, b_gu, w_dn, b_dn):
    bsz, seqlen, d = x.shape
    ntok = bsz * seqlen
    h = x.reshape(ntok, d)
    logits = (h @ router_w + router_b).astype(jnp.float32)
    top_val, top_idx = lax.top_k(logits, MOE_TOP_K)
    gate = jax.nn.softmax(top_val, axis=-1)
    n_assign = ntok * MOE_TOP_K
    flat_e = top_idx.reshape(-1)
    flat_tok = jnp.repeat(jnp.arange(ntok, dtype=jnp.int32), MOE_TOP_K)
    order = jnp.argsort(flat_e)
    e_sorted = flat_e[order]
    tok_sorted = flat_tok[order]
    gate_sorted = gate.reshape(-1)[order]
    counts = jnp.bincount(flat_e, length=MOE_EXPERTS)
    padded = (counts + MOE_BLOCK - 1) // MOE_BLOCK * MOE_BLOCK
    start = jnp.cumsum(counts) - counts
    pend = jnp.cumsum(padded)
    pstart = pend - padded
    dest = pstart[e_sorted] + jnp.arange(n_assign, dtype=jnp.int32) - start[e_sorted]
    n_blocks = n_assign // MOE_BLOCK + MOE_EXPERTS
    slot_tok = jnp.full((n_blocks * MOE_BLOCK,), ntok, jnp.int32).at[dest].set(tok_sorted)
    h_pad = jnp.concatenate([h, jnp.zeros((1, d), h.dtype)], axis=0)
    xs = h_pad[slot_tok].reshape(n_blocks, MOE_BLOCK, d)
    block_expert = jnp.minimum(jnp.searchsorted(pend, jnp.arange(n_blocks, dtype=pend.dtype) * MOE_BLOCK, side='right'), MOE_EXPERTS - 1)

    def expert_block(args):
        xb, e = args
        hu = xb @ w_gu[e] + b_gu[e]
        x_glu, x_lin = hu[:, :MOE_FF], hu[:, MOE_FF:]
        x_glu = jnp.minimum(x_glu, SWIGLU_LIMIT)
        x_lin = jnp.clip(x_lin, -SWIGLU_LIMIT, SWIGLU_LIMIT)
        act = x_glu * jax.nn.sigmoid(SWIGLU_ALPHA * x_glu) * (x_lin + 1.0)
        return act @ w_dn[e] + b_dn[e]

    ys = lax.map(expert_block, (xs, block_expert)).reshape(-1, d)
    y = ys[dest] * gate_sorted[:, None].astype(ys.dtype)
    return jax.ops.segment_sum(y, tok_sorted, num_segments=ntok).reshape(bsz, seqlen, d)


def setup_inputs(seed: int = 0) -> dict:
    key = jax.random.key(seed)
    ks = iter(jax.random.split(key, 40))
    L = DEPTH

    def nrm(shape, std):
        return jax.random.normal(next(ks), shape, jnp.float32) * std

    def unif(shape, lo, hi):
        return jax.random.uniform(next(ks), shape, jnp.float32, lo, hi)

    dt0 = jnp.exp(unif((L, 2, SSD_HEADS), math.log(1e-3), math.log(1e-1)))
    n_idx = jnp.arange(S5_STATE, dtype=jnp.float32)
    return {
        "x": nrm((BATCH, SEQ, D_MODEL), 1.0),
        "w_in": nrm((L, D_MODEL, N_IN), D_MODEL ** -0.5),
        "b_gate": nrm((L, N_BRANCHES, D_MODEL), 0.1),
        "ssd_conv_w": nrm((L, SSD_CONV, SSD_CONV_CH), SSD_CONV ** -0.5),
        "ssd_conv_b": nrm((L, SSD_CONV_CH), 0.02),
        "ssd_a_log": jnp.log(unif((L, 2, SSD_HEADS), 1.0, 16.0)),
        "ssd_dt_bias": dt0 + jnp.log(-jnp.expm1(-dt0)),
        "ssd_d": 1.0 + nrm((L, SSD_HEADS), 0.1),
        "ssd_norm_w": 1.0 + nrm((L, SSD_INNER), 0.02),
        "s5_a_re": -0.5 + nrm((L, 2, S5_GROUPS, S5_STATE), 0.02),
        "s5_a_im": math.pi * n_idx + nrm((L, 2, S5_GROUPS, S5_STATE), 0.02),
        "s5_log_step": unif((L, 2, S5_GROUPS), math.log(1e-3), math.log(1e-1)),
        "s5_b_re": nrm((L, S5_GROUPS, S5_STATE, S5_CH), (2 * S5_CH) ** -0.5),
        "s5_b_im": nrm((L, S5_GROUPS, S5_STATE, S5_CH), (2 * S5_CH) ** -0.5),
        "s5_c_re": nrm((L, 2, S5_GROUPS, S5_CH, S5_STATE), (2 * S5_STATE) ** -0.5),
        "s5_c_im": nrm((L, 2, S5_GROUPS, S5_CH, S5_STATE), (2 * S5_STATE) ** -0.5),
        "s5_d": nrm((L, S5_WIDTH), 1.0),
        "s5_glu_w1": nrm((L, S5_WIDTH, S5_WIDTH), S5_WIDTH ** -0.5),
        "s5_glu_w2": nrm((L, S5_WIDTH, S5_WIDTH), S5_WIDTH ** -0.5),
        "w_br_ssd": nrm((L, SSD_INNER, D_MODEL), SSD_INNER ** -0.5),
        "w_br_attn": nrm((L, ATT_OUT, D_MODEL), ATT_OUT ** -0.5),
        "w_br_s5": nrm((L, S5_WIDTH, D_MODEL), S5_WIDTH ** -0.5),
        "w_out": nrm((L, D_MODEL, D_MODEL), DEEPNORM_BETA * D_MODEL ** -0.5),
        "ln1_g": 1.0 + nrm((L, D_MODEL), 0.02),
        "ln1_b": nrm((L, D_MODEL), 0.02),
        "router_w": nrm((L, D_MODEL, MOE_EXPERTS), D_MODEL ** -0.5),
        "router_b": nrm((L, MOE_EXPERTS), 0.01),
        "exp_w_gate_up": nrm((L, MOE_EXPERTS, D_MODEL, 2 * MOE_FF), D_MODEL ** -0.5),
        "exp_b_gate_up": nrm((L, MOE_EXPERTS, 2 * MOE_FF), 0.02),
        "exp_w_down": nrm((L, MOE_EXPERTS, MOE_FF, D_MODEL), DEEPNORM_BETA * MOE_FF ** -0.5),
        "exp_b_down": nrm((L, MOE_EXPERTS, D_MODEL), 0.02),
        "ln2_g": 1.0 + nrm((L, D_MODEL), 0.02),
        "ln2_b": nrm((L, D_MODEL), 0.02),
    }


def reference(x, w_in, b_gate, ssd_conv_w, ssd_conv_b, ssd_a_log, ssd_dt_bias, ssd_d, ssd_norm_w,
              s5_a_re, s5_a_im, s5_log_step, s5_b_re, s5_b_im, s5_c_re, s5_c_im, s5_d, s5_glu_w1, s5_glu_w2,
              w_br_ssd, w_br_attn, w_br_s5, w_out, ln1_g, ln1_b,
              router_w, router_b, exp_w_gate_up, exp_b_gate_up, exp_w_down, exp_b_down, ln2_g, ln2_b):
    for layer in range(DEPTH):
        mix = hybrid_mixer(x, w_in[layer], b_gate[layer], ssd_conv_w[layer], ssd_conv_b[layer], ssd_a_log[layer],
                           ssd_dt_bias[layer], ssd_d[layer], ssd_norm_w[layer],
                           s5_a_re[layer], s5_a_im[layer], s5_log_step[layer], s5_b_re[layer], s5_b_im[layer],
                           s5_c_re[layer], s5_c_im[layer], s5_d[layer], s5_glu_w1[layer], s5_glu_w2[layer],
                           w_br_ssd[layer], w_br_attn[layer], w_br_s5[layer], w_out[layer])
        x = layer_norm(DEEPNORM_ALPHA * x + mix, ln1_g[layer], ln1_b[layer])
        ffn = moe_ffn(x, router_w[layer], router_b[layer], exp_w_gate_up[layer], exp_b_gate_up[layer],
                      exp_w_down[layer], exp_b_down[layer])
        x = layer_norm(DEEPNORM_ALPHA * x + ffn, ln2_g[layer], ln2_b[layer])
    return x
```

```python
import functools
import math

import jax
import jax.numpy as jnp
from jax import lax
from jax.experimental import pallas as pl
from jax.experimental.pallas import tpu as pltpu

F32 = jnp.float32
BF16 = jnp.bfloat16

D_MODEL = 1024
DEPTH = 4
ALPHA = (2 * DEPTH) ** 0.25
LN_EPS = 1e-5

SSD_INNER = 2 * D_MODEL
SSD_HEAD_DIM = 64
SSD_GROUPS = 8
SSD_HPG = 4
SSD_HEADS = SSD_GROUPS * SSD_HPG
SSD_STATE = 128
SSD_CONV = 5
SSD_CONV_CH = SSD_INNER + 2 * SSD_GROUPS * SSD_STATE
SSD_CHUNK = 128
SSD_SEQ_BLOCK = 1024

ATT_HEAD_DIM = 64
ATT_PATTERNS = ((128, 1), (512, 4), (2048, 16))
ATT_GROUPS = 3
ATT_HPG = 6
ATT_HEADS = ATT_GROUPS * ATT_HPG
ATT_WIDTH = ATT_HEADS * ATT_HEAD_DIM
ATT_OUT = ATT_HPG * ATT_HEAD_DIM
ATT_NEG = -1e30
ATT_RADIUS = 64
ATT_QSUB = 128

S5_CH = 16
S5_STATE = 64
S5_WIDTH = 9 * D_MODEL // 8
S5_GROUPS = S5_WIDTH // S5_CH
S5_TC = 16
S5_ROW = S5_TC * S5_CH
S5_SCAN_GROUPS = 8

N_BRANCHES = 3
IN_SIZES = (N_BRANCHES * D_MODEL, SSD_INNER, SSD_CONV_CH, 2 * SSD_HEADS, 3 * ATT_WIDTH, S5_WIDTH)

MOE_EXPERTS = 32
MOE_TOP_K = 4
MOE_FF = D_MODEL
SWIGLU_LIMIT = 7.0
SWIGLU_ALPHA = 1.702
MOE_ROWS = 256

VMEM_LIMIT = 56 * 1024 * 1024


def _params(*sem):
    return pltpu.CompilerParams(dimension_semantics=sem, vmem_limit_bytes=VMEM_LIMIT)


def _mm_kernel(x_ref, w_ref, o_ref):
    o_ref[...] = jnp.dot(x_ref[...].astype(BF16), w_ref[...], preferred_element_type=F32).astype(o_ref.dtype)


def _matmul(x, w, *, tm, tn, name):
    m, k = x.shape
    n = w.shape[1]
    return pl.pallas_call(
        _mm_kernel,
        grid=(n // tn, m // tm),
        in_specs=[pl.BlockSpec((tm, k), lambda j, i: (i, 0)), pl.BlockSpec((k, tn), lambda j, i: (0, j))],
        out_specs=pl.BlockSpec((tm, tn), lambda j, i: (i, j)),
        out_shape=jax.ShapeDtypeStruct((m, n), F32),
        compiler_params=_params("parallel", "parallel"),
        name=name,
    )(x, w)


def _dt_kernel(x_ref, w_ref, wt_ref, o_ref, ot_ref):
    xb = x_ref[...].astype(BF16)
    o_ref[...] = jnp.dot(xb, w_ref[...], preferred_element_type=F32)
    ot_ref[...] = lax.dot_general(wt_ref[...], xb, (((1,), (1,)), ((), ())), preferred_element_type=F32)


def _dt_proj(x, w, *, tm):
    m, k = x.shape
    n = w.shape[1]
    return pl.pallas_call(
        _dt_kernel,
        grid=(m // tm,),
        in_specs=[pl.BlockSpec((tm, k), lambda i: (i, 0)), pl.BlockSpec((k, n), lambda i: (0, 0)),
                  pl.BlockSpec((n, k), lambda i: (0, 0))],
        out_specs=[pl.BlockSpec((tm, n), lambda i: (i, 0)), pl.BlockSpec((n, tm), lambda i: (0, i))],
        out_shape=[jax.ShapeDtypeStruct((m, n), F32), jax.ShapeDtypeStruct((n, m), F32)],
        compiler_params=_params("parallel"),
        name="dt_proj",
    )(x, w, w.T)


def _conv_kernel(xm_ref, xp_ref, xn_ref, w_ref, b_ref, o_ref, buf):
    i = pl.program_id(1)
    ts = xm_ref.shape[0]
    half = SSD_CONV // 2
    buf[0:8, :] = jnp.where(i > 0, xp_ref[...], 0.0)
    buf[8:8 + ts, :] = xm_ref[...]
    buf[8 + ts:16 + ts, :] = jnp.where(i < pl.num_programs(1) - 1, xn_ref[...], 0.0)
    acc = b_ref[...] + w_ref[0:1, :] * buf[8 - half:8 - half + ts, :]
    for k in range(1, SSD_CONV):
        acc = acc + w_ref[k:k + 1, :] * buf[8 - half + k:8 - half + k + ts, :]
    o_ref[...] = acc * jax.nn.sigmoid(acc)


def _conv_silu(xbc, w, b, *, ts=512, tc=512):
    bsz, seqlen, ch = xbc.shape
    nrow8 = seqlen // 8
    return pl.pallas_call(
        _conv_kernel,
        grid=(bsz, seqlen // ts, ch // tc),
        in_specs=[
            pl.BlockSpec((None, ts, tc), lambda b_, i, j: (b_, i, j)),
            pl.BlockSpec((None, 8, tc), lambda b_, i, j: (b_, jnp.maximum(i * (ts // 8) - 1, 0), j)),
            pl.BlockSpec((None, 8, tc), lambda b_, i, j: (b_, jnp.minimum((i + 1) * (ts // 8), nrow8 - 1), j)),
            pl.BlockSpec((SSD_CONV, tc), lambda b_, i, j: (0, j)),
            pl.BlockSpec((1, tc), lambda b_, i, j: (0, j)),
        ],
        out_specs=pl.BlockSpec((None, ts, tc), lambda b_, i, j: (b_, i, j)),
        out_shape=jax.ShapeDtypeStruct((bsz, seqlen, ch), F32),
        scratch_shapes=[pltpu.VMEM((ts + 16, tc), F32)],
        compiler_params=_params("parallel", "parallel", "parallel"),
        name="ssd_conv",
    )(xbc, xbc, xbc, w, b.reshape(1, ch))


def _split3(a):
    a1 = a.astype(BF16)
    r1 = a - a1.astype(F32)
    a2 = r1.astype(BF16)
    a3 = (r1 - a2.astype(F32)).astype(BF16)
    return a1, a2, a3


def _ssd_kernel(xs_ref, bm_ref, cm_ref, dt_ref, dtt_ref, alog_r_ref, alog_c_ref, bias_r_ref, bias_c_ref, dskip_ref,
                y_ref, carried, *, reverse):
    q = SSD_CHUNK
    nchunk = xs_ref.shape[0] // q

    @pl.when(pl.program_id(2) == 0)
    def _():
        carried[...] = jnp.zeros_like(carried)

    row = lax.broadcasted_iota(jnp.int32, (q, q), 0)
    col = lax.broadcasted_iota(jnp.int32, (q, q), 1)
    lower = col <= row
    upper = col >= row
    mask = upper if reverse else lower
    cum_col = (upper if reverse else lower).astype(BF16)
    cum_row = (lower if reverse else upper).astype(BF16)
    neg_a_r = -jnp.exp(alog_r_ref[...])
    neg_a_c = -jnp.exp(alog_c_ref[...])
    end = 0 if reverse else q - 1

    def chunk(ci, carry):
        c = (nchunk - 1 - ci) if reverse else ci
        r0 = pl.multiple_of(c * q, q)
        xs = xs_ref[pl.ds(r0, q), :]
        bmat = bm_ref[pl.ds(r0, q), :].astype(BF16)
        cmat = cm_ref[pl.ds(r0, q), :].astype(BF16)
        delta = jax.nn.softplus(dt_ref[pl.ds(r0, q), :] + bias_r_ref[...])
        delta_t = jax.nn.softplus(dtt_ref[:, pl.ds(r0, q)] + bias_c_ref[...])
        a_col = neg_a_r * delta
        a_row = neg_a_c * delta_t
        acs_col = sum(jnp.dot(cum_col, p, preferred_element_type=F32) for p in _split3(a_col))
        acs_row = sum(jnp.dot(p, cum_row, preferred_element_type=F32) for p in _split3(a_row))
        cb = lax.dot_general(cmat, bmat, (((1,), (1,)), ((), ())), preferred_element_type=F32)
        for r in range(SSD_HPG):
            lo = r * SSD_HEAD_DIM
            acs_l = acs_col[:, r:r + 1]
            total = acs_col[end:end + 1, r:r + 1]
            seg = acs_l - acs_row[r:r + 1, :]
            decay = jnp.where(mask, jnp.exp(jnp.where(mask, seg, 0.0)), 0.0)
            x_r = xs[:, lo:lo + SSD_HEAD_DIM]
            xdt = x_r * delta[:, r:r + 1]
            y_diag = jnp.dot((cb * decay).astype(BF16), xdt.astype(BF16), preferred_element_type=F32)
            xw = (xdt * jnp.exp(total - acs_l)).astype(BF16)
            states = lax.dot_general(bmat, xw, (((0,), (0,)), ((), ())), preferred_element_type=F32)
            prev = carried[r]
            y_off = jnp.exp(acs_l) * jnp.dot(cmat, prev.astype(BF16), preferred_element_type=F32)
            y_r = y_diag + y_off
            if not reverse:
                y_r = y_r + dskip_ref[:, r:r + 1] * x_r
            y_ref[pl.ds(r0, q), lo:lo + SSD_HEAD_DIM] = y_r
            carried[r] = jnp.exp(total) * prev + states
        return carry

    lax.fori_loop(0, nchunk, chunk, 0)


def _ssd_direction(xbc_act, dt_rows, dt_cols, a_log, dt_bias, d_skip, *, reverse):
    bsz, seqlen, _ = xbc_act.shape
    sb = min(SSD_SEQ_BLOCK, seqlen)
    nblk = seqlen // sb
    d = 1 if reverse else 0
    blk = (lambda i: nblk - 1 - i) if reverse else (lambda i: i)
    n_x = SSD_HPG * SSD_HEAD_DIM
    b_off = SSD_INNER // SSD_STATE
    c_off = b_off + SSD_GROUPS
    alog = a_log[d].reshape(SSD_GROUPS, SSD_HPG)
    bias = dt_bias[d].reshape(SSD_GROUPS, SSD_HPG)
    return pl.pallas_call(
        functools.partial(_ssd_kernel, reverse=reverse),
        grid=(bsz, SSD_GROUPS, nblk),
        in_specs=[
            pl.BlockSpec((None, sb, n_x), lambda b_, g, i: (b_, blk(i), g)),
            pl.BlockSpec((None, sb, SSD_STATE), lambda b_, g, i: (b_, blk(i), b_off + g)),
            pl.BlockSpec((None, sb, SSD_STATE), lambda b_, g, i: (b_, blk(i), c_off + g)),
            pl.BlockSpec((None, None, None, sb, SSD_HPG), lambda b_, g, i: (d, g, b_, blk(i), 0)),
            pl.BlockSpec((None, None, None, SSD_HPG, sb), lambda b_, g, i: (d, g, b_, 0, blk(i))),
            pl.BlockSpec((None, 1, SSD_HPG), lambda b_, g, i: (g, 0, 0)),
            pl.BlockSpec((None, SSD_HPG, 1), lambda b_, g, i: (g, 0, 0)),
            pl.BlockSpec((None, 1, SSD_HPG), lambda b_, g, i: (g, 0, 0)),
            pl.BlockSpec((None, SSD_HPG, 1), lambda b_, g, i: (g, 0, 0)),
            pl.BlockSpec((None, 1, SSD_HPG), lambda b_, g, i: (g, 0, 0)),
        ],
        out_specs=pl.BlockSpec((None, sb, n_x), lambda b_, g, i: (b_, blk(i), g)),
        out_shape=jax.ShapeDtypeStruct((bsz, seqlen, SSD_INNER), F32),
        scratch_shapes=[pltpu.VMEM((SSD_HPG, SSD_STATE, SSD_HEAD_DIM), F32)],
        compiler_params=_params("parallel", "parallel", "arbitrary"),
        name="ssd_bwd" if reverse else "ssd_fwd",
    )(xbc_act, xbc_act, xbc_act, dt_rows, dt_cols,
      alog.reshape(SSD_GROUPS, 1, SSD_HPG), alog.reshape(SSD_GROUPS, SSD_HPG, 1),
      bias.reshape(SSD_GROUPS, 1, SSD_HPG), bias.reshape(SSD_GROUPS, SSD_HPG, 1),
      d_skip.reshape(SSD_GROUPS, 1, SSD_HPG))


def _attn_kernel(q_ref, ko_ref, kp_ref, kn_ref, vo_ref, vp_ref, vn_ref, o_ref, lse_ref, kcat, vcat,
                 *, dil, group, ls):
    tq = q_ref.shape[0]
    rad = ATT_RADIUS
    qs = ATT_QSUB
    kw = qs + 2 * rad
    i = pl.program_id(2)
    kcat[0:rad, :] = kp_ref[...].astype(BF16)
    kcat[rad:rad + tq, :] = ko_ref[...].astype(BF16)
    kcat[rad + tq:2 * rad + tq, :] = kn_ref[...].astype(BF16)
    vcat[0:rad, :] = vp_ref[...].astype(BF16)
    vcat[rad:rad + tq, :] = vo_ref[...].astype(BF16)
    vcat[rad + tq:2 * rad + tq, :] = vn_ref[...].astype(BF16)

    row = lax.broadcasted_iota(jnp.int32, (qs, kw), 0)
    col = lax.broadcasted_iota(jnp.int32, (qs, kw), 1)
    dist = jnp.abs(row - col + rad)
    in_band = dist <= rad
    dist_f = (dil * dist).astype(F32)
    lane = lax.broadcasted_iota(jnp.int32, (qs, 2 * ATT_HEAD_DIM), 1)
    first_half = lane < ATT_HEAD_DIM
    scale = ATT_HEAD_DIM ** -0.5

    for m in range(tq // qs):
        kpos = i * tq + (m * qs - rad) + col
        valid = in_band & (kpos >= 0) & (kpos < ls)
        for pair in range(ATT_HPG // 2):
            lanes = slice(pair * 2 * ATT_HEAD_DIM, (pair + 1) * 2 * ATT_HEAD_DIM)
            q2 = q_ref[m * qs:(m + 1) * qs, lanes]
            k2 = kcat[m * qs:m * qs + kw, lanes]
            v2 = vcat[m * qs:m * qs + kw, lanes]
            outs, lses = [], []
            for half in range(2):
                head = 2 * pair + half
                slope = 2.0 ** (-8.0 * (head * ATT_GROUPS + group + 1) / ATT_HEADS)
                qm = jnp.where(first_half if half == 0 else ~first_half, q2, 0.0).astype(BF16)
                s = lax.dot_general(qm, k2, (((1,), (1,)), ((), ())), preferred_element_type=F32)
                s = s * scale + (-slope) * dist_f
                s = jnp.where(valid, s, ATT_NEG)
                mx = jnp.max(s, axis=-1, keepdims=True)
                p = jnp.exp(s - mx)
                zsum = jnp.sum(p, axis=-1, keepdims=True)
                pv = jnp.dot(p.astype(BF16), v2, preferred_element_type=F32)
                outs.append(pv / zsum)
                lses.append(mx + jnp.log(zsum))
            o_ref[m * qs:(m + 1) * qs, lanes] = jnp.where(first_half, outs[0], outs[1])
            lse_ref[m * qs:(m + 1) * qs, lanes] = jnp.where(first_half, lses[0], lses[1])


def _attention_group(qkv, group):
    window, dil = ATT_PATTERNS[group]
    assert (window // 2) // dil == ATT_RADIUS
    bsz, seqlen, width = qkv.shape
    ls = seqlen // dil
    tq = min(512, ls)
    nblk = ls // tq
    halo_per_blk = tq // ATT_RADIUS
    n_halo = ls // ATT_RADIUS
    cols = width // ATT_OUT
    view = qkv.reshape(bsz, ls, dil * width)

    def own(off):
        return pl.BlockSpec((None, tq, ATT_OUT), lambda b_, r, i: (b_, i, r * cols + off + group))

    def prev(off):
        return pl.BlockSpec((None, ATT_RADIUS, ATT_OUT),
                            lambda b_, r, i: (b_, jnp.maximum(i * halo_per_blk - 1, 0), r * cols + off + group))

    def nxt(off):
        return pl.BlockSpec((None, ATT_RADIUS, ATT_OUT),
                            lambda b_, r, i: (b_, jnp.minimum((i + 1) * halo_per_blk, n_halo - 1), r * cols + off + group))

    out_spec = pl.BlockSpec((None, tq, ATT_OUT), lambda b_, r, i: (b_, i, r))
    o, lse = pl.pallas_call(
        functools.partial(_attn_kernel, dil=dil, group=group, ls=ls),
        grid=(bsz, dil, nblk),
        in_specs=[own(0), own(3), prev(3), nxt(3), own(6), prev(6), nxt(6)],
        out_specs=[out_spec, out_spec],
        out_shape=[jax.ShapeDtypeStruct((bsz, ls, dil * ATT_OUT), F32)] * 2,
        scratch_shapes=[pltpu.VMEM((tq + 2 * ATT_RADIUS, ATT_OUT), BF16)] * 2,
        compiler_params=_params("parallel", "parallel", "parallel"),
        name=f"attn_g{group}",
    )(view, view, view, view, view, view, view)
    return o.reshape(bsz, seqlen, ATT_OUT), lse.reshape(bsz, seqlen, ATT_OUT)


def _s5_tables(a_re, a_im, log_step, b_re, b_im, c_re, c_im):
    hp = lax.Precision.HIGHEST
    tc = S5_TC
    taus = jnp.arange(tc + 1, dtype=F32)
    step = jnp.exp(log_step)[..., None]
    ar, ai = a_re, a_im
    mag = jnp.exp(step * ar)
    abr, abi = mag * jnp.cos(step * ai), mag * jnp.sin(step * ai)
    den = ar * ar + ai * ai
    fr = ((abr - 1.0) * ar + abi * ai) / den
    fi = (abi * ar - (abr - 1.0) * ai) / den
    bbr = fr[..., None] * b_re[None] - fi[..., None] * b_im[None]
    bbi = fr[..., None] * b_im[None] + fi[..., None] * b_re[None]
    pmag = jnp.exp(step[..., None] * ar[..., None] * taus)
    pr = pmag * jnp.cos(step[..., None] * ai[..., None] * taus)
    pi = pmag * jnp.sin(step[..., None] * ai[..., None] * taus)
    cap_r = c_re[..., None] * pr[:, :, None] - c_im[..., None] * pi[:, :, None]
    cap_i = c_re[..., None] * pi[:, :, None] + c_im[..., None] * pr[:, :, None]
    taps = (jnp.einsum('dgonk,dgni->dgkoi', cap_r, bbr, precision=hp)
            - jnp.einsum('dgonk,dgni->dgkoi', cap_i, bbi, precision=hp))
    s_idx = jnp.arange(tc)[:, None]
    t_idx = jnp.arange(tc)[None, :]
    lag_f = jnp.clip(t_idx - s_idx, 0, tc)
    lag_b = jnp.clip(s_idx - t_idx, 0, tc)
    kf = jnp.where((t_idx >= s_idx)[None, :, :, None, None], taps[0][:, lag_f], 0.0)
    kb = jnp.where((s_idx >= t_idx)[None, :, :, None, None], taps[1][:, lag_b], 0.0)
    toep = (kf + kb).transpose(0, 1, 4, 2, 3).reshape(S5_GROUPS, S5_ROW, S5_ROW)
    e_f = jnp.arange(tc - 1, -1, -1)
    e_b = jnp.arange(tc)

    def end_state(d, e):
        pr_e, pi_e = pr[d][..., e], pi[d][..., e]
        re = pr_e[..., None] * bbr[d][:, :, None] - pi_e[..., None] * bbi[d][:, :, None]
        im = pr_e[..., None] * bbi[d][:, :, None] + pi_e[..., None] * bbr[d][:, :, None]
        return [x.transpose(0, 2, 3, 1).reshape(S5_GROUPS, S5_ROW, S5_STATE) for x in (re, im)]

    b_end = jnp.concatenate(end_state(0, e_f) + end_state(1, e_b), axis=-1)
    o_f = jnp.arange(1, tc + 1)
    o_b = jnp.arange(tc, 0, -1)

    def from_state(d, e):
        re = cap_r[d][..., e].transpose(0, 2, 3, 1).reshape(S5_GROUPS, S5_STATE, S5_ROW)
        im = -cap_i[d][..., e].transpose(0, 2, 3, 1).reshape(S5_GROUPS, S5_STATE, S5_ROW)
        return [re, im]

    c_out = jnp.concatenate(from_state(0, o_f) + from_state(1, o_b), axis=1)
    ptr, pti = pr[..., tc], pi[..., tc]
    p_mul = jnp.concatenate([ptr[0], ptr[0], ptr[1], ptr[1]], axis=-1)
    q_mul = jnp.concatenate([-pti[0], pti[0], -pti[1], pti[1]], axis=-1)
    return toep, b_end, c_out, p_mul, q_mul


def _s5_in_kernel(u_ref, toep_ref, bend_ref, y_ref, sf_ref, sb_ref):
    half = 2 * S5_STATE
    ub = u_ref[...].astype(BF16)
    y_ref[...] = jnp.dot(ub, toep_ref[...], preferred_element_type=F32)
    s = jnp.dot(ub, bend_ref[...], preferred_element_type=F32)
    sf_ref[...] = s[:, 0:half]
    sb_ref[...] = s[:, half:2 * half]


def _s5_scan_kernel(sf_ref, sb_ref, p_ref, q_ref, hf_ref, hb_ref, *, rows_per_seq):
    n_chain = p_ref.shape[0]
    half = 2 * S5_STATE
    pf, qf = p_ref[:, 0:half], q_ref[:, 0:half]
    pb, qb = p_ref[:, half:2 * half], q_ref[:, half:2 * half]
    qf_sw = pltpu.roll(qf, S5_STATE, 1)
    qb_sw = pltpu.roll(qb, S5_STATE, 1)
    zero = jnp.zeros((n_chain, half), F32)

    def step(c, carry):
        hf, hf_sw, hb, hb_sw = carry
        rows_f = pl.ds(c, n_chain, stride=rows_per_seq)
        rows_b = pl.ds(rows_per_seq - 1 - c, n_chain, stride=rows_per_seq)
        hf_ref[rows_f, :] = hf
        hb_ref[rows_b, :] = hb
        sf = sf_ref[rows_f, :]
        sb = sb_ref[rows_b, :]
        sf_sw = pltpu.roll(sf, S5_STATE, 1)
        sb_sw = pltpu.roll(sb, S5_STATE, 1)
        hf_new = pf * hf + qf * hf_sw + sf
        hf_sw_new = pf * hf_sw + qf_sw * hf + sf_sw
        hb_new = pb * hb + qb * hb_sw + sb
        hb_sw_new = pb * hb_sw + qb_sw * hb + sb_sw
        return hf_new, hf_sw_new, hb_new, hb_sw_new

    lax.fori_loop(0, rows_per_seq, step, (zero, zero, zero, zero))


def _s5_out_kernel(hf_ref, hb_ref, cout_ref, y_ref, u_ref, d_ref, o_ref):
    half = 2 * S5_STATE
    y = y_ref[...] + jnp.dot(hf_ref[...].astype(BF16), cout_ref[0:half, :], preferred_element_type=F32)
    y = y + jnp.dot(hb_ref[...].astype(BF16), cout_ref[half:2 * half, :], preferred_element_type=F32)
    y = y + u_ref[...] * d_ref[...]
    o_ref[...] = jax.nn.gelu(y)


def _s5_mix(u, a_re, a_im, log_step, b_re, b_im, c_re, c_im, d_skip):
    bsz, seqlen, _ = u.shape
    rows_per_seq = seqlen // S5_TC
    rows = bsz * rows_per_seq
    toep, b_end, c_out, p_mul, q_mul = _s5_tables(a_re, a_im, log_step, b_re, b_im, c_re, c_im)
    toep, b_end, c_out = toep.astype(BF16), b_end.astype(BF16), c_out.astype(BF16)
    ug = u.reshape(bsz, rows_per_seq, S5_TC, S5_GROUPS, S5_CH).transpose(3, 0, 1, 2, 4).reshape(S5_GROUPS, rows, S5_ROW)
    half = 2 * S5_STATE
    tile = pl.BlockSpec((None, rows, S5_ROW), lambda g: (g, 0, 0))
    stile = pl.BlockSpec((None, rows, half), lambda g: (g, 0, 0))
    wtile = pl.BlockSpec((None, S5_ROW, S5_ROW), lambda g: (g, 0, 0))
    state_shape = jax.ShapeDtypeStruct((S5_GROUPS, rows, half), F32)
    y_in, s_f, s_b = pl.pallas_call(
        _s5_in_kernel,
        grid=(S5_GROUPS,),
        in_specs=[tile, wtile, wtile],
        out_specs=[tile, stile, stile],
        out_shape=[jax.ShapeDtypeStruct((S5_GROUPS, rows, S5_ROW), F32), state_shape, state_shape],
        compiler_params=_params("parallel"),
        name="s5_in",
    )(ug, toep, b_end)

    gs = S5_SCAN_GROUPS
    n_chain = gs * bsz
    chain_p = jnp.repeat(p_mul, bsz, axis=0)
    chain_q = jnp.repeat(q_mul, bsz, axis=0)
    flat = jax.ShapeDtypeStruct((S5_GROUPS * rows, half), F32)
    srows = pl.BlockSpec((gs * rows, half), lambda t: (t, 0))
    chain = pl.BlockSpec((n_chain, S5_ROW), lambda t: (t, 0))
    h_f, h_b = pl.pallas_call(
        functools.partial(_s5_scan_kernel, rows_per_seq=rows_per_seq),
        grid=(S5_GROUPS // gs,),
        in_specs=[srows, srows, chain, chain],
        out_specs=[srows, srows],
        out_shape=[flat, flat],
        compiler_params=_params("parallel"),
        name="s5_scan",
    )(s_f.reshape(S5_GROUPS * rows, half), s_b.reshape(S5_GROUPS * rows, half), chain_p, chain_q)

    d_rows = jnp.tile(d_skip.reshape(S5_GROUPS, 1, S5_CH), (1, S5_TC, 1)).reshape(S5_GROUPS, 1, S5_ROW)
    hg = pl.pallas_call(
        _s5_out_kernel,
        grid=(S5_GROUPS,),
        in_specs=[stile, stile, wtile, tile, tile, pl.BlockSpec((None, 1, S5_ROW), lambda g: (g, 0, 0))],
        out_specs=tile,
        out_shape=jax.ShapeDtypeStruct((S5_GROUPS, rows, S5_ROW), F32),
        compiler_params=_params("parallel"),
        name="s5_out",
    )(h_f.reshape(S5_GROUPS, rows, half), h_b.reshape(S5_GROUPS, rows, half), c_out, y_in, ug, d_rows)
    return hg.reshape(S5_GROUPS, bsz, rows_per_seq, S5_TC, S5_CH).transpose(1, 2, 3, 0, 4).reshape(bsz, seqlen, S5_WIDTH)


def _ssd_proj_kernel(yf_ref, yb_ref, z_ref, nw_ref, w_ref, gate_ref, bg_ref, o_ref):
    z = z_ref[...]
    y = (yf_ref[...] + yb_ref[...]) * (z * jax.nn.sigmoid(z))
    y = y * lax.rsqrt(jnp.mean(y * y, axis=-1, keepdims=True) + LN_EPS) * nw_ref[...]
    p = jnp.dot(y.astype(BF16), w_ref[...], preferred_element_type=F32)
    o_ref[...] = jax.nn.sigmoid(gate_ref[...] + bg_ref[...]) * p


def _ssd_project(yf, yb, z, norm_w, w_br, gates, b_gate, *, tm=256):
    m = yf.shape[0]
    wide = pl.BlockSpec((tm, SSD_INNER), lambda i: (i, 0))
    return pl.pallas_call(
        _ssd_proj_kernel,
        grid=(m // tm,),
        in_specs=[wide, wide, wide, pl.BlockSpec((1, SSD_INNER), lambda i: (0, 0)),
                  pl.BlockSpec((SSD_INNER, D_MODEL), lambda i: (0, 0)),
                  pl.BlockSpec((tm, D_MODEL), lambda i: (i, 0)),
                  pl.BlockSpec((1, D_MODEL), lambda i: (0, 0))],
        out_specs=pl.BlockSpec((tm, D_MODEL), lambda i: (i, 0)),
        out_shape=jax.ShapeDtypeStruct((m, D_MODEL), F32),
        compiler_params=_params("parallel"),
        name="ssd_project",
    )(yf, yb, z, norm_w.reshape(1, -1), w_br, gates, b_gate[0:1])


def _s5_proj_kernel(h_ref, w1_ref, w2_ref, w_ref, gate_ref, bg_ref, o_ref):
    hb = h_ref[...].astype(BF16)
    a = jnp.dot(hb, w1_ref[...], preferred_element_type=F32)
    b = jnp.dot(hb, w2_ref[...], preferred_element_type=F32)
    p = jnp.dot((a * jax.nn.sigmoid(b)).astype(BF16), w_ref[...], preferred_element_type=F32)
    o_ref[...] = jax.nn.sigmoid(gate_ref[...] + bg_ref[...]) * p


def _s5_project(h, w1, w2, w_br, gates, b_gate, *, tm=512):
    m = h.shape[0]
    sq = pl.BlockSpec((S5_WIDTH, S5_WIDTH), lambda i: (0, 0))
    return pl.pallas_call(
        _s5_proj_kernel,
        grid=(m // tm,),
        in_specs=[pl.BlockSpec((tm, S5_WIDTH), lambda i: (i, 0)), sq, sq,
                  pl.BlockSpec((S5_WIDTH, D_MODEL), lambda i: (0, 0)),
                  pl.BlockSpec((tm, D_MODEL), lambda i: (i, 2)),
                  pl.BlockSpec((1, D_MODEL), lambda i: (0, 0))],
        out_specs=pl.BlockSpec((tm, D_MODEL), lambda i: (i, 0)),
        out_shape=jax.ShapeDtypeStruct((m, D_MODEL), F32),
        compiler_params=_params("parallel"),
        name="s5_project",
    )(h, w1, w2, w_br, gates, b_gate[2:3])


def _layer_norm(v, g, b):
    mu = jnp.mean(v, axis=-1, keepdims=True)
    c = v - mu
    var = jnp.mean(c * c, axis=-1, keepdims=True)
    return c * lax.rsqrt(var + LN_EPS) * g + b


def _merge_kernel(o0_ref, o1_ref, o2_ref, l0_ref, l1_ref, l2_ref, w_att_ref, gate_ref, bg_ref, pssd_ref, ps5_ref,
                  w_out_ref, x_ref, g_ref, b_ref, o_ref):
    l0, l1, l2 = l0_ref[...], l1_ref[...], l2_ref[...]
    mx = jnp.maximum(jnp.maximum(l0, l1), l2)
    e0, e1, e2 = jnp.exp(l0 - mx), jnp.exp(l1 - mx), jnp.exp(l2 - mx)
    den = e0 + e1 + e2
    att = (e0 / den) * o0_ref[...] + (e1 / den) * o1_ref[...] + (e2 / den) * o2_ref[...]
    p_att = jnp.dot(att.astype(BF16), w_att_ref[...], preferred_element_type=F32)
    merged = pssd_ref[...] + jax.nn.sigmoid(gate_ref[...] + bg_ref[...]) * p_att + ps5_ref[...]
    mix = jnp.dot(merged.astype(BF16), w_out_ref[...], preferred_element_type=F32)
    o_ref[...] = _layer_norm(ALPHA * x_ref[...] + mix, g_ref[...], b_ref[...])


def _merge(att_o, att_l, w_att, gates, b_gate, p_ssd, p_s5, w_out, x, ln_g, ln_b, *, tm=512):
    m = x.shape[0]
    att = pl.BlockSpec((tm, ATT_OUT), lambda i: (i, 0))
    row = pl.BlockSpec((tm, D_MODEL), lambda i: (i, 0))
    vec = pl.BlockSpec((1, D_MODEL), lambda i: (0, 0))
    return pl.pallas_call(
        _merge_kernel,
        grid=(m // tm,),
        in_specs=[att] * 6 + [pl.BlockSpec((ATT_OUT, D_MODEL), lambda i: (0, 0)),
                              pl.BlockSpec((tm, D_MODEL), lambda i: (i, 1)), vec, row, row,
                              pl.BlockSpec((D_MODEL, D_MODEL), lambda i: (0, 0)), row, vec, vec],
        out_specs=row,
        out_shape=jax.ShapeDtypeStruct((m, D_MODEL), F32),
        compiler_params=_params("parallel"),
        name="merge_ln1",
    )(*att_o, *att_l, w_att, gates, b_gate[1:2], p_ssd, p_s5, w_out, x, ln_g.reshape(1, -1), ln_b.reshape(1, -1))


def _router_kernel(x_ref, wt_ref, b_ref, idx_ref, gate_ref):
    x = x_ref[...]
    wt = wt_ref[...]
    xh = x.astype(BF16)
    xl = (x - xh.astype(F32)).astype(BF16)
    wh = wt.astype(BF16)
    wl = (wt - wh.astype(F32)).astype(BF16)
    nt = (((1,), (1,)), ((), ()))
    logits = (lax.dot_general(wh, xh, nt, preferred_element_type=F32)
              + lax.dot_general(wl, xh, nt, preferred_element_type=F32)
              + lax.dot_general(wh, xl, nt, preferred_element_type=F32)) + b_ref[...]
    eid = lax.broadcasted_iota(jnp.int32, logits.shape, 0)
    vals, idxs = [], []
    for _ in range(MOE_TOP_K):
        mx = jnp.max(logits, axis=0, keepdims=True)
        sel = jnp.min(jnp.where(logits == mx, eid, MOE_EXPERTS), axis=0, keepdims=True)
        vals.append(mx)
        idxs.append(sel)
        logits = jnp.where(eid == sel, -jnp.inf, logits)
    es = [jnp.exp(v - vals[0]) for v in vals]
    den = es[0] + es[1] + es[2] + es[3]
    for k in range(MOE_TOP_K):
        idx_ref[k:k + 1, :] = idxs[k]
        gate_ref[k:k + 1, :] = es[k] / den


def _router(x, router_w, router_b, *, tm=1024):
    m = x.shape[0]
    out = pl.BlockSpec((MOE_TOP_K, tm), lambda i: (0, i))
    return pl.pallas_call(
        _router_kernel,
        grid=(m // tm,),
        in_specs=[pl.BlockSpec((tm, D_MODEL), lambda i: (i, 0)),
                  pl.BlockSpec((MOE_EXPERTS, D_MODEL), lambda i: (0, 0)),
                  pl.BlockSpec((MOE_EXPERTS, 1), lambda i: (0, 0))],
        out_specs=[out, out],
        out_shape=[jax.ShapeDtypeStruct((MOE_TOP_K, m), jnp.int32), jax.ShapeDtypeStruct((MOE_TOP_K, m), F32)],
        compiler_params=_params("parallel"),
        name="router",
    )(x, router_w.T, router_b.reshape(-1, 1))


def _expert_kernel(be_ref, used_ref, x_ref, wgu_ref, bgu_ref, wdn_ref, bdn_ref, o_ref):
    i = pl.program_id(0)

    @pl.when(i < used_ref[0])
    def _():
        hu = jnp.dot(x_ref[...].astype(BF16), wgu_ref[...], preferred_element_type=F32) + bgu_ref[...]
        x_glu = jnp.minimum(hu[:, :MOE_FF], SWIGLU_LIMIT)
        x_lin = jnp.clip(hu[:, MOE_FF:], -SWIGLU_LIMIT, SWIGLU_LIMIT)
        act = x_glu * jax.nn.sigmoid(SWIGLU_ALPHA * x_glu) * (x_lin + 1.0)
        o_ref[...] = jnp.dot(act.astype(BF16), wdn_ref[...], preferred_element_type=F32) + bdn_ref[...]

    @pl.when(i >= used_ref[0])
    def _():
        o_ref[...] = jnp.zeros_like(o_ref)


def _experts(xs, block_expert, n_used, w_gu, b_gu, w_dn, b_dn):
    n_rows = xs.shape[0]
    n_blocks = n_rows // MOE_ROWS
    grid_spec = pltpu.PrefetchScalarGridSpec(
        num_scalar_prefetch=2,
        grid=(n_blocks,),
        in_specs=[
            pl.BlockSpec((MOE_ROWS, D_MODEL), lambda i, be, nu: (i, 0)),
            pl.BlockSpec((None, D_MODEL, 2 * MOE_FF), lambda i, be, nu: (be[i], 0, 0)),
            pl.BlockSpec((None, 1, 2 * MOE_FF), lambda i, be, nu: (be[i], 0, 0)),
            pl.BlockSpec((None, MOE_FF, D_MODEL), lambda i, be, nu: (be[i], 0, 0)),
            pl.BlockSpec((None, 1, D_MODEL), lambda i, be, nu: (be[i], 0, 0)),
        ],
        out_specs=pl.BlockSpec((MOE_ROWS, D_MODEL), lambda i, be, nu: (i, 0)),
    )
    return pl.pallas_call(
        _expert_kernel,
        grid_spec=grid_spec,
        out_shape=jax.ShapeDtypeStruct((n_rows, D_MODEL), F32),
        compiler_params=_params("arbitrary"),
        name="experts",
    )(block_expert, n_used, xs, w_gu, b_gu.reshape(MOE_EXPERTS, 1, -1), w_dn, b_dn.reshape(MOE_EXPERTS, 1, -1))


def _combine_kernel(x_ref, y_ref, gate_ref, g_ref, b_ref, o_ref):
    ffn = gate_ref[:, 0:1] * y_ref[:, 0:D_MODEL]
    for k in range(1, MOE_TOP_K):
        ffn = ffn + gate_ref[:, k:k + 1] * y_ref[:, k * D_MODEL:(k + 1) * D_MODEL]
    o_ref[...] = _layer_norm(ALPHA * x_ref[...] + ffn, g_ref[...], b_ref[...])


def _combine(x, y_rows, gate, ln_g, ln_b, *, tm=512):
    m = x.shape[0]
    row = pl.BlockSpec((tm, D_MODEL), lambda i: (i, 0))
    vec = pl.BlockSpec((1, D_MODEL), lambda i: (0, 0))
    return pl.pallas_call(
        _combine_kernel,
        grid=(m // tm,),
        in_specs=[row, pl.BlockSpec((tm, MOE_TOP_K * D_MODEL), lambda i: (i, 0)),
                  pl.BlockSpec((tm, MOE_TOP_K), lambda i: (i, 0)), vec, vec],
        out_specs=row,
        out_shape=jax.ShapeDtypeStruct((m, D_MODEL), F32),
        compiler_params=_params("parallel"),
        name="combine_ln2",
    )(x, y_rows, gate, ln_g.reshape(1, -1), ln_b.reshape(1, -1))


def _moe(x, router_w, router_b, w_gu, b_gu, w_dn, b_dn, ln_g, ln_b):
    ntok = x.shape[0]
    idx_t, gate_t = _router(x, router_w, router_b)
    flat_e = idx_t.T.reshape(-1)
    gate = gate_t.T
    n_assign = ntok * MOE_TOP_K
    onehot = (flat_e[:, None] == jnp.arange(MOE_EXPERTS, dtype=jnp.int32)[None, :]).astype(jnp.int32)
    running = jnp.cumsum(onehot, axis=0)
    rank = jnp.sum(onehot * (running - 1), axis=1)
    counts = running[-1]
    padded = (counts + MOE_ROWS - 1) // MOE_ROWS * MOE_ROWS
    pend = jnp.cumsum(padded)
    pstart = pend - padded
    dest = pstart[flat_e] + rank
    n_blocks = n_assign // MOE_ROWS + MOE_EXPERTS
    flat_tok = jnp.arange(n_assign, dtype=jnp.int32) // MOE_TOP_K
    slot_tok = jnp.full((n_blocks * MOE_ROWS,), ntok, jnp.int32).at[dest].set(flat_tok)
    x_pad = jnp.concatenate([x, jnp.zeros((1, x.shape[1]), x.dtype)], axis=0)
    xs = x_pad[slot_tok]
    block_start = jnp.arange(n_blocks, dtype=jnp.int32) * MOE_ROWS
    block_expert = jnp.minimum(jnp.searchsorted(pend, block_start, side='right'), MOE_EXPERTS - 1).astype(jnp.int32)
    n_used = (pend[-1:] // MOE_ROWS).astype(jnp.int32)
    ys = _experts(xs, block_expert, n_used, w_gu, b_gu, w_dn, b_dn)
    y_rows = ys[dest].reshape(ntok, MOE_TOP_K * D_MODEL)
    return _combine(x, y_rows, gate, ln_g, ln_b)


def _layer(x, w_in, b_gate, ssd_conv_w, ssd_conv_b, ssd_a_log, ssd_dt_bias, ssd_d, ssd_norm_w,
           s5_a_re, s5_a_im, s5_log_step, s5_b_re, s5_b_im, s5_c_re, s5_c_im, s5_d, s5_glu_w1, s5_glu_w2,
           w_br_ssd, w_br_attn, w_br_s5, w_out, ln1_g, ln1_b,
           router_w, router_b, w_gu, b_gu, w_dn, b_dn, ln2_g, ln2_b):
    bsz, seqlen, d = x.shape
    ntok = bsz * seqlen
    xf = x.reshape(ntok, d)
    offs = [0]
    for s in IN_SIZES:
        offs.append(offs[-1] + s)

    def seg(k):
        return w_in[:, offs[k]:offs[k + 1]].astype(BF16)

    gates = _matmul(xf, seg(0), tm=512, tn=1024, name="proj_gates")
    z = _matmul(xf, seg(1), tm=512, tn=1024, name="proj_z")
    xbc = _matmul(xf, seg(2), tm=512, tn=1024, name="proj_xbc")
    dt, dt_t = _dt_proj(xf, seg(3), tm=1024)
    qkv = _matmul(xf, seg(4), tm=512, tn=1152, name="proj_qkv")
    u = _matmul(xf, seg(5), tm=512, tn=1152, name="proj_u")

    xbc_act = _conv_silu(xbc.reshape(bsz, seqlen, SSD_CONV_CH), ssd_conv_w, ssd_conv_b)
    dt_rows = dt.reshape(bsz, seqlen, 2, SSD_GROUPS, SSD_HPG).transpose(2, 3, 0, 1, 4)
    dt_cols = dt_t.reshape(2, SSD_GROUPS, SSD_HPG, bsz, seqlen).transpose(0, 1, 3, 2, 4)
    yf = _ssd_direction(xbc_act, dt_rows, dt_cols, ssd_a_log, ssd_dt_bias, ssd_d, reverse=False)
    yb = _ssd_direction(xbc_act, dt_rows, dt_cols, ssd_a_log, ssd_dt_bias, ssd_d, reverse=True)
    p_ssd = _ssd_project(yf.reshape(ntok, SSD_INNER), yb.reshape(ntok, SSD_INNER), z, ssd_norm_w,
                         w_br_ssd.astype(BF16), gates, b_gate)

    qkv3 = qkv.reshape(bsz, seqlen, 3 * ATT_WIDTH)
    att = [_attention_group(qkv3, g) for g in range(ATT_GROUPS)]
    att_o = [o.reshape(ntok, ATT_OUT) for o, _ in att]
    att_l = [l.reshape(ntok, ATT_OUT) for _, l in att]

    h_s5 = _s5_mix(u.reshape(bsz, seqlen, S5_WIDTH), s5_a_re, s5_a_im, s5_log_step, s5_b_re, s5_b_im,
                   s5_c_re, s5_c_im, s5_d)
    p_s5 = _s5_project(h_s5.reshape(ntok, S5_WIDTH), s5_glu_w1.astype(BF16), s5_glu_w2.astype(BF16),
                       w_br_s5.astype(BF16), gates, b_gate)

    x1 = _merge(att_o, att_l, w_br_attn.astype(BF16), gates, b_gate, p_ssd, p_s5, w_out.astype(BF16),
                xf, ln1_g, ln1_b)
    x2 = _moe(x1, router_w, router_b, w_gu.astype(BF16), b_gu, w_dn.astype(BF16), b_dn, ln2_g, ln2_b)
    return x2.reshape(bsz, seqlen, d)


def kernel(x, w_in, b_gate, ssd_conv_w, ssd_conv_b, ssd_a_log, ssd_dt_bias, ssd_d, ssd_norm_w, s5_a_re, s5_a_im, s5_log_step, s5_b_re, s5_b_im, s5_c_re, s5_c_im, s5_d, s5_glu_w1, s5_glu_w2, w_br_ssd, w_br_attn, w_br_s5, w_out, ln1_g, ln1_b, router_w, router_b, exp_w_gate_up, exp_b_gate_up, exp_w_down, exp_b_down, ln2_g, ln2_b):
    per_layer = (w_in, b_gate, ssd_conv_w, ssd_conv_b, ssd_a_log, ssd_dt_bias, ssd_d, ssd_norm_w, s5_a_re, s5_a_im,
                 s5_log_step, s5_b_re, s5_b_im, s5_c_re, s5_c_im, s5_d, s5_glu_w1, s5_glu_w2, w_br_ssd, w_br_attn,
                 w_br_s5, w_out, ln1_g, ln1_b, router_w, router_b, exp_w_gate_up, exp_b_gate_up, exp_w_down,
                 exp_b_down, ln2_g, ln2_b)
    for layer in range(DEPTH):
        x = _layer(x, *[p[layer] for p in per_layer])
    return x
```

```python
import functools
import math

import jax
import jax.numpy as jnp
from jax import lax
from jax.experimental import pallas as pl
from jax.experimental.pallas import tpu as pltpu

F32 = jnp.float32
BF16 = jnp.bfloat16

D_MODEL = 1024
DEPTH = 4
ALPHA = (2 * DEPTH) ** 0.25
LN_EPS = 1e-5

SSD_INNER = 2 * D_MODEL
SSD_HEAD_DIM = 64
SSD_GROUPS = 8
SSD_HPG = 4
SSD_HEADS = SSD_GROUPS * SSD_HPG
SSD_STATE = 128
SSD_CONV = 5
SSD_CONV_CH = SSD_INNER + 2 * SSD_GROUPS * SSD_STATE
SSD_CHUNK = 128
SSD_SEQ_BLOCK = 1024

ATT_HEAD_DIM = 64
ATT_PATTERNS = ((128, 1), (512, 4), (2048, 16))
ATT_GROUPS = 3
ATT_HPG = 6
ATT_HEADS = ATT_GROUPS * ATT_HPG
ATT_WIDTH = ATT_HEADS * ATT_HEAD_DIM
ATT_OUT = ATT_HPG * ATT_HEAD_DIM
ATT_NEG = -1e30
ATT_RADIUS = 64
ATT_QSUB = 128

S5_CH = 16
S5_STATE = 64
S5_WIDTH = 9 * D_MODEL // 8
S5_GROUPS = S5_WIDTH // S5_CH
S5_TC = 16
S5_ROW = S5_TC * S5_CH
S5_SCAN_GROUPS = 8

N_BRANCHES = 3
IN_SIZES = (N_BRANCHES * D_MODEL, SSD_INNER, SSD_CONV_CH, 2 * SSD_HEADS, 3 * ATT_WIDTH, S5_WIDTH)

MOE_EXPERTS = 32
MOE_TOP_K = 4
MOE_FF = D_MODEL
SWIGLU_LIMIT = 7.0
SWIGLU_ALPHA = 1.702
MOE_ROWS = 256

VMEM_LIMIT = 56 * 1024 * 1024


def _params(*sem):
    return pltpu.CompilerParams(dimension_semantics=sem, vmem_limit_bytes=VMEM_LIMIT)


def _mm_kernel(x_ref, w_ref, o_ref):
    o_ref[...] = jnp.dot(x_ref[...].astype(BF16), w_ref[...], preferred_element_type=F32).astype(o_ref.dtype)


def _matmul(x, w, *, tm, tn, name):
    m, k = x.shape
    n = w.shape[1]
    return pl.pallas_call(
        _mm_kernel,
        grid=(n // tn, m // tm),
        in_specs=[pl.BlockSpec((tm, k), lambda j, i: (i, 0)), pl.BlockSpec((k, tn), lambda j, i: (0, j))],
        out_specs=pl.BlockSpec((tm, tn), lambda j, i: (i, j)),
        out_shape=jax.ShapeDtypeStruct((m, n), F32),
        compiler_params=_params("parallel", "parallel"),
        name=name,
    )(x, w)


def _split3(a):
    a1 = a.astype(BF16)
    r1 = a - a1.astype(F32)
    a2 = r1.astype(BF16)
    a3 = (r1 - a2.astype(F32)).astype(BF16)
    return a1, a2, a3


def _decay_kernel(x_ref, w_ref, wt_ref, alog_r_ref, alog_c_ref, bias_r_ref, bias_c_ref,
                  acs_c_ref, eacs_c_ref, wend_c_ref, acs_r_ref, delta_r_ref):
    q = SSD_CHUNK
    nh = SSD_HEADS
    xb = x_ref[...].astype(BF16)
    dt_c = jnp.dot(xb, w_ref[...], preferred_element_type=F32)
    dt_r = lax.dot_general(wt_ref[...], xb, (((1,), (1,)), ((), ())), preferred_element_type=F32)
    delta_c = jax.nn.softplus(dt_c + bias_r_ref[...])
    delta_r = jax.nn.softplus(dt_r + bias_c_ref[...])
    a_c = -jnp.exp(alog_r_ref[...]) * delta_c
    a_r = -jnp.exp(alog_c_ref[...]) * delta_r
    delta_r_ref[...] = delta_r
    row = lax.broadcasted_iota(jnp.int32, (q, q), 0)
    col = lax.broadcasted_iota(jnp.int32, (q, q), 1)
    lower = (col <= row).astype(BF16)
    upper = (col >= row).astype(BF16)
    fwd_lane = lax.broadcasted_iota(jnp.int32, (q, dt_c.shape[1]), 1) < nh
    fwd_row = lax.broadcasted_iota(jnp.int32, (2 * nh, q), 0) < nh
    for c in range(x_ref.shape[0] // q):
        rows = slice(c * q, (c + 1) * q)
        pieces = _split3(a_c[rows])
        prefix = sum(jnp.dot(lower, p, preferred_element_type=F32) for p in pieces)
        suffix = sum(jnp.dot(upper, p, preferred_element_type=F32) for p in pieces)
        acs = jnp.where(fwd_lane, prefix, suffix)
        total = jnp.where(fwd_lane[0:1], prefix[q - 1:q], suffix[0:1])
        acs_c_ref[rows, :] = acs
        eacs_c_ref[rows, :] = jnp.exp(acs)
        wend_c_ref[rows, :] = delta_c[rows] * jnp.exp(total - acs)
        pieces = _split3(a_r[:, rows])
        prefix = sum(jnp.dot(p, upper, preferred_element_type=F32) for p in pieces)
        suffix = sum(jnp.dot(p, lower, preferred_element_type=F32) for p in pieces)
        acs_r_ref[:, rows] = jnp.where(fwd_row, prefix, suffix)


def _ssd_decay(x, w_dt, a_log, dt_bias, *, tm=1024):
    m, k = x.shape
    n = w_dt.shape[1]
    lanes = 128
    pad = lanes - n
    w_pad = jnp.pad(w_dt, ((0, 0), (0, pad)))
    alog = a_log.reshape(-1)
    bias = dt_bias.reshape(-1)
    col = pl.BlockSpec((tm, lanes), lambda i: (i, 0))
    rowf = pl.BlockSpec((n, tm), lambda i: (0, i))
    return pl.pallas_call(
        _decay_kernel,
        grid=(m // tm,),
        in_specs=[pl.BlockSpec((tm, k), lambda i: (i, 0)), pl.BlockSpec((k, lanes), lambda i: (0, 0)),
                  pl.BlockSpec((n, k), lambda i: (0, 0)),
                  pl.BlockSpec((1, lanes), lambda i: (0, 0)), pl.BlockSpec((n, 1), lambda i: (0, 0)),
                  pl.BlockSpec((1, lanes), lambda i: (0, 0)), pl.BlockSpec((n, 1), lambda i: (0, 0))],
        out_specs=[col, col, col, rowf, rowf],
        out_shape=[jax.ShapeDtypeStruct((m, lanes), F32)] * 3 + [jax.ShapeDtypeStruct((n, m), F32)] * 2,
        compiler_params=_params("parallel"),
        name="ssd_decay",
    )(x, w_pad, w_dt.T, jnp.pad(alog, (0, pad)).reshape(1, lanes), alog.reshape(n, 1),
      jnp.pad(bias, (0, pad)).reshape(1, lanes), bias.reshape(n, 1))


def _conv_kernel(xm_ref, xp_ref, xn_ref, w_ref, b_ref, o_ref, buf):
    i = pl.program_id(1)
    ts = xm_ref.shape[0]
    half = SSD_CONV // 2
    buf[0:8, :] = jnp.where(i > 0, xp_ref[...], 0.0)
    buf[8:8 + ts, :] = xm_ref[...]
    buf[8 + ts:16 + ts, :] = jnp.where(i < pl.num_programs(1) - 1, xn_ref[...], 0.0)
    acc = b_ref[...] + w_ref[0:1, :] * buf[8 - half:8 - half + ts, :]
    for k in range(1, SSD_CONV):
        acc = acc + w_ref[k:k + 1, :] * buf[8 - half + k:8 - half + k + ts, :]
    o_ref[...] = acc * jax.nn.sigmoid(acc)


def _conv_silu(xbc, w, b, *, ts=512, tc=512):
    bsz, seqlen, ch = xbc.shape
    nrow8 = seqlen // 8
    return pl.pallas_call(
        _conv_kernel,
        grid=(bsz, seqlen // ts, ch // tc),
        in_specs=[
            pl.BlockSpec((None, ts, tc), lambda b_, i, j: (b_, i, j)),
            pl.BlockSpec((None, 8, tc), lambda b_, i, j: (b_, jnp.maximum(i * (ts // 8) - 1, 0), j)),
            pl.BlockSpec((None, 8, tc), lambda b_, i, j: (b_, jnp.minimum((i + 1) * (ts // 8), nrow8 - 1), j)),
            pl.BlockSpec((SSD_CONV, tc), lambda b_, i, j: (0, j)),
            pl.BlockSpec((1, tc), lambda b_, i, j: (0, j)),
        ],
        out_specs=pl.BlockSpec((None, ts, tc), lambda b_, i, j: (b_, i, j)),
        out_shape=jax.ShapeDtypeStruct((bsz, seqlen, ch), F32),
        scratch_shapes=[pltpu.VMEM((ts + 16, tc), F32)],
        compiler_params=_params("parallel", "parallel", "parallel"),
        name="ssd_conv",
    )(xbc, xbc, xbc, w, b.reshape(1, ch))


def _per_head(cols):
    rows = cols[0].shape[0]
    first = lax.broadcasted_iota(jnp.int32, (rows, 2 * SSD_HEAD_DIM), 1) < SSD_HEAD_DIM
    return jnp.concatenate([jnp.where(first, cols[0], cols[1]), jnp.where(first, cols[2], cols[3])], axis=1)


def _group_lanes(ref, rows, g):
    lanes = ref.shape[1]
    return pltpu.roll(ref[rows, :], (lanes - SSD_HPG * g) % lanes, 1)


def _group_rows(ref, first, cols, g):
    tile = 8
    per_tile = tile // SSD_HPG
    base = pl.multiple_of(first + (g // per_tile) * tile, tile)
    return pltpu.roll(ref[pl.ds(base, tile), cols], (tile - SSD_HPG * (g % per_tile)) % tile, 0)


def _ssd_back_kernel(xs_ref, bm_ref, eacs_ref, wend_ref, out_ref, carried):
    q = SSD_CHUNK
    nchunk = xs_ref.shape[0] // q
    g = pl.program_id(1)
    bwd = SSD_HEADS

    @pl.when(pl.program_id(2) == 0)
    def _():
        carried[...] = jnp.zeros_like(carried)

    def chunk(ci, carry):
        c = nchunk - 1 - ci
        rows = pl.ds(pl.multiple_of(c * q, q), q)
        wend = _group_lanes(wend_ref, rows, g)
        eacs = _group_lanes(eacs_ref, rows, g)
        xw = xs_ref[rows, :] * _per_head([wend[:, bwd + r:bwd + r + 1] for r in range(SSD_HPG)])
        states = lax.dot_general(bm_ref[rows, :].astype(BF16), xw.astype(BF16), (((0,), (0,)), ((), ())),
                                 preferred_element_type=F32)
        prev = carried[...]
        out_ref[c] = prev.astype(out_ref.dtype)
        carried[...] = _per_head([eacs[0:1, bwd + r:bwd + r + 1] for r in range(SSD_HPG)]) * prev + states
        return carry

    lax.fori_loop(0, nchunk, chunk, 0)


def _ssd_main_kernel(xs_ref, bm_ref, cm_ref, acs_c_ref, eacs_c_ref, wend_c_ref, acs_r_ref, delta_r_ref, back_ref,
                     dskip_ref, y_ref, carried):
    q = SSD_CHUNK
    nchunk = xs_ref.shape[0] // q
    g = pl.program_id(1)
    bwd = SSD_HEADS
    hd = SSD_HEAD_DIM

    @pl.when(pl.program_id(2) == 0)
    def _():
        carried[...] = jnp.zeros_like(carried)

    row = lax.broadcasted_iota(jnp.int32, (q, q), 0)
    col = lax.broadcasted_iota(jnp.int32, (q, q), 1)
    lower = col <= row
    upper = col >= row
    d_wide = _per_head([dskip_ref[:, r:r + 1] for r in range(SSD_HPG)])

    def chunk(c, carry):
        r0 = pl.multiple_of(c * q, q)
        rows = pl.ds(r0, q)
        xs = xs_ref[rows, :]
        bmat = bm_ref[rows, :].astype(BF16)
        cmat = cm_ref[rows, :].astype(BF16)
        acs = _group_lanes(acs_c_ref, rows, g)
        eacs = _group_lanes(eacs_c_ref, rows, g)
        wend = _group_lanes(wend_c_ref, rows, g)
        acs_rf = _group_rows(acs_r_ref, 0, rows, g)
        acs_rb = _group_rows(acs_r_ref, bwd, rows, g)
        del_rf = _group_rows(delta_r_ref, 0, rows, g)
        del_rb = _group_rows(delta_r_ref, bwd, rows, g)
        cb = lax.dot_general(cmat, bmat, (((1,), (1,)), ((), ())), preferred_element_type=F32)
        prev = carried[...]
        states_in = jnp.concatenate([prev.astype(BF16), back_ref[c]], axis=1)
        y_off = jnp.dot(cmat, states_in, preferred_element_type=F32)
        e_f = _per_head([eacs[:, r:r + 1] for r in range(SSD_HPG)])
        e_b = _per_head([eacs[:, bwd + r:bwd + r + 1] for r in range(SSD_HPG)])
        rest = e_f * y_off[:, 0:SSD_HPG * hd] + e_b * y_off[:, SSD_HPG * hd:] + d_wide * xs
        for r in range(SSD_HPG):
            seg_f = acs[:, r:r + 1] - acs_rf[r:r + 1, :]
            seg_b = acs[:, bwd + r:bwd + r + 1] - acs_rb[r:r + 1, :]
            w = (jnp.exp(jnp.where(lower, seg_f, ATT_NEG)) * del_rf[r:r + 1, :]
                 + jnp.exp(jnp.where(upper, seg_b, ATT_NEG)) * del_rb[r:r + 1, :])
            y_diag = jnp.dot((cb * w).astype(BF16), xs[:, r * hd:(r + 1) * hd].astype(BF16),
                             preferred_element_type=F32)
            y_ref[rows, r * hd:(r + 1) * hd] = y_diag + rest[:, r * hd:(r + 1) * hd]
        xw = xs * _per_head([wend[:, r:r + 1] for r in range(SSD_HPG)])
        states = lax.dot_general(bmat, xw.astype(BF16), (((0,), (0,)), ((), ())), preferred_element_type=F32)
        carried[...] = _per_head([eacs[q - 1:q, r:r + 1] for r in range(SSD_HPG)]) * prev + states
        return carry

    lax.fori_loop(0, nchunk, chunk, 0)


def _ssd_scan(xbc_act, acs_c, eacs_c, wend_c, acs_r, delta_r, d_skip):
    bsz, seqlen, _ = xbc_act.shape
    sb = min(SSD_SEQ_BLOCK, seqlen)
    nblk = seqlen // sb
    cpb = sb // SSD_CHUNK
    n_x = SSD_HPG * SSD_HEAD_DIM
    b_off = SSD_INNER // SSD_STATE
    c_off = b_off + SSD_GROUPS
    lanes = acs_c.shape[1]
    nrow = acs_r.shape[0]

    def specs(blk):
        xs = pl.BlockSpec((None, sb, n_x), lambda b_, g, i: (b_, blk(i), g))
        bm = pl.BlockSpec((None, sb, SSD_STATE), lambda b_, g, i: (b_, blk(i), b_off + g))
        cm = pl.BlockSpec((None, sb, SSD_STATE), lambda b_, g, i: (b_, blk(i), c_off + g))
        colf = pl.BlockSpec((sb, lanes), lambda b_, g, i: (b_ * nblk + blk(i), 0))
        rowf = pl.BlockSpec((nrow, sb), lambda b_, g, i: (0, b_ * nblk + blk(i)))
        back = pl.BlockSpec((None, None, cpb, SSD_STATE, n_x), lambda b_, g, i: (b_, g, blk(i), 0, 0))
        return xs, bm, cm, colf, rowf, back

    xs, bm, cm, colf, rowf, back = specs(lambda i: nblk - 1 - i)
    state = pltpu.VMEM((SSD_STATE, n_x), F32)
    back_states = pl.pallas_call(
        _ssd_back_kernel,
        grid=(bsz, SSD_GROUPS, nblk),
        in_specs=[xs, bm, colf, colf],
        out_specs=back,
        out_shape=jax.ShapeDtypeStruct((bsz, SSD_GROUPS, seqlen // SSD_CHUNK, SSD_STATE, n_x), BF16),
        scratch_shapes=[state],
        compiler_params=_params("parallel", "parallel", "arbitrary"),
        name="ssd_back",
    )(xbc_act, xbc_act, eacs_c, wend_c)

    xs, bm, cm, colf, rowf, back = specs(lambda i: i)
    return pl.pallas_call(
        _ssd_main_kernel,
        grid=(bsz, SSD_GROUPS, nblk),
        in_specs=[xs, bm, cm, colf, colf, colf, rowf, rowf, back,
                  pl.BlockSpec((None, 1, SSD_HPG), lambda b_, g, i: (g, 0, 0))],
        out_specs=pl.BlockSpec((None, sb, n_x), lambda b_, g, i: (b_, i, g)),
        out_shape=jax.ShapeDtypeStruct((bsz, seqlen, SSD_INNER), F32),
        scratch_shapes=[state],
        compiler_params=_params("parallel", "parallel", "arbitrary"),
        name="ssd_main",
    )(xbc_act, xbc_act, xbc_act, acs_c, eacs_c, wend_c, acs_r, delta_r, back_states,
      d_skip.reshape(SSD_GROUPS, 1, SSD_HPG))


def _attn_kernel(q_ref, ko_ref, kp_ref, kn_ref, vo_ref, vp_ref, vn_ref, o_ref, lse_ref, kcat, vcat,
                 *, dil, group, ls):
    tq = q_ref.shape[0]
    rad = ATT_RADIUS
    qs = ATT_QSUB
    kw = qs + 2 * rad
    i = pl.program_id(1)
    kcat[0:rad, :] = kp_ref[...].astype(BF16)
    kcat[rad:rad + tq, :] = ko_ref[...].astype(BF16)
    kcat[rad + tq:2 * rad + tq, :] = kn_ref[...].astype(BF16)
    vcat[0:rad, :] = vp_ref[...].astype(BF16)
    vcat[rad:rad + tq, :] = vo_ref[...].astype(BF16)
    vcat[rad + tq:2 * rad + tq, :] = vn_ref[...].astype(BF16)

    row = lax.broadcasted_iota(jnp.int32, (qs, kw), 0)
    col = lax.broadcasted_iota(jnp.int32, (qs, kw), 1)
    dist = jnp.abs(row - col + rad)
    in_band = dist <= rad
    dist_f = (dil * dist).astype(F32)
    lane = lax.broadcasted_iota(jnp.int32, (qs, 2 * ATT_HEAD_DIM), 1)
    first_half = lane < ATT_HEAD_DIM
    scale = ATT_HEAD_DIM ** -0.5

    for m in range(tq // qs):
        kpos = i * tq + (m * qs - rad) + col
        valid = in_band & (kpos >= 0) & (kpos < ls)
        for pair in range(ATT_HPG // 2):
            lanes = slice(pair * 2 * ATT_HEAD_DIM, (pair + 1) * 2 * ATT_HEAD_DIM)
            q2 = q_ref[m * qs:(m + 1) * qs, lanes]
            k2 = kcat[m * qs:m * qs + kw, lanes]
            v2 = vcat[m * qs:m * qs + kw, lanes]
            outs, lses = [], []
            for half in range(2):
                head = 2 * pair + half
                slope = 2.0 ** (-8.0 * (head * ATT_GROUPS + group + 1) / ATT_HEADS)
                qm = jnp.where(first_half if half == 0 else ~first_half, q2, 0.0).astype(BF16)
                s = lax.dot_general(qm, k2, (((1,), (1,)), ((), ())), preferred_element_type=F32)
                s = s * scale + (-slope) * dist_f
                s = jnp.where(valid, s, ATT_NEG)
                mx = jnp.max(s, axis=-1, keepdims=True)
                p = jnp.exp(s - mx)
                zsum = jnp.sum(p, axis=-1, keepdims=True)
                pv = jnp.dot(p.astype(BF16), v2, preferred_element_type=F32)
                outs.append(pv / zsum)
                lses.append(mx + jnp.log(zsum))
            o_ref[m * qs:(m + 1) * qs, lanes] = jnp.where(first_half, outs[0], outs[1])
            lse_ref[m * qs:(m + 1) * qs, lanes] = jnp.where(first_half, lses[0], lses[1])


def _attention_group(qkv, group, dil):
    nseq, ls, _ = qkv.shape
    tq = min(512, ls)
    nblk = ls // tq
    halo_per_blk = tq // ATT_RADIUS
    n_halo = ls // ATT_RADIUS

    def own(off):
        return pl.BlockSpec((None, tq, ATT_OUT), lambda s, i: (s, i, off))

    def prev(off):
        return pl.BlockSpec((None, ATT_RADIUS, ATT_OUT), lambda s, i: (s, jnp.maximum(i * halo_per_blk - 1, 0), off))

    def nxt(off):
        return pl.BlockSpec((None, ATT_RADIUS, ATT_OUT),
                            lambda s, i: (s, jnp.minimum((i + 1) * halo_per_blk, n_halo - 1), off))

    out_spec = pl.BlockSpec((None, tq, ATT_OUT), lambda s, i: (s, i, 0))
    return pl.pallas_call(
        functools.partial(_attn_kernel, dil=dil, group=group, ls=ls),
        grid=(nseq, nblk),
        in_specs=[own(0), own(1), prev(1), nxt(1), own(2), prev(2), nxt(2)],
        out_specs=[out_spec, out_spec],
        out_shape=[jax.ShapeDtypeStruct((nseq, ls, ATT_OUT), F32)] * 2,
        scratch_shapes=[pltpu.VMEM((tq + 2 * ATT_RADIUS, ATT_OUT), BF16)] * 2,
        compiler_params=_params("parallel", "parallel"),
        name=f"attn_g{group}",
    )(qkv, qkv, qkv, qkv, qkv, qkv, qkv)


def _s5_tables(a_re, a_im, log_step, b_re, b_im, c_re, c_im):
    hp = lax.Precision.HIGHEST
    tc = S5_TC
    taus = jnp.arange(tc + 1, dtype=F32)
    step = jnp.exp(log_step)[..., None]
    ar, ai = a_re, a_im
    mag = jnp.exp(step * ar)
    abr, abi = mag * jnp.cos(step * ai), mag * jnp.sin(step * ai)
    den = ar * ar + ai * ai
    fr = ((abr - 1.0) * ar + abi * ai) / den
    fi = (abi * ar - (abr - 1.0) * ai) / den
    bbr = fr[..., None] * b_re[None] - fi[..., None] * b_im[None]
    bbi = fr[..., None] * b_im[None] + fi[..., None] * b_re[None]
    pmag = jnp.exp(step[..., None] * ar[..., None] * taus)
    pr = pmag * jnp.cos(step[..., None] * ai[..., None] * taus)
    pi = pmag * jnp.sin(step[..., None] * ai[..., None] * taus)
    cap_r = c_re[..., None] * pr[:, :, None] - c_im[..., None] * pi[:, :, None]
    cap_i = c_re[..., None] * pi[:, :, None] + c_im[..., None] * pr[:, :, None]
    taps = (jnp.einsum('dgonk,dgni->dgkoi', cap_r, bbr, precision=hp)
            - jnp.einsum('dgonk,dgni->dgkoi', cap_i, bbi, precision=hp))
    s_idx = jnp.arange(tc)[:, None]
    t_idx = jnp.arange(tc)[None, :]
    lag_f = jnp.clip(t_idx - s_idx, 0, tc)
    lag_b = jnp.clip(s_idx - t_idx, 0, tc)
    kf = jnp.where((t_idx >= s_idx)[None, :, :, None, None], taps[0][:, lag_f], 0.0)
    kb = jnp.where((s_idx >= t_idx)[None, :, :, None, None], taps[1][:, lag_b], 0.0)
    toep = (kf + kb).transpose(0, 1, 4, 2, 3).reshape(S5_GROUPS, S5_ROW, S5_ROW)
    e_f = jnp.arange(tc - 1, -1, -1)
    e_b = jnp.arange(tc)

    def end_state(d, e):
        pr_e, pi_e = pr[d][..., e], pi[d][..., e]
        re = pr_e[..., None] * bbr[d][:, :, None] - pi_e[..., None] * bbi[d][:, :, None]
        im = pr_e[..., None] * bbi[d][:, :, None] + pi_e[..., None] * bbr[d][:, :, None]
        return [x.transpose(0, 2, 3, 1).reshape(S5_GROUPS, S5_ROW, S5_STATE) for x in (re, im)]

    b_end = jnp.concatenate(end_state(0, e_f) + end_state(1, e_b), axis=-1)
    o_f = jnp.arange(1, tc + 1)
    o_b = jnp.arange(tc, 0, -1)

    def from_state(d, e):
        re = cap_r[d][..., e].transpose(0, 2, 3, 1).reshape(S5_GROUPS, S5_STATE, S5_ROW)
        im = -cap_i[d][..., e].transpose(0, 2, 3, 1).reshape(S5_GROUPS, S5_STATE, S5_ROW)
        return [re, im]

    c_out = jnp.concatenate(from_state(0, o_f) + from_state(1, o_b), axis=1)
    ptr, pti = pr[..., tc], pi[..., tc]
    p_mul = jnp.concatenate([ptr[0], ptr[0], ptr[1], ptr[1]], axis=-1)
    q_mul = jnp.concatenate([-pti[0], pti[0], -pti[1], pti[1]], axis=-1)
    return toep, b_end, c_out, p_mul, q_mul


def _s5_in_kernel(u_ref, toep_ref, bend_ref, y_ref, sf_ref, sb_ref):
    half = 2 * S5_STATE
    ub = u_ref[...].astype(BF16)
    y_ref[...] = jnp.dot(ub, toep_ref[...], preferred_element_type=F32)
    s = jnp.dot(ub, bend_ref[...], preferred_element_type=F32)
    sf_ref[...] = s[:, 0:half]
    sb_ref[...] = s[:, half:2 * half]


def _s5_scan_kernel(sf_ref, sb_ref, p_ref, q_ref, hf_ref, hb_ref, *, rows_per_seq):
    n_chain = p_ref.shape[0]
    half = 2 * S5_STATE
    pf, qf = p_ref[:, 0:half], q_ref[:, 0:half]
    pb, qb = p_ref[:, half:2 * half], q_ref[:, half:2 * half]
    qf_sw = pltpu.roll(qf, S5_STATE, 1)
    qb_sw = pltpu.roll(qb, S5_STATE, 1)
    zero = jnp.zeros((n_chain, half), F32)

    def step(c, carry):
        hf, hf_sw, hb, hb_sw = carry
        rows_f = pl.ds(c, n_chain, stride=rows_per_seq)
        rows_b = pl.ds(rows_per_seq - 1 - c, n_chain, stride=rows_per_seq)
        hf_ref[rows_f, :] = hf
        hb_ref[rows_b, :] = hb
        sf = sf_ref[rows_f, :]
        sb = sb_ref[rows_b, :]
        sf_sw = pltpu.roll(sf, S5_STATE, 1)
        sb_sw = pltpu.roll(sb, S5_STATE, 1)
        hf_new = pf * hf + qf * hf_sw + sf
        hf_sw_new = pf * hf_sw + qf_sw * hf + sf_sw
        hb_new = pb * hb + qb * hb_sw + sb
        hb_sw_new = pb * hb_sw + qb_sw * hb + sb_sw
        return hf_new, hf_sw_new, hb_new, hb_sw_new

    lax.fori_loop(0, rows_per_seq, step, (zero, zero, zero, zero))


def _s5_out_kernel(hf_ref, hb_ref, cout_ref, y_ref, u_ref, d_ref, o_ref):
    half = 2 * S5_STATE
    y = y_ref[...] + jnp.dot(hf_ref[...].astype(BF16), cout_ref[0:half, :], preferred_element_type=F32)
    y = y + jnp.dot(hb_ref[...].astype(BF16), cout_ref[half:2 * half, :], preferred_element_type=F32)
    y = y + u_ref[...] * d_ref[...]
    o_ref[...] = jax.nn.gelu(y)


def _s5_mix(u, a_re, a_im, log_step, b_re, b_im, c_re, c_im, d_skip):
    bsz, seqlen, _ = u.shape
    rows_per_seq = seqlen // S5_TC
    rows = bsz * rows_per_seq
    toep, b_end, c_out, p_mul, q_mul = _s5_tables(a_re, a_im, log_step, b_re, b_im, c_re, c_im)
    toep, b_end, c_out = toep.astype(BF16), b_end.astype(BF16), c_out.astype(BF16)
    ug = u.reshape(bsz, rows_per_seq, S5_TC, S5_GROUPS, S5_CH).transpose(3, 0, 1, 2, 4).reshape(S5_GROUPS, rows, S5_ROW)
    half = 2 * S5_STATE
    tile = pl.BlockSpec((None, rows, S5_ROW), lambda g: (g, 0, 0))
    stile = pl.BlockSpec((None, rows, half), lambda g: (g, 0, 0))
    wtile = pl.BlockSpec((None, S5_ROW, S5_ROW), lambda g: (g, 0, 0))
    state_shape = jax.ShapeDtypeStruct((S5_GROUPS, rows, half), F32)
    y_in, s_f, s_b = pl.pallas_call(
        _s5_in_kernel,
        grid=(S5_GROUPS,),
        in_specs=[tile, wtile, wtile],
        out_specs=[tile, stile, stile],
        out_shape=[jax.ShapeDtypeStruct((S5_GROUPS, rows, S5_ROW), F32), state_shape, state_shape],
        compiler_params=_params("parallel"),
        name="s5_in",
    )(ug, toep, b_end)

    gs = S5_SCAN_GROUPS
    n_chain = gs * bsz
    chain_p = jnp.repeat(p_mul, bsz, axis=0)
    chain_q = jnp.repeat(q_mul, bsz, axis=0)
    flat = jax.ShapeDtypeStruct((S5_GROUPS * rows, half), F32)
    srows = pl.BlockSpec((gs * rows, half), lambda t: (t, 0))
    chain = pl.BlockSpec((n_chain, S5_ROW), lambda t: (t, 0))
    h_f, h_b = pl.pallas_call(
        functools.partial(_s5_scan_kernel, rows_per_seq=rows_per_seq),
        grid=(S5_GROUPS // gs,),
        in_specs=[srows, srows, chain, chain],
        out_specs=[srows, srows],
        out_shape=[flat, flat],
        compiler_params=_params("parallel"),
        name="s5_scan",
    )(s_f.reshape(S5_GROUPS * rows, half), s_b.reshape(S5_GROUPS * rows, half), chain_p, chain_q)

    d_rows = jnp.tile(d_skip.reshape(S5_GROUPS, 1, S5_CH), (1, S5_TC, 1)).reshape(S5_GROUPS, 1, S5_ROW)
    hg = pl.pallas_call(
        _s5_out_kernel,
        grid=(S5_GROUPS,),
        in_specs=[stile, stile, wtile, tile, tile, pl.BlockSpec((None, 1, S5_ROW), lambda g: (g, 0, 0))],
        out_specs=tile,
        out_shape=jax.ShapeDtypeStruct((S5_GROUPS, rows, S5_ROW), F32),
        compiler_params=_params("parallel"),
        name="s5_out",
    )(h_f.reshape(S5_GROUPS, rows, half), h_b.reshape(S5_GROUPS, rows, half), c_out, y_in, ug, d_rows)
    return hg.reshape(S5_GROUPS, bsz, rows_per_seq, S5_TC, S5_CH).transpose(1, 2, 3, 0, 4).reshape(bsz, seqlen, S5_WIDTH)


def _ssd_proj_kernel(y_ref, z_ref, nw_ref, w_ref, gate_ref, bg_ref, o_ref):
    z = z_ref[...]
    y = y_ref[...] * (z * jax.nn.sigmoid(z))
    y = y * lax.rsqrt(jnp.mean(y * y, axis=-1, keepdims=True) + LN_EPS) * nw_ref[...]
    p = jnp.dot(y.astype(BF16), w_ref[...], preferred_element_type=F32)
    o_ref[...] = jax.nn.sigmoid(gate_ref[...] + bg_ref[...]) * p


def _ssd_project(y, z, norm_w, w_br, gates, b_gate, *, tm=512):
    m = y.shape[0]
    wide = pl.BlockSpec((tm, SSD_INNER), lambda i: (i, 0))
    return pl.pallas_call(
        _ssd_proj_kernel,
        grid=(m // tm,),
        in_specs=[wide, wide, pl.BlockSpec((1, SSD_INNER), lambda i: (0, 0)),
                  pl.BlockSpec((SSD_INNER, D_MODEL), lambda i: (0, 0)),
                  pl.BlockSpec((tm, D_MODEL), lambda i: (i, 0)),
                  pl.BlockSpec((1, D_MODEL), lambda i: (0, 0))],
        out_specs=pl.BlockSpec((tm, D_MODEL), lambda i: (i, 0)),
        out_shape=jax.ShapeDtypeStruct((m, D_MODEL), F32),
        compiler_params=_params("parallel"),
        name="ssd_project",
    )(y, z, norm_w.reshape(1, -1), w_br, gates, b_gate[0:1])


def _s5_proj_kernel(h_ref, w1_ref, w2_ref, w_ref, gate_ref, bg_ref, o_ref):
    hb = h_ref[...].astype(BF16)
    a = jnp.dot(hb, w1_ref[...], preferred_element_type=F32)
    b = jnp.dot(hb, w2_ref[...], preferred_element_type=F32)
    p = jnp.dot((a * jax.nn.sigmoid(b)).astype(BF16), w_ref[...], preferred_element_type=F32)
    o_ref[...] = jax.nn.sigmoid(gate_ref[...] + bg_ref[...]) * p


def _s5_project(h, w1, w2, w_br, gates, b_gate, *, tm=512):
    m = h.shape[0]
    sq = pl.BlockSpec((S5_WIDTH, S5_WIDTH), lambda i: (0, 0))
    return pl.pallas_call(
        _s5_proj_kernel,
        grid=(m // tm,),
        in_specs=[pl.BlockSpec((tm, S5_WIDTH), lambda i: (i, 0)), sq, sq,
                  pl.BlockSpec((S5_WIDTH, D_MODEL), lambda i: (0, 0)),
                  pl.BlockSpec((tm, D_MODEL), lambda i: (i, 2)),
                  pl.BlockSpec((1, D_MODEL), lambda i: (0, 0))],
        out_specs=pl.BlockSpec((tm, D_MODEL), lambda i: (i, 0)),
        out_shape=jax.ShapeDtypeStruct((m, D_MODEL), F32),
        compiler_params=_params("parallel"),
        name="s5_project",
    )(h, w1, w2, w_br, gates, b_gate[2:3])


def _layer_norm(v, g, b):
    mu = jnp.mean(v, axis=-1, keepdims=True)
    c = v - mu
    var = jnp.mean(c * c, axis=-1, keepdims=True)
    return c * lax.rsqrt(var + LN_EPS) * g + b


def _merge_kernel(o0_ref, o1_ref, o2_ref, l0_ref, l1_ref, l2_ref, w_att_ref, gate_ref, bg_ref, pssd_ref, ps5_ref,
                  w_out_ref, x_ref, g_ref, b_ref, o_ref):
    l0, l1, l2 = l0_ref[...], l1_ref[...], l2_ref[...]
    mx = jnp.maximum(jnp.maximum(l0, l1), l2)
    e0, e1, e2 = jnp.exp(l0 - mx), jnp.exp(l1 - mx), jnp.exp(l2 - mx)
    den = e0 + e1 + e2
    att = (e0 / den) * o0_ref[...] + (e1 / den) * o1_ref[...] + (e2 / den) * o2_ref[...]
    p_att = jnp.dot(att.astype(BF16), w_att_ref[...], preferred_element_type=F32)
    merged = pssd_ref[...] + jax.nn.sigmoid(gate_ref[...] + bg_ref[...]) * p_att + ps5_ref[...]
    mix = jnp.dot(merged.astype(BF16), w_out_ref[...], preferred_element_type=F32)
    o_ref[...] = _layer_norm(ALPHA * x_ref[...] + mix, g_ref[...], b_ref[...])


def _merge(att_o, att_l, w_att, gates, b_gate, p_ssd, p_s5, w_out, x, ln_g, ln_b, *, tm=512):
    m = x.shape[0]
    att = pl.BlockSpec((tm, ATT_OUT), lambda i: (i, 0))
    row = pl.BlockSpec((tm, D_MODEL), lambda i: (i, 0))
    vec = pl.BlockSpec((1, D_MODEL), lambda i: (0, 0))
    return pl.pallas_call(
        _merge_kernel,
        grid=(m // tm,),
        in_specs=[att] * 6 + [pl.BlockSpec((ATT_OUT, D_MODEL), lambda i: (0, 0)),
                              pl.BlockSpec((tm, D_MODEL), lambda i: (i, 1)), vec, row, row,
                              pl.BlockSpec((D_MODEL, D_MODEL), lambda i: (0, 0)), row, vec, vec],
        out_specs=row,
        out_shape=jax.ShapeDtypeStruct((m, D_MODEL), F32),
        compiler_params=_params("parallel"),
        name="merge_ln1",
    )(*att_o, *att_l, w_att, gates, b_gate[1:2], p_ssd, p_s5, w_out, x, ln_g.reshape(1, -1), ln_b.reshape(1, -1))


def _router_kernel(x_ref, wt_ref, b_ref, idx_ref, gate_ref):
    x = x_ref[...]
    wt = wt_ref[...]
    xh = x.astype(BF16)
    xl = (x - xh.astype(F32)).astype(BF16)
    wh = wt.astype(BF16)
    wl = (wt - wh.astype(F32)).astype(BF16)
    nt = (((1,), (1,)), ((), ()))
    logits = (lax.dot_general(wh, xh, nt, preferred_element_type=F32)
              + lax.dot_general(wl, xh, nt, preferred_element_type=F32)
              + lax.dot_general(wh, xl, nt, preferred_element_type=F32)) + b_ref[...]
    eid = lax.broadcasted_iota(jnp.int32, logits.shape, 0)
    vals, idxs = [], []
    for _ in range(MOE_TOP_K):
        mx = jnp.max(logits, axis=0, keepdims=True)
        sel = jnp.min(jnp.where(logits == mx, eid, MOE_EXPERTS), axis=0, keepdims=True)
        vals.append(mx)
        idxs.append(sel)
        logits = jnp.where(eid == sel, -jnp.inf, logits)
    es = [jnp.exp(v - vals[0]) for v in vals]
    den = es[0] + es[1] + es[2] + es[3]
    for k in range(MOE_TOP_K):
        idx_ref[k:k + 1, :] = idxs[k]
        gate_ref[k:k + 1, :] = es[k] / den


def _router(x, router_w, router_b, *, tm=1024):
    m = x.shape[0]
    out = pl.BlockSpec((MOE_TOP_K, tm), lambda i: (0, i))
    return pl.pallas_call(
        _router_kernel,
        grid=(m // tm,),
        in_specs=[pl.BlockSpec((tm, D_MODEL), lambda i: (i, 0)),
                  pl.BlockSpec((MOE_EXPERTS, D_MODEL), lambda i: (0, 0)),
                  pl.BlockSpec((MOE_EXPERTS, 1), lambda i: (0, 0))],
        out_specs=[out, out],
        out_shape=[jax.ShapeDtypeStruct((MOE_TOP_K, m), jnp.int32), jax.ShapeDtypeStruct((MOE_TOP_K, m), F32)],
        compiler_params=_params("parallel"),
        name="router",
    )(x, router_w.T, router_b.reshape(-1, 1))


def _expert_kernel(be_ref, used_ref, x_ref, wgu_ref, bgu_ref, wdn_ref, bdn_ref, o_ref):
    i = pl.program_id(0)

    @pl.when(i < used_ref[0])
    def _():
        hu = jnp.dot(x_ref[...].astype(BF16), wgu_ref[...], preferred_element_type=F32) + bgu_ref[...]
        x_glu = jnp.minimum(hu[:, :MOE_FF], SWIGLU_LIMIT)
        x_lin = jnp.clip(hu[:, MOE_FF:], -SWIGLU_LIMIT, SWIGLU_LIMIT)
        act = x_glu * jax.nn.sigmoid(SWIGLU_ALPHA * x_glu) * (x_lin + 1.0)
        o_ref[...] = jnp.dot(act.astype(BF16), wdn_ref[...], preferred_element_type=F32) + bdn_ref[...]

    @pl.when(i >= used_ref[0])
    def _():
        o_ref[...] = jnp.zeros_like(o_ref)


def _experts(xs, block_expert, n_used, w_gu, b_gu, w_dn, b_dn):
    n_rows = xs.shape[0]
    n_blocks = n_rows // MOE_ROWS
    grid_spec = pltpu.PrefetchScalarGridSpec(
        num_scalar_prefetch=2,
        grid=(n_blocks,),
        in_specs=[
            pl.BlockSpec((MOE_ROWS, D_MODEL), lambda i, be, nu: (i, 0)),
            pl.BlockSpec((None, D_MODEL, 2 * MOE_FF), lambda i, be, nu: (be[i], 0, 0)),
            pl.BlockSpec((None, 1, 2 * MOE_FF), lambda i, be, nu: (be[i], 0, 0)),
            pl.BlockSpec((None, MOE_FF, D_MODEL), lambda i, be, nu: (be[i], 0, 0)),
            pl.BlockSpec((None, 1, D_MODEL), lambda i, be, nu: (be[i], 0, 0)),
        ],
        out_specs=pl.BlockSpec((MOE_ROWS, D_MODEL), lambda i, be, nu: (i, 0)),
    )
    return pl.pallas_call(
        _expert_kernel,
        grid_spec=grid_spec,
        out_shape=jax.ShapeDtypeStruct((n_rows, D_MODEL), F32),
        compiler_params=_params("arbitrary"),
        name="experts",
    )(block_expert, n_used, xs, w_gu, b_gu.reshape(MOE_EXPERTS, 1, -1), w_dn, b_dn.reshape(MOE_EXPERTS, 1, -1))


def _combine_kernel(x_ref, y_ref, gate_ref, g_ref, b_ref, o_ref):
    ffn = gate_ref[:, 0:1] * y_ref[:, 0:D_MODEL]
    for k in range(1, MOE_TOP_K):
        ffn = ffn + gate_ref[:, k:k + 1] * y_ref[:, k * D_MODEL:(k + 1) * D_MODEL]
    o_ref[...] = _layer_norm(ALPHA * x_ref[...] + ffn, g_ref[...], b_ref[...])


def _combine(x, y_rows, gate, ln_g, ln_b, *, tm=512):
    m = x.shape[0]
    row = pl.BlockSpec((tm, D_MODEL), lambda i: (i, 0))
    vec = pl.BlockSpec((1, D_MODEL), lambda i: (0, 0))
    return pl.pallas_call(
        _combine_kernel,
        grid=(m // tm,),
        in_specs=[row, pl.BlockSpec((tm, MOE_TOP_K * D_MODEL), lambda i: (i, 0)),
                  pl.BlockSpec((tm, MOE_TOP_K), lambda i: (i, 0)), vec, vec],
        out_specs=row,
        out_shape=jax.ShapeDtypeStruct((m, D_MODEL), F32),
        compiler_params=_params("parallel"),
        name="combine_ln2",
    )(x, y_rows, gate, ln_g.reshape(1, -1), ln_b.reshape(1, -1))


def _moe(x, router_w, router_b, w_gu, b_gu, w_dn, b_dn, ln_g, ln_b):
    ntok = x.shape[0]
    idx_t, gate_t = _router(x, router_w, router_b)
    flat_e = idx_t.T.reshape(-1)
    gate = gate_t.T
    n_assign = ntok * MOE_TOP_K
    onehot = (flat_e[:, None] == jnp.arange(MOE_EXPERTS, dtype=jnp.int32)[None, :]).astype(jnp.int32)
    running = jnp.cumsum(onehot, axis=0)
    rank = jnp.sum(onehot * (running - 1), axis=1)
    counts = running[-1]
    padded = (counts + MOE_ROWS - 1) // MOE_ROWS * MOE_ROWS
    pend = jnp.cumsum(padded)
    pstart = pend - padded
    dest = pstart[flat_e] + rank
    n_blocks = n_assign // MOE_ROWS + MOE_EXPERTS
    flat_tok = jnp.arange(n_assign, dtype=jnp.int32) // MOE_TOP_K
    slot_tok = jnp.full((n_blocks * MOE_ROWS,), ntok, jnp.int32).at[dest].set(flat_tok)
    x_pad = jnp.concatenate([x, jnp.zeros((1, x.shape[1]), x.dtype)], axis=0)
    xs = x_pad[slot_tok]
    block_start = jnp.arange(n_blocks, dtype=jnp.int32) * MOE_ROWS
    block_expert = jnp.sum((pend[None, :] <= block_start[:, None]).astype(jnp.int32), axis=1)
    block_expert = jnp.minimum(block_expert, MOE_EXPERTS - 1)
    n_used = (pend[-1:] // MOE_ROWS).astype(jnp.int32)
    ys = _experts(xs, block_expert, n_used, w_gu, b_gu, w_dn, b_dn)
    y_rows = ys[dest].reshape(ntok, MOE_TOP_K * D_MODEL)
    return _combine(x, y_rows, gate, ln_g, ln_b)


def _layer(x, w_in, b_gate, ssd_conv_w, ssd_conv_b, ssd_a_log, ssd_dt_bias, ssd_d, ssd_norm_w,
           s5_a_re, s5_a_im, s5_log_step, s5_b_re, s5_b_im, s5_c_re, s5_c_im, s5_d, s5_glu_w1, s5_glu_w2,
           w_br_ssd, w_br_attn, w_br_s5, w_out, ln1_g, ln1_b,
           router_w, router_b, w_gu, b_gu, w_dn, b_dn, ln2_g, ln2_b):
    bsz, seqlen, d = x.shape
    ntok = bsz * seqlen
    xf = x.reshape(ntok, d)
    offs = [0]
    for s in IN_SIZES:
        offs.append(offs[-1] + s)

    def seg(k):
        return w_in[:, offs[k]:offs[k + 1]].astype(BF16)

    gates = _matmul(xf, seg(0), tm=512, tn=1024, name="proj_gates")
    z = _matmul(xf, seg(1), tm=512, tn=1024, name="proj_z")
    xbc = _matmul(xf, seg(2), tm=512, tn=1024, name="proj_xbc")
    u = _matmul(xf, seg(5), tm=512, tn=1152, name="proj_u")

    xbc_act = _conv_silu(xbc.reshape(bsz, seqlen, SSD_CONV_CH), ssd_conv_w, ssd_conv_b)
    decay = _ssd_decay(xf, seg(3), ssd_a_log, ssd_dt_bias)
    y_ssd = _ssd_scan(xbc_act, *decay, ssd_d)
    p_ssd = _ssd_project(y_ssd.reshape(ntok, SSD_INNER), z, ssd_norm_w, w_br_ssd.astype(BF16), gates, b_gate)

    w_qkv = w_in[:, offs[4]:offs[5]].reshape(d, 3, ATT_GROUPS, ATT_OUT)
    att_o, att_l = [], []
    for g, (window, dil) in enumerate(ATT_PATTERNS):
        assert (window // 2) // dil == ATT_RADIUS
        ls = seqlen // dil
        x_cls = x.reshape(bsz, ls, dil, d).transpose(0, 2, 1, 3).astype(BF16).reshape(ntok, d)
        qkv_g = _matmul(x_cls, w_qkv[:, :, g].reshape(d, 3 * ATT_OUT).astype(BF16), tm=512, tn=3 * ATT_OUT,
                        name=f"proj_qkv{g}")
        o, lse = _attention_group(qkv_g.reshape(bsz * dil, ls, 3 * ATT_OUT), g, dil)
        att_o.append(o.reshape(bsz, dil, ls, ATT_OUT).transpose(0, 2, 1, 3).reshape(ntok, ATT_OUT))
        att_l.append(lse.reshape(bsz, dil, ls, ATT_OUT).transpose(0, 2, 1, 3).reshape(ntok, ATT_OUT))

    h_s5 = _s5_mix(u.reshape(bsz, seqlen, S5_WIDTH), s5_a_re, s5_a_im, s5_log_step, s5_b_re, s5_b_im,
                   s5_c_re, s5_c_im, s5_d)
    p_s5 = _s5_project(h_s5.reshape(ntok, S5_WIDTH), s5_glu_w1.astype(BF16), s5_glu_w2.astype(BF16),
                       w_br_s5.astype(BF16), gates, b_gate)

    x1 = _merge(att_o, att_l, w_br_attn.astype(BF16), gates, b_gate, p_ssd, p_s5, w_out.astype(BF16),
                xf, ln1_g, ln1_b)
    x2 = _moe(x1, router_w, router_b, w_gu.astype(BF16), b_gu, w_dn.astype(BF16), b_dn, ln2_g, ln2_b)
    return x2.reshape(bsz, seqlen, d)


def kernel(x, w_in, b_gate, ssd_conv_w, ssd_conv_b, ssd_a_log, ssd_dt_bias, ssd_d, ssd_norm_w, s5_a_re, s5_a_im, s5_log_step, s5_b_re, s5_b_im, s5_c_re, s5_c_im, s5_d, s5_glu_w1, s5_glu_w2, w_br_ssd, w_br_attn, w_br_s5, w_out, ln1_g, ln1_b, router_w, router_b, exp_w_gate_up, exp_b_gate_up, exp_w_down, exp_b_down, ln2_g, ln2_b):
    per_layer = (w_in, b_gate, ssd_conv_w, ssd_conv_b, ssd_a_log, ssd_dt_bias, ssd_d, ssd_norm_w, s5_a_re, s5_a_im,
                 s5_log_step, s5_b_re, s5_b_im, s5_c_re, s5_c_im, s5_d, s5_glu_w1, s5_glu_w2, w_br_ssd, w_br_attn,
                 w_br_s5, w_out, ln1_g, ln1_b, router_w, router_b, exp_w_gate_up, exp_b_gate_up, exp_w_down,
                 exp_b_down, ln2_g, ln2_b)
    for layer in range(DEPTH):
        x = _layer(x, *[p[layer] for p in per_layer])
    return x
```

```python
import functools
import math

import jax
import jax.numpy as jnp
from jax import lax
from jax.experimental import pallas as pl
from jax.experimental.pallas import tpu as pltpu

F32 = jnp.float32
BF16 = jnp.bfloat16

D_MODEL = 1024
DEPTH = 4
ALPHA = (2 * DEPTH) ** 0.25
LN_EPS = 1e-5

SSD_INNER = 2 * D_MODEL
SSD_HEAD_DIM = 64
SSD_GROUPS = 8
SSD_HPG = 4
SSD_HEADS = SSD_GROUPS * SSD_HPG
SSD_STATE = 128
SSD_CONV = 5
SSD_CONV_CH = SSD_INNER + 2 * SSD_GROUPS * SSD_STATE
SSD_CHUNK = 128
SSD_SEQ_BLOCK = 1024

ATT_HEAD_DIM = 64
ATT_PATTERNS = ((128, 1), (512, 4), (2048, 16))
ATT_GROUPS = 3
ATT_HPG = 6
ATT_HEADS = ATT_GROUPS * ATT_HPG
ATT_WIDTH = ATT_HEADS * ATT_HEAD_DIM
ATT_OUT = ATT_HPG * ATT_HEAD_DIM
ATT_NEG = -1e30
ATT_RADIUS = 64
ATT_QSUB = 128

S5_CH = 16
S5_STATE = 64
S5_WIDTH = 9 * D_MODEL // 8
S5_GROUPS = S5_WIDTH // S5_CH
S5_TC = 16
S5_ROW = S5_TC * S5_CH
S5_SCAN_GROUPS = 8
S5_LANE_GROUPS = 128 // S5_CH
S5_TILES = S5_GROUPS // S5_LANE_GROUPS
S5_ROW_BLOCK = 256
S5_RELAYOUT_ROWS = 16

N_BRANCHES = 3
IN_SIZES = (N_BRANCHES * D_MODEL, SSD_INNER, SSD_CONV_CH, 2 * SSD_HEADS, 3 * ATT_WIDTH, S5_WIDTH)

MOE_EXPERTS = 32
MOE_TOP_K = 4
MOE_FF = D_MODEL
SWIGLU_LIMIT = 7.0
SWIGLU_ALPHA = 1.702
MOE_ROWS = 256

VMEM_LIMIT = 56 * 1024 * 1024


def _params(*sem):
    return pltpu.CompilerParams(dimension_semantics=sem, vmem_limit_bytes=VMEM_LIMIT)


def _mm_kernel(x_ref, w_ref, o_ref):
    o_ref[...] = jnp.dot(x_ref[...].astype(BF16), w_ref[...], preferred_element_type=F32).astype(o_ref.dtype)


def _matmul(x, w, *, tm, tn, name):
    m, k = x.shape
    n = w.shape[1]
    return pl.pallas_call(
        _mm_kernel,
        grid=(n // tn, m // tm),
        in_specs=[pl.BlockSpec((tm, k), lambda j, i: (i, 0)), pl.BlockSpec((k, tn), lambda j, i: (0, j))],
        out_specs=pl.BlockSpec((tm, tn), lambda j, i: (i, j)),
        out_shape=jax.ShapeDtypeStruct((m, n), F32),
        compiler_params=_params("parallel", "parallel"),
        name=name,
    )(x, w)


def _mm_tiles_kernel(x_ref, w_ref, o_ref):
    acc = jnp.dot(x_ref[...].astype(BF16), w_ref[...], preferred_element_type=F32)
    for t in range(o_ref.shape[0]):
        o_ref[t] = acc[:, 128 * t:128 * (t + 1)]


def _matmul_tiles(x, w, *, tm, name):
    m, k = x.shape
    n = w.shape[1]
    nt = n // 128
    return pl.pallas_call(
        _mm_tiles_kernel,
        grid=(m // tm,),
        in_specs=[pl.BlockSpec((tm, k), lambda i: (i, 0)), pl.BlockSpec((k, n), lambda i: (0, 0))],
        out_specs=pl.BlockSpec((nt, tm, 128), lambda i: (0, i, 0)),
        out_shape=jax.ShapeDtypeStruct((nt, m, 128), F32),
        compiler_params=_params("parallel"),
        name=name,
    )(x, w)


def _split3(a):
    a1 = a.astype(BF16)
    r1 = a - a1.astype(F32)
    a2 = r1.astype(BF16)
    a3 = (r1 - a2.astype(F32)).astype(BF16)
    return a1, a2, a3


def _decay_kernel(x_ref, w_ref, wt_ref, alog_r_ref, alog_c_ref, bias_r_ref, bias_c_ref,
                  acs_c_ref, wend_c_ref, acs_r_ref, delta_r_ref):
    q = SSD_CHUNK
    nh = SSD_HEADS
    xb = x_ref[...].astype(BF16)
    dt_c = jnp.dot(xb, w_ref[...], preferred_element_type=F32)
    dt_r = lax.dot_general(wt_ref[...], xb, (((1,), (1,)), ((), ())), preferred_element_type=F32)
    delta_c = jax.nn.softplus(dt_c + bias_r_ref[...])
    delta_r = jax.nn.softplus(dt_r + bias_c_ref[...])
    a_c = -jnp.exp(alog_r_ref[...]) * delta_c
    a_r = -jnp.exp(alog_c_ref[...]) * delta_r
    delta_r_ref[...] = delta_r
    row = lax.broadcasted_iota(jnp.int32, (q, q), 0)
    col = lax.broadcasted_iota(jnp.int32, (q, q), 1)
    lower = (col <= row).astype(BF16)
    upper = (col >= row).astype(BF16)
    fwd_lane = lax.broadcasted_iota(jnp.int32, (q, dt_c.shape[1]), 1) < nh
    fwd_row = lax.broadcasted_iota(jnp.int32, (2 * nh, q), 0) < nh
    for c in range(x_ref.shape[0] // q):
        rows = slice(c * q, (c + 1) * q)
        pieces = _split3(a_c[rows])
        prefix = sum(jnp.dot(lower, p, preferred_element_type=F32) for p in pieces)
        suffix = sum(jnp.dot(upper, p, preferred_element_type=F32) for p in pieces)
        acs = jnp.where(fwd_lane, prefix, suffix)
        total = jnp.where(fwd_lane[0:1], prefix[q - 1:q], suffix[0:1])
        acs_c_ref[rows, :] = acs
        wend_c_ref[rows, :] = delta_c[rows] * jnp.exp(total - acs)
        pieces = _split3(a_r[:, rows])
        prefix = sum(jnp.dot(p, upper, preferred_element_type=F32) for p in pieces)
        suffix = sum(jnp.dot(p, lower, preferred_element_type=F32) for p in pieces)
        acs_r_ref[:, rows] = jnp.where(fwd_row, prefix, suffix)


def _ssd_decay(x, w_dt, a_log, dt_bias, *, tm=1024):
    m, k = x.shape
    n = w_dt.shape[1]
    lanes = 128
    pad = lanes - n
    w_pad = jnp.pad(w_dt, ((0, 0), (0, pad)))
    alog = a_log.reshape(-1)
    bias = dt_bias.reshape(-1)
    col = pl.BlockSpec((tm, lanes), lambda i: (i, 0))
    rowf = pl.BlockSpec((n, tm), lambda i: (0, i))
    return pl.pallas_call(
        _decay_kernel,
        grid=(m // tm,),
        in_specs=[pl.BlockSpec((tm, k), lambda i: (i, 0)), pl.BlockSpec((k, lanes), lambda i: (0, 0)),
                  pl.BlockSpec((n, k), lambda i: (0, 0)),
                  pl.BlockSpec((1, lanes), lambda i: (0, 0)), pl.BlockSpec((n, 1), lambda i: (0, 0)),
                  pl.BlockSpec((1, lanes), lambda i: (0, 0)), pl.BlockSpec((n, 1), lambda i: (0, 0))],
        out_specs=[col, col, rowf, rowf],
        out_shape=[jax.ShapeDtypeStruct((m, lanes), F32)] * 2 + [jax.ShapeDtypeStruct((n, m), F32)] * 2,
        compiler_params=_params("parallel"),
        name="ssd_decay",
    )(x, w_pad, w_dt.T, jnp.pad(alog, (0, pad)).reshape(1, lanes), alog.reshape(n, 1),
      jnp.pad(bias, (0, pad)).reshape(1, lanes), bias.reshape(n, 1))


def _xbc_conv_kernel(xm_ref, xp_ref, xn_ref, w_ref, cw_ref, cb_ref, o_ref, buf, *, tiles_per_seq):
    pos = pl.program_id(1) % tiles_per_seq
    ts = xm_ref.shape[0]
    half = SSD_CONV // 2
    rows = jnp.concatenate([xp_ref[...], xm_ref[...], xn_ref[...]], axis=0).astype(BF16)
    buf[...] = jnp.dot(rows, w_ref[...], preferred_element_type=F32)

    @pl.when(pos == 0)
    def _():
        buf[0:8, :] = jnp.zeros((8, buf.shape[1]), F32)

    @pl.when(pos == tiles_per_seq - 1)
    def _():
        buf[8 + ts:16 + ts, :] = jnp.zeros((8, buf.shape[1]), F32)

    acc = cb_ref[...] + cw_ref[0:1, :] * buf[8 - half:8 - half + ts, :]
    for k in range(1, SSD_CONV):
        acc = acc + cw_ref[k:k + 1, :] * buf[8 - half + k:8 - half + k + ts, :]
    o_ref[...] = acc * jax.nn.sigmoid(acc)


def _xbc_conv(x, w, conv_w, conv_b, seqlen, *, tm=512, tn=1024):
    m, k = x.shape
    n = w.shape[1]
    nrow8 = m // 8
    return pl.pallas_call(
        functools.partial(_xbc_conv_kernel, tiles_per_seq=seqlen // tm),
        grid=(n // tn, m // tm),
        in_specs=[
            pl.BlockSpec((tm, k), lambda j, i: (i, 0)),
            pl.BlockSpec((8, k), lambda j, i: (jnp.maximum(i * (tm // 8) - 1, 0), 0)),
            pl.BlockSpec((8, k), lambda j, i: (jnp.minimum((i + 1) * (tm // 8), nrow8 - 1), 0)),
            pl.BlockSpec((k, tn), lambda j, i: (0, j)),
            pl.BlockSpec((SSD_CONV, tn), lambda j, i: (0, j)),
            pl.BlockSpec((1, tn), lambda j, i: (0, j)),
        ],
        out_specs=pl.BlockSpec((tm, tn), lambda j, i: (i, j)),
        out_shape=jax.ShapeDtypeStruct((m, n), F32),
        scratch_shapes=[pltpu.VMEM((tm + 16, tn), F32)],
        compiler_params=_params("parallel", "parallel"),
        name="proj_xbc_conv",
    )(x, x, x, w, conv_w, conv_b.reshape(1, n))


def _per_head(cols):
    rows = cols[0].shape[0]
    first = lax.broadcasted_iota(jnp.int32, (rows, 2 * SSD_HEAD_DIM), 1) < SSD_HEAD_DIM
    return jnp.concatenate([jnp.where(first, cols[0], cols[1]), jnp.where(first, cols[2], cols[3])], axis=1)


def _group_lanes(ref, rows, g):
    lanes = ref.shape[1]
    return pltpu.roll(ref[rows, :], (lanes - SSD_HPG * g) % lanes, 1)


def _group_rows(ref, first, cols, g):
    tile = 8
    per_tile = tile // SSD_HPG
    base = pl.multiple_of(first + (g // per_tile) * tile, tile)
    return pltpu.roll(ref[pl.ds(base, tile), cols], (tile - SSD_HPG * (g % per_tile)) % tile, 0)


def _head_selector(lanes, first):
    width = SSD_HPG * SSD_HEAD_DIM
    src = lax.broadcasted_iota(jnp.int32, (lanes, width), 0)
    dst = lax.broadcasted_iota(jnp.int32, (lanes, width), 1) // SSD_HEAD_DIM
    return (src == dst + first).astype(BF16)


def _expand_heads(cols, selector):
    return sum(jnp.dot(p, selector, preferred_element_type=F32) for p in _split3(cols))


def _ssd_back_kernel(xs_ref, bm_ref, acs_ref, wend_ref, out_ref, carried):
    q = SSD_CHUNK
    nchunk = xs_ref.shape[0] // q
    g = pl.program_id(1)
    bwd = SSD_HEADS
    selector = _head_selector(wend_ref.shape[1], bwd)

    @pl.when(pl.program_id(2) == 0)
    def _():
        carried[...] = jnp.zeros_like(carried)

    def chunk(ci, carry):
        c = nchunk - 1 - ci
        rows = pl.ds(pl.multiple_of(c * q, q), q)
        first_row = pl.ds(pl.multiple_of(c * q, q), 8)
        wend = _group_lanes(wend_ref, rows, g)
        total = _group_lanes(acs_ref, first_row, g)[0:1]
        xw = xs_ref[rows, :] * _expand_heads(wend, selector)
        states = lax.dot_general(bm_ref[rows, :].astype(BF16), xw.astype(BF16), (((0,), (0,)), ((), ())),
                                 preferred_element_type=F32)
        prev = carried[...]
        out_ref[c] = prev.astype(out_ref.dtype)
        carried[...] = jnp.exp(_per_head([total[:, bwd + r:bwd + r + 1] for r in range(SSD_HPG)])) * prev + states
        return carry

    lax.fori_loop(0, nchunk, chunk, 0)


def _ssd_main_kernel(xs_ref, bm_ref, cm_ref, acs_c_ref, wend_c_ref, acs_r_ref, delta_r_ref, back_ref,
                     dskip_ref, y_ref, carried):
    q = SSD_CHUNK
    nchunk = xs_ref.shape[0] // q
    g = pl.program_id(1)
    bwd = SSD_HEADS
    hd = SSD_HEAD_DIM

    @pl.when(pl.program_id(2) == 0)
    def _():
        carried[...] = jnp.zeros_like(carried)

    row = lax.broadcasted_iota(jnp.int32, (q, q), 0)
    col = lax.broadcasted_iota(jnp.int32, (q, q), 1)
    lower = col <= row
    upper = col >= row
    d_wide = _per_head([dskip_ref[:, r:r + 1] for r in range(SSD_HPG)])
    selector = _head_selector(wend_c_ref.shape[1], 0)
    first = lax.broadcasted_iota(jnp.int32, (q, 2 * hd), 1) < hd

    def per_head_lanes(blocks):
        return jnp.concatenate([jnp.where(first, blocks[0], blocks[1]), jnp.where(first, blocks[2], blocks[3])], axis=1)

    def chunk(c, carry):
        r0 = pl.multiple_of(c * q, q)
        rows = pl.ds(r0, q)
        xs = xs_ref[rows, :]
        bmat = bm_ref[rows, :].astype(BF16)
        cmat = cm_ref[rows, :].astype(BF16)
        acs = _group_lanes(acs_c_ref, rows, g)
        wend = _group_lanes(wend_c_ref, rows, g)
        acs_f = [jnp.broadcast_to(acs[:, r:r + 1], (q, q)) for r in range(SSD_HPG)]
        acs_b = [jnp.broadcast_to(acs[:, bwd + r:bwd + r + 1], (q, q)) for r in range(SSD_HPG)]
        acs_rf = _group_rows(acs_r_ref, 0, rows, g)
        acs_rb = _group_rows(acs_r_ref, bwd, rows, g)
        del_rf = _group_rows(delta_r_ref, 0, rows, g)
        del_rb = _group_rows(delta_r_ref, bwd, rows, g)
        cb = lax.dot_general(cmat, bmat, (((1,), (1,)), ((), ())), preferred_element_type=F32)
        prev = carried[...]
        states_in = jnp.concatenate([prev.astype(BF16), back_ref[c]], axis=1)
        y_off = jnp.dot(cmat, states_in, preferred_element_type=F32)
        e_f = jnp.exp(per_head_lanes(acs_f))
        e_b = jnp.exp(per_head_lanes(acs_b))
        rest = e_f * y_off[:, 0:SSD_HPG * hd] + e_b * y_off[:, SSD_HPG * hd:] + d_wide * xs
        for r in range(SSD_HPG):
            seg_f = acs_f[r] - acs_rf[r:r + 1, :]
            seg_b = acs_b[r] - acs_rb[r:r + 1, :]
            w = (jnp.exp(jnp.where(lower, seg_f, ATT_NEG)) * del_rf[r:r + 1, :]
                 + jnp.exp(jnp.where(upper, seg_b, ATT_NEG)) * del_rb[r:r + 1, :])
            y_diag = jnp.dot((cb * w).astype(BF16), xs[:, r * hd:(r + 1) * hd].astype(BF16),
                             preferred_element_type=F32)
            y_ref[rows, r * hd:(r + 1) * hd] = y_diag + rest[:, r * hd:(r + 1) * hd]
        xw = xs * _expand_heads(wend, selector)
        states = lax.dot_general(bmat, xw.astype(BF16), (((0,), (0,)), ((), ())), preferred_element_type=F32)
        total = acs[q - 1:q]
        carried[...] = jnp.exp(_per_head([total[:, r:r + 1] for r in range(SSD_HPG)])) * prev + states
        return carry

    lax.fori_loop(0, nchunk, chunk, 0)


def _ssd_scan(xbc_act, acs_c, wend_c, acs_r, delta_r, d_skip):
    bsz, seqlen, _ = xbc_act.shape
    sb = min(SSD_SEQ_BLOCK, seqlen)
    nblk = seqlen // sb
    cpb = sb // SSD_CHUNK
    n_x = SSD_HPG * SSD_HEAD_DIM
    b_off = SSD_INNER // SSD_STATE
    c_off = b_off + SSD_GROUPS
    lanes = acs_c.shape[1]
    nrow = acs_r.shape[0]

    def specs(blk):
        xs = pl.BlockSpec((None, sb, n_x), lambda b_, g, i: (b_, blk(i), g))
        bm = pl.BlockSpec((None, sb, SSD_STATE), lambda b_, g, i: (b_, blk(i), b_off + g))
        cm = pl.BlockSpec((None, sb, SSD_STATE), lambda b_, g, i: (b_, blk(i), c_off + g))
        colf = pl.BlockSpec((sb, lanes), lambda b_, g, i: (b_ * nblk + blk(i), 0))
        rowf = pl.BlockSpec((nrow, sb), lambda b_, g, i: (0, b_ * nblk + blk(i)))
        back = pl.BlockSpec((None, None, cpb, SSD_STATE, n_x), lambda b_, g, i: (b_, g, blk(i), 0, 0))
        return xs, bm, cm, colf, rowf, back

    xs, bm, cm, colf, rowf, back = specs(lambda i: nblk - 1 - i)
    state = pltpu.VMEM((SSD_STATE, n_x), F32)
    back_states = pl.pallas_call(
        _ssd_back_kernel,
        grid=(bsz, SSD_GROUPS, nblk),
        in_specs=[xs, bm, colf, colf],
        out_specs=back,
        out_shape=jax.ShapeDtypeStruct((bsz, SSD_GROUPS, seqlen // SSD_CHUNK, SSD_STATE, n_x), BF16),
        scratch_shapes=[state],
        compiler_params=_params("parallel", "parallel", "arbitrary"),
        name="ssd_back",
    )(xbc_act, xbc_act, acs_c, wend_c)

    xs, bm, cm, colf, rowf, back = specs(lambda i: i)
    return pl.pallas_call(
        _ssd_main_kernel,
        grid=(bsz, SSD_GROUPS, nblk),
        in_specs=[xs, bm, cm, colf, colf, rowf, rowf, back,
                  pl.BlockSpec((None, 1, SSD_HPG), lambda b_, g, i: (g, 0, 0))],
        out_specs=pl.BlockSpec((None, sb, n_x), lambda b_, g, i: (b_, i, g)),
        out_shape=jax.ShapeDtypeStruct((bsz, seqlen, SSD_INNER), F32),
        scratch_shapes=[state],
        compiler_params=_params("parallel", "parallel", "arbitrary"),
        name="ssd_main",
    )(xbc_act, xbc_act, xbc_act, acs_c, wend_c, acs_r, delta_r, back_states,
      d_skip.reshape(SSD_GROUPS, 1, SSD_HPG))


def _attn_kernel(q_ref, ko_ref, kp_ref, kn_ref, vo_ref, vp_ref, vn_ref, o_ref, lse_ref, kcat, vcat,
                 *, dil, group, ls):
    tq = q_ref.shape[0]
    rad = ATT_RADIUS
    qs = ATT_QSUB
    kw = qs + 2 * rad
    i = pl.program_id(1)
    kcat[0:rad, :] = kp_ref[...].astype(BF16)
    kcat[rad:rad + tq, :] = ko_ref[...].astype(BF16)
    kcat[rad + tq:2 * rad + tq, :] = kn_ref[...].astype(BF16)
    vcat[0:rad, :] = vp_ref[...].astype(BF16)
    vcat[rad:rad + tq, :] = vo_ref[...].astype(BF16)
    vcat[rad + tq:2 * rad + tq, :] = vn_ref[...].astype(BF16)

    row = lax.broadcasted_iota(jnp.int32, (qs, kw), 0)
    col = lax.broadcasted_iota(jnp.int32, (qs, kw), 1)
    dist = jnp.abs(row - col + rad)
    in_band = dist <= rad
    dist_f = (dil * dist).astype(F32)
    lane = lax.broadcasted_iota(jnp.int32, (qs, 2 * ATT_HEAD_DIM), 1)
    first_half = lane < ATT_HEAD_DIM
    scale = ATT_HEAD_DIM ** -0.5

    for m in range(tq // qs):
        kpos = i * tq + (m * qs - rad) + col
        valid = in_band & (kpos >= 0) & (kpos < ls)
        for pair in range(ATT_HPG // 2):
            lanes = slice(pair * 2 * ATT_HEAD_DIM, (pair + 1) * 2 * ATT_HEAD_DIM)
            q2 = q_ref[m * qs:(m + 1) * qs, lanes]
            k2 = kcat[m * qs:m * qs + kw, lanes]
            v2 = vcat[m * qs:m * qs + kw, lanes]
            outs, lses = [], []
            for half in range(2):
                head = 2 * pair + half
                slope = 2.0 ** (-8.0 * (head * ATT_GROUPS + group + 1) / ATT_HEADS)
                qm = jnp.where(first_half if half == 0 else ~first_half, q2, 0.0).astype(BF16)
                s = lax.dot_general(qm, k2, (((1,), (1,)), ((), ())), preferred_element_type=F32)
                s = s * scale + (-slope) * dist_f
                s = jnp.where(valid, s, ATT_NEG)
                mx = jnp.max(s, axis=-1, keepdims=True)
                p = jnp.exp(s - mx)
                zsum = jnp.sum(p, axis=-1, keepdims=True)
                pv = jnp.dot(p.astype(BF16), v2, preferred_element_type=F32)
                outs.append(pv / zsum)
                lses.append(mx + jnp.log(zsum))
            o_ref[m * qs:(m + 1) * qs, lanes] = jnp.where(first_half, outs[0], outs[1])
            lse_ref[m * qs:(m + 1) * qs, lanes] = jnp.where(first_half, lses[0], lses[1])


def _attention_group(qkv, group, dil):
    nseq, ls, _ = qkv.shape
    tq = min(512, ls)
    nblk = ls // tq
    halo_per_blk = tq // ATT_RADIUS
    n_halo = ls // ATT_RADIUS

    def own(off):
        return pl.BlockSpec((None, tq, ATT_OUT), lambda s, i: (s, i, off))

    def prev(off):
        return pl.BlockSpec((None, ATT_RADIUS, ATT_OUT), lambda s, i: (s, jnp.maximum(i * halo_per_blk - 1, 0), off))

    def nxt(off):
        return pl.BlockSpec((None, ATT_RADIUS, ATT_OUT),
                            lambda s, i: (s, jnp.minimum((i + 1) * halo_per_blk, n_halo - 1), off))

    out_spec = pl.BlockSpec((None, tq, ATT_OUT), lambda s, i: (s, i, 0))
    return pl.pallas_call(
        functools.partial(_attn_kernel, dil=dil, group=group, ls=ls),
        grid=(nseq, nblk),
        in_specs=[own(0), own(1), prev(1), nxt(1), own(2), prev(2), nxt(2)],
        out_specs=[out_spec, out_spec],
        out_shape=[jax.ShapeDtypeStruct((nseq, ls, ATT_OUT), F32)] * 2,
        scratch_shapes=[pltpu.VMEM((tq + 2 * ATT_RADIUS, ATT_OUT), BF16)] * 2,
        compiler_params=_params("parallel", "parallel"),
        name=f"attn_g{group}",
    )(qkv, qkv, qkv, qkv, qkv, qkv, qkv)


def _s5_tables(a_re, a_im, log_step, b_re, b_im, c_re, c_im):
    hp = lax.Precision.HIGHEST
    tc = S5_TC
    taus = jnp.arange(tc + 1, dtype=F32)
    step = jnp.exp(log_step)[..., None]
    ar, ai = a_re, a_im
    mag = jnp.exp(step * ar)
    abr, abi = mag * jnp.cos(step * ai), mag * jnp.sin(step * ai)
    den = ar * ar + ai * ai
    fr = ((abr - 1.0) * ar + abi * ai) / den
    fi = (abi * ar - (abr - 1.0) * ai) / den
    bbr = fr[..., None] * b_re[None] - fi[..., None] * b_im[None]
    bbi = fr[..., None] * b_im[None] + fi[..., None] * b_re[None]
    pmag = jnp.exp(step[..., None] * ar[..., None] * taus)
    pr = pmag * jnp.cos(step[..., None] * ai[..., None] * taus)
    pi = pmag * jnp.sin(step[..., None] * ai[..., None] * taus)
    cap_r = c_re[..., None] * pr[:, :, None] - c_im[..., None] * pi[:, :, None]
    cap_i = c_re[..., None] * pi[:, :, None] + c_im[..., None] * pr[:, :, None]
    taps = (jnp.einsum('dgonk,dgni->dgkoi', cap_r, bbr, precision=hp)
            - jnp.einsum('dgonk,dgni->dgkoi', cap_i, bbi, precision=hp))
    s_idx = jnp.arange(tc)[:, None]
    t_idx = jnp.arange(tc)[None, :]
    lag_f = jnp.clip(t_idx - s_idx, 0, tc)
    lag_b = jnp.clip(s_idx - t_idx, 0, tc)
    kf = jnp.where((t_idx >= s_idx)[None, :, :, None, None], taps[0][:, lag_f], 0.0)
    kb = jnp.where((s_idx >= t_idx)[None, :, :, None, None], taps[1][:, lag_b], 0.0)
    toep = (kf + kb).transpose(0, 1, 4, 2, 3).reshape(S5_GROUPS, S5_ROW, S5_ROW)
    e_f = jnp.arange(tc - 1, -1, -1)
    e_b = jnp.arange(tc)

    def end_state(d, e):
        pr_e, pi_e = pr[d][..., e], pi[d][..., e]
        re = pr_e[..., None] * bbr[d][:, :, None] - pi_e[..., None] * bbi[d][:, :, None]
        im = pr_e[..., None] * bbi[d][:, :, None] + pi_e[..., None] * bbr[d][:, :, None]
        return [x.transpose(0, 2, 3, 1).reshape(S5_GROUPS, S5_ROW, S5_STATE) for x in (re, im)]

    b_end = jnp.concatenate(end_state(0, e_f) + end_state(1, e_b), axis=-1)
    o_f = jnp.arange(1, tc + 1)
    o_b = jnp.arange(tc, 0, -1)

    def from_state(d, e):
        re = cap_r[d][..., e].transpose(0, 2, 3, 1).reshape(S5_GROUPS, S5_STATE, S5_ROW)
        im = -cap_i[d][..., e].transpose(0, 2, 3, 1).reshape(S5_GROUPS, S5_STATE, S5_ROW)
        return [re, im]

    c_out = jnp.concatenate(from_state(0, o_f) + from_state(1, o_b), axis=1)
    ptr, pti = pr[..., tc], pi[..., tc]
    p_mul = jnp.concatenate([ptr[0], ptr[0], ptr[1], ptr[1]], axis=-1)
    q_mul = jnp.concatenate([-pti[0], pti[0], -pti[1], pti[1]], axis=-1)
    return toep, b_end, c_out, p_mul, q_mul


def _s5_rows_to_groups(u_ref, x_scr):
    rr = S5_RELAYOUT_ROWS
    per = S5_LANE_GROUPS
    granule = lax.broadcasted_iota(jnp.int32, (rr, 128), 1) // S5_CH

    def body(rb, carry):
        base = rb * (rr * S5_TC)
        pieces = [u_ref[pl.ds(base + tau, rr, stride=S5_TC), :] for tau in range(S5_TC)]
        rows = pl.ds(pl.multiple_of(rb * rr, rr), rr)
        for gl in range(per):
            for h in range(S5_TC // per):
                acc = None
                for t in range(per):
                    p = pieces[per * h + t]
                    if t != gl:
                        p = pltpu.roll(p, ((t - gl) * S5_CH) % 128, 1)
                    acc = p if acc is None else jnp.where(granule == t, p, acc)
                x_scr[gl, rows, 128 * h:128 * (h + 1)] = acc
        return carry

    lax.fori_loop(0, x_scr.shape[1] // rr, body, 0)


def _s5_in_kernel(u_ref, toep_ref, bend_ref, y_ref, sf_ref, sb_ref, x_scr):
    half = 2 * S5_STATE
    _s5_rows_to_groups(u_ref, x_scr)
    for gl in range(S5_LANE_GROUPS):
        xb = x_scr[gl].astype(BF16)
        y_ref[gl] = jnp.dot(xb, toep_ref[gl], preferred_element_type=F32)
        s = jnp.dot(xb, bend_ref[gl], preferred_element_type=F32)
        sf_ref[gl] = s[:, 0:half]
        sb_ref[gl] = s[:, half:2 * half]


def _s5_scan_kernel(sf_ref, sb_ref, p_ref, q_ref, hf_ref, hb_ref, sf_sw_ref, sb_sw_ref, *, rows_per_seq):
    n_chain = p_ref.shape[0]
    half = 2 * S5_STATE
    pf, qf = p_ref[:, 0:half], q_ref[:, 0:half]
    pb, qb = p_ref[:, half:2 * half], q_ref[:, half:2 * half]
    qf_sw = pltpu.roll(qf, S5_STATE, 1)
    qb_sw = pltpu.roll(qb, S5_STATE, 1)
    zero = jnp.zeros((n_chain, half), F32)
    piece = 256

    def swap(k, carry):
        rows = pl.ds(pl.multiple_of(k * piece, piece), piece)
        sf_sw_ref[rows, :] = pltpu.roll(sf_ref[rows, :], S5_STATE, 1)
        sb_sw_ref[rows, :] = pltpu.roll(sb_ref[rows, :], S5_STATE, 1)
        return carry

    lax.fori_loop(0, sf_ref.shape[0] // piece, swap, 0)

    def step(c, carry):
        hf, hf_sw, hb, hb_sw = carry
        rows_f = pl.ds(c, n_chain, stride=rows_per_seq)
        rows_b = pl.ds(rows_per_seq - 1 - c, n_chain, stride=rows_per_seq)
        hf_ref[rows_f, :] = hf
        hb_ref[rows_b, :] = hb
        sf = sf_ref[rows_f, :]
        sb = sb_ref[rows_b, :]
        sf_sw = sf_sw_ref[rows_f, :]
        sb_sw = sb_sw_ref[rows_b, :]
        hf_new = pf * hf + qf * hf_sw + sf
        hf_sw_new = pf * hf_sw + qf_sw * hf + sf_sw
        hb_new = pb * hb + qb * hb_sw + sb
        hb_sw_new = pb * hb_sw + qb_sw * hb + sb_sw
        return hf_new, hf_sw_new, hb_new, hb_sw_new

    lax.fori_loop(0, rows_per_seq, step, (zero, zero, zero, zero))


def _s5_out_kernel(hf_ref, hb_ref, cout_ref, y_ref, u_ref, d_ref, o_ref, y_scr):
    half = 2 * S5_STATE
    per = S5_LANE_GROUPS
    for gl in range(per):
        y = y_ref[gl] + jnp.dot(hf_ref[gl].astype(BF16), cout_ref[gl, 0:half, :], preferred_element_type=F32)
        y_scr[gl] = y + jnp.dot(hb_ref[gl].astype(BF16), cout_ref[gl, half:2 * half, :], preferred_element_type=F32)
    rr = S5_RELAYOUT_ROWS
    granule = lax.broadcasted_iota(jnp.int32, (rr, 128), 1) // S5_CH
    d_skip = d_ref[...]

    def body(rb, carry):
        rows = pl.ds(pl.multiple_of(rb * rr, rr), rr)
        halves = [[y_scr[gl, rows, 128 * h:128 * (h + 1)] for h in range(S5_TC // per)] for gl in range(per)]
        base = rb * (rr * S5_TC)
        for tau in range(S5_TC):
            h, t = divmod(tau, per)
            acc = None
            for gl in range(per):
                p = halves[gl][h]
                if gl != t:
                    p = pltpu.roll(p, ((gl - t) * S5_CH) % 128, 1)
                acc = p if acc is None else jnp.where(granule == gl, p, acc)
            tok = pl.ds(base + tau, rr, stride=S5_TC)
            o_ref[tok, :] = jax.nn.gelu(acc + u_ref[tok, :] * d_skip)
        return carry

    lax.fori_loop(0, y_scr.shape[1] // rr, body, 0)


def _s5_mix(u_tiles, seqlen, a_re, a_im, log_step, b_re, b_im, c_re, c_im, d_skip):
    ntile, ntok, lanes = u_tiles.shape
    bsz = ntok // seqlen
    rows_per_seq = seqlen // S5_TC
    rows = bsz * rows_per_seq
    rblk = min(S5_ROW_BLOCK, rows)
    per = S5_LANE_GROUPS
    half = 2 * S5_STATE
    toep, b_end, c_out, p_mul, q_mul = _s5_tables(a_re, a_im, log_step, b_re, b_im, c_re, c_im)
    toep, b_end, c_out = toep.astype(BF16), b_end.astype(BF16), c_out.astype(BF16)
    utile = pl.BlockSpec((None, rblk * S5_TC, lanes), lambda j, i: (j, i, 0))
    ytile = pl.BlockSpec((per, rblk, S5_ROW), lambda j, i: (j, i, 0))
    stile = pl.BlockSpec((per, rblk, half), lambda j, i: (j, i, 0))
    wtile = pl.BlockSpec((per, S5_ROW, S5_ROW), lambda j, i: (j, 0, 0))
    state_shape = jax.ShapeDtypeStruct((S5_GROUPS, rows, half), F32)
    grid = (ntile, rows // rblk)
    y_in, s_f, s_b = pl.pallas_call(
        _s5_in_kernel,
        grid=grid,
        in_specs=[utile, wtile, wtile],
        out_specs=[ytile, stile, stile],
        out_shape=[jax.ShapeDtypeStruct((S5_GROUPS, rows, S5_ROW), F32), state_shape, state_shape],
        scratch_shapes=[pltpu.VMEM((per, rblk, S5_ROW), F32)],
        compiler_params=_params("parallel", "parallel"),
        name="s5_in",
    )(u_tiles, toep, b_end)

    gs = S5_SCAN_GROUPS
    n_chain = gs * bsz
    chain_p = jnp.repeat(p_mul, bsz, axis=0)
    chain_q = jnp.repeat(q_mul, bsz, axis=0)
    flat = jax.ShapeDtypeStruct((S5_GROUPS * rows, half), F32)
    srows = pl.BlockSpec((gs * rows, half), lambda t: (t, 0))
    chain = pl.BlockSpec((n_chain, S5_ROW), lambda t: (t, 0))
    h_f, h_b = pl.pallas_call(
        functools.partial(_s5_scan_kernel, rows_per_seq=rows_per_seq),
        grid=(S5_GROUPS // gs,),
        in_specs=[srows, srows, chain, chain],
        out_specs=[srows, srows],
        out_shape=[flat, flat],
        scratch_shapes=[pltpu.VMEM((gs * rows, half), F32)] * 2,
        compiler_params=_params("parallel"),
        name="s5_scan",
    )(s_f.reshape(S5_GROUPS * rows, half), s_b.reshape(S5_GROUPS * rows, half), chain_p, chain_q)

    return pl.pallas_call(
        _s5_out_kernel,
        grid=grid,
        in_specs=[stile, stile, wtile, ytile, utile, pl.BlockSpec((None, 1, lanes), lambda j, i: (j, 0, 0))],
        out_specs=utile,
        out_shape=jax.ShapeDtypeStruct((ntile, ntok, lanes), F32),
        scratch_shapes=[pltpu.VMEM((per, rblk, S5_ROW), F32)],
        compiler_params=_params("parallel", "parallel"),
        name="s5_out",
    )(h_f.reshape(S5_GROUPS, rows, half), h_b.reshape(S5_GROUPS, rows, half), c_out, y_in, u_tiles,
      d_skip.reshape(ntile, 1, lanes))


def _ssd_proj_kernel(y_ref, z_ref, nw_ref, w_ref, gate_ref, bg_ref, o_ref):
    z = z_ref[...]
    y = y_ref[...] * (z * jax.nn.sigmoid(z))
    y = y * lax.rsqrt(jnp.mean(y * y, axis=-1, keepdims=True) + LN_EPS) * nw_ref[...]
    p = jnp.dot(y.astype(BF16), w_ref[...], preferred_element_type=F32)
    o_ref[...] = jax.nn.sigmoid(gate_ref[...] + bg_ref[...]) * p


def _ssd_project(y, z, norm_w, w_br, gates, b_gate, *, tm=512):
    m = y.shape[0]
    wide = pl.BlockSpec((tm, SSD_INNER), lambda i: (i, 0))
    return pl.pallas_call(
        _ssd_proj_kernel,
        grid=(m // tm,),
        in_specs=[wide, wide, pl.BlockSpec((1, SSD_INNER), lambda i: (0, 0)),
                  pl.BlockSpec((SSD_INNER, D_MODEL), lambda i: (0, 0)),
                  pl.BlockSpec((tm, D_MODEL), lambda i: (i, 0)),
                  pl.BlockSpec((1, D_MODEL), lambda i: (0, 0))],
        out_specs=pl.BlockSpec((tm, D_MODEL), lambda i: (i, 0)),
        out_shape=jax.ShapeDtypeStruct((m, D_MODEL), F32),
        compiler_params=_params("parallel"),
        name="ssd_project",
    )(y, z, norm_w.reshape(1, -1), w_br, gates, b_gate[0:1])


def _s5_proj_kernel(h_ref, w1_ref, w2_ref, w_ref, gate_ref, bg_ref, o_ref):
    hb = jnp.concatenate([h_ref[t] for t in range(h_ref.shape[0])], axis=1).astype(BF16)
    a = jnp.dot(hb, w1_ref[...], preferred_element_type=F32)
    b = jnp.dot(hb, w2_ref[...], preferred_element_type=F32)
    p = jnp.dot((a * jax.nn.sigmoid(b)).astype(BF16), w_ref[...], preferred_element_type=F32)
    o_ref[...] = jax.nn.sigmoid(gate_ref[...] + bg_ref[...]) * p


def _s5_project(h_tiles, w1, w2, w_br, gates, b_gate, *, tm=512):
    ntile, m, lanes = h_tiles.shape
    sq = pl.BlockSpec((S5_WIDTH, S5_WIDTH), lambda i: (0, 0))
    return pl.pallas_call(
        _s5_proj_kernel,
        grid=(m // tm,),
        in_specs=[pl.BlockSpec((ntile, tm, lanes), lambda i: (0, i, 0)), sq, sq,
                  pl.BlockSpec((S5_WIDTH, D_MODEL), lambda i: (0, 0)),
                  pl.BlockSpec((tm, D_MODEL), lambda i: (i, 2)),
                  pl.BlockSpec((1, D_MODEL), lambda i: (0, 0))],
        out_specs=pl.BlockSpec((tm, D_MODEL), lambda i: (i, 0)),
        out_shape=jax.ShapeDtypeStruct((m, D_MODEL), F32),
        compiler_params=_params("parallel"),
        name="s5_project",
    )(h_tiles, w1, w2, w_br, gates, b_gate[2:3])


def _layer_norm(v, g, b):
    mu = jnp.mean(v, axis=-1, keepdims=True)
    c = v - mu
    var = jnp.mean(c * c, axis=-1, keepdims=True)
    return c * lax.rsqrt(var + LN_EPS) * g + b


def _merge_kernel(o0_ref, o1_ref, o2_ref, l0_ref, l1_ref, l2_ref, w_att_ref, gate_ref, bg_ref, pssd_ref, ps5_ref,
                  w_out_ref, x_ref, g_ref, b_ref, o_ref):
    l0, l1, l2 = l0_ref[...], l1_ref[...], l2_ref[...]
    mx = jnp.maximum(jnp.maximum(l0, l1), l2)
    e0, e1, e2 = jnp.exp(l0 - mx), jnp.exp(l1 - mx), jnp.exp(l2 - mx)
    den = e0 + e1 + e2
    att = (e0 / den) * o0_ref[...] + (e1 / den) * o1_ref[...] + (e2 / den) * o2_ref[...]
    p_att = jnp.dot(att.astype(BF16), w_att_ref[...], preferred_element_type=F32)
    merged = pssd_ref[...] + jax.nn.sigmoid(gate_ref[...] + bg_ref[...]) * p_att + ps5_ref[...]
    mix = jnp.dot(merged.astype(BF16), w_out_ref[...], preferred_element_type=F32)
    o_ref[...] = _layer_norm(ALPHA * x_ref[...] + mix, g_ref[...], b_ref[...])


def _merge(att_o, att_l, w_att, gates, b_gate, p_ssd, p_s5, w_out, x, ln_g, ln_b, *, tm=512):
    m = x.shape[0]
    att = pl.BlockSpec((tm, ATT_OUT), lambda i: (i, 0))
    row = pl.BlockSpec((tm, D_MODEL), lambda i: (i, 0))
    vec = pl.BlockSpec((1, D_MODEL), lambda i: (0, 0))
    return pl.pallas_call(
        _merge_kernel,
        grid=(m // tm,),
        in_specs=[att] * 6 + [pl.BlockSpec((ATT_OUT, D_MODEL), lambda i: (0, 0)),
                              pl.BlockSpec((tm, D_MODEL), lambda i: (i, 1)), vec, row, row,
                              pl.BlockSpec((D_MODEL, D_MODEL), lambda i: (0, 0)), row, vec, vec],
        out_specs=row,
        out_shape=jax.ShapeDtypeStruct((m, D_MODEL), F32),
        compiler_params=_params("parallel"),
        name="merge_ln1",
    )(*att_o, *att_l, w_att, gates, b_gate[1:2], p_ssd, p_s5, w_out, x, ln_g.reshape(1, -1), ln_b.reshape(1, -1))


def _router_kernel(x_ref, wt_ref, b_ref, idx_ref, gate_ref, rank_ref, count_ref, before_ref):
    tm = x_ref.shape[0]

    @pl.when(pl.program_id(0) == 0)
    def _():
        count_ref[...] = jnp.zeros_like(count_ref)
        earlier = lax.broadcasted_iota(jnp.int32, (tm, tm), 0) < lax.broadcasted_iota(jnp.int32, (tm, tm), 1)
        before_ref[...] = earlier.astype(before_ref.dtype)

    x = x_ref[...]
    wt = wt_ref[...]
    xh = x.astype(BF16)
    xl = (x - xh.astype(F32)).astype(BF16)
    wh = wt.astype(BF16)
    wl = (wt - wh.astype(F32)).astype(BF16)
    nt = (((1,), (1,)), ((), ()))
    logits = (lax.dot_general(wh, xh, nt, preferred_element_type=F32)
              + lax.dot_general(wl, xh, nt, preferred_element_type=F32)
              + lax.dot_general(wh, xl, nt, preferred_element_type=F32)) + b_ref[...]
    eid = lax.broadcasted_iota(jnp.int32, logits.shape, 0)
    vals, idxs = [], []
    for _ in range(MOE_TOP_K):
        mx = jnp.max(logits, axis=0, keepdims=True)
        sel = jnp.min(jnp.where(logits == mx, eid, MOE_EXPERTS), axis=0, keepdims=True)
        vals.append(mx)
        idxs.append(sel)
        logits = jnp.where(eid == sel, -jnp.inf, logits)
    es = [jnp.exp(v - vals[0]) for v in vals]
    den = es[0] + es[1] + es[2] + es[3]
    for k in range(MOE_TOP_K):
        idx_ref[k:k + 1, :] = idxs[k]
        gate_ref[k:k + 1, :] = es[k] / den
    chosen = [eid == idxs[k] for k in range(MOE_TOP_K)]
    stacked = jnp.concatenate([c.astype(before_ref.dtype) for c in chosen], axis=0)
    earlier = jnp.dot(stacked, before_ref[...], preferred_element_type=F32)
    seen = count_ref[...]
    for k in range(MOE_TOP_K):
        pos = earlier[k * MOE_EXPERTS:(k + 1) * MOE_EXPERTS] + seen
        rank_ref[k:k + 1, :] = jnp.sum(jnp.where(chosen[k], pos, 0.0), axis=0, keepdims=True).astype(jnp.int32)
        seen = seen + jnp.sum(chosen[k].astype(F32), axis=1, keepdims=True)
    count_ref[...] = seen


def _router(x, router_w, router_b, *, tm=1024):
    m = x.shape[0]
    out = pl.BlockSpec((MOE_TOP_K, tm), lambda i: (0, i))
    return pl.pallas_call(
        _router_kernel,
        grid=(m // tm,),
        in_specs=[pl.BlockSpec((tm, D_MODEL), lambda i: (i, 0)),
                  pl.BlockSpec((MOE_EXPERTS, D_MODEL), lambda i: (0, 0)),
                  pl.BlockSpec((MOE_EXPERTS, 1), lambda i: (0, 0))],
        out_specs=[out, out, out, pl.BlockSpec((MOE_EXPERTS, 1), lambda i: (0, 0))],
        out_shape=[jax.ShapeDtypeStruct((MOE_TOP_K, m), jnp.int32), jax.ShapeDtypeStruct((MOE_TOP_K, m), F32),
                   jax.ShapeDtypeStruct((MOE_TOP_K, m), jnp.int32), jax.ShapeDtypeStruct((MOE_EXPERTS, 1), F32)],
        scratch_shapes=[pltpu.VMEM((tm, tm), BF16)],
        compiler_params=_params("arbitrary"),
        name="router",
    )(x, router_w.T, router_b.reshape(-1, 1))


def _expert_kernel(be_ref, used_ref, x_ref, wgu_ref, bgu_ref, wdn_ref, bdn_ref, o_ref):
    i = pl.program_id(0)

    @pl.when(i < used_ref[0])
    def _():
        hu = jnp.dot(x_ref[...].astype(BF16), wgu_ref[...], preferred_element_type=F32) + bgu_ref[...]
        x_glu = jnp.minimum(hu[:, :MOE_FF], SWIGLU_LIMIT)
        x_lin = jnp.clip(hu[:, MOE_FF:], -SWIGLU_LIMIT, SWIGLU_LIMIT)
        act = x_glu * jax.nn.sigmoid(SWIGLU_ALPHA * x_glu) * (x_lin + 1.0)
        o_ref[...] = jnp.dot(act.astype(BF16), wdn_ref[...], preferred_element_type=F32) + bdn_ref[...]

    @pl.when(i >= used_ref[0])
    def _():
        o_ref[...] = jnp.zeros_like(o_ref)


def _experts(xs, block_expert, n_used, w_gu, b_gu, w_dn, b_dn):
    n_rows = xs.shape[0]
    n_blocks = n_rows // MOE_ROWS
    grid_spec = pltpu.PrefetchScalarGridSpec(
        num_scalar_prefetch=2,
        grid=(n_blocks,),
        in_specs=[
            pl.BlockSpec((MOE_ROWS, D_MODEL), lambda i, be, nu: (i, 0)),
            pl.BlockSpec((None, D_MODEL, 2 * MOE_FF), lambda i, be, nu: (be[i], 0, 0)),
            pl.BlockSpec((None, 1, 2 * MOE_FF), lambda i, be, nu: (be[i], 0, 0)),
            pl.BlockSpec((None, MOE_FF, D_MODEL), lambda i, be, nu: (be[i], 0, 0)),
            pl.BlockSpec((None, 1, D_MODEL), lambda i, be, nu: (be[i], 0, 0)),
        ],
        out_specs=pl.BlockSpec((MOE_ROWS, D_MODEL), lambda i, be, nu: (i, 0)),
    )
    return pl.pallas_call(
        _expert_kernel,
        grid_spec=grid_spec,
        out_shape=jax.ShapeDtypeStruct((n_rows, D_MODEL), F32),
        compiler_params=_params("arbitrary"),
        name="experts",
    )(block_expert, n_used, xs, w_gu, b_gu.reshape(MOE_EXPERTS, 1, -1), w_dn, b_dn.reshape(MOE_EXPERTS, 1, -1))


def _combine_kernel(x_ref, y_ref, gate_ref, g_ref, b_ref, o_ref):
    ffn = gate_ref[:, 0:1] * y_ref[:, 0:D_MODEL]
    for k in range(1, MOE_TOP_K):
        ffn = ffn + gate_ref[:, k:k + 1] * y_ref[:, k * D_MODEL:(k + 1) * D_MODEL]
    o_ref[...] = _layer_norm(ALPHA * x_ref[...] + ffn, g_ref[...], b_ref[...])


def _combine(x, y_rows, gate, ln_g, ln_b, *, tm=512):
    m = x.shape[0]
    row = pl.BlockSpec((tm, D_MODEL), lambda i: (i, 0))
    vec = pl.BlockSpec((1, D_MODEL), lambda i: (0, 0))
    return pl.pallas_call(
        _combine_kernel,
        grid=(m // tm,),
        in_specs=[row, pl.BlockSpec((tm, MOE_TOP_K * D_MODEL), lambda i: (i, 0)),
                  pl.BlockSpec((tm, MOE_TOP_K), lambda i: (i, 0)), vec, vec],
        out_specs=row,
        out_shape=jax.ShapeDtypeStruct((m, D_MODEL), F32),
        compiler_params=_params("parallel"),
        name="combine_ln2",
    )(x, y_rows, gate, ln_g.reshape(1, -1), ln_b.reshape(1, -1))


def _moe(x, router_w, router_b, w_gu, b_gu, w_dn, b_dn, ln_g, ln_b):
    ntok = x.shape[0]
    idx_t, gate_t, rank_t, counts = _router(x, router_w, router_b)
    gate = gate_t.T
    n_assign = ntok * MOE_TOP_K
    counts = counts.reshape(-1).astype(jnp.int32)
    padded = (counts + MOE_ROWS - 1) // MOE_ROWS * MOE_ROWS
    pend = jnp.cumsum(padded)
    pstart = pend - padded
    first = jnp.cumsum(counts) - counts
    dest = (pstart[idx_t] + rank_t).T.reshape(-1)
    n_blocks = n_assign // MOE_ROWS + MOE_EXPERTS
    block_start = jnp.arange(n_blocks, dtype=jnp.int32) * MOE_ROWS
    block_expert = jnp.sum((pend[None, :] <= block_start[:, None]).astype(jnp.int32), axis=1)
    block_expert = jnp.minimum(block_expert, MOE_EXPERTS - 1)
    n_used = (pend[-1:] // MOE_ROWS).astype(jnp.int32)
    flat_tok = jnp.arange(n_assign, dtype=jnp.int32) // MOE_TOP_K
    _, tok_by_slot = lax.sort_key_val(dest, flat_tok)
    slot = jnp.arange(n_blocks * MOE_ROWS, dtype=jnp.int32)
    slot_expert = jnp.repeat(block_expert, MOE_ROWS)
    within = slot - pstart[slot_expert]
    packed = jnp.clip(first[slot_expert] + within, 0, n_assign - 1)
    slot_tok = jnp.where(within < counts[slot_expert], tok_by_slot[packed], ntok)
    x_pad = jnp.concatenate([x.astype(BF16), jnp.zeros((1, x.shape[1]), BF16)], axis=0)
    xs = x_pad[slot_tok]
    ys = _experts(xs, block_expert, n_used, w_gu, b_gu, w_dn, b_dn)
    y_rows = ys[dest].reshape(ntok, MOE_TOP_K * D_MODEL)
    return _combine(x, y_rows, gate, ln_g, ln_b)


def _layer(x, w_in, b_gate, ssd_conv_w, ssd_conv_b, ssd_a_log, ssd_dt_bias, ssd_d, ssd_norm_w,
           s5_a_re, s5_a_im, s5_log_step, s5_b_re, s5_b_im, s5_c_re, s5_c_im, s5_d, s5_glu_w1, s5_glu_w2,
           w_br_ssd, w_br_attn, w_br_s5, w_out, ln1_g, ln1_b,
           router_w, router_b, w_gu, b_gu, w_dn, b_dn, ln2_g, ln2_b):
    bsz, seqlen, d = x.shape
    ntok = bsz * seqlen
    xf = x.reshape(ntok, d)
    offs = [0]
    for s in IN_SIZES:
        offs.append(offs[-1] + s)

    def seg(k):
        return w_in[:, offs[k]:offs[k + 1]].astype(BF16)

    gates = _matmul(xf, seg(0), tm=512, tn=1024, name="proj_gates")
    z = _matmul(xf, seg(1), tm=512, tn=1024, name="proj_z")
    u = _matmul_tiles(xf, seg(5), tm=512, name="proj_u")

    xbc_act = _xbc_conv(xf, seg(2), ssd_conv_w, ssd_conv_b, seqlen).reshape(bsz, seqlen, SSD_CONV_CH)
    decay = _ssd_decay(xf, seg(3), ssd_a_log, ssd_dt_bias)
    y_ssd = _ssd_scan(xbc_act, *decay, ssd_d)
    p_ssd = _ssd_project(y_ssd.reshape(ntok, SSD_INNER), z, ssd_norm_w, w_br_ssd.astype(BF16), gates, b_gate)

    w_qkv = w_in[:, offs[4]:offs[5]].reshape(d, 3, ATT_GROUPS, ATT_OUT)
    att_o, att_l = [], []
    for g, (window, dil) in enumerate(ATT_PATTERNS):
        assert (window // 2) // dil == ATT_RADIUS
        ls = seqlen // dil
        x_cls = x.reshape(bsz, ls, dil, d).transpose(0, 2, 1, 3).astype(BF16).reshape(ntok, d)
        qkv_g = _matmul(x_cls, w_qkv[:, :, g].reshape(d, 3 * ATT_OUT).astype(BF16), tm=512, tn=3 * ATT_OUT,
                        name=f"proj_qkv{g}")
        o, lse = _attention_group(qkv_g.reshape(bsz * dil, ls, 3 * ATT_OUT), g, dil)
        att_o.append(o.reshape(bsz, dil, ls, ATT_OUT).transpose(0, 2, 1, 3).reshape(ntok, ATT_OUT))
        att_l.append(lse.reshape(bsz, dil, ls, ATT_OUT).transpose(0, 2, 1, 3).reshape(ntok, ATT_OUT))

    h_s5 = _s5_mix(u, seqlen, s5_a_re, s5_a_im, s5_log_step, s5_b_re, s5_b_im, s5_c_re, s5_c_im, s5_d)
    p_s5 = _s5_project(h_s5, s5_glu_w1.astype(BF16), s5_glu_w2.astype(BF16), w_br_s5.astype(BF16), gates, b_gate)

    x1 = _merge(att_o, att_l, w_br_attn.astype(BF16), gates, b_gate, p_ssd, p_s5, w_out.astype(BF16),
                xf, ln1_g, ln1_b)
    x2 = _moe(x1, router_w, router_b, w_gu.astype(BF16), b_gu, w_dn.astype(BF16), b_dn, ln2_g, ln2_b)
    return x2.reshape(bsz, seqlen, d)


def kernel(x, w_in, b_gate, ssd_conv_w, ssd_conv_b, ssd_a_log, ssd_dt_bias, ssd_d, ssd_norm_w, s5_a_re, s5_a_im, s5_log_step, s5_b_re, s5_b_im, s5_c_re, s5_c_im, s5_d, s5_glu_w1, s5_glu_w2, w_br_ssd, w_br_attn, w_br_s5, w_out, ln1_g, ln1_b, router_w, router_b, exp_w_gate_up, exp_b_gate_up, exp_w_down, exp_b_down, ln2_g, ln2_b):
    per_layer = (w_in, b_gate, ssd_conv_w, ssd_conv_b, ssd_a_log, ssd_dt_bias, ssd_d, ssd_norm_w, s5_a_re, s5_a_im,
                 s5_log_step, s5_b_re, s5_b_im, s5_c_re, s5_c_im, s5_d, s5_glu_w1, s5_glu_w2, w_br_ssd, w_br_attn,
                 w_br_s5, w_out, ln1_g, ln1_b, router_w, router_b, exp_w_gate_up, exp_b_gate_up, exp_w_down,
                 exp_b_down, ln2_g, ln2_b)
    for layer in range(DEPTH):
        x = _layer(x, *[p[layer] for p in per_layer])
    return x
```

```python
import functools
import math

import jax
import jax.numpy as jnp
from jax import lax
from jax.experimental import pallas as pl
from jax.experimental.pallas import tpu as pltpu

F32 = jnp.float32
BF16 = jnp.bfloat16

D_MODEL = 1024
DEPTH = 4
ALPHA = (2 * DEPTH) ** 0.25
LN_EPS = 1e-5

SSD_INNER = 2 * D_MODEL
SSD_HEAD_DIM = 64
SSD_GROUPS = 8
SSD_HPG = 4
SSD_HEADS = SSD_GROUPS * SSD_HPG
SSD_STATE = 128
SSD_CONV = 5
SSD_CONV_CH = SSD_INNER + 2 * SSD_GROUPS * SSD_STATE
SSD_CHUNK = 128
SSD_SEQ_BLOCK = 1024

ATT_HEAD_DIM = 64
ATT_PATTERNS = ((128, 1), (512, 4), (2048, 16))
ATT_GROUPS = 3
ATT_HPG = 6
ATT_HEADS = ATT_GROUPS * ATT_HPG
ATT_WIDTH = ATT_HEADS * ATT_HEAD_DIM
ATT_OUT = ATT_HPG * ATT_HEAD_DIM
ATT_NEG = -1e30
ATT_RADIUS = 64
ATT_QSUB = 128

S5_CH = 16
S5_STATE = 64
S5_WIDTH = 9 * D_MODEL // 8
S5_GROUPS = S5_WIDTH // S5_CH
S5_TC = 16
S5_ROW = S5_TC * S5_CH
S5_SCAN_GROUPS = 8
S5_LANE_GROUPS = 128 // S5_CH
S5_TILES = S5_GROUPS // S5_LANE_GROUPS
S5_ROW_BLOCK = 256
S5_RELAYOUT_ROWS = 16

N_BRANCHES = 3
IN_SIZES = (N_BRANCHES * D_MODEL, SSD_INNER, SSD_CONV_CH, 2 * SSD_HEADS, 3 * ATT_WIDTH, S5_WIDTH)

MOE_EXPERTS = 32
MOE_TOP_K = 4
MOE_FF = D_MODEL
SWIGLU_LIMIT = 7.0
SWIGLU_ALPHA = 1.702
MOE_ROWS = 256

VMEM_LIMIT = 56 * 1024 * 1024


def _params(*sem):
    return pltpu.CompilerParams(dimension_semantics=sem, vmem_limit_bytes=VMEM_LIMIT)


def _mm_kernel(x_ref, w_ref, o_ref):
    o_ref[...] = jnp.dot(x_ref[...].astype(BF16), w_ref[...], preferred_element_type=F32).astype(o_ref.dtype)


def _matmul(x, w, *, tm, tn, name):
    m, k = x.shape
    n = w.shape[1]
    return pl.pallas_call(
        _mm_kernel,
        grid=(n // tn, m // tm),
        in_specs=[pl.BlockSpec((tm, k), lambda j, i: (i, 0)), pl.BlockSpec((k, tn), lambda j, i: (0, j))],
        out_specs=pl.BlockSpec((tm, tn), lambda j, i: (i, j)),
        out_shape=jax.ShapeDtypeStruct((m, n), F32),
        compiler_params=_params("parallel", "parallel"),
        name=name,
    )(x, w)


def _mm_tiles_kernel(x_ref, w_ref, o_ref):
    acc = jnp.dot(x_ref[...].astype(BF16), w_ref[...], preferred_element_type=F32)
    for t in range(o_ref.shape[0]):
        o_ref[t] = acc[:, 128 * t:128 * (t + 1)]


def _matmul_tiles(x, w, *, tm, name):
    m, k = x.shape
    n = w.shape[1]
    nt = n // 128
    return pl.pallas_call(
        _mm_tiles_kernel,
        grid=(m // tm,),
        in_specs=[pl.BlockSpec((tm, k), lambda i: (i, 0)), pl.BlockSpec((k, n), lambda i: (0, 0))],
        out_specs=pl.BlockSpec((nt, tm, 128), lambda i: (0, i, 0)),
        out_shape=jax.ShapeDtypeStruct((nt, m, 128), F32),
        compiler_params=_params("parallel"),
        name=name,
    )(x, w)


def _split3(a):
    a1 = a.astype(BF16)
    r1 = a - a1.astype(F32)
    a2 = r1.astype(BF16)
    a3 = (r1 - a2.astype(F32)).astype(BF16)
    return a1, a2, a3


def _decay_kernel(x_ref, w_ref, wt_ref, alog_r_ref, alog_c_ref, bias_r_ref, bias_c_ref,
                  acs_c_ref, wend_c_ref, acs_r_ref, delta_r_ref):
    q = SSD_CHUNK
    nh = SSD_HEADS
    xb = x_ref[...].astype(BF16)
    dt_c = jnp.dot(xb, w_ref[...], preferred_element_type=F32)
    dt_r = lax.dot_general(wt_ref[...], xb, (((1,), (1,)), ((), ())), preferred_element_type=F32)
    delta_c = jax.nn.softplus(dt_c + bias_r_ref[...])
    delta_r = jax.nn.softplus(dt_r + bias_c_ref[...])
    a_c = -jnp.exp(alog_r_ref[...]) * delta_c
    a_r = -jnp.exp(alog_c_ref[...]) * delta_r
    delta_r_ref[...] = delta_r
    row = lax.broadcasted_iota(jnp.int32, (q, q), 0)
    col = lax.broadcasted_iota(jnp.int32, (q, q), 1)
    lower = (col <= row).astype(BF16)
    upper = (col >= row).astype(BF16)
    fwd_lane = lax.broadcasted_iota(jnp.int32, (q, dt_c.shape[1]), 1) < nh
    fwd_row = lax.broadcasted_iota(jnp.int32, (2 * nh, q), 0) < nh
    for c in range(x_ref.shape[0] // q):
        rows = slice(c * q, (c + 1) * q)
        pieces = _split3(a_c[rows])
        prefix = sum(jnp.dot(lower, p, preferred_element_type=F32) for p in pieces)
        suffix = sum(jnp.dot(upper, p, preferred_element_type=F32) for p in pieces)
        acs = jnp.where(fwd_lane, prefix, suffix)
        total = jnp.where(fwd_lane[0:1], prefix[q - 1:q], suffix[0:1])
        acs_c_ref[rows, :] = acs
        wend_c_ref[rows, :] = delta_c[rows] * jnp.exp(total - acs)
        pieces = _split3(a_r[:, rows])
        prefix = sum(jnp.dot(p, upper, preferred_element_type=F32) for p in pieces)
        suffix = sum(jnp.dot(p, lower, preferred_element_type=F32) for p in pieces)
        acs_r_ref[:, rows] = jnp.where(fwd_row, prefix, suffix)


def _ssd_decay(x, w_dt, a_log, dt_bias, *, tm=1024):
    m, k = x.shape
    n = w_dt.shape[1]
    lanes = 128
    pad = lanes - n
    w_pad = jnp.pad(w_dt, ((0, 0), (0, pad)))
    alog = a_log.reshape(-1)
    bias = dt_bias.reshape(-1)
    col = pl.BlockSpec((tm, lanes), lambda i: (i, 0))
    rowf = pl.BlockSpec((n, tm), lambda i: (0, i))
    return pl.pallas_call(
        _decay_kernel,
        grid=(m // tm,),
        in_specs=[pl.BlockSpec((tm, k), lambda i: (i, 0)), pl.BlockSpec((k, lanes), lambda i: (0, 0)),
                  pl.BlockSpec((n, k), lambda i: (0, 0)),
                  pl.BlockSpec((1, lanes), lambda i: (0, 0)), pl.BlockSpec((n, 1), lambda i: (0, 0)),
                  pl.BlockSpec((1, lanes), lambda i: (0, 0)), pl.BlockSpec((n, 1), lambda i: (0, 0))],
        out_specs=[col, col, rowf, rowf],
        out_shape=[jax.ShapeDtypeStruct((m, lanes), F32)] * 2 + [jax.ShapeDtypeStruct((n, m), F32)] * 2,
        compiler_params=_params("parallel"),
        name="ssd_decay",
    )(x, w_pad, w_dt.T, jnp.pad(alog, (0, pad)).reshape(1, lanes), alog.reshape(n, 1),
      jnp.pad(bias, (0, pad)).reshape(1, lanes), bias.reshape(n, 1))


def _xbc_conv_kernel(xm_ref, xp_ref, xn_ref, w_ref, cw_ref, cb_ref, o_ref, buf, *, tiles_per_seq):
    pos = pl.program_id(1) % tiles_per_seq
    ts = xm_ref.shape[0]
    half = SSD_CONV // 2
    rows = jnp.concatenate([xp_ref[...], xm_ref[...], xn_ref[...]], axis=0).astype(BF16)
    buf[...] = jnp.dot(rows, w_ref[...], preferred_element_type=F32)

    @pl.when(pos == 0)
    def _():
        buf[0:8, :] = jnp.zeros((8, buf.shape[1]), F32)

    @pl.when(pos == tiles_per_seq - 1)
    def _():
        buf[8 + ts:16 + ts, :] = jnp.zeros((8, buf.shape[1]), F32)

    acc = cb_ref[...] + cw_ref[0:1, :] * buf[8 - half:8 - half + ts, :]
    for k in range(1, SSD_CONV):
        acc = acc + cw_ref[k:k + 1, :] * buf[8 - half + k:8 - half + k + ts, :]
    o_ref[...] = acc * jax.nn.sigmoid(acc)


def _xbc_conv(x, w, conv_w, conv_b, seqlen, *, tm=512, tn=1024):
    m, k = x.shape
    n = w.shape[1]
    nrow8 = m // 8
    return pl.pallas_call(
        functools.partial(_xbc_conv_kernel, tiles_per_seq=seqlen // tm),
        grid=(n // tn, m // tm),
        in_specs=[
            pl.BlockSpec((tm, k), lambda j, i: (i, 0)),
            pl.BlockSpec((8, k), lambda j, i: (jnp.maximum(i * (tm // 8) - 1, 0), 0)),
            pl.BlockSpec((8, k), lambda j, i: (jnp.minimum((i + 1) * (tm // 8), nrow8 - 1), 0)),
            pl.BlockSpec((k, tn), lambda j, i: (0, j)),
            pl.BlockSpec((SSD_CONV, tn), lambda j, i: (0, j)),
            pl.BlockSpec((1, tn), lambda j, i: (0, j)),
        ],
        out_specs=pl.BlockSpec((tm, tn), lambda j, i: (i, j)),
        out_shape=jax.ShapeDtypeStruct((m, n), F32),
        scratch_shapes=[pltpu.VMEM((tm + 16, tn), F32)],
        compiler_params=_params("parallel", "parallel"),
        name="proj_xbc_conv",
    )(x, x, x, w, conv_w, conv_b.reshape(1, n))


def _per_head(cols):
    rows = cols[0].shape[0]
    first = lax.broadcasted_iota(jnp.int32, (rows, 2 * SSD_HEAD_DIM), 1) < SSD_HEAD_DIM
    return jnp.concatenate([jnp.where(first, cols[0], cols[1]), jnp.where(first, cols[2], cols[3])], axis=1)


def _group_lanes(ref, rows, g):
    lanes = ref.shape[1]
    return pltpu.roll(ref[rows, :], (lanes - SSD_HPG * g) % lanes, 1)


def _group_rows(ref, first, cols, g):
    tile = 8
    per_tile = tile // SSD_HPG
    base = pl.multiple_of(first + (g // per_tile) * tile, tile)
    return pltpu.roll(ref[pl.ds(base, tile), cols], (tile - SSD_HPG * (g % per_tile)) % tile, 0)


def _head_selector(lanes, first):
    width = SSD_HPG * SSD_HEAD_DIM
    src = lax.broadcasted_iota(jnp.int32, (lanes, width), 0)
    dst = lax.broadcasted_iota(jnp.int32, (lanes, width), 1) // SSD_HEAD_DIM
    return (src == dst + first).astype(BF16)


def _expand_heads(cols, selector):
    return sum(jnp.dot(p, selector, preferred_element_type=F32) for p in _split3(cols))


def _ssd_back_kernel(xs_ref, bm_ref, acs_ref, wend_ref, out_ref, carried):
    q = SSD_CHUNK
    nchunk = xs_ref.shape[0] // q
    g = pl.program_id(1)
    bwd = SSD_HEADS
    selector = _head_selector(wend_ref.shape[1], bwd)

    @pl.when(pl.program_id(2) == 0)
    def _():
        carried[...] = jnp.zeros_like(carried)

    def chunk(ci, carry):
        c = nchunk - 1 - ci
        rows = pl.ds(pl.multiple_of(c * q, q), q)
        first_row = pl.ds(pl.multiple_of(c * q, q), 8)
        wend = _group_lanes(wend_ref, rows, g)
        total = _group_lanes(acs_ref, first_row, g)[0:1]
        xw = xs_ref[rows, :] * _expand_heads(wend, selector)
        states = lax.dot_general(bm_ref[rows, :].astype(BF16), xw.astype(BF16), (((0,), (0,)), ((), ())),
                                 preferred_element_type=F32)
        prev = carried[...]
        out_ref[c] = prev.astype(out_ref.dtype)
        carried[...] = jnp.exp(_per_head([total[:, bwd + r:bwd + r + 1] for r in range(SSD_HPG)])) * prev + states
        return carry

    lax.fori_loop(0, nchunk, chunk, 0, unroll=4)


def _ssd_main_kernel(xs_ref, bm_ref, cm_ref, acs_c_ref, wend_c_ref, acs_r_ref, delta_r_ref, back_ref,
                     dskip_ref, y_ref, carried):
    q = SSD_CHUNK
    nchunk = xs_ref.shape[0] // q
    g = pl.program_id(1)
    bwd = SSD_HEADS
    hd = SSD_HEAD_DIM

    @pl.when(pl.program_id(2) == 0)
    def _():
        carried[...] = jnp.zeros_like(carried)

    row = lax.broadcasted_iota(jnp.int32, (q, q), 0)
    col = lax.broadcasted_iota(jnp.int32, (q, q), 1)
    lower = col <= row
    upper = col >= row
    d_wide = _per_head([dskip_ref[:, r:r + 1] for r in range(SSD_HPG)])
    selector = _head_selector(wend_c_ref.shape[1], 0)
    first = lax.broadcasted_iota(jnp.int32, (q, 2 * hd), 1) < hd

    def per_head_lanes(blocks):
        return jnp.concatenate([jnp.where(first, blocks[0], blocks[1]), jnp.where(first, blocks[2], blocks[3])], axis=1)

    def chunk(c, carry):
        r0 = pl.multiple_of(c * q, q)
        rows = pl.ds(r0, q)
        xs = xs_ref[rows, :]
        bmat = bm_ref[rows, :].astype(BF16)
        cmat = cm_ref[rows, :].astype(BF16)
        acs = _group_lanes(acs_c_ref, rows, g)
        wend = _group_lanes(wend_c_ref, rows, g)
        acs_f = [jnp.broadcast_to(acs[:, r:r + 1], (q, q)) for r in range(SSD_HPG)]
        acs_b = [jnp.broadcast_to(acs[:, bwd + r:bwd + r + 1], (q, q)) for r in range(SSD_HPG)]
        acs_rf = _group_rows(acs_r_ref, 0, rows, g)
        acs_rb = _group_rows(acs_r_ref, bwd, rows, g)
        del_rf = _group_rows(delta_r_ref, 0, rows, g)
        del_rb = _group_rows(delta_r_ref, bwd, rows, g)
        cb = lax.dot_general(cmat, bmat, (((1,), (1,)), ((), ())), preferred_element_type=F32)
        prev = carried[...]
        states_in = jnp.concatenate([prev.astype(BF16), back_ref[c]], axis=1)
        y_off = jnp.dot(cmat, states_in, preferred_element_type=F32)
        e_f = jnp.exp(per_head_lanes(acs_f))
        e_b = jnp.exp(per_head_lanes(acs_b))
        rest = e_f * y_off[:, 0:SSD_HPG * hd] + e_b * y_off[:, SSD_HPG * hd:] + d_wide * xs
        for r in range(SSD_HPG):
            seg_f = acs_f[r] - acs_rf[r:r + 1, :]
            seg_b = acs_b[r] - acs_rb[r:r + 1, :]
            w = (jnp.exp(jnp.where(lower, seg_f, ATT_NEG)) * del_rf[r:r + 1, :]
                 + jnp.exp(jnp.where(upper, seg_b, ATT_NEG)) * del_rb[r:r + 1, :])
            y_diag = jnp.dot((cb * w).astype(BF16), xs[:, r * hd:(r + 1) * hd].astype(BF16),
                             preferred_element_type=F32)
            y_ref[rows, r * hd:(r + 1) * hd] = y_diag + rest[:, r * hd:(r + 1) * hd]
        xw = xs * _expand_heads(wend, selector)
        states = lax.dot_general(bmat, xw.astype(BF16), (((0,), (0,)), ((), ())), preferred_element_type=F32)
        total = acs[q - 1:q]
        carried[...] = jnp.exp(_per_head([total[:, r:r + 1] for r in range(SSD_HPG)])) * prev + states
        return carry

    lax.fori_loop(0, nchunk, chunk, 0, unroll=2)


def _ssd_scan(xbc_act, acs_c, wend_c, acs_r, delta_r, d_skip):
    bsz, seqlen, _ = xbc_act.shape
    sb = min(SSD_SEQ_BLOCK, seqlen)
    nblk = seqlen // sb
    cpb = sb // SSD_CHUNK
    n_x = SSD_HPG * SSD_HEAD_DIM
    b_off = SSD_INNER // SSD_STATE
    c_off = b_off + SSD_GROUPS
    lanes = acs_c.shape[1]
    nrow = acs_r.shape[0]

    def specs(blk):
        xs = pl.BlockSpec((None, sb, n_x), lambda b_, g, i: (b_, blk(i), g))
        bm = pl.BlockSpec((None, sb, SSD_STATE), lambda b_, g, i: (b_, blk(i), b_off + g))
        cm = pl.BlockSpec((None, sb, SSD_STATE), lambda b_, g, i: (b_, blk(i), c_off + g))
        colf = pl.BlockSpec((sb, lanes), lambda b_, g, i: (b_ * nblk + blk(i), 0))
        rowf = pl.BlockSpec((nrow, sb), lambda b_, g, i: (0, b_ * nblk + blk(i)))
        back = pl.BlockSpec((None, None, cpb, SSD_STATE, n_x), lambda b_, g, i: (b_, g, blk(i), 0, 0))
        return xs, bm, cm, colf, rowf, back

    xs, bm, cm, colf, rowf, back = specs(lambda i: nblk - 1 - i)
    state = pltpu.VMEM((SSD_STATE, n_x), F32)
    back_states = pl.pallas_call(
        _ssd_back_kernel,
        grid=(bsz, SSD_GROUPS, nblk),
        in_specs=[xs, bm, colf, colf],
        out_specs=back,
        out_shape=jax.ShapeDtypeStruct((bsz, SSD_GROUPS, seqlen // SSD_CHUNK, SSD_STATE, n_x), BF16),
        scratch_shapes=[state],
        compiler_params=_params("parallel", "parallel", "arbitrary"),
        name="ssd_back",
    )(xbc_act, xbc_act, acs_c, wend_c)

    xs, bm, cm, colf, rowf, back = specs(lambda i: i)
    return pl.pallas_call(
        _ssd_main_kernel,
        grid=(bsz, SSD_GROUPS, nblk),
        in_specs=[xs, bm, cm, colf, colf, rowf, rowf, back,
                  pl.BlockSpec((None, 1, SSD_HPG), lambda b_, g, i: (g, 0, 0))],
        out_specs=pl.BlockSpec((None, sb, n_x), lambda b_, g, i: (b_, i, g)),
        out_shape=jax.ShapeDtypeStruct((bsz, seqlen, SSD_INNER), F32),
        scratch_shapes=[state],
        compiler_params=_params("parallel", "parallel", "arbitrary"),
        name="ssd_main",
    )(xbc_act, xbc_act, xbc_act, acs_c, wend_c, acs_r, delta_r, back_states,
      d_skip.reshape(SSD_GROUPS, 1, SSD_HPG))


def _attn_kernel(q_ref, ko_ref, kp_ref, kn_ref, vo_ref, vp_ref, vn_ref, o_ref, lse_ref, kcat, vcat,
                 *, dil, group, ls):
    tq = q_ref.shape[0]
    rad = ATT_RADIUS
    qs = ATT_QSUB
    kw = qs + 2 * rad
    i = pl.program_id(1)
    kcat[0:rad, :] = kp_ref[...].astype(BF16)
    kcat[rad:rad + tq, :] = ko_ref[...].astype(BF16)
    kcat[rad + tq:2 * rad + tq, :] = kn_ref[...].astype(BF16)
    vcat[0:rad, :] = vp_ref[...].astype(BF16)
    vcat[rad:rad + tq, :] = vo_ref[...].astype(BF16)
    vcat[rad + tq:2 * rad + tq, :] = vn_ref[...].astype(BF16)

    row = lax.broadcasted_iota(jnp.int32, (qs, kw), 0)
    col = lax.broadcasted_iota(jnp.int32, (qs, kw), 1)
    dist = jnp.abs(row - col + rad)
    in_band = dist <= rad
    dist_f = (dil * dist).astype(F32)
    lane = lax.broadcasted_iota(jnp.int32, (qs, 2 * ATT_HEAD_DIM), 1)
    first_half = lane < ATT_HEAD_DIM
    scale = ATT_HEAD_DIM ** -0.5

    for m in range(tq // qs):
        kpos = i * tq + (m * qs - rad) + col
        valid = in_band & (kpos >= 0) & (kpos < ls)
        for pair in range(ATT_HPG // 2):
            lanes = slice(pair * 2 * ATT_HEAD_DIM, (pair + 1) * 2 * ATT_HEAD_DIM)
            q2 = q_ref[m * qs:(m + 1) * qs, lanes]
            k2 = kcat[m * qs:m * qs + kw, lanes]
            v2 = vcat[m * qs:m * qs + kw, lanes]
            outs, lses = [], []
            for half in range(2):
                head = 2 * pair + half
                slope = 2.0 ** (-8.0 * (head * ATT_GROUPS + group + 1) / ATT_HEADS)
                qm = jnp.where(first_half if half == 0 else ~first_half, q2, 0.0).astype(BF16)
                s = lax.dot_general(qm, k2, (((1,), (1,)), ((), ())), preferred_element_type=F32)
                s = s * scale + (-slope) * dist_f
                s = jnp.where(valid, s, ATT_NEG)
                mx = jnp.max(s, axis=-1, keepdims=True)
                p = jnp.exp(s - mx)
                zsum = jnp.sum(p, axis=-1, keepdims=True)
                pv = jnp.dot(p.astype(BF16), v2, preferred_element_type=F32)
                outs.append(pv / zsum)
                lses.append(mx + jnp.log(zsum))
            o_ref[m * qs:(m + 1) * qs, lanes] = jnp.where(first_half, outs[0], outs[1])
            lse_ref[m * qs:(m + 1) * qs, lanes] = jnp.where(first_half, lses[0], lses[1])


def _attention_group(qkv, group, dil):
    nseq, ls, _ = qkv.shape
    tq = min(512, ls)
    nblk = ls // tq
    halo_per_blk = tq // ATT_RADIUS
    n_halo = ls // ATT_RADIUS

    def own(off):
        return pl.BlockSpec((None, tq, ATT_OUT), lambda s, i: (s, i, off))

    def prev(off):
        return pl.BlockSpec((None, ATT_RADIUS, ATT_OUT), lambda s, i: (s, jnp.maximum(i * halo_per_blk - 1, 0), off))

    def nxt(off):
        return pl.BlockSpec((None, ATT_RADIUS, ATT_OUT),
                            lambda s, i: (s, jnp.minimum((i + 1) * halo_per_blk, n_halo - 1), off))

    out_spec = pl.BlockSpec((None, tq, ATT_OUT), lambda s, i: (s, i, 0))
    return pl.pallas_call(
        functools.partial(_attn_kernel, dil=dil, group=group, ls=ls),
        grid=(nseq, nblk),
        in_specs=[own(0), own(1), prev(1), nxt(1), own(2), prev(2), nxt(2)],
        out_specs=[out_spec, out_spec],
        out_shape=[jax.ShapeDtypeStruct((nseq, ls, ATT_OUT), F32)] * 2,
        scratch_shapes=[pltpu.VMEM((tq + 2 * ATT_RADIUS, ATT_OUT), BF16)] * 2,
        compiler_params=_params("parallel", "parallel"),
        name=f"attn_g{group}",
    )(qkv, qkv, qkv, qkv, qkv, qkv, qkv)


def _s5_tables(a_re, a_im, log_step, b_re, b_im, c_re, c_im):
    hp = lax.Precision.HIGHEST
    tc = S5_TC
    ng, width = S5_GROUPS, S5_ROW
    step = jnp.exp(log_step)[..., None]
    ar, ai = a_re, a_im
    mag = jnp.exp(step * ar)
    abr, abi = mag * jnp.cos(step * ai), mag * jnp.sin(step * ai)
    den = ar * ar + ai * ai
    fr = ((abr - 1.0) * ar + abi * ai) / den
    fi = (abi * ar - (abr - 1.0) * ai) / den
    b_re_t, b_im_t = b_re.transpose(0, 2, 1), b_im.transpose(0, 2, 1)
    bbr = fr[:, :, None, :] * b_re_t[None] - fi[:, :, None, :] * b_im_t[None]
    bbi = fr[:, :, None, :] * b_im_t[None] + fi[:, :, None, :] * b_re_t[None]
    ks = jnp.arange(tc + 1, dtype=F32)
    arg_r, arg_i = step * ar, step * ai
    pk_r = jnp.exp(arg_r[:, :, None, :] * ks[:, None]) * jnp.cos(arg_i[:, :, None, :] * ks[:, None])
    pk_i = jnp.exp(arg_r[:, :, None, :] * ks[:, None]) * jnp.sin(arg_i[:, :, None, :] * ks[:, None])
    pn_r = jnp.exp(arg_r[..., None] * ks) * jnp.cos(arg_i[..., None] * ks)
    pn_i = jnp.exp(arg_r[..., None] * ks) * jnp.sin(arg_i[..., None] * ks)
    c_re_t, c_im_t = c_re.transpose(0, 1, 3, 2), c_im.transpose(0, 1, 3, 2)
    cap_r = (c_re_t[:, :, :, None, :] * pn_r[..., None] - c_im_t[:, :, :, None, :] * pn_i[..., None])
    cap_i = (c_re_t[:, :, :, None, :] * pn_i[..., None] + c_im_t[:, :, :, None, :] * pn_r[..., None])
    cap_r = cap_r.reshape(2, ng, S5_STATE, (tc + 1) * S5_CH)
    cap_i = cap_i.reshape(2, ng, S5_STATE, (tc + 1) * S5_CH)
    taps = (jnp.einsum('dgin,dgnm->dgim', bbr, cap_r, precision=hp)
            - jnp.einsum('dgin,dgnm->dgim', bbi, cap_i, precision=hp))[..., :width]
    fwd_src = jnp.pad(taps[0], ((0, 0), (0, 0), (width, 0)))
    bwd_rev = taps[1].reshape(ng, S5_CH, tc, S5_CH)[:, :, ::-1, :].reshape(ng, S5_CH, width)
    bwd_src = jnp.pad(bwd_rev, ((0, 0), (0, 0), (0, width)))
    toep = jnp.stack([fwd_src[:, :, width - S5_CH * s:2 * width - S5_CH * s]
                      + bwd_src[:, :, S5_CH * (tc - 1 - s):S5_CH * (tc - 1 - s) + width] for s in range(tc)], axis=1)
    toep = toep.reshape(ng, width, width)

    def end_state(d, pw_r, pw_i):
        re = pw_r[:, :, None, :] * bbr[d][:, None] - pw_i[:, :, None, :] * bbi[d][:, None]
        im = pw_r[:, :, None, :] * bbi[d][:, None] + pw_i[:, :, None, :] * bbr[d][:, None]
        return [re.reshape(ng, width, S5_STATE), im.reshape(ng, width, S5_STATE)]

    b_end = jnp.concatenate(end_state(0, pk_r[0][:, tc - 1::-1], pk_i[0][:, tc - 1::-1])
                            + end_state(1, pk_r[1][:, :tc], pk_i[1][:, :tc]), axis=-1)

    def lag_reversed(a):
        return a.reshape(ng, S5_STATE, tc, S5_CH)[:, :, ::-1, :].reshape(ng, S5_STATE, width)

    c_out = jnp.concatenate([cap_r[0][..., S5_CH:], -cap_i[0][..., S5_CH:],
                             lag_reversed(cap_r[1][..., S5_CH:]), -lag_reversed(cap_i[1][..., S5_CH:])], axis=1)
    ptr, pti = pk_r[:, :, tc], pk_i[:, :, tc]
    p_mul = jnp.concatenate([ptr[0], ptr[0], ptr[1], ptr[1]], axis=-1)
    q_mul = jnp.concatenate([-pti[0], pti[0], -pti[1], pti[1]], axis=-1)
    return toep, b_end, c_out, p_mul, q_mul


def _s5_rows_to_groups(u_ref, x_scr):
    rr = S5_RELAYOUT_ROWS
    per = S5_LANE_GROUPS
    granule = lax.broadcasted_iota(jnp.int32, (rr, 128), 1) // S5_CH

    def body(rb, carry):
        base = rb * (rr * S5_TC)
        pieces = [u_ref[pl.ds(base + tau, rr, stride=S5_TC), :] for tau in range(S5_TC)]
        rows = pl.ds(pl.multiple_of(rb * rr, rr), rr)
        for gl in range(per):
            for h in range(S5_TC // per):
                acc = None
                for t in range(per):
                    p = pieces[per * h + t]
                    if t != gl:
                        p = pltpu.roll(p, ((t - gl) * S5_CH) % 128, 1)
                    acc = p if acc is None else jnp.where(granule == t, p, acc)
                x_scr[gl, rows, 128 * h:128 * (h + 1)] = acc
        return carry

    lax.fori_loop(0, x_scr.shape[1] // rr, body, 0)


def _s5_in_kernel(u_ref, toep_ref, bend_ref, y_ref, sf_ref, sb_ref, x_scr):
    half = 2 * S5_STATE
    _s5_rows_to_groups(u_ref, x_scr)
    for gl in range(S5_LANE_GROUPS):
        xb = x_scr[gl].astype(BF16)
        y_ref[gl] = jnp.dot(xb, toep_ref[gl], preferred_element_type=F32)
        s = jnp.dot(xb, bend_ref[gl], preferred_element_type=F32)
        sf_ref[gl] = s[:, 0:half]
        sb_ref[gl] = s[:, half:2 * half]


def _s5_scan_kernel(sf_ref, sb_ref, p_ref, q_ref, hf_ref, hb_ref, sf_sw_ref, sb_sw_ref, *, rows_per_seq):
    n_chain = p_ref.shape[0]
    half = 2 * S5_STATE
    pf, qf = p_ref[:, 0:half], q_ref[:, 0:half]
    pb, qb = p_ref[:, half:2 * half], q_ref[:, half:2 * half]
    qf_sw = pltpu.roll(qf, S5_STATE, 1)
    qb_sw = pltpu.roll(qb, S5_STATE, 1)
    zero = jnp.zeros((n_chain, half), F32)
    piece = 256

    def swap(k, carry):
        rows = pl.ds(pl.multiple_of(k * piece, piece), piece)
        sf_sw_ref[rows, :] = pltpu.roll(sf_ref[rows, :], S5_STATE, 1)
        sb_sw_ref[rows, :] = pltpu.roll(sb_ref[rows, :], S5_STATE, 1)
        return carry

    lax.fori_loop(0, sf_ref.shape[0] // piece, swap, 0)

    def step(c, carry):
        hf, hf_sw, hb, hb_sw = carry
        rows_f = pl.ds(c, n_chain, stride=rows_per_seq)
        rows_b = pl.ds(rows_per_seq - 1 - c, n_chain, stride=rows_per_seq)
        hf_ref[rows_f, :] = hf
        hb_ref[rows_b, :] = hb
        sf = sf_ref[rows_f, :]
        sb = sb_ref[rows_b, :]
        sf_sw = sf_sw_ref[rows_f, :]
        sb_sw = sb_sw_ref[rows_b, :]
        hf_new = pf * hf + qf * hf_sw + sf
        hf_sw_new = pf * hf_sw + qf_sw * hf + sf_sw
        hb_new = pb * hb + qb * hb_sw + sb
        hb_sw_new = pb * hb_sw + qb_sw * hb + sb_sw
        return hf_new, hf_sw_new, hb_new, hb_sw_new

    lax.fori_loop(0, rows_per_seq, step, (zero, zero, zero, zero))


def _s5_out_kernel(hf_ref, hb_ref, cout_ref, y_ref, u_ref, d_ref, o_ref, y_scr):
    half = 2 * S5_STATE
    per = S5_LANE_GROUPS
    for gl in range(per):
        y = y_ref[gl] + jnp.dot(hf_ref[gl].astype(BF16), cout_ref[gl, 0:half, :], preferred_element_type=F32)
        y_scr[gl] = y + jnp.dot(hb_ref[gl].astype(BF16), cout_ref[gl, half:2 * half, :], preferred_element_type=F32)
    rr = S5_RELAYOUT_ROWS
    granule = lax.broadcasted_iota(jnp.int32, (rr, 128), 1) // S5_CH
    d_skip = d_ref[...]

    def body(rb, carry):
        rows = pl.ds(pl.multiple_of(rb * rr, rr), rr)
        halves = [[y_scr[gl, rows, 128 * h:128 * (h + 1)] for h in range(S5_TC // per)] for gl in range(per)]
        base = rb * (rr * S5_TC)
        for tau in range(S5_TC):
            h, t = divmod(tau, per)
            acc = None
            for gl in range(per):
                p = halves[gl][h]
                if gl != t:
                    p = pltpu.roll(p, ((gl - t) * S5_CH) % 128, 1)
                acc = p if acc is None else jnp.where(granule == gl, p, acc)
            tok = pl.ds(base + tau, rr, stride=S5_TC)
            o_ref[tok, :] = jax.nn.gelu(acc + u_ref[tok, :] * d_skip)
        return carry

    lax.fori_loop(0, y_scr.shape[1] // rr, body, 0)


def _s5_mix(u_tiles, seqlen, a_re, a_im, log_step, b_re, b_im, c_re, c_im, d_skip):
    ntile, ntok, lanes = u_tiles.shape
    bsz = ntok // seqlen
    rows_per_seq = seqlen // S5_TC
    rows = bsz * rows_per_seq
    rblk = min(S5_ROW_BLOCK, rows)
    per = S5_LANE_GROUPS
    half = 2 * S5_STATE
    toep, b_end, c_out, p_mul, q_mul = _s5_tables(a_re, a_im, log_step, b_re, b_im, c_re, c_im)
    toep, b_end, c_out = toep.astype(BF16), b_end.astype(BF16), c_out.astype(BF16)
    utile = pl.BlockSpec((None, rblk * S5_TC, lanes), lambda j, i: (j, i, 0))
    ytile = pl.BlockSpec((per, rblk, S5_ROW), lambda j, i: (j, i, 0))
    stile = pl.BlockSpec((per, rblk, half), lambda j, i: (j, i, 0))
    wtile = pl.BlockSpec((per, S5_ROW, S5_ROW), lambda j, i: (j, 0, 0))
    state_shape = jax.ShapeDtypeStruct((S5_GROUPS, rows, half), F32)
    grid = (ntile, rows // rblk)
    y_in, s_f, s_b = pl.pallas_call(
        _s5_in_kernel,
        grid=grid,
        in_specs=[utile, wtile, wtile],
        out_specs=[ytile, stile, stile],
        out_shape=[jax.ShapeDtypeStruct((S5_GROUPS, rows, S5_ROW), F32), state_shape, state_shape],
        scratch_shapes=[pltpu.VMEM((per, rblk, S5_ROW), F32)],
        compiler_params=_params("parallel", "parallel"),
        name="s5_in",
    )(u_tiles, toep, b_end)

    gs = S5_SCAN_GROUPS
    n_chain = gs * bsz
    chain_p = jnp.repeat(p_mul, bsz, axis=0)
    chain_q = jnp.repeat(q_mul, bsz, axis=0)
    flat = jax.ShapeDtypeStruct((S5_GROUPS * rows, half), F32)
    srows = pl.BlockSpec((gs * rows, half), lambda t: (t, 0))
    chain = pl.BlockSpec((n_chain, S5_ROW), lambda t: (t, 0))
    h_f, h_b = pl.pallas_call(
        functools.partial(_s5_scan_kernel, rows_per_seq=rows_per_seq),
        grid=(S5_GROUPS // gs,),
        in_specs=[srows, srows, chain, chain],
        out_specs=[srows, srows],
        out_shape=[flat, flat],
        scratch_shapes=[pltpu.VMEM((gs * rows, half), F32)] * 2,
        compiler_params=_params("parallel"),
        name="s5_scan",
    )(s_f.reshape(S5_GROUPS * rows, half), s_b.reshape(S5_GROUPS * rows, half), chain_p, chain_q)

    return pl.pallas_call(
        _s5_out_kernel,
        grid=grid,
        in_specs=[stile, stile, wtile, ytile, utile, pl.BlockSpec((None, 1, lanes), lambda j, i: (j, 0, 0))],
        out_specs=utile,
        out_shape=jax.ShapeDtypeStruct((ntile, ntok, lanes), F32),
        scratch_shapes=[pltpu.VMEM((per, rblk, S5_ROW), F32)],
        compiler_params=_params("parallel", "parallel"),
        name="s5_out",
    )(h_f.reshape(S5_GROUPS, rows, half), h_b.reshape(S5_GROUPS, rows, half), c_out, y_in, u_tiles,
      d_skip.reshape(ntile, 1, lanes))


def _ssd_proj_kernel(y_ref, z_ref, nw_ref, w_ref, gate_ref, bg_ref, o_ref):
    z = z_ref[...]
    y = y_ref[...] * (z * jax.nn.sigmoid(z))
    y = y * lax.rsqrt(jnp.mean(y * y, axis=-1, keepdims=True) + LN_EPS) * nw_ref[...]
    p = jnp.dot(y.astype(BF16), w_ref[...], preferred_element_type=F32)
    o_ref[...] = jax.nn.sigmoid(gate_ref[...] + bg_ref[...]) * p


def _ssd_project(y, z, norm_w, w_br, gates, b_gate, *, tm=512):
    m = y.shape[0]
    wide = pl.BlockSpec((tm, SSD_INNER), lambda i: (i, 0))
    return pl.pallas_call(
        _ssd_proj_kernel,
        grid=(m // tm,),
        in_specs=[wide, wide, pl.BlockSpec((1, SSD_INNER), lambda i: (0, 0)),
                  pl.BlockSpec((SSD_INNER, D_MODEL), lambda i: (0, 0)),
                  pl.BlockSpec((tm, D_MODEL), lambda i: (i, 0)),
                  pl.BlockSpec((1, D_MODEL), lambda i: (0, 0))],
        out_specs=pl.BlockSpec((tm, D_MODEL), lambda i: (i, 0)),
        out_shape=jax.ShapeDtypeStruct((m, D_MODEL), F32),
        compiler_params=_params("parallel"),
        name="ssd_project",
    )(y, z, norm_w.reshape(1, -1), w_br, gates, b_gate[0:1])


def _s5_proj_kernel(h_ref, w1_ref, w2_ref, w_ref, gate_ref, bg_ref, o_ref):
    hb = jnp.concatenate([h_ref[t] for t in range(h_ref.shape[0])], axis=1).astype(BF16)
    a = jnp.dot(hb, w1_ref[...], preferred_element_type=F32)
    b = jnp.dot(hb, w2_ref[...], preferred_element_type=F32)
    p = jnp.dot((a * jax.nn.sigmoid(b)).astype(BF16), w_ref[...], preferred_element_type=F32)
    o_ref[...] = jax.nn.sigmoid(gate_ref[...] + bg_ref[...]) * p


def _s5_project(h_tiles, w1, w2, w_br, gates, b_gate, *, tm=512):
    ntile, m, lanes = h_tiles.shape
    sq = pl.BlockSpec((S5_WIDTH, S5_WIDTH), lambda i: (0, 0))
    return pl.pallas_call(
        _s5_proj_kernel,
        grid=(m // tm,),
        in_specs=[pl.BlockSpec((ntile, tm, lanes), lambda i: (0, i, 0)), sq, sq,
                  pl.BlockSpec((S5_WIDTH, D_MODEL), lambda i: (0, 0)),
                  pl.BlockSpec((tm, D_MODEL), lambda i: (i, 2)),
                  pl.BlockSpec((1, D_MODEL), lambda i: (0, 0))],
        out_specs=pl.BlockSpec((tm, D_MODEL), lambda i: (i, 0)),
        out_shape=jax.ShapeDtypeStruct((m, D_MODEL), F32),
        compiler_params=_params("parallel"),
        name="s5_project",
    )(h_tiles, w1, w2, w_br, gates, b_gate[2:3])


def _layer_norm(v, g, b):
    mu = jnp.mean(v, axis=-1, keepdims=True)
    c = v - mu
    var = jnp.mean(c * c, axis=-1, keepdims=True)
    return c * lax.rsqrt(var + LN_EPS) * g + b


def _merge_kernel(o0_ref, o1_ref, o2_ref, l0_ref, l1_ref, l2_ref, w_att_ref, gate_ref, bg_ref, pssd_ref, ps5_ref,
                  w_out_ref, x_ref, g_ref, b_ref, o_ref):
    l0, l1, l2 = l0_ref[...], l1_ref[...], l2_ref[...]
    mx = jnp.maximum(jnp.maximum(l0, l1), l2)
    e0, e1, e2 = jnp.exp(l0 - mx), jnp.exp(l1 - mx), jnp.exp(l2 - mx)
    den = e0 + e1 + e2
    att = (e0 / den) * o0_ref[...] + (e1 / den) * o1_ref[...] + (e2 / den) * o2_ref[...]
    p_att = jnp.dot(att.astype(BF16), w_att_ref[...], preferred_element_type=F32)
    merged = pssd_ref[...] + jax.nn.sigmoid(gate_ref[...] + bg_ref[...]) * p_att + ps5_ref[...]
    mix = jnp.dot(merged.astype(BF16), w_out_ref[...], preferred_element_type=F32)
    o_ref[...] = _layer_norm(ALPHA * x_ref[...] + mix, g_ref[...], b_ref[...])


def _merge(att_o, att_l, w_att, gates, b_gate, p_ssd, p_s5, w_out, x, ln_g, ln_b, *, tm=512):
    m = x.shape[0]
    att = pl.BlockSpec((tm, ATT_OUT), lambda i: (i, 0))
    row = pl.BlockSpec((tm, D_MODEL), lambda i: (i, 0))
    vec = pl.BlockSpec((1, D_MODEL), lambda i: (0, 0))
    return pl.pallas_call(
        _merge_kernel,
        grid=(m // tm,),
        in_specs=[att] * 6 + [pl.BlockSpec((ATT_OUT, D_MODEL), lambda i: (0, 0)),
                              pl.BlockSpec((tm, D_MODEL), lambda i: (i, 1)), vec, row, row,
                              pl.BlockSpec((D_MODEL, D_MODEL), lambda i: (0, 0)), row, vec, vec],
        out_specs=row,
        out_shape=jax.ShapeDtypeStruct((m, D_MODEL), F32),
        compiler_params=_params("parallel"),
        name="merge_ln1",
    )(*att_o, *att_l, w_att, gates, b_gate[1:2], p_ssd, p_s5, w_out, x, ln_g.reshape(1, -1), ln_b.reshape(1, -1))


def _router_kernel(x_ref, wt_ref, b_ref, idx_ref, gate_ref, rank_ref, count_ref, before_ref):
    tm = x_ref.shape[0]

    @pl.when(pl.program_id(0) == 0)
    def _():
        count_ref[...] = jnp.zeros_like(count_ref)
        earlier = lax.broadcasted_iota(jnp.int32, (tm, tm), 0) < lax.broadcasted_iota(jnp.int32, (tm, tm), 1)
        before_ref[...] = earlier.astype(before_ref.dtype)

    x = x_ref[...]
    wt = wt_ref[...]
    xh = x.astype(BF16)
    xl = (x - xh.astype(F32)).astype(BF16)
    wh = wt.astype(BF16)
    wl = (wt - wh.astype(F32)).astype(BF16)
    nt = (((1,), (1,)), ((), ()))
    logits = (lax.dot_general(wh, xh, nt, preferred_element_type=F32)
              + lax.dot_general(wl, xh, nt, preferred_element_type=F32)
              + lax.dot_general(wh, xl, nt, preferred_element_type=F32)) + b_ref[...]
    eid = lax.broadcasted_iota(jnp.int32, logits.shape, 0)
    vals, idxs = [], []
    for _ in range(MOE_TOP_K):
        mx = jnp.max(logits, axis=0, keepdims=True)
        sel = jnp.min(jnp.where(logits == mx, eid, MOE_EXPERTS), axis=0, keepdims=True)
        vals.append(mx)
        idxs.append(sel)
        logits = jnp.where(eid == sel, -jnp.inf, logits)
    es = [jnp.exp(v - vals[0]) for v in vals]
    den = es[0] + es[1] + es[2] + es[3]
    for k in range(MOE_TOP_K):
        idx_ref[k:k + 1, :] = idxs[k]
        gate_ref[k:k + 1, :] = es[k] / den
    chosen = [eid == idxs[k] for k in range(MOE_TOP_K)]
    stacked = jnp.concatenate([c.astype(before_ref.dtype) for c in chosen], axis=0)
    earlier = jnp.dot(stacked, before_ref[...], preferred_element_type=F32)
    seen = count_ref[...]
    for k in range(MOE_TOP_K):
        pos = earlier[k * MOE_EXPERTS:(k + 1) * MOE_EXPERTS] + seen
        rank_ref[k:k + 1, :] = jnp.sum(jnp.where(chosen[k], pos, 0.0), axis=0, keepdims=True).astype(jnp.int32)
        seen = seen + jnp.sum(chosen[k].astype(F32), axis=1, keepdims=True)
    count_ref[...] = seen


def _router(x, router_w, router_b, *, tm=1024):
    m = x.shape[0]
    out = pl.BlockSpec((MOE_TOP_K, tm), lambda i: (0, i))
    return pl.pallas_call(
        _router_kernel,
        grid=(m // tm,),
        in_specs=[pl.BlockSpec((tm, D_MODEL), lambda i: (i, 0)),
                  pl.BlockSpec((MOE_EXPERTS, D_MODEL), lambda i: (0, 0)),
                  pl.BlockSpec((MOE_EXPERTS, 1), lambda i: (0, 0))],
        out_specs=[out, out, out, pl.BlockSpec((MOE_EXPERTS, 1), lambda i: (0, 0))],
        out_shape=[jax.ShapeDtypeStruct((MOE_TOP_K, m), jnp.int32), jax.ShapeDtypeStruct((MOE_TOP_K, m), F32),
                   jax.ShapeDtypeStruct((MOE_TOP_K, m), jnp.int32), jax.ShapeDtypeStruct((MOE_EXPERTS, 1), F32)],
        scratch_shapes=[pltpu.VMEM((tm, tm), BF16)],
        compiler_params=_params("arbitrary"),
        name="router",
    )(x, router_w.T, router_b.reshape(-1, 1))


def _expert_kernel(be_ref, used_ref, x_ref, wgu_ref, bgu_ref, wdn_ref, bdn_ref, o_ref, wgu_lo, wdn_lo):
    i = pl.program_id(0)
    used = i < used_ref[0]

    @pl.when(used & ((i == 0) | (be_ref[i] != be_ref[jnp.maximum(i - 1, 0)])))
    def _():
        wgu_lo[...] = wgu_ref[...].astype(wgu_lo.dtype)
        wdn_lo[...] = wdn_ref[...].astype(wdn_lo.dtype)

    @pl.when(used)
    def _():
        hu = jnp.dot(x_ref[...].astype(BF16), wgu_lo[...], preferred_element_type=F32) + bgu_ref[...]
        x_glu = jnp.minimum(hu[:, :MOE_FF], SWIGLU_LIMIT)
        x_lin = jnp.clip(hu[:, MOE_FF:], -SWIGLU_LIMIT, SWIGLU_LIMIT)
        act = x_glu * jax.nn.sigmoid(SWIGLU_ALPHA * x_glu) * (x_lin + 1.0)
        o_ref[...] = jnp.dot(act.astype(BF16), wdn_lo[...], preferred_element_type=F32) + bdn_ref[...]

    @pl.when(jnp.logical_not(used))
    def _():
        o_ref[...] = jnp.zeros_like(o_ref)


def _experts(xs, block_expert, n_used, w_gu, b_gu, w_dn, b_dn, layer):
    n_rows = xs.shape[0]
    n_blocks = n_rows // MOE_ROWS
    grid_spec = pltpu.PrefetchScalarGridSpec(
        num_scalar_prefetch=2,
        grid=(n_blocks,),
        in_specs=[
            pl.BlockSpec((MOE_ROWS, D_MODEL), lambda i, be, nu: (i, 0)),
            pl.BlockSpec((None, None, D_MODEL, 2 * MOE_FF), lambda i, be, nu: (layer, be[i], 0, 0)),
            pl.BlockSpec((None, 1, 2 * MOE_FF), lambda i, be, nu: (be[i], 0, 0)),
            pl.BlockSpec((None, None, MOE_FF, D_MODEL), lambda i, be, nu: (layer, be[i], 0, 0)),
            pl.BlockSpec((None, 1, D_MODEL), lambda i, be, nu: (be[i], 0, 0)),
        ],
        out_specs=pl.BlockSpec((MOE_ROWS, D_MODEL), lambda i, be, nu: (i, 0)),
        scratch_shapes=[pltpu.VMEM((D_MODEL, 2 * MOE_FF), BF16), pltpu.VMEM((MOE_FF, D_MODEL), BF16)],
    )
    return pl.pallas_call(
        _expert_kernel,
        grid_spec=grid_spec,
        out_shape=jax.ShapeDtypeStruct((n_rows, D_MODEL), F32),
        compiler_params=_params("arbitrary"),
        name="experts",
    )(block_expert, n_used, xs, w_gu, b_gu.reshape(MOE_EXPERTS, 1, -1), w_dn, b_dn.reshape(MOE_EXPERTS, 1, -1))


def _combine_kernel(x_ref, y0_ref, y1_ref, y2_ref, y3_ref, gate_ref, g_ref, b_ref, o_ref):
    ffn = gate_ref[:, 0:1] * y0_ref[...]
    for k, y_ref in enumerate((y1_ref, y2_ref, y3_ref), start=1):
        ffn = ffn + gate_ref[:, k:k + 1] * y_ref[...]
    o_ref[...] = _layer_norm(ALPHA * x_ref[...] + ffn, g_ref[...], b_ref[...])


def _combine(x, y_rows, gate, ln_g, ln_b, *, tm=512):
    m = x.shape[0]
    row = pl.BlockSpec((tm, D_MODEL), lambda i: (i, 0))
    vec = pl.BlockSpec((1, D_MODEL), lambda i: (0, 0))
    return pl.pallas_call(
        _combine_kernel,
        grid=(m // tm,),
        in_specs=[row] * (1 + MOE_TOP_K) + [pl.BlockSpec((tm, MOE_TOP_K), lambda i: (i, 0)), vec, vec],
        out_specs=row,
        out_shape=jax.ShapeDtypeStruct((m, D_MODEL), F32),
        compiler_params=_params("parallel"),
        name="combine_ln2",
    )(x, *y_rows, gate, ln_g.reshape(1, -1), ln_b.reshape(1, -1))


def _moe(x, router_w, router_b, w_gu, b_gu, w_dn, b_dn, ln_g, ln_b, layer):
    ntok = x.shape[0]
    idx_t, gate_t, rank_t, counts = _router(x, router_w, router_b)
    gate = gate_t.T
    n_assign = ntok * MOE_TOP_K
    counts = counts.reshape(-1).astype(jnp.int32)
    padded = (counts + MOE_ROWS - 1) // MOE_ROWS * MOE_ROWS
    pend = jnp.cumsum(padded)
    pstart = pend - padded
    experts = jnp.arange(MOE_EXPERTS, dtype=jnp.int32)

    def lookup(table, idx):
        return jnp.sum(jnp.where(idx[..., None] == experts, table, 0), axis=-1)

    dest_t = lookup(pstart, idx_t) + rank_t
    n_blocks = n_assign // MOE_ROWS + MOE_EXPERTS
    n_pad = MOE_EXPERTS * MOE_ROWS
    block_start = jnp.arange(n_blocks, dtype=jnp.int32) * MOE_ROWS
    block_expert = jnp.sum((pend[None, :] <= block_start[:, None]).astype(jnp.int32), axis=1)
    block_expert = jnp.minimum(block_expert, MOE_EXPERTS - 1)
    n_used = (pend[-1:] // MOE_ROWS).astype(jnp.int32)
    gap = padded - counts
    gap_end = jnp.cumsum(gap)
    pad_id = jnp.arange(n_pad, dtype=jnp.int32)
    pad_expert = jnp.sum((gap_end[None, :] <= pad_id[:, None]).astype(jnp.int32), axis=1)
    tail_of = lookup(pstart + counts - (gap_end - gap), jnp.minimum(pad_expert, MOE_EXPERTS - 1)) + pad_id
    pad_slot = jnp.where(pad_expert < MOE_EXPERTS, tail_of, pend[-1] + pad_id - gap_end[-1])
    keys = jnp.concatenate([dest_t.reshape(-1), pad_slot])
    toks = jnp.concatenate([jnp.tile(jnp.arange(ntok, dtype=jnp.int32), MOE_TOP_K), jnp.full((n_pad,), ntok, jnp.int32)])
    _, slot_tok = lax.sort_key_val(keys, toks)
    x_pad = jnp.concatenate([x, jnp.zeros((1, x.shape[1]), x.dtype)], axis=0)
    xs = x_pad[slot_tok]
    ys = _experts(xs, block_expert, n_used, w_gu, b_gu, w_dn, b_dn, layer)
    y_rows = [ys[dest_t[k]] for k in range(MOE_TOP_K)]
    return _combine(x, y_rows, gate, ln_g, ln_b)


def _layer(x, w_in, b_gate, ssd_conv_w, ssd_conv_b, ssd_a_log, ssd_dt_bias, ssd_d, ssd_norm_w,
           s5_a_re, s5_a_im, s5_log_step, s5_b_re, s5_b_im, s5_c_re, s5_c_im, s5_d, s5_glu_w1, s5_glu_w2,
           w_br_ssd, w_br_attn, w_br_s5, w_out, ln1_g, ln1_b,
           router_w, router_b, b_gu, b_dn, ln2_g, ln2_b, *, layer, w_gu, w_dn):
    bsz, seqlen, d = x.shape
    ntok = bsz * seqlen
    xf = x.reshape(ntok, d)
    offs = [0]
    for s in IN_SIZES:
        offs.append(offs[-1] + s)

    def seg(k):
        return w_in[:, offs[k]:offs[k + 1]].astype(BF16)

    gates = _matmul(xf, seg(0), tm=512, tn=1024, name="proj_gates")
    z = _matmul(xf, seg(1), tm=512, tn=1024, name="proj_z")
    u = _matmul_tiles(xf, seg(5), tm=512, name="proj_u")

    xbc_act = _xbc_conv(xf, seg(2), ssd_conv_w, ssd_conv_b, seqlen).reshape(bsz, seqlen, SSD_CONV_CH)
    decay = _ssd_decay(xf, seg(3), ssd_a_log, ssd_dt_bias)
    y_ssd = _ssd_scan(xbc_act, *decay, ssd_d)
    p_ssd = _ssd_project(y_ssd.reshape(ntok, SSD_INNER), z, ssd_norm_w, w_br_ssd.astype(BF16), gates, b_gate)

    w_qkv = w_in[:, offs[4]:offs[5]].reshape(d, 3, ATT_GROUPS, ATT_OUT)
    att_o, att_l = [], []
    for g, (window, dil) in enumerate(ATT_PATTERNS):
        assert (window // 2) // dil == ATT_RADIUS
        ls = seqlen // dil
        x_cls = x.reshape(bsz, ls, dil, d).transpose(0, 2, 1, 3).astype(BF16).reshape(ntok, d)
        qkv_g = _matmul(x_cls, w_qkv[:, :, g].reshape(d, 3 * ATT_OUT).astype(BF16), tm=512, tn=3 * ATT_OUT,
                        name=f"proj_qkv{g}")
        o, lse = _attention_group(qkv_g.reshape(bsz * dil, ls, 3 * ATT_OUT), g, dil)
        att_o.append(o.reshape(bsz, dil, ls, ATT_OUT).transpose(0, 2, 1, 3).reshape(ntok, ATT_OUT))
        att_l.append(lse.reshape(bsz, dil, ls, ATT_OUT).transpose(0, 2, 1, 3).reshape(ntok, ATT_OUT))

    h_s5 = _s5_mix(u, seqlen, s5_a_re, s5_a_im, s5_log_step, s5_b_re, s5_b_im, s5_c_re, s5_c_im, s5_d)
    p_s5 = _s5_project(h_s5, s5_glu_w1.astype(BF16), s5_glu_w2.astype(BF16), w_br_s5.astype(BF16), gates, b_gate)

    x1 = _merge(att_o, att_l, w_br_attn.astype(BF16), gates, b_gate, p_ssd, p_s5, w_out.astype(BF16),
                xf, ln1_g, ln1_b)
    x2 = _moe(x1, router_w, router_b, w_gu, b_gu, w_dn, b_dn, ln2_g, ln2_b, layer)
    return x2.reshape(bsz, seqlen, d)


def kernel(x, w_in, b_gate, ssd_conv_w, ssd_conv_b, ssd_a_log, ssd_dt_bias, ssd_d, ssd_norm_w, s5_a_re, s5_a_im, s5_log_step, s5_b_re, s5_b_im, s5_c_re, s5_c_im, s5_d, s5_glu_w1, s5_glu_w2, w_br_ssd, w_br_attn, w_br_s5, w_out, ln1_g, ln1_b, router_w, router_b, exp_w_gate_up, exp_b_gate_up, exp_w_down, exp_b_down, ln2_g, ln2_b):
    per_layer = (w_in, b_gate, ssd_conv_w, ssd_conv_b, ssd_a_log, ssd_dt_bias, ssd_d, ssd_norm_w, s5_a_re, s5_a_im,
                 s5_log_step, s5_b_re, s5_b_im, s5_c_re, s5_c_im, s5_d, s5_glu_w1, s5_glu_w2, w_br_ssd, w_br_attn,
                 w_br_s5, w_out, ln1_g, ln1_b, router_w, router_b, exp_b_gate_up, exp_b_down, ln2_g, ln2_b)
    for layer in range(DEPTH):
        x = _layer(x, *[p[layer] for p in per_layer], layer=layer, w_gu=exp_w_gate_up, w_dn=exp_w_down)
    return x
```

```python
import functools
import math

import jax
import jax.numpy as jnp
from jax import lax
from jax.experimental import pallas as pl
from jax.experimental.pallas import tpu as pltpu

F32 = jnp.float32
BF16 = jnp.bfloat16

D_MODEL = 1024
DEPTH = 4
ALPHA = (2 * DEPTH) ** 0.25
LN_EPS = 1e-5

SSD_INNER = 2 * D_MODEL
SSD_HEAD_DIM = 64
SSD_GROUPS = 8
SSD_HPG = 4
SSD_HEADS = SSD_GROUPS * SSD_HPG
SSD_STATE = 128
SSD_CONV = 5
SSD_CONV_CH = SSD_INNER + 2 * SSD_GROUPS * SSD_STATE
SSD_CHUNK = 128
SSD_SEQ_BLOCK = 1024
CONV_HALO_ROWS = 16

ATT_HEAD_DIM = 64
ATT_PATTERNS = ((128, 1), (512, 4), (2048, 16))
ATT_GROUPS = 3
ATT_HPG = 6
ATT_HEADS = ATT_GROUPS * ATT_HPG
ATT_WIDTH = ATT_HEADS * ATT_HEAD_DIM
ATT_OUT = ATT_HPG * ATT_HEAD_DIM
ATT_NEG = -1e30
ATT_RADIUS = 64
ATT_QSUB = 128

S5_CH = 16
S5_STATE = 64
S5_WIDTH = 9 * D_MODEL // 8
S5_GROUPS = S5_WIDTH // S5_CH
S5_TC = 16
S5_ROW = S5_TC * S5_CH
S5_SCAN_GROUPS = 8
S5_LANE_GROUPS = 128 // S5_CH
S5_TILES = S5_GROUPS // S5_LANE_GROUPS
S5_ROW_BLOCK = 256
S5_RELAYOUT_ROWS = 16

N_BRANCHES = 3
IN_SIZES = (N_BRANCHES * D_MODEL, SSD_INNER, SSD_CONV_CH, 2 * SSD_HEADS, 3 * ATT_WIDTH, S5_WIDTH)

MOE_EXPERTS = 32
MOE_TOP_K = 4
MOE_FF = D_MODEL
SWIGLU_LIMIT = 7.0
SWIGLU_ALPHA = 1.702
MOE_ROWS = 256

VMEM_LIMIT = 56 * 1024 * 1024


def _params(*sem):
    return pltpu.CompilerParams(dimension_semantics=sem, vmem_limit_bytes=VMEM_LIMIT)


def _mm_kernel(x_ref, w_ref, o_ref):
    o_ref[...] = jnp.dot(x_ref[...].astype(BF16), w_ref[...], preferred_element_type=F32).astype(o_ref.dtype)


def _matmul(x, w, *, tm, tn, name):
    m, k = x.shape
    n = w.shape[1]
    return pl.pallas_call(
        _mm_kernel,
        grid=(n // tn, m // tm),
        in_specs=[pl.BlockSpec((tm, k), lambda j, i: (i, 0)), pl.BlockSpec((k, tn), lambda j, i: (0, j))],
        out_specs=pl.BlockSpec((tm, tn), lambda j, i: (i, j)),
        out_shape=jax.ShapeDtypeStruct((m, n), F32),
        compiler_params=_params("parallel", "parallel"),
        name=name,
    )(x, w)


def _mm_tiles_kernel(x_ref, w_ref, o_ref):
    acc = jnp.dot(x_ref[...].astype(BF16), w_ref[...], preferred_element_type=F32)
    for t in range(o_ref.shape[0]):
        o_ref[t] = acc[:, 128 * t:128 * (t + 1)]


def _matmul_tiles(x, w, *, tm, name):
    m, k = x.shape
    n = w.shape[1]
    nt = n // 128
    return pl.pallas_call(
        _mm_tiles_kernel,
        grid=(m // tm,),
        in_specs=[pl.BlockSpec((tm, k), lambda i: (i, 0)), pl.BlockSpec((k, n), lambda i: (0, 0))],
        out_specs=pl.BlockSpec((nt, tm, 128), lambda i: (0, i, 0)),
        out_shape=jax.ShapeDtypeStruct((nt, m, 128), F32),
        compiler_params=_params("parallel"),
        name=name,
    )(x, w)


def _split3(a):
    a1 = a.astype(BF16)
    r1 = a - a1.astype(F32)
    a2 = r1.astype(BF16)
    a3 = (r1 - a2.astype(F32)).astype(BF16)
    return a1, a2, a3


def _decay_kernel(x_ref, w_ref, wt_ref, alog_r_ref, alog_c_ref, bias_r_ref, bias_c_ref,
                  acs_c_ref, wend_c_ref, acs_r_ref, delta_r_ref):
    q = SSD_CHUNK
    nh = SSD_HEADS
    xb = x_ref[...].astype(BF16)
    dt_c = jnp.dot(xb, w_ref[...], preferred_element_type=F32)
    dt_r = lax.dot_general(wt_ref[...], xb, (((1,), (1,)), ((), ())), preferred_element_type=F32)
    delta_c = jax.nn.softplus(dt_c + bias_r_ref[...])
    delta_r = jax.nn.softplus(dt_r + bias_c_ref[...])
    a_c = -jnp.exp(alog_r_ref[...]) * delta_c
    a_r = -jnp.exp(alog_c_ref[...]) * delta_r
    delta_r_ref[...] = delta_r
    row = lax.broadcasted_iota(jnp.int32, (q, q), 0)
    col = lax.broadcasted_iota(jnp.int32, (q, q), 1)
    lower = (col <= row).astype(BF16)
    upper = (col >= row).astype(BF16)
    fwd_lane = lax.broadcasted_iota(jnp.int32, (q, dt_c.shape[1]), 1) < nh
    fwd_row = lax.broadcasted_iota(jnp.int32, (2 * nh, q), 0) < nh
    for c in range(x_ref.shape[0] // q):
        rows = slice(c * q, (c + 1) * q)
        pieces = _split3(a_c[rows])
        prefix = sum(jnp.dot(lower, p, preferred_element_type=F32) for p in pieces)
        suffix = sum(jnp.dot(upper, p, preferred_element_type=F32) for p in pieces)
        acs = jnp.where(fwd_lane, prefix, suffix)
        total = jnp.where(fwd_lane[0:1], prefix[q - 1:q], suffix[0:1])
        acs_c_ref[rows, :] = acs
        wend_c_ref[rows, :] = delta_c[rows] * jnp.exp(total - acs)
        pieces = _split3(a_r[:, rows])
        prefix = sum(jnp.dot(p, upper, preferred_element_type=F32) for p in pieces)
        suffix = sum(jnp.dot(p, lower, preferred_element_type=F32) for p in pieces)
        acs_r_ref[:, rows] = jnp.where(fwd_row, prefix, suffix)


def _ssd_decay(x, w_dt, a_log, dt_bias, *, tm=1024):
    m, k = x.shape
    n = w_dt.shape[1]
    lanes = 128
    pad = lanes - n
    w_pad = jnp.pad(w_dt, ((0, 0), (0, pad)))
    alog = a_log.reshape(-1)
    bias = dt_bias.reshape(-1)
    col = pl.BlockSpec((tm, lanes), lambda i: (i, 0))
    rowf = pl.BlockSpec((n, tm), lambda i: (0, i))
    return pl.pallas_call(
        _decay_kernel,
        grid=(m // tm,),
        in_specs=[pl.BlockSpec((tm, k), lambda i: (i, 0)), pl.BlockSpec((k, lanes), lambda i: (0, 0)),
                  pl.BlockSpec((n, k), lambda i: (0, 0)),
                  pl.BlockSpec((1, lanes), lambda i: (0, 0)), pl.BlockSpec((n, 1), lambda i: (0, 0)),
                  pl.BlockSpec((1, lanes), lambda i: (0, 0)), pl.BlockSpec((n, 1), lambda i: (0, 0))],
        out_specs=[col, col, rowf, rowf],
        out_shape=[jax.ShapeDtypeStruct((m, lanes), F32)] * 2 + [jax.ShapeDtypeStruct((n, m), F32)] * 2,
        compiler_params=_params("parallel"),
        name="ssd_decay",
    )(x, w_pad, w_dt.T, jnp.pad(alog, (0, pad)).reshape(1, lanes), alog.reshape(n, 1),
      jnp.pad(bias, (0, pad)).reshape(1, lanes), bias.reshape(n, 1))


def _xbc_conv_kernel(xm_ref, xp_ref, xn_ref, w_ref, cw_ref, cb_ref, o_ref, buf, *, tiles_per_seq):
    pos = pl.program_id(1) % tiles_per_seq
    ts = xm_ref.shape[0]
    halo = xp_ref.shape[0]
    half = SSD_CONV // 2
    rows = jnp.concatenate([xp_ref[...], xm_ref[...], xn_ref[...]], axis=0).astype(BF16)
    buf[...] = jnp.dot(rows, w_ref[...], preferred_element_type=F32)

    @pl.when(pos == 0)
    def _():
        buf[0:halo, :] = jnp.zeros((halo, buf.shape[1]), F32)

    @pl.when(pos == tiles_per_seq - 1)
    def _():
        buf[halo + ts:2 * halo + ts, :] = jnp.zeros((halo, buf.shape[1]), F32)

    acc = cb_ref[...] + cw_ref[0:1, :] * buf[halo - half:halo - half + ts, :]
    for k in range(1, SSD_CONV):
        acc = acc + cw_ref[k:k + 1, :] * buf[halo - half + k:halo - half + k + ts, :]
    o_ref[...] = acc * jax.nn.sigmoid(acc)


def _xbc_conv(x, w, conv_w, conv_b, seqlen, *, tm=512, tn=1024):
    m, k = x.shape
    n = w.shape[1]
    halo = CONV_HALO_ROWS
    nhalo = m // halo
    return pl.pallas_call(
        functools.partial(_xbc_conv_kernel, tiles_per_seq=seqlen // tm),
        grid=(n // tn, m // tm),
        in_specs=[
            pl.BlockSpec((tm, k), lambda j, i: (i, 0)),
            pl.BlockSpec((halo, k), lambda j, i: (jnp.maximum(i * (tm // halo) - 1, 0), 0)),
            pl.BlockSpec((halo, k), lambda j, i: (jnp.minimum((i + 1) * (tm // halo), nhalo - 1), 0)),
            pl.BlockSpec((k, tn), lambda j, i: (0, j)),
            pl.BlockSpec((SSD_CONV, tn), lambda j, i: (0, j)),
            pl.BlockSpec((1, tn), lambda j, i: (0, j)),
        ],
        out_specs=pl.BlockSpec((tm, tn), lambda j, i: (i, j)),
        out_shape=jax.ShapeDtypeStruct((m, n), F32),
        scratch_shapes=[pltpu.VMEM((tm + 2 * halo, tn), F32)],
        compiler_params=_params("parallel", "parallel"),
        name="proj_xbc_conv",
    )(x, x, x, w, conv_w, conv_b.reshape(1, n))


def _per_head(cols):
    rows = cols[0].shape[0]
    first = lax.broadcasted_iota(jnp.int32, (rows, 2 * SSD_HEAD_DIM), 1) < SSD_HEAD_DIM
    return jnp.concatenate([jnp.where(first, cols[0], cols[1]), jnp.where(first, cols[2], cols[3])], axis=1)


def _group_lanes(ref, rows, g):
    lanes = ref.shape[1]
    return pltpu.roll(ref[rows, :], (lanes - SSD_HPG * g) % lanes, 1)


def _group_rows(ref, first, cols, g):
    tile = 8
    per_tile = tile // SSD_HPG
    base = pl.multiple_of(first + (g // per_tile) * tile, tile)
    return pltpu.roll(ref[pl.ds(base, tile), cols], (tile - SSD_HPG * (g % per_tile)) % tile, 0)


def _head_selector(lanes, first):
    width = SSD_HPG * SSD_HEAD_DIM
    src = lax.broadcasted_iota(jnp.int32, (lanes, width), 0)
    dst = lax.broadcasted_iota(jnp.int32, (lanes, width), 1) // SSD_HEAD_DIM
    return (src == dst + first).astype(BF16)


def _expand_heads(cols, selector):
    return sum(jnp.dot(p, selector, preferred_element_type=F32) for p in _split3(cols))


def _weighted_inputs(xs_ref, wend_ref, g, first, xw_scr):
    rows = slice(0, xs_ref.shape[0])
    wend = _group_lanes(wend_ref, rows, g)
    xw_scr[...] = (xs_ref[...] * _expand_heads(wend, _head_selector(wend_ref.shape[1], first))).astype(xw_scr.dtype)


def _ssd_back_kernel(xs_ref, bm_ref, acs_ref, wend_ref, out_ref, carried, xw_scr):
    q = SSD_CHUNK
    nchunk = xs_ref.shape[0] // q
    g = pl.program_id(1)
    bwd = SSD_HEADS

    @pl.when(pl.program_id(2) == 0)
    def _():
        carried[...] = jnp.zeros_like(carried)

    _weighted_inputs(xs_ref, wend_ref, g, bwd, xw_scr)

    def chunk(ci, carry):
        c = nchunk - 1 - ci
        rows = pl.ds(pl.multiple_of(c * q, q), q)
        first_row = pl.ds(pl.multiple_of(c * q, q), 8)
        total = _group_lanes(acs_ref, first_row, g)[0:1]
        states = lax.dot_general(bm_ref[rows, :].astype(BF16), xw_scr[rows, :], (((0,), (0,)), ((), ())),
                                 preferred_element_type=F32)
        prev = carried[...]
        out_ref[c] = prev.astype(out_ref.dtype)
        carried[...] = jnp.exp(_per_head([total[:, bwd + r:bwd + r + 1] for r in range(SSD_HPG)])) * prev + states
        return carry

    lax.fori_loop(0, nchunk, chunk, 0, unroll=4)


def _ssd_main_kernel(xs_ref, bm_ref, cm_ref, acs_c_ref, wend_c_ref, acs_r_ref, delta_r_ref, back_ref,
                     dskip_ref, y_ref, carried):
    q = SSD_CHUNK
    nchunk = xs_ref.shape[0] // q
    g = pl.program_id(1)
    bwd = SSD_HEADS
    hd = SSD_HEAD_DIM

    @pl.when(pl.program_id(2) == 0)
    def _():
        carried[...] = jnp.zeros_like(carried)

    row = lax.broadcasted_iota(jnp.int32, (q, q), 0)
    col = lax.broadcasted_iota(jnp.int32, (q, q), 1)
    lower = col <= row
    upper = col >= row
    d_wide = _per_head([dskip_ref[:, r:r + 1] for r in range(SSD_HPG)])
    selector = _head_selector(wend_c_ref.shape[1], 0)
    first = lax.broadcasted_iota(jnp.int32, (q, 2 * hd), 1) < hd

    def per_head_lanes(blocks):
        return jnp.concatenate([jnp.where(first, blocks[0], blocks[1]), jnp.where(first, blocks[2], blocks[3])], axis=1)

    def chunk(c, carry):
        r0 = pl.multiple_of(c * q, q)
        rows = pl.ds(r0, q)
        xs = xs_ref[rows, :]
        bmat = bm_ref[rows, :].astype(BF16)
        cmat = cm_ref[rows, :].astype(BF16)
        acs = _group_lanes(acs_c_ref, rows, g)
        wend = _group_lanes(wend_c_ref, rows, g)
        acs_f = [jnp.broadcast_to(acs[:, r:r + 1], (q, q)) for r in range(SSD_HPG)]
        acs_b = [jnp.broadcast_to(acs[:, bwd + r:bwd + r + 1], (q, q)) for r in range(SSD_HPG)]
        acs_rf = _group_rows(acs_r_ref, 0, rows, g)
        acs_rb = _group_rows(acs_r_ref, bwd, rows, g)
        del_rf = _group_rows(delta_r_ref, 0, rows, g)
        del_rb = _group_rows(delta_r_ref, bwd, rows, g)
        cb = lax.dot_general(cmat, bmat, (((1,), (1,)), ((), ())), preferred_element_type=F32)
        prev = carried[...]
        states_in = jnp.concatenate([prev.astype(BF16), back_ref[c]], axis=1)
        y_off = jnp.dot(cmat, states_in, preferred_element_type=F32)
        e_f = jnp.exp(per_head_lanes(acs_f))
        e_b = jnp.exp(per_head_lanes(acs_b))
        rest = e_f * y_off[:, 0:SSD_HPG * hd] + e_b * y_off[:, SSD_HPG * hd:] + d_wide * xs
        for r in range(SSD_HPG):
            seg_f = acs_f[r] - acs_rf[r:r + 1, :]
            seg_b = acs_b[r] - acs_rb[r:r + 1, :]
            w = (jnp.exp(jnp.where(lower, seg_f, ATT_NEG)) * del_rf[r:r + 1, :]
                 + jnp.exp(jnp.where(upper, seg_b, ATT_NEG)) * del_rb[r:r + 1, :])
            y_diag = jnp.dot((cb * w).astype(BF16), xs[:, r * hd:(r + 1) * hd].astype(BF16),
                             preferred_element_type=F32)
            y_ref[rows, r * hd:(r + 1) * hd] = y_diag + rest[:, r * hd:(r + 1) * hd]
        xw = xs * _expand_heads(wend, selector)
        states = lax.dot_general(bmat, xw.astype(BF16), (((0,), (0,)), ((), ())), preferred_element_type=F32)
        total = acs[q - 1:q]
        carried[...] = jnp.exp(_per_head([total[:, r:r + 1] for r in range(SSD_HPG)])) * prev + states
        return carry

    lax.fori_loop(0, nchunk, chunk, 0, unroll=2)


def _ssd_scan(xbc_act, acs_c, wend_c, acs_r, delta_r, d_skip):
    bsz, seqlen, _ = xbc_act.shape
    sb = min(SSD_SEQ_BLOCK, seqlen)
    nblk = seqlen // sb
    cpb = sb // SSD_CHUNK
    n_x = SSD_HPG * SSD_HEAD_DIM
    b_off = SSD_INNER // SSD_STATE
    c_off = b_off + SSD_GROUPS
    lanes = acs_c.shape[1]
    nrow = acs_r.shape[0]

    def specs(blk):
        xs = pl.BlockSpec((None, sb, n_x), lambda b_, g, i: (b_, blk(i), g))
        bm = pl.BlockSpec((None, sb, SSD_STATE), lambda b_, g, i: (b_, blk(i), b_off + g))
        cm = pl.BlockSpec((None, sb, SSD_STATE), lambda b_, g, i: (b_, blk(i), c_off + g))
        colf = pl.BlockSpec((sb, lanes), lambda b_, g, i: (b_ * nblk + blk(i), 0))
        rowf = pl.BlockSpec((nrow, sb), lambda b_, g, i: (0, b_ * nblk + blk(i)))
        back = pl.BlockSpec((None, None, cpb, SSD_STATE, n_x), lambda b_, g, i: (b_, g, blk(i), 0, 0))
        return xs, bm, cm, colf, rowf, back

    xs, bm, cm, colf, rowf, back = specs(lambda i: nblk - 1 - i)
    assert SSD_CHUNK == 2 * SSD_HEAD_DIM and SSD_HPG == 4
    state = [pltpu.VMEM((SSD_STATE, n_x), F32), pltpu.VMEM((sb, n_x), BF16)]
    back_states = pl.pallas_call(
        _ssd_back_kernel,
        grid=(bsz, SSD_GROUPS, nblk),
        in_specs=[xs, bm, colf, colf],
        out_specs=back,
        out_shape=jax.ShapeDtypeStruct((bsz, SSD_GROUPS, seqlen // SSD_CHUNK, SSD_STATE, n_x), BF16),
        scratch_shapes=state,
        compiler_params=_params("parallel", "parallel", "arbitrary"),
        name="ssd_back",
    )(xbc_act, xbc_act, acs_c, wend_c)

    xs, bm, cm, colf, rowf, back = specs(lambda i: i)
    return pl.pallas_call(
        _ssd_main_kernel,
        grid=(bsz, SSD_GROUPS, nblk),
        in_specs=[xs, bm, cm, colf, colf, rowf, rowf, back,
                  pl.BlockSpec((None, 1, SSD_HPG), lambda b_, g, i: (g, 0, 0))],
        out_specs=pl.BlockSpec((None, sb, n_x), lambda b_, g, i: (b_, i, g)),
        out_shape=jax.ShapeDtypeStruct((bsz, seqlen, SSD_INNER), F32),
        scratch_shapes=state[:1],
        compiler_params=_params("parallel", "parallel", "arbitrary"),
        name="ssd_main",
    )(xbc_act, xbc_act, xbc_act, acs_c, wend_c, acs_r, delta_r, back_states,
      d_skip.reshape(SSD_GROUPS, 1, SSD_HPG))


def _attn_kernel(q_ref, ko_ref, kp_ref, kn_ref, vo_ref, vp_ref, vn_ref, o_ref, lse_ref, kcat, vcat,
                 *, dil, group, ls):
    tq = q_ref.shape[0]
    rad = ATT_RADIUS
    qs = ATT_QSUB
    kw = qs + 2 * rad
    i = pl.program_id(1)
    kcat[0:rad, :] = kp_ref[...].astype(BF16)
    kcat[rad:rad + tq, :] = ko_ref[...].astype(BF16)
    kcat[rad + tq:2 * rad + tq, :] = kn_ref[...].astype(BF16)
    vcat[0:rad, :] = vp_ref[...].astype(BF16)
    vcat[rad:rad + tq, :] = vo_ref[...].astype(BF16)
    vcat[rad + tq:2 * rad + tq, :] = vn_ref[...].astype(BF16)

    row = lax.broadcasted_iota(jnp.int32, (qs, kw), 0)
    col = lax.broadcasted_iota(jnp.int32, (qs, kw), 1)
    dist = jnp.abs(row - col + rad)
    in_band = dist <= rad
    dist_f = (dil * dist).astype(F32)
    lane = lax.broadcasted_iota(jnp.int32, (qs, 2 * ATT_HEAD_DIM), 1)
    first_half = lane < ATT_HEAD_DIM
    scale = ATT_HEAD_DIM ** -0.5

    for m in range(tq // qs):
        kpos = i * tq + (m * qs - rad) + col
        valid = in_band & (kpos >= 0) & (kpos < ls)
        for pair in range(ATT_HPG // 2):
            lanes = slice(pair * 2 * ATT_HEAD_DIM, (pair + 1) * 2 * ATT_HEAD_DIM)
            q2 = q_ref[m * qs:(m + 1) * qs, lanes]
            k2 = kcat[m * qs:m * qs + kw, lanes]
            v2 = vcat[m * qs:m * qs + kw, lanes]
            outs, lses = [], []
            for half in range(2):
                head = 2 * pair + half
                slope = 2.0 ** (-8.0 * (head * ATT_GROUPS + group + 1) / ATT_HEADS)
                qm = jnp.where(first_half if half == 0 else ~first_half, q2, 0.0).astype(BF16)
                s = lax.dot_general(qm, k2, (((1,), (1,)), ((), ())), preferred_element_type=F32)
                s = s * scale + (-slope) * dist_f
                s = jnp.where(valid, s, ATT_NEG)
                mx = jnp.max(s, axis=-1, keepdims=True)
                p = jnp.exp(s - mx)
                zsum = jnp.sum(p, axis=-1, keepdims=True)
                pv = jnp.dot(p.astype(BF16), v2, preferred_element_type=F32)
                outs.append(pv / zsum)
                lses.append(mx + jnp.log(zsum))
            o_ref[m * qs:(m + 1) * qs, lanes] = jnp.where(first_half, outs[0], outs[1])
            lse_ref[m * qs:(m + 1) * qs, lanes] = jnp.where(first_half, lses[0], lses[1])


def _attention_group(qkv, group, dil):
    nseq, ls, _ = qkv.shape
    tq = min(512, ls)
    nblk = ls // tq
    halo_per_blk = tq // ATT_RADIUS
    n_halo = ls // ATT_RADIUS

    def own(off):
        return pl.BlockSpec((None, tq, ATT_OUT), lambda s, i: (s, i, off))

    def prev(off):
        return pl.BlockSpec((None, ATT_RADIUS, ATT_OUT), lambda s, i: (s, jnp.maximum(i * halo_per_blk - 1, 0), off))

    def nxt(off):
        return pl.BlockSpec((None, ATT_RADIUS, ATT_OUT),
                            lambda s, i: (s, jnp.minimum((i + 1) * halo_per_blk, n_halo - 1), off))

    out_spec = pl.BlockSpec((None, tq, ATT_OUT), lambda s, i: (s, i, 0))
    return pl.pallas_call(
        functools.partial(_attn_kernel, dil=dil, group=group, ls=ls),
        grid=(nseq, nblk),
        in_specs=[own(0), own(1), prev(1), nxt(1), own(2), prev(2), nxt(2)],
        out_specs=[out_spec, out_spec],
        out_shape=[jax.ShapeDtypeStruct((nseq, ls, ATT_OUT), F32)] * 2,
        scratch_shapes=[pltpu.VMEM((tq + 2 * ATT_RADIUS, ATT_OUT), BF16)] * 2,
        compiler_params=_params("parallel", "parallel"),
        name=f"attn_g{group}",
    )(qkv, qkv, qkv, qkv, qkv, qkv, qkv)


def _s5_tables(a_re, a_im, log_step, b_re, b_im, c_re, c_im):
    hp = lax.Precision.HIGHEST
    tc = S5_TC
    ng, width = S5_GROUPS, S5_ROW
    step = jnp.exp(log_step)[..., None]
    ar, ai = a_re, a_im
    mag = jnp.exp(step * ar)
    abr, abi = mag * jnp.cos(step * ai), mag * jnp.sin(step * ai)
    den = ar * ar + ai * ai
    fr = ((abr - 1.0) * ar + abi * ai) / den
    fi = (abi * ar - (abr - 1.0) * ai) / den
    b_re_t, b_im_t = b_re.transpose(0, 2, 1), b_im.transpose(0, 2, 1)
    bbr = fr[:, :, None, :] * b_re_t[None] - fi[:, :, None, :] * b_im_t[None]
    bbi = fr[:, :, None, :] * b_im_t[None] + fi[:, :, None, :] * b_re_t[None]
    ks = jnp.arange(tc + 1, dtype=F32)
    arg_r, arg_i = step * ar, step * ai
    pk_r = jnp.exp(arg_r[:, :, None, :] * ks[:, None]) * jnp.cos(arg_i[:, :, None, :] * ks[:, None])
    pk_i = jnp.exp(arg_r[:, :, None, :] * ks[:, None]) * jnp.sin(arg_i[:, :, None, :] * ks[:, None])
    pn_r = jnp.exp(arg_r[..., None] * ks) * jnp.cos(arg_i[..., None] * ks)
    pn_i = jnp.exp(arg_r[..., None] * ks) * jnp.sin(arg_i[..., None] * ks)
    c_re_t, c_im_t = c_re.transpose(0, 1, 3, 2), c_im.transpose(0, 1, 3, 2)
    cap_r = (c_re_t[:, :, :, None, :] * pn_r[..., None] - c_im_t[:, :, :, None, :] * pn_i[..., None])
    cap_i = (c_re_t[:, :, :, None, :] * pn_i[..., None] + c_im_t[:, :, :, None, :] * pn_r[..., None])
    cap_r = cap_r.reshape(2, ng, S5_STATE, (tc + 1) * S5_CH)
    cap_i = cap_i.reshape(2, ng, S5_STATE, (tc + 1) * S5_CH)
    taps = (jnp.einsum('dgin,dgnm->dgim', bbr, cap_r, precision=hp)
            - jnp.einsum('dgin,dgnm->dgim', bbi, cap_i, precision=hp))[..., :width]
    fwd_src = jnp.pad(taps[0], ((0, 0), (0, 0), (width, 0)))
    bwd_rev = taps[1].reshape(ng, S5_CH, tc, S5_CH)[:, :, ::-1, :].reshape(ng, S5_CH, width)
    bwd_src = jnp.pad(bwd_rev, ((0, 0), (0, 0), (0, width)))
    toep = jnp.stack([fwd_src[:, :, width - S5_CH * s:2 * width - S5_CH * s]
                      + bwd_src[:, :, S5_CH * (tc - 1 - s):S5_CH * (tc - 1 - s) + width] for s in range(tc)], axis=1)
    toep = toep.reshape(ng, width, width)

    def end_state(d, pw_r, pw_i):
        re = pw_r[:, :, None, :] * bbr[d][:, None] - pw_i[:, :, None, :] * bbi[d][:, None]
        im = pw_r[:, :, None, :] * bbi[d][:, None] + pw_i[:, :, None, :] * bbr[d][:, None]
        return [re.reshape(ng, width, S5_STATE), im.reshape(ng, width, S5_STATE)]

    b_end = jnp.concatenate(end_state(0, pk_r[0][:, tc - 1::-1], pk_i[0][:, tc - 1::-1])
                            + end_state(1, pk_r[1][:, :tc], pk_i[1][:, :tc]), axis=-1)

    def lag_reversed(a):
        return a.reshape(ng, S5_STATE, tc, S5_CH)[:, :, ::-1, :].reshape(ng, S5_STATE, width)

    c_out = jnp.concatenate([cap_r[0][..., S5_CH:], -cap_i[0][..., S5_CH:],
                             lag_reversed(cap_r[1][..., S5_CH:]), -lag_reversed(cap_i[1][..., S5_CH:])], axis=1)
    ptr, pti = pk_r[:, :, tc], pk_i[:, :, tc]
    p_mul = jnp.concatenate([ptr[0], ptr[0], ptr[1], ptr[1]], axis=-1)
    q_mul = jnp.concatenate([-pti[0], pti[0], -pti[1], pti[1]], axis=-1)
    return toep, b_end, c_out, p_mul, q_mul


def _granule_transpose(regs, granule):
    n = len(regs)
    d = n // 2
    while d >= 1:
        low = (granule // d) % 2 == 0
        nxt = list(regs)
        for i in range(n):
            if (i // d) % 2 == 0:
                a, b = regs[i], regs[i + d]
                nxt[i] = jnp.where(low, a, pltpu.roll(b, d * S5_CH, 1))
                nxt[i + d] = jnp.where(low, pltpu.roll(a, 128 - d * S5_CH, 1), b)
        regs = nxt
        d //= 2
    return regs


def _s5_rows_to_groups(u_ref, x_scr):
    rr = S5_RELAYOUT_ROWS
    per = S5_LANE_GROUPS
    granule = lax.broadcasted_iota(jnp.int32, (rr, 128), 1) // S5_CH

    def body(rb, carry):
        base = rb * (rr * S5_TC)
        rows = pl.ds(pl.multiple_of(rb * rr, rr), rr)
        for h in range(S5_TC // per):
            steps = [u_ref[pl.ds(base + per * h + t, rr, stride=S5_TC), :] for t in range(per)]
            for gl, block in enumerate(_granule_transpose(steps, granule)):
                x_scr[gl, rows, 128 * h:128 * (h + 1)] = block
        return carry

    lax.fori_loop(0, x_scr.shape[1] // rr, body, 0)


def _s5_in_kernel(u_ref, toep_ref, bend_ref, y_ref, sf_ref, sb_ref, x_scr):
    half = 2 * S5_STATE
    _s5_rows_to_groups(u_ref, x_scr)
    for gl in range(S5_LANE_GROUPS):
        xb = x_scr[gl].astype(BF16)
        y_ref[gl] = jnp.dot(xb, toep_ref[gl], preferred_element_type=F32)
        s = jnp.dot(xb, bend_ref[gl], preferred_element_type=F32)
        sf_ref[gl] = s[:, 0:half]
        sb_ref[gl] = s[:, half:2 * half]


def _s5_scan_kernel(sf_ref, sb_ref, p_ref, q_ref, hf_ref, hb_ref, sf_sw_ref, sb_sw_ref, *, rows_per_seq):
    n_chain = p_ref.shape[0]
    half = 2 * S5_STATE
    pf, qf = p_ref[:, 0:half], q_ref[:, 0:half]
    pb, qb = p_ref[:, half:2 * half], q_ref[:, half:2 * half]
    qf_sw = pltpu.roll(qf, S5_STATE, 1)
    qb_sw = pltpu.roll(qb, S5_STATE, 1)
    zero = jnp.zeros((n_chain, half), F32)
    piece = 256

    def swap(k, carry):
        rows = pl.ds(pl.multiple_of(k * piece, piece), piece)
        sf_sw_ref[rows, :] = pltpu.roll(sf_ref[rows, :], S5_STATE, 1)
        sb_sw_ref[rows, :] = pltpu.roll(sb_ref[rows, :], S5_STATE, 1)
        return carry

    lax.fori_loop(0, sf_ref.shape[0] // piece, swap, 0)

    def step(c, carry):
        hf, hf_sw, hb, hb_sw = carry
        rows_f = pl.ds(c, n_chain, stride=rows_per_seq)
        rows_b = pl.ds(rows_per_seq - 1 - c, n_chain, stride=rows_per_seq)
        hf_ref[rows_f, :] = hf
        hb_ref[rows_b, :] = hb
        sf = sf_ref[rows_f, :]
        sb = sb_ref[rows_b, :]
        sf_sw = sf_sw_ref[rows_f, :]
        sb_sw = sb_sw_ref[rows_b, :]
        hf_new = pf * hf + qf * hf_sw + sf
        hf_sw_new = pf * hf_sw + qf_sw * hf + sf_sw
        hb_new = pb * hb + qb * hb_sw + sb
        hb_sw_new = pb * hb_sw + qb_sw * hb + sb_sw
        return hf_new, hf_sw_new, hb_new, hb_sw_new

    lax.fori_loop(0, rows_per_seq, step, (zero, zero, zero, zero))


def _s5_out_kernel(hf_ref, hb_ref, cout_ref, y_ref, u_ref, d_ref, o_ref, y_scr):
    half = 2 * S5_STATE
    per = S5_LANE_GROUPS
    for gl in range(per):
        y = y_ref[gl] + jnp.dot(hf_ref[gl].astype(BF16), cout_ref[gl, 0:half, :], preferred_element_type=F32)
        y_scr[gl] = y + jnp.dot(hb_ref[gl].astype(BF16), cout_ref[gl, half:2 * half, :], preferred_element_type=F32)
    rr = S5_RELAYOUT_ROWS
    granule = lax.broadcasted_iota(jnp.int32, (rr, 128), 1) // S5_CH
    d_skip = d_ref[...]

    def body(rb, carry):
        rows = pl.ds(pl.multiple_of(rb * rr, rr), rr)
        base = rb * (rr * S5_TC)
        for h in range(S5_TC // per):
            groups = [y_scr[gl, rows, 128 * h:128 * (h + 1)] for gl in range(per)]
            for t, block in enumerate(_granule_transpose(groups, granule)):
                tok = pl.ds(base + per * h + t, rr, stride=S5_TC)
                o_ref[tok, :] = jax.nn.gelu(block + u_ref[tok, :] * d_skip)
        return carry

    lax.fori_loop(0, y_scr.shape[1] // rr, body, 0)


def _s5_mix(u_tiles, seqlen, a_re, a_im, log_step, b_re, b_im, c_re, c_im, d_skip):
    ntile, ntok, lanes = u_tiles.shape
    bsz = ntok // seqlen
    rows_per_seq = seqlen // S5_TC
    rows = bsz * rows_per_seq
    rblk = min(S5_ROW_BLOCK, rows)
    per = S5_LANE_GROUPS
    half = 2 * S5_STATE
    toep, b_end, c_out, p_mul, q_mul = _s5_tables(a_re, a_im, log_step, b_re, b_im, c_re, c_im)
    toep, b_end, c_out = toep.astype(BF16), b_end.astype(BF16), c_out.astype(BF16)
    utile = pl.BlockSpec((None, rblk * S5_TC, lanes), lambda j, i: (j, i, 0))
    ytile = pl.BlockSpec((per, rblk, S5_ROW), lambda j, i: (j, i, 0))
    stile = pl.BlockSpec((per, rblk, half), lambda j, i: (j, i, 0))
    wtile = pl.BlockSpec((per, S5_ROW, S5_ROW), lambda j, i: (j, 0, 0))
    state_shape = jax.ShapeDtypeStruct((S5_GROUPS, rows, half), F32)
    grid = (ntile, rows // rblk)
    y_in, s_f, s_b = pl.pallas_call(
        _s5_in_kernel,
        grid=grid,
        in_specs=[utile, wtile, wtile],
        out_specs=[ytile, stile, stile],
        out_shape=[jax.ShapeDtypeStruct((S5_GROUPS, rows, S5_ROW), F32), state_shape, state_shape],
        scratch_shapes=[pltpu.VMEM((per, rblk, S5_ROW), F32)],
        compiler_params=_params("parallel", "parallel"),
        name="s5_in",
    )(u_tiles, toep, b_end)

    gs = S5_SCAN_GROUPS
    n_chain = gs * bsz
    chain_p = jnp.repeat(p_mul, bsz, axis=0)
    chain_q = jnp.repeat(q_mul, bsz, axis=0)
    flat = jax.ShapeDtypeStruct((S5_GROUPS * rows, half), F32)
    srows = pl.BlockSpec((gs * rows, half), lambda t: (t, 0))
    chain = pl.BlockSpec((n_chain, S5_ROW), lambda t: (t, 0))
    h_f, h_b = pl.pallas_call(
        functools.partial(_s5_scan_kernel, rows_per_seq=rows_per_seq),
        grid=(S5_GROUPS // gs,),
        in_specs=[srows, srows, chain, chain],
        out_specs=[srows, srows],
        out_shape=[flat, flat],
        scratch_shapes=[pltpu.VMEM((gs * rows, half), F32)] * 2,
        compiler_params=_params("parallel"),
        name="s5_scan",
    )(s_f.reshape(S5_GROUPS * rows, half), s_b.reshape(S5_GROUPS * rows, half), chain_p, chain_q)

    return pl.pallas_call(
        _s5_out_kernel,
        grid=grid,
        in_specs=[stile, stile, wtile, ytile, utile, pl.BlockSpec((None, 1, lanes), lambda j, i: (j, 0, 0))],
        out_specs=utile,
        out_shape=jax.ShapeDtypeStruct((ntile, ntok, lanes), F32),
        scratch_shapes=[pltpu.VMEM((per, rblk, S5_ROW), F32)],
        compiler_params=_params("parallel", "parallel"),
        name="s5_out",
    )(h_f.reshape(S5_GROUPS, rows, half), h_b.reshape(S5_GROUPS, rows, half), c_out, y_in, u_tiles,
      d_skip.reshape(ntile, 1, lanes))


def _ssd_proj_kernel(y_ref, z_ref, nw_ref, w_ref, gate_ref, bg_ref, o_ref):
    z = z_ref[...]
    y = y_ref[...] * (z * jax.nn.sigmoid(z))
    y = y * lax.rsqrt(jnp.mean(y * y, axis=-1, keepdims=True) + LN_EPS) * nw_ref[...]
    p = jnp.dot(y.astype(BF16), w_ref[...], preferred_element_type=F32)
    o_ref[...] = jax.nn.sigmoid(gate_ref[...] + bg_ref[...]) * p


def _ssd_project(y, z, norm_w, w_br, gates, b_gate, *, tm=512):
    m = y.shape[0]
    wide = pl.BlockSpec((tm, SSD_INNER), lambda i: (i, 0))
    return pl.pallas_call(
        _ssd_proj_kernel,
        grid=(m // tm,),
        in_specs=[wide, wide, pl.BlockSpec((1, SSD_INNER), lambda i: (0, 0)),
                  pl.BlockSpec((SSD_INNER, D_MODEL), lambda i: (0, 0)),
                  pl.BlockSpec((tm, D_MODEL), lambda i: (i, 0)),
                  pl.BlockSpec((1, D_MODEL), lambda i: (0, 0))],
        out_specs=pl.BlockSpec((tm, D_MODEL), lambda i: (i, 0)),
        out_shape=jax.ShapeDtypeStruct((m, D_MODEL), F32),
        compiler_params=_params("parallel"),
        name="ssd_project",
    )(y, z, norm_w.reshape(1, -1), w_br, gates, b_gate[0:1])


def _s5_proj_kernel(h_ref, w1_ref, w2_ref, w_ref, gate_ref, bg_ref, o_ref):
    hb = jnp.concatenate([h_ref[t] for t in range(h_ref.shape[0])], axis=1).astype(BF16)
    a = jnp.dot(hb, w1_ref[...], preferred_element_type=F32)
    b = jnp.dot(hb, w2_ref[...], preferred_element_type=F32)
    p = jnp.dot((a * jax.nn.sigmoid(b)).astype(BF16), w_ref[...], preferred_element_type=F32)
    o_ref[...] = jax.nn.sigmoid(gate_ref[...] + bg_ref[...]) * p


def _s5_project(h_tiles, w1, w2, w_br, gates, b_gate, *, tm=512):
    ntile, m, lanes = h_tiles.shape
    sq = pl.BlockSpec((S5_WIDTH, S5_WIDTH), lambda i: (0, 0))
    return pl.pallas_call(
        _s5_proj_kernel,
        grid=(m // tm,),
        in_specs=[pl.BlockSpec((ntile, tm, lanes), lambda i: (0, i, 0)), sq, sq,
                  pl.BlockSpec((S5_WIDTH, D_MODEL), lambda i: (0, 0)),
                  pl.BlockSpec((tm, D_MODEL), lambda i: (i, 2)),
                  pl.BlockSpec((1, D_MODEL), lambda i: (0, 0))],
        out_specs=pl.BlockSpec((tm, D_MODEL), lambda i: (i, 0)),
        out_shape=jax.ShapeDtypeStruct((m, D_MODEL), F32),
        compiler_params=_params("parallel"),
        name="s5_project",
    )(h_tiles, w1, w2, w_br, gates, b_gate[2:3])


def _layer_norm(v, g, b):
    mu = jnp.mean(v, axis=-1, keepdims=True)
    c = v - mu
    var = jnp.mean(c * c, axis=-1, keepdims=True)
    return c * lax.rsqrt(var + LN_EPS) * g + b


def _merge_kernel(o0_ref, o1_ref, o2_ref, l0_ref, l1_ref, l2_ref, w_att_ref, gate_ref, bg_ref, pssd_ref, ps5_ref,
                  w_out_ref, x_ref, g_ref, b_ref, o_ref):
    l0, l1, l2 = l0_ref[...], l1_ref[...], l2_ref[...]
    mx = jnp.maximum(jnp.maximum(l0, l1), l2)
    e0, e1, e2 = jnp.exp(l0 - mx), jnp.exp(l1 - mx), jnp.exp(l2 - mx)
    den = e0 + e1 + e2
    att = (e0 / den) * o0_ref[...] + (e1 / den) * o1_ref[...] + (e2 / den) * o2_ref[...]
    p_att = jnp.dot(att.astype(BF16), w_att_ref[...], preferred_element_type=F32)
    merged = pssd_ref[...] + jax.nn.sigmoid(gate_ref[...] + bg_ref[...]) * p_att + ps5_ref[...]
    mix = jnp.dot(merged.astype(BF16), w_out_ref[...], preferred_element_type=F32)
    o_ref[...] = _layer_norm(ALPHA * x_ref[...] + mix, g_ref[...], b_ref[...])


def _merge(att_o, att_l, w_att, gates, b_gate, p_ssd, p_s5, w_out, x, ln_g, ln_b, *, tm=512):
    m = x.shape[0]
    att = pl.BlockSpec((tm, ATT_OUT), lambda i: (i, 0))
    row = pl.BlockSpec((tm, D_MODEL), lambda i: (i, 0))
    vec = pl.BlockSpec((1, D_MODEL), lambda i: (0, 0))
    return pl.pallas_call(
        _merge_kernel,
        grid=(m // tm,),
        in_specs=[att] * 6 + [pl.BlockSpec((ATT_OUT, D_MODEL), lambda i: (0, 0)),
                              pl.BlockSpec((tm, D_MODEL), lambda i: (i, 1)), vec, row, row,
                              pl.BlockSpec((D_MODEL, D_MODEL), lambda i: (0, 0)), row, vec, vec],
        out_specs=row,
        out_shape=jax.ShapeDtypeStruct((m, D_MODEL), F32),
        compiler_params=_params("parallel"),
        name="merge_ln1",
    )(*att_o, *att_l, w_att, gates, b_gate[1:2], p_ssd, p_s5, w_out, x, ln_g.reshape(1, -1), ln_b.reshape(1, -1))


def _router_kernel(x_ref, wt_ref, b_ref, idx_ref, gate_ref, rank_ref, count_ref, before_ref):
    tm = x_ref.shape[0]

    @pl.when(pl.program_id(0) == 0)
    def _():
        count_ref[...] = jnp.zeros_like(count_ref)
        earlier = lax.broadcasted_iota(jnp.int32, (tm, tm), 0) < lax.broadcasted_iota(jnp.int32, (tm, tm), 1)
        before_ref[...] = earlier.astype(before_ref.dtype)

    x = x_ref[...]
    wt = wt_ref[...]
    xh = x.astype(BF16)
    xl = (x - xh.astype(F32)).astype(BF16)
    wh = wt.astype(BF16)
    wl = (wt - wh.astype(F32)).astype(BF16)
    nt = (((1,), (1,)), ((), ()))
    logits = (lax.dot_general(wh, xh, nt, preferred_element_type=F32)
              + lax.dot_general(wl, xh, nt, preferred_element_type=F32)
              + lax.dot_general(wh, xl, nt, preferred_element_type=F32)) + b_ref[...]
    eid = lax.broadcasted_iota(jnp.int32, logits.shape, 0)
    vals, idxs = [], []
    for _ in range(MOE_TOP_K):
        mx = jnp.max(logits, axis=0, keepdims=True)
        sel = jnp.min(jnp.where(logits == mx, eid, MOE_EXPERTS), axis=0, keepdims=True)
        vals.append(mx)
        idxs.append(sel)
        logits = jnp.where(eid == sel, -jnp.inf, logits)
    es = [jnp.exp(v - vals[0]) for v in vals]
    den = es[0] + es[1] + es[2] + es[3]
    for k in range(MOE_TOP_K):
        idx_ref[k:k + 1, :] = idxs[k]
        gate_ref[k:k + 1, :] = es[k] / den
    chosen = [eid == idxs[k] for k in range(MOE_TOP_K)]
    stacked = jnp.concatenate([c.astype(before_ref.dtype) for c in chosen], axis=0)
    earlier = jnp.dot(stacked, before_ref[...], preferred_element_type=F32)
    seen = count_ref[...]
    for k in range(MOE_TOP_K):
        pos = earlier[k * MOE_EXPERTS:(k + 1) * MOE_EXPERTS] + seen
        rank_ref[k:k + 1, :] = jnp.sum(jnp.where(chosen[k], pos, 0.0), axis=0, keepdims=True).astype(jnp.int32)
        seen = seen + jnp.sum(chosen[k].astype(F32), axis=1, keepdims=True)
    count_ref[...] = seen


def _router(x, router_w, router_b, *, tm=1024):
    m = x.shape[0]
    out = pl.BlockSpec((MOE_TOP_K, tm), lambda i: (0, i))
    return pl.pallas_call(
        _router_kernel,
        grid=(m // tm,),
        in_specs=[pl.BlockSpec((tm, D_MODEL), lambda i: (i, 0)),
                  pl.BlockSpec((MOE_EXPERTS, D_MODEL), lambda i: (0, 0)),
                  pl.BlockSpec((MOE_EXPERTS, 1), lambda i: (0, 0))],
        out_specs=[out, out, out, pl.BlockSpec((MOE_EXPERTS, 1), lambda i: (0, 0))],
        out_shape=[jax.ShapeDtypeStruct((MOE_TOP_K, m), jnp.int32), jax.ShapeDtypeStruct((MOE_TOP_K, m), F32),
                   jax.ShapeDtypeStruct((MOE_TOP_K, m), jnp.int32), jax.ShapeDtypeStruct((MOE_EXPERTS, 1), F32)],
        scratch_shapes=[pltpu.VMEM((tm, tm), BF16)],
        compiler_params=_params("arbitrary"),
        name="router",
    )(x, router_w.T, router_b.reshape(-1, 1))


def _expert_kernel(be_ref, used_ref, x_ref, wgu_ref, bgu_ref, wdn_ref, bdn_ref, o_ref, wgu_lo, wdn_lo):
    i = pl.program_id(0)
    used = i < used_ref[0]

    @pl.when(used & ((i == 0) | (be_ref[i] != be_ref[jnp.maximum(i - 1, 0)])))
    def _():
        wgu_lo[...] = wgu_ref[...].astype(wgu_lo.dtype)
        wdn_lo[...] = wdn_ref[...].astype(wdn_lo.dtype)

    @pl.when(used)
    def _():
        hu = jnp.dot(x_ref[...].astype(BF16), wgu_lo[...], preferred_element_type=F32) + bgu_ref[...]
        x_glu = jnp.minimum(hu[:, :MOE_FF], SWIGLU_LIMIT)
        x_lin = jnp.clip(hu[:, MOE_FF:], -SWIGLU_LIMIT, SWIGLU_LIMIT)
        act = x_glu * jax.nn.sigmoid(SWIGLU_ALPHA * x_glu) * (x_lin + 1.0)
        o_ref[...] = jnp.dot(act.astype(BF16), wdn_lo[...], preferred_element_type=F32) + bdn_ref[...]

    @pl.when(jnp.logical_not(used))
    def _():
        o_ref[...] = jnp.zeros_like(o_ref)


def _experts(xs, block_expert, n_used, w_gu, b_gu, w_dn, b_dn, layer):
    n_rows = xs.shape[0]
    n_blocks = n_rows // MOE_ROWS
    grid_spec = pltpu.PrefetchScalarGridSpec(
        num_scalar_prefetch=2,
        grid=(n_blocks,),
        in_specs=[
            pl.BlockSpec((MOE_ROWS, D_MODEL), lambda i, be, nu: (i, 0)),
            pl.BlockSpec((None, None, D_MODEL, 2 * MOE_FF), lambda i, be, nu: (layer, be[i], 0, 0)),
            pl.BlockSpec((None, 1, 2 * MOE_FF), lambda i, be, nu: (be[i], 0, 0)),
            pl.BlockSpec((None, None, MOE_FF, D_MODEL), lambda i, be, nu: (layer, be[i], 0, 0)),
            pl.BlockSpec((None, 1, D_MODEL), lambda i, be, nu: (be[i], 0, 0)),
        ],
        out_specs=pl.BlockSpec((MOE_ROWS, D_MODEL), lambda i, be, nu: (i, 0)),
        scratch_shapes=[pltpu.VMEM((D_MODEL, 2 * MOE_FF), BF16), pltpu.VMEM((MOE_FF, D_MODEL), BF16)],
    )
    return pl.pallas_call(
        _expert_kernel,
        grid_spec=grid_spec,
        out_shape=jax.ShapeDtypeStruct((n_rows, D_MODEL), F32),
        compiler_params=_params("arbitrary"),
        name="experts",
    )(block_expert, n_used, xs, w_gu, b_gu.reshape(MOE_EXPERTS, 1, -1), w_dn, b_dn.reshape(MOE_EXPERTS, 1, -1))


def _combine_kernel(x_ref, y0_ref, y1_ref, y2_ref, y3_ref, gate_ref, g_ref, b_ref, o_ref):
    ffn = gate_ref[:, 0:1] * y0_ref[...]
    for k, y_ref in enumerate((y1_ref, y2_ref, y3_ref), start=1):
        ffn = ffn + gate_ref[:, k:k + 1] * y_ref[...]
    o_ref[...] = _layer_norm(ALPHA * x_ref[...] + ffn, g_ref[...], b_ref[...])


def _combine(x, y_rows, gate, ln_g, ln_b, *, tm=512):
    m = x.shape[0]
    row = pl.BlockSpec((tm, D_MODEL), lambda i: (i, 0))
    vec = pl.BlockSpec((1, D_MODEL), lambda i: (0, 0))
    return pl.pallas_call(
        _combine_kernel,
        grid=(m // tm,),
        in_specs=[row] * (1 + MOE_TOP_K) + [pl.BlockSpec((tm, MOE_TOP_K), lambda i: (i, 0)), vec, vec],
        out_specs=row,
        out_shape=jax.ShapeDtypeStruct((m, D_MODEL), F32),
        compiler_params=_params("parallel"),
        name="combine_ln2",
    )(x, *y_rows, gate, ln_g.reshape(1, -1), ln_b.reshape(1, -1))


def _moe(x, router_w, router_b, w_gu, b_gu, w_dn, b_dn, ln_g, ln_b, layer):
    ntok = x.shape[0]
    idx_t, gate_t, rank_t, counts = _router(x, router_w, router_b)
    gate = gate_t.T
    n_assign = ntok * MOE_TOP_K
    counts = counts.reshape(-1).astype(jnp.int32)
    padded = (counts + MOE_ROWS - 1) // MOE_ROWS * MOE_ROWS
    pend = jnp.cumsum(padded)
    pstart = pend - padded
    experts = jnp.arange(MOE_EXPERTS, dtype=jnp.int32)

    def lookup(table, idx):
        return jnp.sum(jnp.where(idx[..., None] == experts, table, 0), axis=-1)

    dest_t = lookup(pstart, idx_t) + rank_t
    n_blocks = n_assign // MOE_ROWS + MOE_EXPERTS
    n_pad = MOE_EXPERTS * MOE_ROWS
    block_start = jnp.arange(n_blocks, dtype=jnp.int32) * MOE_ROWS
    block_expert = jnp.sum((pend[None, :] <= block_start[:, None]).astype(jnp.int32), axis=1)
    block_expert = jnp.minimum(block_expert, MOE_EXPERTS - 1)
    n_used = (pend[-1:] // MOE_ROWS).astype(jnp.int32)
    gap = padded - counts
    gap_end = jnp.cumsum(gap)
    pad_id = jnp.arange(n_pad, dtype=jnp.int32)
    pad_expert = jnp.sum((gap_end[None, :] <= pad_id[:, None]).astype(jnp.int32), axis=1)
    tail_of = lookup(pstart + counts - (gap_end - gap), jnp.minimum(pad_expert, MOE_EXPERTS - 1)) + pad_id
    pad_slot = jnp.where(pad_expert < MOE_EXPERTS, tail_of, pend[-1] + pad_id - gap_end[-1])
    keys = jnp.concatenate([dest_t.reshape(-1), pad_slot])
    toks = jnp.concatenate([jnp.tile(jnp.arange(ntok, dtype=jnp.int32), MOE_TOP_K), jnp.full((n_pad,), ntok, jnp.int32)])
    _, slot_tok = lax.sort_key_val(keys, toks)
    x_pad = jnp.concatenate([x, jnp.zeros((1, x.shape[1]), x.dtype)], axis=0)
    xs = x_pad[slot_tok]
    ys = _experts(xs, block_expert, n_used, w_gu, b_gu, w_dn, b_dn, layer)
    y_rows = [ys[dest_t[k]] for k in range(MOE_TOP_K)]
    return _combine(x, y_rows, gate, ln_g, ln_b)


def _layer(x, w_in, b_gate, ssd_conv_w, ssd_conv_b, ssd_a_log, ssd_dt_bias, ssd_d, ssd_norm_w,
           s5_a_re, s5_a_im, s5_log_step, s5_b_re, s5_b_im, s5_c_re, s5_c_im, s5_d, s5_glu_w1, s5_glu_w2,
           w_br_ssd, w_br_attn, w_br_s5, w_out, ln1_g, ln1_b,
           router_w, router_b, b_gu, b_dn, ln2_g, ln2_b, *, layer, w_gu, w_dn):
    bsz, seqlen, d = x.shape
    ntok = bsz * seqlen
    xf = x.reshape(ntok, d)
    offs = [0]
    for s in IN_SIZES:
        offs.append(offs[-1] + s)

    def seg(k):
        return w_in[:, offs[k]:offs[k + 1]].astype(BF16)

    xb = xf.astype(BF16)
    gates = _matmul(xb, seg(0), tm=1024, tn=1024, name="proj_gates")
    z = _matmul(xb, seg(1), tm=1024, tn=1024, name="proj_z")
    u = _matmul_tiles(xb, seg(5), tm=512, name="proj_u")

    xbc_act = _xbc_conv(xb, seg(2), ssd_conv_w, ssd_conv_b, seqlen).reshape(bsz, seqlen, SSD_CONV_CH)
    decay = _ssd_decay(xb, seg(3), ssd_a_log, ssd_dt_bias)
    y_ssd = _ssd_scan(xbc_act, *decay, ssd_d)
    p_ssd = _ssd_project(y_ssd.reshape(ntok, SSD_INNER), z, ssd_norm_w, w_br_ssd.astype(BF16), gates, b_gate)

    w_qkv = w_in[:, offs[4]:offs[5]].reshape(d, 3, ATT_GROUPS, ATT_OUT)
    att_o, att_l = [], []
    for g, (window, dil) in enumerate(ATT_PATTERNS):
        assert (window // 2) // dil == ATT_RADIUS
        ls = seqlen // dil
        x_cls = xb.reshape(bsz, ls, dil, d).transpose(0, 2, 1, 3).reshape(ntok, d)
        qkv_g = _matmul(x_cls, w_qkv[:, :, g].reshape(d, 3 * ATT_OUT).astype(BF16), tm=1024, tn=3 * ATT_OUT,
                        name=f"proj_qkv{g}")
        o, lse = _attention_group(qkv_g.reshape(bsz * dil, ls, 3 * ATT_OUT), g, dil)
        att_o.append(o.reshape(bsz, dil, ls, ATT_OUT).transpose(0, 2, 1, 3).reshape(ntok, ATT_OUT))
        att_l.append(lse.reshape(bsz, dil, ls, ATT_OUT).transpose(0, 2, 1, 3).reshape(ntok, ATT_OUT))

    h_s5 = _s5_mix(u, seqlen, s5_a_re, s5_a_im, s5_log_step, s5_b_re, s5_b_im, s5_c_re, s5_c_im, s5_d)
    p_s5 = _s5_project(h_s5, s5_glu_w1.astype(BF16), s5_glu_w2.astype(BF16), w_br_s5.astype(BF16), gates, b_gate)

    x1 = _merge(att_o, att_l, w_br_attn.astype(BF16), gates, b_gate, p_ssd, p_s5, w_out.astype(BF16),
                xf, ln1_g, ln1_b)
    x2 = _moe(x1, router_w, router_b, w_gu, b_gu, w_dn, b_dn, ln2_g, ln2_b, layer)
    return x2.reshape(bsz, seqlen, d)


def kernel(x, w_in, b_gate, ssd_conv_w, ssd_conv_b, ssd_a_log, ssd_dt_bias, ssd_d, ssd_norm_w, s5_a_re, s5_a_im, s5_log_step, s5_b_re, s5_b_im, s5_c_re, s5_c_im, s5_d, s5_glu_w1, s5_glu_w2, w_br_ssd, w_br_attn, w_br_s5, w_out, ln1_g, ln1_b, router_w, router_b, exp_w_gate_up, exp_b_gate_up, exp_w_down, exp_b_down, ln2_g, ln2_b):
    per_layer = (w_in, b_gate, ssd_conv_w, ssd_conv_b, ssd_a_log, ssd_dt_bias, ssd_d, ssd_norm_w, s5_a_re, s5_a_im,
                 s5_log_step, s5_b_re, s5_b_im, s5_c_re, s5_c_im, s5_d, s5_glu_w1, s5_glu_w2, w_br_ssd, w_br_attn,
                 w_br_s5, w_out, ln1_g, ln1_b, router_w, router_b, exp_b_gate_up, exp_b_down, ln2_g, ln2_b)
    for layer in range(DEPTH):
        x = _layer(x, *[p[layer] for p in per_layer], layer=layer, w_gu=exp_w_gate_up, w_dn=exp_w_down)
    return x
```

```python
import functools
import math

import jax
import jax.numpy as jnp
from jax import lax
from jax.experimental import pallas as pl
from jax.experimental.pallas import tpu as pltpu

F32 = jnp.float32
BF16 = jnp.bfloat16

D_MODEL = 1024
DEPTH = 4
ALPHA = (2 * DEPTH) ** 0.25
LN_EPS = 1e-5

SSD_INNER = 2 * D_MODEL
SSD_HEAD_DIM = 64
SSD_GROUPS = 8
SSD_HPG = 4
SSD_HEADS = SSD_GROUPS * SSD_HPG
SSD_STATE = 128
SSD_CONV = 5
SSD_CONV_CH = SSD_INNER + 2 * SSD_GROUPS * SSD_STATE
SSD_CHUNK = 128
SSD_SEQ_BLOCK = 1024
CONV_HALO_ROWS = 16

ATT_HEAD_DIM = 64
ATT_PATTERNS = ((128, 1), (512, 4), (2048, 16))
ATT_GROUPS = 3
ATT_HPG = 6
ATT_HEADS = ATT_GROUPS * ATT_HPG
ATT_WIDTH = ATT_HEADS * ATT_HEAD_DIM
ATT_OUT = ATT_HPG * ATT_HEAD_DIM
ATT_NEG = -1e30
ATT_RADIUS = 64
ATT_QSUB = 128

S5_CH = 16
S5_STATE = 64
S5_WIDTH = 9 * D_MODEL // 8
S5_GROUPS = S5_WIDTH // S5_CH
S5_TC = 16
S5_ROW = S5_TC * S5_CH
S5_SCAN_GROUPS = 8
S5_LANE_GROUPS = 128 // S5_CH
S5_TILES = S5_GROUPS // S5_LANE_GROUPS
S5_ROW_BLOCK = 256
S5_RELAYOUT_ROWS = 16

N_BRANCHES = 3
IN_SIZES = (N_BRANCHES * D_MODEL, SSD_INNER, SSD_CONV_CH, 2 * SSD_HEADS, 3 * ATT_WIDTH, S5_WIDTH)

MOE_EXPERTS = 32
MOE_TOP_K = 4
MOE_FF = D_MODEL
SWIGLU_LIMIT = 7.0
SWIGLU_ALPHA = 1.702
MOE_ROWS = 256

VMEM_LIMIT = 56 * 1024 * 1024


def _params(*sem):
    return pltpu.CompilerParams(dimension_semantics=sem, vmem_limit_bytes=VMEM_LIMIT)


def _mm_kernel(x_ref, w_ref, o_ref):
    o_ref[...] = jnp.dot(x_ref[...].astype(BF16), w_ref[...], preferred_element_type=F32).astype(o_ref.dtype)


def _matmul(x, w, *, tm, tn, name, out_dtype=F32):
    m, k = x.shape
    n = w.shape[1]
    return pl.pallas_call(
        _mm_kernel,
        grid=(n // tn, m // tm),
        in_specs=[pl.BlockSpec((tm, k), lambda j, i: (i, 0)), pl.BlockSpec((k, tn), lambda j, i: (0, j))],
        out_specs=pl.BlockSpec((tm, tn), lambda j, i: (i, j)),
        out_shape=jax.ShapeDtypeStruct((m, n), out_dtype),
        compiler_params=_params("parallel", "parallel"),
        name=name,
    )(x, w)


def _mm_tiles_kernel(x_ref, w_ref, o_ref):
    acc = jnp.dot(x_ref[...].astype(BF16), w_ref[...], preferred_element_type=F32)
    for t in range(o_ref.shape[0]):
        o_ref[t] = acc[:, 128 * t:128 * (t + 1)]


def _matmul_tiles(x, w, *, tm, name):
    m, k = x.shape
    n = w.shape[1]
    nt = n // 128
    return pl.pallas_call(
        _mm_tiles_kernel,
        grid=(m // tm,),
        in_specs=[pl.BlockSpec((tm, k), lambda i: (i, 0)), pl.BlockSpec((k, n), lambda i: (0, 0))],
        out_specs=pl.BlockSpec((nt, tm, 128), lambda i: (0, i, 0)),
        out_shape=jax.ShapeDtypeStruct((nt, m, 128), F32),
        compiler_params=_params("parallel"),
        name=name,
    )(x, w)


def _split3(a):
    a1 = a.astype(BF16)
    r1 = a - a1.astype(F32)
    a2 = r1.astype(BF16)
    a3 = (r1 - a2.astype(F32)).astype(BF16)
    return a1, a2, a3


def _decay_kernel(x_ref, w_ref, wt_ref, alog_r_ref, alog_c_ref, bias_r_ref, bias_c_ref,
                  acs_c_ref, wend_c_ref, acs_r_ref, delta_r_ref):
    q = SSD_CHUNK
    nh = SSD_HEADS
    xb = x_ref[...].astype(BF16)
    dt_c = jnp.dot(xb, w_ref[...], preferred_element_type=F32)
    dt_r = lax.dot_general(wt_ref[...], xb, (((1,), (1,)), ((), ())), preferred_element_type=F32)
    delta_c = jax.nn.softplus(dt_c + bias_r_ref[...])
    delta_r = jax.nn.softplus(dt_r + bias_c_ref[...])
    a_c = -jnp.exp(alog_r_ref[...]) * delta_c
    a_r = -jnp.exp(alog_c_ref[...]) * delta_r
    delta_r_ref[...] = delta_r
    row = lax.broadcasted_iota(jnp.int32, (q, q), 0)
    col = lax.broadcasted_iota(jnp.int32, (q, q), 1)
    lower = (col <= row).astype(BF16)
    upper = (col >= row).astype(BF16)
    fwd_lane = lax.broadcasted_iota(jnp.int32, (q, dt_c.shape[1]), 1) < nh
    fwd_row = lax.broadcasted_iota(jnp.int32, (2 * nh, q), 0) < nh
    for c in range(x_ref.shape[0] // q):
        rows = slice(c * q, (c + 1) * q)
        pieces = _split3(a_c[rows])
        prefix = sum(jnp.dot(lower, p, preferred_element_type=F32) for p in pieces)
        suffix = sum(jnp.dot(upper, p, preferred_element_type=F32) for p in pieces)
        acs = jnp.where(fwd_lane, prefix, suffix)
        total = jnp.where(fwd_lane[0:1], prefix[q - 1:q], suffix[0:1])
        acs_c_ref[rows, :] = acs
        wend_c_ref[rows, :] = delta_c[rows] * jnp.exp(total - acs)
        pieces = _split3(a_r[:, rows])
        prefix = sum(jnp.dot(p, upper, preferred_element_type=F32) for p in pieces)
        suffix = sum(jnp.dot(p, lower, preferred_element_type=F32) for p in pieces)
        acs_r_ref[:, rows] = jnp.where(fwd_row, prefix, suffix)


def _ssd_decay(x, w_dt, a_log, dt_bias, *, tm=1024):
    m, k = x.shape
    n = w_dt.shape[1]
    lanes = 128
    pad = lanes - n
    w_pad = jnp.pad(w_dt, ((0, 0), (0, pad)))
    alog = a_log.reshape(-1)
    bias = dt_bias.reshape(-1)
    col = pl.BlockSpec((tm, lanes), lambda i: (i, 0))
    rowf = pl.BlockSpec((n, tm), lambda i: (0, i))
    return pl.pallas_call(
        _decay_kernel,
        grid=(m // tm,),
        in_specs=[pl.BlockSpec((tm, k), lambda i: (i, 0)), pl.BlockSpec((k, lanes), lambda i: (0, 0)),
                  pl.BlockSpec((n, k), lambda i: (0, 0)),
                  pl.BlockSpec((1, lanes), lambda i: (0, 0)), pl.BlockSpec((n, 1), lambda i: (0, 0)),
                  pl.BlockSpec((1, lanes), lambda i: (0, 0)), pl.BlockSpec((n, 1), lambda i: (0, 0))],
        out_specs=[col, col, rowf, rowf],
        out_shape=[jax.ShapeDtypeStruct((m, lanes), F32)] * 2 + [jax.ShapeDtypeStruct((n, m), F32)] * 2,
        compiler_params=_params("parallel"),
        name="ssd_decay",
    )(x, w_pad, w_dt.T, jnp.pad(alog, (0, pad)).reshape(1, lanes), alog.reshape(n, 1),
      jnp.pad(bias, (0, pad)).reshape(1, lanes), bias.reshape(n, 1))


def _xbc_conv_kernel(xm_ref, xp_ref, xn_ref, w_ref, cw_ref, cb_ref, o_ref, buf, *, tiles_per_seq):
    pos = pl.program_id(1) % tiles_per_seq
    ts = xm_ref.shape[0]
    halo = xp_ref.shape[0]
    half = SSD_CONV // 2
    rows = jnp.concatenate([xp_ref[...], xm_ref[...], xn_ref[...]], axis=0).astype(BF16)
    buf[...] = jnp.dot(rows, w_ref[...], preferred_element_type=F32)

    @pl.when(pos == 0)
    def _():
        buf[0:halo, :] = jnp.zeros((halo, buf.shape[1]), F32)

    @pl.when(pos == tiles_per_seq - 1)
    def _():
        buf[halo + ts:2 * halo + ts, :] = jnp.zeros((halo, buf.shape[1]), F32)

    acc = cb_ref[...] + cw_ref[0:1, :] * buf[halo - half:halo - half + ts, :]
    for k in range(1, SSD_CONV):
        acc = acc + cw_ref[k:k + 1, :] * buf[halo - half + k:halo - half + k + ts, :]
    o_ref[...] = (acc * jax.nn.sigmoid(acc)).astype(o_ref.dtype)


def _xbc_conv(x, w, conv_w, conv_b, seqlen, *, tm=512, tn=1024, out_dtype=F32):
    m, k = x.shape
    n = w.shape[1]
    halo = CONV_HALO_ROWS
    nhalo = m // halo
    return pl.pallas_call(
        functools.partial(_xbc_conv_kernel, tiles_per_seq=seqlen // tm),
        grid=(n // tn, m // tm),
        in_specs=[
            pl.BlockSpec((tm, k), lambda j, i: (i, 0)),
            pl.BlockSpec((halo, k), lambda j, i: (jnp.maximum(i * (tm // halo) - 1, 0), 0)),
            pl.BlockSpec((halo, k), lambda j, i: (jnp.minimum((i + 1) * (tm // halo), nhalo - 1), 0)),
            pl.BlockSpec((k, tn), lambda j, i: (0, j)),
            pl.BlockSpec((SSD_CONV, tn), lambda j, i: (0, j)),
            pl.BlockSpec((1, tn), lambda j, i: (0, j)),
        ],
        out_specs=pl.BlockSpec((tm, tn), lambda j, i: (i, j)),
        out_shape=jax.ShapeDtypeStruct((m, n), out_dtype),
        scratch_shapes=[pltpu.VMEM((tm + 2 * halo, tn), F32)],
        compiler_params=_params("parallel", "parallel"),
        name="proj_xbc_conv",
    )(x, x, x, w, conv_w, conv_b.reshape(1, n))


def _per_head(cols):
    rows = cols[0].shape[0]
    first = lax.broadcasted_iota(jnp.int32, (rows, 2 * SSD_HEAD_DIM), 1) < SSD_HEAD_DIM
    return jnp.concatenate([jnp.where(first, cols[0], cols[1]), jnp.where(first, cols[2], cols[3])], axis=1)


def _group_lanes(ref, rows, g):
    lanes = ref.shape[1]
    return pltpu.roll(ref[rows, :], (lanes - SSD_HPG * g) % lanes, 1)


def _group_rows(ref, first, cols, g):
    tile = 8
    per_tile = tile // SSD_HPG
    base = pl.multiple_of(first + (g // per_tile) * tile, tile)
    return pltpu.roll(ref[pl.ds(base, tile), cols], (tile - SSD_HPG * (g % per_tile)) % tile, 0)


def _head_selector(lanes, first):
    width = SSD_HPG * SSD_HEAD_DIM
    src = lax.broadcasted_iota(jnp.int32, (lanes, width), 0)
    dst = lax.broadcasted_iota(jnp.int32, (lanes, width), 1) // SSD_HEAD_DIM
    return (src == dst + first).astype(BF16)


def _expand_heads(cols, selector):
    return sum(jnp.dot(p, selector, preferred_element_type=F32) for p in _split3(cols))


def _weighted_inputs(xs_ref, wend_ref, g, first, xw_scr):
    rows = slice(0, xs_ref.shape[0])
    wend = _group_lanes(wend_ref, rows, g)
    xw_scr[...] = (xs_ref[...] * _expand_heads(wend, _head_selector(wend_ref.shape[1], first))).astype(xw_scr.dtype)


def _ssd_back_kernel(xs_ref, bm_ref, acs_ref, wend_ref, out_ref, carried, xw_scr):
    q = SSD_CHUNK
    nchunk = xs_ref.shape[0] // q
    g = pl.program_id(1)
    bwd = SSD_HEADS

    @pl.when(pl.program_id(2) == 0)
    def _():
        carried[...] = jnp.zeros_like(carried)

    _weighted_inputs(xs_ref, wend_ref, g, bwd, xw_scr)

    def chunk(ci, carry):
        c = nchunk - 1 - ci
        rows = pl.ds(pl.multiple_of(c * q, q), q)
        first_row = pl.ds(pl.multiple_of(c * q, q), 8)
        total = _group_lanes(acs_ref, first_row, g)[0:1]
        states = lax.dot_general(bm_ref[rows, :].astype(BF16), xw_scr[rows, :], (((0,), (0,)), ((), ())),
                                 preferred_element_type=F32)
        prev = carried[...]
        out_ref[c] = prev.astype(out_ref.dtype)
        carried[...] = jnp.exp(_per_head([total[:, bwd + r:bwd + r + 1] for r in range(SSD_HPG)])) * prev + states
        return carry

    lax.fori_loop(0, nchunk, chunk, 0, unroll=4)


def _ssd_main_kernel(xs_ref, bm_ref, cm_ref, acs_c_ref, wend_c_ref, acs_r_ref, delta_r_ref, back_ref,
                     dskip_ref, y_ref, carried):
    q = SSD_CHUNK
    nchunk = xs_ref.shape[0] // q
    g = pl.program_id(1)
    bwd = SSD_HEADS
    hd = SSD_HEAD_DIM

    @pl.when(pl.program_id(2) == 0)
    def _():
        carried[...] = jnp.zeros_like(carried)

    row = lax.broadcasted_iota(jnp.int32, (q, q), 0)
    col = lax.broadcasted_iota(jnp.int32, (q, q), 1)
    lower = col <= row
    upper = col >= row
    d_wide = _per_head([dskip_ref[:, r:r + 1] for r in range(SSD_HPG)])
    selector = _head_selector(wend_c_ref.shape[1], 0)
    first = lax.broadcasted_iota(jnp.int32, (q, 2 * hd), 1) < hd

    def per_head_lanes(blocks):
        return jnp.concatenate([jnp.where(first, blocks[0], blocks[1]), jnp.where(first, blocks[2], blocks[3])], axis=1)

    def chunk(c, carry):
        r0 = pl.multiple_of(c * q, q)
        rows = pl.ds(r0, q)
        xs = xs_ref[rows, :]
        bmat = bm_ref[rows, :].astype(BF16)
        cmat = cm_ref[rows, :].astype(BF16)
        acs = _group_lanes(acs_c_ref, rows, g)
        wend = _group_lanes(wend_c_ref, rows, g)
        acs_f = [jnp.broadcast_to(acs[:, r:r + 1], (q, q)) for r in range(SSD_HPG)]
        acs_b = [jnp.broadcast_to(acs[:, bwd + r:bwd + r + 1], (q, q)) for r in range(SSD_HPG)]
        acs_rf = _group_rows(acs_r_ref, 0, rows, g)
        acs_rb = _group_rows(acs_r_ref, bwd, rows, g)
        del_rf = _group_rows(delta_r_ref, 0, rows, g)
        del_rb = _group_rows(delta_r_ref, bwd, rows, g)
        cb = lax.dot_general(cmat, bmat, (((1,), (1,)), ((), ())), preferred_element_type=F32)
        prev = carried[...]
        states_in = jnp.concatenate([prev.astype(BF16), back_ref[c]], axis=1)
        y_off = jnp.dot(cmat, states_in, preferred_element_type=F32)
        e_f = jnp.exp(per_head_lanes(acs_f))
        e_b = jnp.exp(per_head_lanes(acs_b))
        rest = e_f * y_off[:, 0:SSD_HPG * hd] + e_b * y_off[:, SSD_HPG * hd:] + d_wide * xs
        for r in range(SSD_HPG):
            seg_f = acs_f[r] - acs_rf[r:r + 1, :]
            seg_b = acs_b[r] - acs_rb[r:r + 1, :]
            w = (jnp.exp(jnp.where(lower, seg_f, ATT_NEG)) * del_rf[r:r + 1, :]
                 + jnp.exp(jnp.where(upper, seg_b, ATT_NEG)) * del_rb[r:r + 1, :])
            y_diag = jnp.dot((cb * w).astype(BF16), xs[:, r * hd:(r + 1) * hd].astype(BF16),
                             preferred_element_type=F32)
            y_ref[rows, r * hd:(r + 1) * hd] = y_diag + rest[:, r * hd:(r + 1) * hd]
        xw = xs * _expand_heads(wend, selector)
        states = lax.dot_general(bmat, xw.astype(BF16), (((0,), (0,)), ((), ())), preferred_element_type=F32)
        total = acs[q - 1:q]
        carried[...] = jnp.exp(_per_head([total[:, r:r + 1] for r in range(SSD_HPG)])) * prev + states
        return carry

    lax.fori_loop(0, nchunk, chunk, 0, unroll=2)


def _ssd_scan(xs_act, bc_act, acs_c, wend_c, acs_r, delta_r, d_skip):
    bsz, seqlen, _ = xs_act.shape
    sb = min(SSD_SEQ_BLOCK, seqlen)
    nblk = seqlen // sb
    cpb = sb // SSD_CHUNK
    n_x = SSD_HPG * SSD_HEAD_DIM
    b_off = 0
    c_off = SSD_GROUPS
    lanes = acs_c.shape[1]
    nrow = acs_r.shape[0]

    def specs(blk):
        xs = pl.BlockSpec((None, sb, n_x), lambda b_, g, i: (b_, blk(i), g))
        bm = pl.BlockSpec((None, sb, SSD_STATE), lambda b_, g, i: (b_, blk(i), b_off + g))
        cm = pl.BlockSpec((None, sb, SSD_STATE), lambda b_, g, i: (b_, blk(i), c_off + g))
        colf = pl.BlockSpec((sb, lanes), lambda b_, g, i: (b_ * nblk + blk(i), 0))
        rowf = pl.BlockSpec((nrow, sb), lambda b_, g, i: (0, b_ * nblk + blk(i)))
        back = pl.BlockSpec((None, None, cpb, SSD_STATE, n_x), lambda b_, g, i: (b_, g, blk(i), 0, 0))
        return xs, bm, cm, colf, rowf, back

    xs, bm, cm, colf, rowf, back = specs(lambda i: nblk - 1 - i)
    assert SSD_CHUNK == 2 * SSD_HEAD_DIM and SSD_HPG == 4
    state = [pltpu.VMEM((SSD_STATE, n_x), F32), pltpu.VMEM((sb, n_x), BF16)]
    back_states = pl.pallas_call(
        _ssd_back_kernel,
        grid=(bsz, SSD_GROUPS, nblk),
        in_specs=[xs, bm, colf, colf],
        out_specs=back,
        out_shape=jax.ShapeDtypeStruct((bsz, SSD_GROUPS, seqlen // SSD_CHUNK, SSD_STATE, n_x), BF16),
        scratch_shapes=state,
        compiler_params=_params("parallel", "parallel", "arbitrary"),
        name="ssd_back",
    )(xs_act, bc_act, acs_c, wend_c)

    xs, bm, cm, colf, rowf, back = specs(lambda i: i)
    return pl.pallas_call(
        _ssd_main_kernel,
        grid=(bsz, SSD_GROUPS, nblk),
        in_specs=[xs, bm, cm, colf, colf, rowf, rowf, back,
                  pl.BlockSpec((None, 1, SSD_HPG), lambda b_, g, i: (g, 0, 0))],
        out_specs=pl.BlockSpec((None, sb, n_x), lambda b_, g, i: (b_, i, g)),
        out_shape=jax.ShapeDtypeStruct((bsz, seqlen, SSD_INNER), F32),
        scratch_shapes=state[:1],
        compiler_params=_params("parallel", "parallel", "arbitrary"),
        name="ssd_main",
    )(xs_act, bc_act, bc_act, acs_c, wend_c, acs_r, delta_r, back_states,
      d_skip.reshape(SSD_GROUPS, 1, SSD_HPG))


def _attn_kernel(q_ref, ko_ref, kp_ref, kn_ref, vo_ref, vp_ref, vn_ref, o_ref, lse_ref, kcat, vcat,
                 *, dil, group, ls):
    tq = q_ref.shape[0]
    rad = ATT_RADIUS
    qs = ATT_QSUB
    kw = qs + 2 * rad
    i = pl.program_id(1)
    kcat[0:rad, :] = kp_ref[...].astype(BF16)
    kcat[rad:rad + tq, :] = ko_ref[...].astype(BF16)
    kcat[rad + tq:2 * rad + tq, :] = kn_ref[...].astype(BF16)
    vcat[0:rad, :] = vp_ref[...].astype(BF16)
    vcat[rad:rad + tq, :] = vo_ref[...].astype(BF16)
    vcat[rad + tq:2 * rad + tq, :] = vn_ref[...].astype(BF16)

    row = lax.broadcasted_iota(jnp.int32, (qs, kw), 0)
    col = lax.broadcasted_iota(jnp.int32, (qs, kw), 1)
    dist = jnp.abs(row - col + rad)
    in_band = dist <= rad
    dist_f = (dil * dist).astype(F32)
    lane = lax.broadcasted_iota(jnp.int32, (qs, 2 * ATT_HEAD_DIM), 1)
    first_half = lane < ATT_HEAD_DIM
    scale = ATT_HEAD_DIM ** -0.5

    for m in range(tq // qs):
        kpos = i * tq + (m * qs - rad) + col
        valid = in_band & (kpos >= 0) & (kpos < ls)
        for pair in range(ATT_HPG // 2):
            lanes = slice(pair * 2 * ATT_HEAD_DIM, (pair + 1) * 2 * ATT_HEAD_DIM)
            q2 = q_ref[m * qs:(m + 1) * qs, lanes]
            k2 = kcat[m * qs:m * qs + kw, lanes]
            v2 = vcat[m * qs:m * qs + kw, lanes]
            outs, lses = [], []
            for half in range(2):
                head = 2 * pair + half
                slope = 2.0 ** (-8.0 * (head * ATT_GROUPS + group + 1) / ATT_HEADS)
                qm = jnp.where(first_half if half == 0 else ~first_half, q2, 0.0).astype(BF16)
                s = lax.dot_general(qm, k2, (((1,), (1,)), ((), ())), preferred_element_type=F32)
                s = s * scale + (-slope) * dist_f
                s = jnp.where(valid, s, ATT_NEG)
                mx = jnp.max(s, axis=-1, keepdims=True)
                p = jnp.exp(s - mx)
                zsum = jnp.sum(p, axis=-1, keepdims=True)
                pv = jnp.dot(p.astype(BF16), v2, preferred_element_type=F32)
                outs.append(pv / zsum)
                lses.append(mx + jnp.log(zsum))
            o_ref[m * qs:(m + 1) * qs, lanes] = jnp.where(first_half, outs[0], outs[1])
            lse_ref[m * qs:(m + 1) * qs, lanes] = jnp.where(first_half, lses[0], lses[1])


def _attention_group(qkv, group, dil):
    nseq, ls, _ = qkv.shape
    tq = min(512, ls)
    nblk = ls // tq
    halo_per_blk = tq // ATT_RADIUS
    n_halo = ls // ATT_RADIUS

    def own(off):
        return pl.BlockSpec((None, tq, ATT_OUT), lambda s, i: (s, i, off))

    def prev(off):
        return pl.BlockSpec((None, ATT_RADIUS, ATT_OUT), lambda s, i: (s, jnp.maximum(i * halo_per_blk - 1, 0), off))

    def nxt(off):
        return pl.BlockSpec((None, ATT_RADIUS, ATT_OUT),
                            lambda s, i: (s, jnp.minimum((i + 1) * halo_per_blk, n_halo - 1), off))

    out_spec = pl.BlockSpec((None, tq, ATT_OUT), lambda s, i: (s, i, 0))
    return pl.pallas_call(
        functools.partial(_attn_kernel, dil=dil, group=group, ls=ls),
        grid=(nseq, nblk),
        in_specs=[own(0), own(1), prev(1), nxt(1), own(2), prev(2), nxt(2)],
        out_specs=[out_spec, out_spec],
        out_shape=[jax.ShapeDtypeStruct((nseq, ls, ATT_OUT), F32)] * 2,
        scratch_shapes=[pltpu.VMEM((tq + 2 * ATT_RADIUS, ATT_OUT), BF16)] * 2,
        compiler_params=_params("parallel", "parallel"),
        name=f"attn_g{group}",
    )(qkv, qkv, qkv, qkv, qkv, qkv, qkv)


def _s5_tables(a_re, a_im, log_step, b_re, b_im, c_re, c_im):
    hp = lax.Precision.HIGHEST
    tc = S5_TC
    ng, width = S5_GROUPS, S5_ROW
    step = jnp.exp(log_step)[..., None]
    ar, ai = a_re, a_im
    mag = jnp.exp(step * ar)
    abr, abi = mag * jnp.cos(step * ai), mag * jnp.sin(step * ai)
    den = ar * ar + ai * ai
    fr = ((abr - 1.0) * ar + abi * ai) / den
    fi = (abi * ar - (abr - 1.0) * ai) / den
    b_re_t, b_im_t = b_re.transpose(0, 2, 1), b_im.transpose(0, 2, 1)
    bbr = fr[:, :, None, :] * b_re_t[None] - fi[:, :, None, :] * b_im_t[None]
    bbi = fr[:, :, None, :] * b_im_t[None] + fi[:, :, None, :] * b_re_t[None]
    ks = jnp.arange(tc + 1, dtype=F32)
    arg_r, arg_i = step * ar, step * ai
    pk_r = jnp.exp(arg_r[:, :, None, :] * ks[:, None]) * jnp.cos(arg_i[:, :, None, :] * ks[:, None])
    pk_i = jnp.exp(arg_r[:, :, None, :] * ks[:, None]) * jnp.sin(arg_i[:, :, None, :] * ks[:, None])
    pn_r = jnp.exp(arg_r[..., None] * ks) * jnp.cos(arg_i[..., None] * ks)
    pn_i = jnp.exp(arg_r[..., None] * ks) * jnp.sin(arg_i[..., None] * ks)
    c_re_t, c_im_t = c_re.transpose(0, 1, 3, 2), c_im.transpose(0, 1, 3, 2)
    cap_r = (c_re_t[:, :, :, None, :] * pn_r[..., None] - c_im_t[:, :, :, None, :] * pn_i[..., None])
    cap_i = (c_re_t[:, :, :, None, :] * pn_i[..., None] + c_im_t[:, :, :, None, :] * pn_r[..., None])
    cap_r = cap_r.reshape(2, ng, S5_STATE, (tc + 1) * S5_CH)
    cap_i = cap_i.reshape(2, ng, S5_STATE, (tc + 1) * S5_CH)
    taps = (jnp.einsum('dgin,dgnm->dgim', bbr, cap_r, precision=hp)
            - jnp.einsum('dgin,dgnm->dgim', bbi, cap_i, precision=hp))[..., :width]
    fwd_src = jnp.pad(taps[0], ((0, 0), (0, 0), (width, 0)))
    bwd_rev = taps[1].reshape(ng, S5_CH, tc, S5_CH)[:, :, ::-1, :].reshape(ng, S5_CH, width)
    bwd_src = jnp.pad(bwd_rev, ((0, 0), (0, 0), (0, width)))
    toep = jnp.stack([fwd_src[:, :, width - S5_CH * s:2 * width - S5_CH * s]
                      + bwd_src[:, :, S5_CH * (tc - 1 - s):S5_CH * (tc - 1 - s) + width] for s in range(tc)], axis=1)
    toep = toep.reshape(ng, width, width)

    def end_state(d, pw_r, pw_i):
        re = pw_r[:, :, None, :] * bbr[d][:, None] - pw_i[:, :, None, :] * bbi[d][:, None]
        im = pw_r[:, :, None, :] * bbi[d][:, None] + pw_i[:, :, None, :] * bbr[d][:, None]
        return [re.reshape(ng, width, S5_STATE), im.reshape(ng, width, S5_STATE)]

    b_end = jnp.concatenate(end_state(0, pk_r[0][:, tc - 1::-1], pk_i[0][:, tc - 1::-1])
                            + end_state(1, pk_r[1][:, :tc], pk_i[1][:, :tc]), axis=-1)

    def lag_reversed(a):
        return a.reshape(ng, S5_STATE, tc, S5_CH)[:, :, ::-1, :].reshape(ng, S5_STATE, width)

    c_out = jnp.concatenate([cap_r[0][..., S5_CH:], -cap_i[0][..., S5_CH:],
                             lag_reversed(cap_r[1][..., S5_CH:]), -lag_reversed(cap_i[1][..., S5_CH:])], axis=1)
    ptr, pti = pk_r[:, :, tc], pk_i[:, :, tc]
    p_mul = jnp.concatenate([ptr[0], ptr[0], ptr[1], ptr[1]], axis=-1)
    q_mul = jnp.concatenate([-pti[0], pti[0], -pti[1], pti[1]], axis=-1)
    return toep, b_end, c_out, p_mul, q_mul


def _granule_transpose(regs, granule):
    n = len(regs)
    d = n // 2
    while d >= 1:
        low = (granule // d) % 2 == 0
        nxt = list(regs)
        for i in range(n):
            if (i // d) % 2 == 0:
                a, b = regs[i], regs[i + d]
                nxt[i] = jnp.where(low, a, pltpu.roll(b, d * S5_CH, 1))
                nxt[i + d] = jnp.where(low, pltpu.roll(a, 128 - d * S5_CH, 1), b)
        regs = nxt
        d //= 2
    return regs


def _s5_rows_to_groups(u_ref, x_scr):
    rr = S5_RELAYOUT_ROWS
    per = S5_LANE_GROUPS
    granule = lax.broadcasted_iota(jnp.int32, (rr, 128), 1) // S5_CH

    def body(rb, carry):
        base = rb * (rr * S5_TC)
        rows = pl.ds(pl.multiple_of(rb * rr, rr), rr)
        for h in range(S5_TC // per):
            steps = [u_ref[pl.ds(base + per * h + t, rr, stride=S5_TC), :] for t in range(per)]
            for gl, block in enumerate(_granule_transpose(steps, granule)):
                x_scr[gl, rows, 128 * h:128 * (h + 1)] = block
        return carry

    lax.fori_loop(0, x_scr.shape[1] // rr, body, 0)


def _s5_in_kernel(u_ref, toep_ref, bend_ref, y_ref, sf_ref, sb_ref, x_scr):
    half = 2 * S5_STATE
    _s5_rows_to_groups(u_ref, x_scr)
    for gl in range(S5_LANE_GROUPS):
        xb = x_scr[gl].astype(BF16)
        y_ref[gl] = jnp.dot(xb, toep_ref[gl], preferred_element_type=F32)
        s = jnp.dot(xb, bend_ref[gl], preferred_element_type=F32)
        sf_ref[gl] = s[:, 0:half]
        sb_ref[gl] = s[:, half:2 * half]


def _s5_scan_kernel(sf_ref, sb_ref, p_ref, q_ref, hf_ref, hb_ref, sf_sw_ref, sb_sw_ref, *, rows_per_seq):
    n_chain = p_ref.shape[0]
    half = 2 * S5_STATE
    pf, qf = p_ref[:, 0:half], q_ref[:, 0:half]
    pb, qb = p_ref[:, half:2 * half], q_ref[:, half:2 * half]
    qf_sw = pltpu.roll(qf, S5_STATE, 1)
    qb_sw = pltpu.roll(qb, S5_STATE, 1)
    zero = jnp.zeros((n_chain, half), F32)
    piece = 256

    def swap(k, carry):
        rows = pl.ds(pl.multiple_of(k * piece, piece), piece)
        sf_sw_ref[rows, :] = pltpu.roll(sf_ref[rows, :], S5_STATE, 1)
        sb_sw_ref[rows, :] = pltpu.roll(sb_ref[rows, :], S5_STATE, 1)
        return carry

    lax.fori_loop(0, sf_ref.shape[0] // piece, swap, 0)

    def step(c, carry):
        hf, hf_sw, hb, hb_sw = carry
        rows_f = pl.ds(c, n_chain, stride=rows_per_seq)
        rows_b = pl.ds(rows_per_seq - 1 - c, n_chain, stride=rows_per_seq)
        hf_ref[rows_f, :] = hf
        hb_ref[rows_b, :] = hb
        sf = sf_ref[rows_f, :]
        sb = sb_ref[rows_b, :]
        sf_sw = sf_sw_ref[rows_f, :]
        sb_sw = sb_sw_ref[rows_b, :]
        hf_new = pf * hf + qf * hf_sw + sf
        hf_sw_new = pf * hf_sw + qf_sw * hf + sf_sw
        hb_new = pb * hb + qb * hb_sw + sb
        hb_sw_new = pb * hb_sw + qb_sw * hb + sb_sw
        return hf_new, hf_sw_new, hb_new, hb_sw_new

    lax.fori_loop(0, rows_per_seq, step, (zero, zero, zero, zero))


def _s5_out_kernel(hf_ref, hb_ref, cout_ref, y_ref, u_ref, d_ref, o_ref, y_scr):
    half = 2 * S5_STATE
    per = S5_LANE_GROUPS
    for gl in range(per):
        y = y_ref[gl] + jnp.dot(hf_ref[gl].astype(BF16), cout_ref[gl, 0:half, :], preferred_element_type=F32)
        y_scr[gl] = y + jnp.dot(hb_ref[gl].astype(BF16), cout_ref[gl, half:2 * half, :], preferred_element_type=F32)
    rr = S5_RELAYOUT_ROWS
    granule = lax.broadcasted_iota(jnp.int32, (rr, 128), 1) // S5_CH
    d_skip = d_ref[...]

    def body(rb, carry):
        rows = pl.ds(pl.multiple_of(rb * rr, rr), rr)
        base = rb * (rr * S5_TC)
        for h in range(S5_TC // per):
            groups = [y_scr[gl, rows, 128 * h:128 * (h + 1)] for gl in range(per)]
            for t, block in enumerate(_granule_transpose(groups, granule)):
                tok = pl.ds(base + per * h + t, rr, stride=S5_TC)
                o_ref[tok, :] = jax.nn.gelu(block + u_ref[tok, :] * d_skip)
        return carry

    lax.fori_loop(0, y_scr.shape[1] // rr, body, 0)


def _s5_all_tables(a_re, a_im, log_step, b_re, b_im, c_re, c_im):
    toep, b_end, c_out, p_mul, q_mul = jax.vmap(_s5_tables)(a_re, a_im, log_step, b_re, b_im, c_re, c_im)
    return toep.astype(BF16), b_end.astype(BF16), c_out.astype(BF16), p_mul, q_mul


def _s5_mix(u_tiles, seqlen, tables, layer, d_skip):
    ntile, ntok, lanes = u_tiles.shape
    bsz = ntok // seqlen
    rows_per_seq = seqlen // S5_TC
    rows = bsz * rows_per_seq
    rblk = min(S5_ROW_BLOCK, rows)
    per = S5_LANE_GROUPS
    half = 2 * S5_STATE
    toep, b_end, c_out, p_mul, q_mul = tables
    p_mul, q_mul = p_mul[layer], q_mul[layer]
    utile = pl.BlockSpec((None, rblk * S5_TC, lanes), lambda j, i: (j, i, 0))
    ytile = pl.BlockSpec((per, rblk, S5_ROW), lambda j, i: (j, i, 0))
    stile = pl.BlockSpec((per, rblk, half), lambda j, i: (j, i, 0))
    wtile = pl.BlockSpec((None, per, S5_ROW, S5_ROW), lambda j, i: (layer, j, 0, 0))
    state_shape = jax.ShapeDtypeStruct((S5_GROUPS, rows, half), F32)
    grid = (ntile, rows // rblk)
    y_in, s_f, s_b = pl.pallas_call(
        _s5_in_kernel,
        grid=grid,
        in_specs=[utile, wtile, wtile],
        out_specs=[ytile, stile, stile],
        out_shape=[jax.ShapeDtypeStruct((S5_GROUPS, rows, S5_ROW), F32), state_shape, state_shape],
        scratch_shapes=[pltpu.VMEM((per, rblk, S5_ROW), F32)],
        compiler_params=_params("parallel", "parallel"),
        name="s5_in",
    )(u_tiles, toep, b_end)

    gs = S5_SCAN_GROUPS
    n_chain = gs * bsz
    chain_p = jnp.repeat(p_mul, bsz, axis=0)
    chain_q = jnp.repeat(q_mul, bsz, axis=0)
    flat = jax.ShapeDtypeStruct((S5_GROUPS * rows, half), F32)
    srows = pl.BlockSpec((gs * rows, half), lambda t: (t, 0))
    chain = pl.BlockSpec((n_chain, S5_ROW), lambda t: (t, 0))
    h_f, h_b = pl.pallas_call(
        functools.partial(_s5_scan_kernel, rows_per_seq=rows_per_seq),
        grid=(S5_GROUPS // gs,),
        in_specs=[srows, srows, chain, chain],
        out_specs=[srows, srows],
        out_shape=[flat, flat],
        scratch_shapes=[pltpu.VMEM((gs * rows, half), F32)] * 2,
        compiler_params=_params("parallel"),
        name="s5_scan",
    )(s_f.reshape(S5_GROUPS * rows, half), s_b.reshape(S5_GROUPS * rows, half), chain_p, chain_q)

    return pl.pallas_call(
        _s5_out_kernel,
        grid=grid,
        in_specs=[stile, stile, wtile, ytile, utile, pl.BlockSpec((None, 1, lanes), lambda j, i: (j, 0, 0))],
        out_specs=utile,
        out_shape=jax.ShapeDtypeStruct((ntile, ntok, lanes), F32),
        scratch_shapes=[pltpu.VMEM((per, rblk, S5_ROW), F32)],
        compiler_params=_params("parallel", "parallel"),
        name="s5_out",
    )(h_f.reshape(S5_GROUPS, rows, half), h_b.reshape(S5_GROUPS, rows, half), c_out, y_in, u_tiles,
      d_skip.reshape(ntile, 1, lanes))


def _ssd_proj_kernel(y_ref, z_ref, nw_ref, w_ref, gate_ref, bg_ref, o_ref):
    z = z_ref[...]
    y = y_ref[...] * (z * jax.nn.sigmoid(z))
    y = y * lax.rsqrt(jnp.mean(y * y, axis=-1, keepdims=True) + LN_EPS) * nw_ref[...]
    p = jnp.dot(y.astype(BF16), w_ref[...], preferred_element_type=F32)
    o_ref[...] = jax.nn.sigmoid(gate_ref[...] + bg_ref[...]) * p


def _ssd_project(y, z, norm_w, w_br, gates, b_gate, *, tm=512):
    m = y.shape[0]
    wide = pl.BlockSpec((tm, SSD_INNER), lambda i: (i, 0))
    return pl.pallas_call(
        _ssd_proj_kernel,
        grid=(m // tm,),
        in_specs=[wide, wide, pl.BlockSpec((1, SSD_INNER), lambda i: (0, 0)),
                  pl.BlockSpec((SSD_INNER, D_MODEL), lambda i: (0, 0)),
                  pl.BlockSpec((tm, D_MODEL), lambda i: (i, 0)),
                  pl.BlockSpec((1, D_MODEL), lambda i: (0, 0))],
        out_specs=pl.BlockSpec((tm, D_MODEL), lambda i: (i, 0)),
        out_shape=jax.ShapeDtypeStruct((m, D_MODEL), F32),
        compiler_params=_params("parallel"),
        name="ssd_project",
    )(y, z, norm_w.reshape(1, -1), w_br, gates, b_gate[0:1])


def _s5_proj_kernel(h_ref, w1_ref, w2_ref, w_ref, gate_ref, bg_ref, o_ref):
    hb = jnp.concatenate([h_ref[t] for t in range(h_ref.shape[0])], axis=1).astype(BF16)
    a = jnp.dot(hb, w1_ref[...], preferred_element_type=F32)
    b = jnp.dot(hb, w2_ref[...], preferred_element_type=F32)
    p = jnp.dot((a * jax.nn.sigmoid(b)).astype(BF16), w_ref[...], preferred_element_type=F32)
    o_ref[...] = jax.nn.sigmoid(gate_ref[...] + bg_ref[...]) * p


def _s5_project(h_tiles, w1, w2, w_br, gates, b_gate, *, tm=512):
    ntile, m, lanes = h_tiles.shape
    sq = pl.BlockSpec((S5_WIDTH, S5_WIDTH), lambda i: (0, 0))
    return pl.pallas_call(
        _s5_proj_kernel,
        grid=(m // tm,),
        in_specs=[pl.BlockSpec((ntile, tm, lanes), lambda i: (0, i, 0)), sq, sq,
                  pl.BlockSpec((S5_WIDTH, D_MODEL), lambda i: (0, 0)),
                  pl.BlockSpec((tm, D_MODEL), lambda i: (i, 2)),
                  pl.BlockSpec((1, D_MODEL), lambda i: (0, 0))],
        out_specs=pl.BlockSpec((tm, D_MODEL), lambda i: (i, 0)),
        out_shape=jax.ShapeDtypeStruct((m, D_MODEL), F32),
        compiler_params=_params("parallel"),
        name="s5_project",
    )(h_tiles, w1, w2, w_br, gates, b_gate[2:3])


def _layer_norm(v, g, b):
    mu = jnp.mean(v, axis=-1, keepdims=True)
    c = v - mu
    var = jnp.mean(c * c, axis=-1, keepdims=True)
    return c * lax.rsqrt(var + LN_EPS) * g + b


def _merge_kernel(o0_ref, o1_ref, o2_ref, l0_ref, l1_ref, l2_ref, w_att_ref, gate_ref, bg_ref, pssd_ref, ps5_ref,
                  w_out_ref, x_ref, g_ref, b_ref, o_ref):
    l0, l1, l2 = l0_ref[...], l1_ref[...], l2_ref[...]
    mx = jnp.maximum(jnp.maximum(l0, l1), l2)
    e0, e1, e2 = jnp.exp(l0 - mx), jnp.exp(l1 - mx), jnp.exp(l2 - mx)
    den = e0 + e1 + e2
    att = (e0 / den) * o0_ref[...] + (e1 / den) * o1_ref[...] + (e2 / den) * o2_ref[...]
    p_att = jnp.dot(att.astype(BF16), w_att_ref[...], preferred_element_type=F32)
    merged = pssd_ref[...] + jax.nn.sigmoid(gate_ref[...] + bg_ref[...]) * p_att + ps5_ref[...]
    mix = jnp.dot(merged.astype(BF16), w_out_ref[...], preferred_element_type=F32)
    o_ref[...] = _layer_norm(ALPHA * x_ref[...] + mix, g_ref[...], b_ref[...])


def _merge(att_o, att_l, w_att, gates, b_gate, p_ssd, p_s5, w_out, x, ln_g, ln_b, *, tm=512):
    m = x.shape[0]
    att = pl.BlockSpec((tm, ATT_OUT), lambda i: (i, 0))
    row = pl.BlockSpec((tm, D_MODEL), lambda i: (i, 0))
    vec = pl.BlockSpec((1, D_MODEL), lambda i: (0, 0))
    return pl.pallas_call(
        _merge_kernel,
        grid=(m // tm,),
        in_specs=[att] * 6 + [pl.BlockSpec((ATT_OUT, D_MODEL), lambda i: (0, 0)),
                              pl.BlockSpec((tm, D_MODEL), lambda i: (i, 1)), vec, row, row,
                              pl.BlockSpec((D_MODEL, D_MODEL), lambda i: (0, 0)), row, vec, vec],
        out_specs=row,
        out_shape=jax.ShapeDtypeStruct((m, D_MODEL), F32),
        compiler_params=_params("parallel"),
        name="merge_ln1",
    )(*att_o, *att_l, w_att, gates, b_gate[1:2], p_ssd, p_s5, w_out, x, ln_g.reshape(1, -1), ln_b.reshape(1, -1))


def _router_kernel(x_ref, wt_ref, b_ref, idx_ref, gate_ref, rank_ref, count_ref, before_ref):
    tm = x_ref.shape[0]

    @pl.when(pl.program_id(0) == 0)
    def _():
        count_ref[...] = jnp.zeros_like(count_ref)
        earlier = lax.broadcasted_iota(jnp.int32, (tm, tm), 0) < lax.broadcasted_iota(jnp.int32, (tm, tm), 1)
        before_ref[...] = earlier.astype(before_ref.dtype)

    x = x_ref[...]
    wt = wt_ref[...]
    xh = x.astype(BF16)
    xl = (x - xh.astype(F32)).astype(BF16)
    wh = wt.astype(BF16)
    wl = (wt - wh.astype(F32)).astype(BF16)
    nt = (((1,), (1,)), ((), ()))
    logits = (lax.dot_general(wh, xh, nt, preferred_element_type=F32)
              + lax.dot_general(wl, xh, nt, preferred_element_type=F32)
              + lax.dot_general(wh, xl, nt, preferred_element_type=F32)) + b_ref[...]
    eid = lax.broadcasted_iota(jnp.int32, logits.shape, 0)
    vals, idxs = [], []
    for _ in range(MOE_TOP_K):
        mx = jnp.max(logits, axis=0, keepdims=True)
        sel = jnp.min(jnp.where(logits == mx, eid, MOE_EXPERTS), axis=0, keepdims=True)
        vals.append(mx)
        idxs.append(sel)
        logits = jnp.where(eid == sel, -jnp.inf, logits)
    es = [jnp.exp(v - vals[0]) for v in vals]
    den = es[0] + es[1] + es[2] + es[3]
    for k in range(MOE_TOP_K):
        idx_ref[k:k + 1, :] = idxs[k]
        gate_ref[k:k + 1, :] = es[k] / den
    chosen = [eid == idxs[k] for k in range(MOE_TOP_K)]
    stacked = jnp.concatenate([c.astype(before_ref.dtype) for c in chosen], axis=0)
    earlier = jnp.dot(stacked, before_ref[...], preferred_element_type=F32)
    seen = count_ref[...]
    for k in range(MOE_TOP_K):
        pos = earlier[k * MOE_EXPERTS:(k + 1) * MOE_EXPERTS] + seen
        rank_ref[k:k + 1, :] = jnp.sum(jnp.where(chosen[k], pos, 0.0), axis=0, keepdims=True).astype(jnp.int32)
        seen = seen + jnp.sum(chosen[k].astype(F32), axis=1, keepdims=True)
    count_ref[...] = seen


def _router(x, router_w, router_b, *, tm=1024):
    m = x.shape[0]
    out = pl.BlockSpec((MOE_TOP_K, tm), lambda i: (0, i))
    return pl.pallas_call(
        _router_kernel,
        grid=(m // tm,),
        in_specs=[pl.BlockSpec((tm, D_MODEL), lambda i: (i, 0)),
                  pl.BlockSpec((MOE_EXPERTS, D_MODEL), lambda i: (0, 0)),
                  pl.BlockSpec((MOE_EXPERTS, 1), lambda i: (0, 0))],
        out_specs=[out, out, out, pl.BlockSpec((MOE_EXPERTS, 1), lambda i: (0, 0))],
        out_shape=[jax.ShapeDtypeStruct((MOE_TOP_K, m), jnp.int32), jax.ShapeDtypeStruct((MOE_TOP_K, m), F32),
                   jax.ShapeDtypeStruct((MOE_TOP_K, m), jnp.int32), jax.ShapeDtypeStruct((MOE_EXPERTS, 1), F32)],
        scratch_shapes=[pltpu.VMEM((tm, tm), BF16)],
        compiler_params=_params("arbitrary"),
        name="router",
    )(x, router_w.T, router_b.reshape(-1, 1))


def _expert_kernel(be_ref, used_ref, x_ref, wgu_ref, bgu_ref, wdn_ref, bdn_ref, o_ref, wgu_lo, wdn_lo):
    i = pl.program_id(0)
    used = i < used_ref[0]

    @pl.when(used & ((i == 0) | (be_ref[i] != be_ref[jnp.maximum(i - 1, 0)])))
    def _():
        wgu_lo[...] = wgu_ref[...].astype(wgu_lo.dtype)
        wdn_lo[...] = wdn_ref[...].astype(wdn_lo.dtype)

    @pl.when(used)
    def _():
        hu = jnp.dot(x_ref[...].astype(BF16), wgu_lo[...], preferred_element_type=F32) + bgu_ref[...]
        x_glu = jnp.minimum(hu[:, :MOE_FF], SWIGLU_LIMIT)
        x_lin = jnp.clip(hu[:, MOE_FF:], -SWIGLU_LIMIT, SWIGLU_LIMIT)
        act = x_glu * jax.nn.sigmoid(SWIGLU_ALPHA * x_glu) * (x_lin + 1.0)
        o_ref[...] = jnp.dot(act.astype(BF16), wdn_lo[...], preferred_element_type=F32) + bdn_ref[...]

    @pl.when(jnp.logical_not(used))
    def _():
        o_ref[...] = jnp.zeros_like(o_ref)


def _experts(xs, block_expert, n_used, w_gu, b_gu, w_dn, b_dn, layer):
    n_rows = xs.shape[0]
    n_blocks = n_rows // MOE_ROWS
    grid_spec = pltpu.PrefetchScalarGridSpec(
        num_scalar_prefetch=2,
        grid=(n_blocks,),
        in_specs=[
            pl.BlockSpec((MOE_ROWS, D_MODEL), lambda i, be, nu: (i, 0)),
            pl.BlockSpec((None, None, D_MODEL, 2 * MOE_FF), lambda i, be, nu: (layer, be[i], 0, 0)),
            pl.BlockSpec((None, 1, 2 * MOE_FF), lambda i, be, nu: (be[i], 0, 0)),
            pl.BlockSpec((None, None, MOE_FF, D_MODEL), lambda i, be, nu: (layer, be[i], 0, 0)),
            pl.BlockSpec((None, 1, D_MODEL), lambda i, be, nu: (be[i], 0, 0)),
        ],
        out_specs=pl.BlockSpec((MOE_ROWS, D_MODEL), lambda i, be, nu: (i, 0)),
        scratch_shapes=[pltpu.VMEM((D_MODEL, 2 * MOE_FF), BF16), pltpu.VMEM((MOE_FF, D_MODEL), BF16)],
    )
    return pl.pallas_call(
        _expert_kernel,
        grid_spec=grid_spec,
        out_shape=jax.ShapeDtypeStruct((n_rows, D_MODEL), F32),
        compiler_params=_params("arbitrary"),
        name="experts",
    )(block_expert, n_used, xs, w_gu, b_gu.reshape(MOE_EXPERTS, 1, -1), w_dn, b_dn.reshape(MOE_EXPERTS, 1, -1))


def _combine_kernel(x_ref, y0_ref, y1_ref, y2_ref, y3_ref, gate_ref, g_ref, b_ref, o_ref):
    ffn = gate_ref[:, 0:1] * y0_ref[...]
    for k, y_ref in enumerate((y1_ref, y2_ref, y3_ref), start=1):
        ffn = ffn + gate_ref[:, k:k + 1] * y_ref[...]
    o_ref[...] = _layer_norm(ALPHA * x_ref[...] + ffn, g_ref[...], b_ref[...])


def _combine(x, y_rows, gate, ln_g, ln_b, *, tm=512):
    m = x.shape[0]
    row = pl.BlockSpec((tm, D_MODEL), lambda i: (i, 0))
    vec = pl.BlockSpec((1, D_MODEL), lambda i: (0, 0))
    return pl.pallas_call(
        _combine_kernel,
        grid=(m // tm,),
        in_specs=[row] * (1 + MOE_TOP_K) + [pl.BlockSpec((tm, MOE_TOP_K), lambda i: (i, 0)), vec, vec],
        out_specs=row,
        out_shape=jax.ShapeDtypeStruct((m, D_MODEL), F32),
        compiler_params=_params("parallel"),
        name="combine_ln2",
    )(x, *y_rows, gate, ln_g.reshape(1, -1), ln_b.reshape(1, -1))


def _moe(x, router_w, router_b, w_gu, b_gu, w_dn, b_dn, ln_g, ln_b, layer):
    ntok = x.shape[0]
    idx_t, gate_t, rank_t, counts = _router(x, router_w, router_b)
    gate = gate_t.T
    n_assign = ntok * MOE_TOP_K
    counts = counts.reshape(-1).astype(jnp.int32)
    padded = (counts + MOE_ROWS - 1) // MOE_ROWS * MOE_ROWS
    pend = jnp.cumsum(padded)
    pstart = pend - padded
    experts = jnp.arange(MOE_EXPERTS, dtype=jnp.int32)

    def lookup(table, idx):
        return jnp.sum(jnp.where(idx[..., None] == experts, table, 0), axis=-1)

    dest_t = lookup(pstart, idx_t) + rank_t
    n_blocks = n_assign // MOE_ROWS + MOE_EXPERTS
    n_pad = MOE_EXPERTS * MOE_ROWS
    block_start = jnp.arange(n_blocks, dtype=jnp.int32) * MOE_ROWS
    block_expert = jnp.sum((pend[None, :] <= block_start[:, None]).astype(jnp.int32), axis=1)
    block_expert = jnp.minimum(block_expert, MOE_EXPERTS - 1)
    n_used = (pend[-1:] // MOE_ROWS).astype(jnp.int32)
    gap = padded - counts
    gap_end = jnp.cumsum(gap)
    pad_id = jnp.arange(n_pad, dtype=jnp.int32)
    pad_expert = jnp.sum((gap_end[None, :] <= pad_id[:, None]).astype(jnp.int32), axis=1)
    tail_of = lookup(pstart + counts - (gap_end - gap), jnp.minimum(pad_expert, MOE_EXPERTS - 1)) + pad_id
    pad_slot = jnp.where(pad_expert < MOE_EXPERTS, tail_of, pend[-1] + pad_id - gap_end[-1])
    keys = jnp.concatenate([dest_t.reshape(-1), pad_slot])
    toks = jnp.concatenate([jnp.tile(jnp.arange(ntok, dtype=jnp.int32), MOE_TOP_K), jnp.zeros((n_pad,), jnp.int32)])
    _, slot_tok = lax.sort_key_val(keys, toks)
    xs = x[slot_tok]
    ys = _experts(xs, block_expert, n_used, w_gu, b_gu, w_dn, b_dn, layer)
    y_rows = [ys[dest_t[k]] for k in range(MOE_TOP_K)]
    return _combine(x, y_rows, gate, ln_g, ln_b)


def _layer(x, w_in, b_gate, ssd_conv_w, ssd_conv_b, ssd_a_log, ssd_dt_bias, ssd_d, ssd_norm_w,
           s5_d, s5_glu_w1, s5_glu_w2, w_br_ssd, w_br_attn, w_br_s5, w_out, ln1_g, ln1_b,
           router_w, router_b, b_gu, b_dn, ln2_g, ln2_b, *, layer, w_gu, w_dn, s5_tables):
    bsz, seqlen, d = x.shape
    ntok = bsz * seqlen
    xf = x.reshape(ntok, d)
    offs = [0]
    for s in IN_SIZES:
        offs.append(offs[-1] + s)

    def seg(k):
        return w_in[:, offs[k]:offs[k + 1]].astype(BF16)

    xb = xf.astype(BF16)
    gates = _matmul(xb, seg(0), tm=1024, tn=1024, name="proj_gates", out_dtype=BF16)
    z = _matmul(xb, seg(1), tm=1024, tn=1024, name="proj_z")
    u = _matmul_tiles(xb, seg(5), tm=512, name="proj_u")

    w_xbc = seg(2)
    xs_act = _xbc_conv(xb, w_xbc[:, :SSD_INNER], ssd_conv_w[:, :SSD_INNER], ssd_conv_b[:SSD_INNER], seqlen)
    bc_act = _xbc_conv(xb, w_xbc[:, SSD_INNER:], ssd_conv_w[:, SSD_INNER:], ssd_conv_b[SSD_INNER:], seqlen,
                       out_dtype=BF16)
    xs_act = xs_act.reshape(bsz, seqlen, SSD_INNER)
    bc_act = bc_act.reshape(bsz, seqlen, SSD_CONV_CH - SSD_INNER)
    decay = _ssd_decay(xb, seg(3), ssd_a_log, ssd_dt_bias)
    y_ssd = _ssd_scan(xs_act, bc_act, *decay, ssd_d)
    p_ssd = _ssd_project(y_ssd.reshape(ntok, SSD_INNER), z, ssd_norm_w, w_br_ssd.astype(BF16), gates, b_gate)

    w_qkv = w_in[:, offs[4]:offs[5]].reshape(d, 3, ATT_GROUPS, ATT_OUT)
    att_o, att_l = [], []
    for g, (window, dil) in enumerate(ATT_PATTERNS):
        assert (window // 2) // dil == ATT_RADIUS
        ls = seqlen // dil
        x_cls = xb.reshape(bsz, ls, dil, d).transpose(0, 2, 1, 3).reshape(ntok, d)
        qkv_g = _matmul(x_cls, w_qkv[:, :, g].reshape(d, 3 * ATT_OUT).astype(BF16), tm=1024, tn=3 * ATT_OUT,
                        name=f"proj_qkv{g}", out_dtype=BF16)
        o, lse = _attention_group(qkv_g.reshape(bsz * dil, ls, 3 * ATT_OUT), g, dil)
        att_o.append(o.reshape(bsz, dil, ls, ATT_OUT).transpose(0, 2, 1, 3).reshape(ntok, ATT_OUT))
        att_l.append(lse.reshape(bsz, dil, ls, ATT_OUT).transpose(0, 2, 1, 3).reshape(ntok, ATT_OUT))

    h_s5 = _s5_mix(u, seqlen, s5_tables, layer, s5_d)
    p_s5 = _s5_project(h_s5, s5_glu_w1.astype(BF16), s5_glu_w2.astype(BF16), w_br_s5.astype(BF16), gates, b_gate)

    x1 = _merge(att_o, att_l, w_br_attn.astype(BF16), gates, b_gate, p_ssd, p_s5, w_out.astype(BF16),
                xf, ln1_g, ln1_b)
    x2 = _moe(x1, router_w, router_b, w_gu, b_gu, w_dn, b_dn, ln2_g, ln2_b, layer)
    return x2.reshape(bsz, seqlen, d)


def kernel(x, w_in, b_gate, ssd_conv_w, ssd_conv_b, ssd_a_log, ssd_dt_bias, ssd_d, ssd_norm_w, s5_a_re, s5_a_im, s5_log_step, s5_b_re, s5_b_im, s5_c_re, s5_c_im, s5_d, s5_glu_w1, s5_glu_w2, w_br_ssd, w_br_attn, w_br_s5, w_out, ln1_g, ln1_b, router_w, router_b, exp_w_gate_up, exp_b_gate_up, exp_w_down, exp_b_down, ln2_g, ln2_b):
    per_layer = (w_in, b_gate, ssd_conv_w, ssd_conv_b, ssd_a_log, ssd_dt_bias, ssd_d, ssd_norm_w, s5_d, s5_glu_w1,
                 s5_glu_w2, w_br_ssd, w_br_attn, w_br_s5, w_out, ln1_g, ln1_b, router_w, router_b, exp_b_gate_up,
                 exp_b_down, ln2_g, ln2_b)
    s5_tables = _s5_all_tables(s5_a_re, s5_a_im, s5_log_step, s5_b_re, s5_b_im, s5_c_re, s5_c_im)
    for layer in range(DEPTH):
        x = _layer(x, *[p[layer] for p in per_layer], layer=layer, w_gu=exp_w_gate_up, w_dn=exp_w_down,
                   s5_tables=s5_tables)
    return x
```

```python
import functools
import math

import jax
import jax.numpy as jnp
from jax import lax
from jax.experimental import pallas as pl
from jax.experimental.pallas import tpu as pltpu

F32 = jnp.float32
BF16 = jnp.bfloat16

D_MODEL = 1024
DEPTH = 4
ALPHA = (2 * DEPTH) ** 0.25
LN_EPS = 1e-5

SSD_INNER = 2 * D_MODEL
SSD_HEAD_DIM = 64
SSD_GROUPS = 8
SSD_HPG = 4
SSD_HEADS = SSD_GROUPS * SSD_HPG
SSD_STATE = 128
SSD_CONV = 5
SSD_CONV_CH = SSD_INNER + 2 * SSD_GROUPS * SSD_STATE
SSD_CHUNK = 128
SSD_SEQ_BLOCK = 1024
CONV_HALO_ROWS = 16

ATT_HEAD_DIM = 64
ATT_PATTERNS = ((128, 1), (512, 4), (2048, 16))
ATT_GROUPS = 3
ATT_HPG = 6
ATT_HEADS = ATT_GROUPS * ATT_HPG
ATT_WIDTH = ATT_HEADS * ATT_HEAD_DIM
ATT_OUT = ATT_HPG * ATT_HEAD_DIM
ATT_NEG = -1e30
ATT_RADIUS = 64
ATT_QSUB = 128

S5_CH = 16
S5_STATE = 64
S5_WIDTH = 9 * D_MODEL // 8
S5_GROUPS = S5_WIDTH // S5_CH
S5_TC = 16
S5_ROW = S5_TC * S5_CH
S5_SCAN_GROUPS = 8
S5_LANE_GROUPS = 128 // S5_CH
S5_TILES = S5_GROUPS // S5_LANE_GROUPS
S5_ROW_BLOCK = 256
S5_RELAYOUT_ROWS = 16

N_BRANCHES = 3
IN_SIZES = (N_BRANCHES * D_MODEL, SSD_INNER, SSD_CONV_CH, 2 * SSD_HEADS, 3 * ATT_WIDTH, S5_WIDTH)

MOE_EXPERTS = 32
MOE_TOP_K = 4
MOE_FF = D_MODEL
SWIGLU_LIMIT = 7.0
SWIGLU_ALPHA = 1.702
MOE_ROWS = 256

VMEM_LIMIT = 56 * 1024 * 1024


def _params(*sem):
    return pltpu.CompilerParams(dimension_semantics=sem, vmem_limit_bytes=VMEM_LIMIT)


def _mm_kernel(x_ref, w_ref, o_ref):
    o_ref[...] = jnp.dot(x_ref[...].astype(BF16), w_ref[...], preferred_element_type=F32).astype(o_ref.dtype)


def _matmul(x, w, *, tm, tn, name, out_dtype=F32):
    m, k = x.shape
    n = w.shape[1]
    return pl.pallas_call(
        _mm_kernel,
        grid=(n // tn, m // tm),
        in_specs=[pl.BlockSpec((tm, k), lambda j, i: (i, 0)), pl.BlockSpec((k, tn), lambda j, i: (0, j))],
        out_specs=pl.BlockSpec((tm, tn), lambda j, i: (i, j)),
        out_shape=jax.ShapeDtypeStruct((m, n), out_dtype),
        compiler_params=_params("parallel", "parallel"),
        name=name,
    )(x, w)


def _mm_tiles_kernel(x_ref, w_ref, o_ref):
    acc = jnp.dot(x_ref[...].astype(BF16), w_ref[...], preferred_element_type=F32)
    for t in range(o_ref.shape[0]):
        o_ref[t] = acc[:, 128 * t:128 * (t + 1)]


def _matmul_tiles(x, w, *, tm, name):
    m, k = x.shape
    n = w.shape[1]
    nt = n // 128
    return pl.pallas_call(
        _mm_tiles_kernel,
        grid=(m // tm,),
        in_specs=[pl.BlockSpec((tm, k), lambda i: (i, 0)), pl.BlockSpec((k, n), lambda i: (0, 0))],
        out_specs=pl.BlockSpec((nt, tm, 128), lambda i: (0, i, 0)),
        out_shape=jax.ShapeDtypeStruct((nt, m, 128), F32),
        compiler_params=_params("parallel"),
        name=name,
    )(x, w)


def _split3(a):
    a1 = a.astype(BF16)
    r1 = a - a1.astype(F32)
    a2 = r1.astype(BF16)
    a3 = (r1 - a2.astype(F32)).astype(BF16)
    return a1, a2, a3


def _decay_kernel(x_ref, w_ref, wt_ref, alog_r_ref, alog_c_ref, bias_r_ref, bias_c_ref,
                  acs_c_ref, wend_c_ref, acs_r_ref, delta_r_ref):
    q = SSD_CHUNK
    nh = SSD_HEADS
    xb = x_ref[...].astype(BF16)
    dt_c = jnp.dot(xb, w_ref[...], preferred_element_type=F32)
    dt_r = lax.dot_general(wt_ref[...], xb, (((1,), (1,)), ((), ())), preferred_element_type=F32)
    delta_c = jax.nn.softplus(dt_c + bias_r_ref[...])
    delta_r = jax.nn.softplus(dt_r + bias_c_ref[...])
    a_c = -jnp.exp(alog_r_ref[...]) * delta_c
    a_r = -jnp.exp(alog_c_ref[...]) * delta_r
    delta_r_ref[...] = delta_r
    row = lax.broadcasted_iota(jnp.int32, (q, q), 0)
    col = lax.broadcasted_iota(jnp.int32, (q, q), 1)
    lower = (col <= row).astype(BF16)
    upper = (col >= row).astype(BF16)
    fwd_lane = lax.broadcasted_iota(jnp.int32, (q, dt_c.shape[1]), 1) < nh
    fwd_row = lax.broadcasted_iota(jnp.int32, (2 * nh, q), 0) < nh
    for c in range(x_ref.shape[0] // q):
        rows = slice(c * q, (c + 1) * q)
        pieces = _split3(a_c[rows])
        prefix = sum(jnp.dot(lower, p, preferred_element_type=F32) for p in pieces)
        suffix = sum(jnp.dot(upper, p, preferred_element_type=F32) for p in pieces)
        acs = jnp.where(fwd_lane, prefix, suffix)
        total = jnp.where(fwd_lane[0:1], prefix[q - 1:q], suffix[0:1])
        acs_c_ref[rows, :] = acs
        wend_c_ref[rows, :] = delta_c[rows] * jnp.exp(total - acs)
        pieces = _split3(a_r[:, rows])
        prefix = sum(jnp.dot(p, upper, preferred_element_type=F32) for p in pieces)
        suffix = sum(jnp.dot(p, lower, preferred_element_type=F32) for p in pieces)
        acs_r_ref[:, rows] = jnp.where(fwd_row, prefix, suffix)


def _ssd_decay(x, w_dt, a_log, dt_bias, *, tm=1024):
    m, k = x.shape
    n = w_dt.shape[1]
    lanes = 128
    pad = lanes - n
    w_pad = jnp.pad(w_dt, ((0, 0), (0, pad)))
    alog = a_log.reshape(-1)
    bias = dt_bias.reshape(-1)
    col = pl.BlockSpec((tm, lanes), lambda i: (i, 0))
    rowf = pl.BlockSpec((n, tm), lambda i: (0, i))
    return pl.pallas_call(
        _decay_kernel,
        grid=(m // tm,),
        in_specs=[pl.BlockSpec((tm, k), lambda i: (i, 0)), pl.BlockSpec((k, lanes), lambda i: (0, 0)),
                  pl.BlockSpec((n, k), lambda i: (0, 0)),
                  pl.BlockSpec((1, lanes), lambda i: (0, 0)), pl.BlockSpec((n, 1), lambda i: (0, 0)),
                  pl.BlockSpec((1, lanes), lambda i: (0, 0)), pl.BlockSpec((n, 1), lambda i: (0, 0))],
        out_specs=[col, col, rowf, rowf],
        out_shape=[jax.ShapeDtypeStruct((m, lanes), F32)] * 2 + [jax.ShapeDtypeStruct((n, m), F32)] * 2,
        compiler_params=_params("parallel"),
        name="ssd_decay",
    )(x, w_pad, w_dt.T, jnp.pad(alog, (0, pad)).reshape(1, lanes), alog.reshape(n, 1),
      jnp.pad(bias, (0, pad)).reshape(1, lanes), bias.reshape(n, 1))


def _xbc_conv_kernel(xm_ref, xp_ref, xn_ref, w_ref, cw_ref, cb_ref, o_ref, buf, *, tiles_per_seq):
    pos = pl.program_id(1) % tiles_per_seq
    ts = xm_ref.shape[0]
    halo = xp_ref.shape[0]
    half = SSD_CONV // 2
    rows = jnp.concatenate([xp_ref[...], xm_ref[...], xn_ref[...]], axis=0).astype(BF16)
    buf[...] = jnp.dot(rows, w_ref[...], preferred_element_type=F32)

    @pl.when(pos == 0)
    def _():
        buf[0:halo, :] = jnp.zeros((halo, buf.shape[1]), F32)

    @pl.when(pos == tiles_per_seq - 1)
    def _():
        buf[halo + ts:2 * halo + ts, :] = jnp.zeros((halo, buf.shape[1]), F32)

    acc = cb_ref[...] + cw_ref[0:1, :] * buf[halo - half:halo - half + ts, :]
    for k in range(1, SSD_CONV):
        acc = acc + cw_ref[k:k + 1, :] * buf[halo - half + k:halo - half + k + ts, :]
    o_ref[...] = (acc * jax.nn.sigmoid(acc)).astype(o_ref.dtype)


def _xbc_conv(x, w, conv_w, conv_b, seqlen, *, tm=512, tn=1024, out_dtype=F32):
    m, k = x.shape
    n = w.shape[1]
    halo = CONV_HALO_ROWS
    nhalo = m // halo
    return pl.pallas_call(
        functools.partial(_xbc_conv_kernel, tiles_per_seq=seqlen // tm),
        grid=(n // tn, m // tm),
        in_specs=[
            pl.BlockSpec((tm, k), lambda j, i: (i, 0)),
            pl.BlockSpec((halo, k), lambda j, i: (jnp.maximum(i * (tm // halo) - 1, 0), 0)),
            pl.BlockSpec((halo, k), lambda j, i: (jnp.minimum((i + 1) * (tm // halo), nhalo - 1), 0)),
            pl.BlockSpec((k, tn), lambda j, i: (0, j)),
            pl.BlockSpec((SSD_CONV, tn), lambda j, i: (0, j)),
            pl.BlockSpec((1, tn), lambda j, i: (0, j)),
        ],
        out_specs=pl.BlockSpec((tm, tn), lambda j, i: (i, j)),
        out_shape=jax.ShapeDtypeStruct((m, n), out_dtype),
        scratch_shapes=[pltpu.VMEM((tm + 2 * halo, tn), F32)],
        compiler_params=_params("parallel", "parallel"),
        name="proj_xbc_conv",
    )(x, x, x, w, conv_w, conv_b.reshape(1, n))


def _per_head(cols):
    rows = cols[0].shape[0]
    first = lax.broadcasted_iota(jnp.int32, (rows, 2 * SSD_HEAD_DIM), 1) < SSD_HEAD_DIM
    return jnp.concatenate([jnp.where(first, cols[0], cols[1]), jnp.where(first, cols[2], cols[3])], axis=1)


def _group_lanes(ref, rows, g):
    lanes = ref.shape[1]
    return pltpu.roll(ref[rows, :], (lanes - SSD_HPG * g) % lanes, 1)


def _group_rows(ref, first, cols, g):
    tile = 8
    per_tile = tile // SSD_HPG
    base = pl.multiple_of(first + (g // per_tile) * tile, tile)
    return pltpu.roll(ref[pl.ds(base, tile), cols], (tile - SSD_HPG * (g % per_tile)) % tile, 0)


def _head_selector(lanes, first):
    width = SSD_HPG * SSD_HEAD_DIM
    src = lax.broadcasted_iota(jnp.int32, (lanes, width), 0)
    dst = lax.broadcasted_iota(jnp.int32, (lanes, width), 1) // SSD_HEAD_DIM
    return (src == dst + first).astype(BF16)


def _expand_heads(cols, selector):
    return sum(jnp.dot(p, selector, preferred_element_type=F32) for p in _split3(cols))


def _weighted_inputs(xs_ref, wend_ref, g, first, xw_scr):
    rows = slice(0, xs_ref.shape[0])
    wend = _group_lanes(wend_ref, rows, g)
    xw_scr[...] = (xs_ref[...] * _expand_heads(wend, _head_selector(wend_ref.shape[1], first))).astype(xw_scr.dtype)


def _ssd_back_kernel(xs_ref, bm_ref, acs_ref, wend_ref, out_ref, carried, xw_scr):
    q = SSD_CHUNK
    nchunk = xs_ref.shape[0] // q
    g = pl.program_id(1)
    bwd = SSD_HEADS

    @pl.when(pl.program_id(2) == 0)
    def _():
        carried[...] = jnp.zeros_like(carried)

    _weighted_inputs(xs_ref, wend_ref, g, bwd, xw_scr)

    def chunk(ci, carry):
        c = nchunk - 1 - ci
        rows = pl.ds(pl.multiple_of(c * q, q), q)
        first_row = pl.ds(pl.multiple_of(c * q, q), 8)
        total = _group_lanes(acs_ref, first_row, g)[0:1]
        states = lax.dot_general(bm_ref[rows, :].astype(BF16), xw_scr[rows, :], (((0,), (0,)), ((), ())),
                                 preferred_element_type=F32)
        prev = carried[...]
        out_ref[c] = prev.astype(out_ref.dtype)
        carried[...] = jnp.exp(_per_head([total[:, bwd + r:bwd + r + 1] for r in range(SSD_HPG)])) * prev + states
        return carry

    lax.fori_loop(0, nchunk, chunk, 0, unroll=4)


def _ssd_main_kernel(xs_ref, bm_ref, cm_ref, acs_c_ref, wend_c_ref, acs_r_ref, delta_r_ref, back_ref,
                     dskip_ref, y_ref, carried):
    q = SSD_CHUNK
    nchunk = xs_ref.shape[0] // q
    g = pl.program_id(1)
    bwd = SSD_HEADS
    hd = SSD_HEAD_DIM

    @pl.when(pl.program_id(2) == 0)
    def _():
        carried[...] = jnp.zeros_like(carried)

    row = lax.broadcasted_iota(jnp.int32, (q, q), 0)
    col = lax.broadcasted_iota(jnp.int32, (q, q), 1)
    lower = col <= row
    upper = col >= row
    d_wide = _per_head([dskip_ref[:, r:r + 1] for r in range(SSD_HPG)])
    selector = _head_selector(wend_c_ref.shape[1], 0)
    first = lax.broadcasted_iota(jnp.int32, (q, 2 * hd), 1) < hd

    def per_head_lanes(blocks):
        return jnp.concatenate([jnp.where(first, blocks[0], blocks[1]), jnp.where(first, blocks[2], blocks[3])], axis=1)

    def chunk(c, carry):
        r0 = pl.multiple_of(c * q, q)
        rows = pl.ds(r0, q)
        xs = xs_ref[rows, :]
        bmat = bm_ref[rows, :].astype(BF16)
        cmat = cm_ref[rows, :].astype(BF16)
        acs = _group_lanes(acs_c_ref, rows, g)
        wend = _group_lanes(wend_c_ref, rows, g)
        acs_f = [jnp.broadcast_to(acs[:, r:r + 1], (q, q)) for r in range(SSD_HPG)]
        acs_b = [jnp.broadcast_to(acs[:, bwd + r:bwd + r + 1], (q, q)) for r in range(SSD_HPG)]
        acs_rf = _group_rows(acs_r_ref, 0, rows, g)
        acs_rb = _group_rows(acs_r_ref, bwd, rows, g)
        del_rf = _group_rows(delta_r_ref, 0, rows, g)
        del_rb = _group_rows(delta_r_ref, bwd, rows, g)
        cb = lax.dot_general(cmat, bmat, (((1,), (1,)), ((), ())), preferred_element_type=F32)
        prev = carried[...]
        states_in = jnp.concatenate([prev.astype(BF16), back_ref[c]], axis=1)
        y_off = jnp.dot(cmat, states_in, preferred_element_type=F32)
        e_f = jnp.exp(per_head_lanes(acs_f))
        e_b = jnp.exp(per_head_lanes(acs_b))
        rest = e_f * y_off[:, 0:SSD_HPG * hd] + e_b * y_off[:, SSD_HPG * hd:] + d_wide * xs
        for r in range(SSD_HPG):
            seg_f = acs_f[r] - acs_rf[r:r + 1, :]
            seg_b = acs_b[r] - acs_rb[r:r + 1, :]
            w = (jnp.exp(jnp.where(lower, seg_f, ATT_NEG)) * del_rf[r:r + 1, :]
                 + jnp.exp(jnp.where(upper, seg_b, ATT_NEG)) * del_rb[r:r + 1, :])
            y_diag = jnp.dot((cb * w).astype(BF16), xs[:, r * hd:(r + 1) * hd].astype(BF16),
                             preferred_element_type=F32)
            y_ref[rows, r * hd:(r + 1) * hd] = y_diag + rest[:, r * hd:(r + 1) * hd]
        xw = xs * _expand_heads(wend, selector)
        states = lax.dot_general(bmat, xw.astype(BF16), (((0,), (0,)), ((), ())), preferred_element_type=F32)
        total = acs[q - 1:q]
        carried[...] = jnp.exp(_per_head([total[:, r:r + 1] for r in range(SSD_HPG)])) * prev + states
        return carry

    lax.fori_loop(0, nchunk, chunk, 0, unroll=2)


def _ssd_scan(xs_act, bc_act, acs_c, wend_c, acs_r, delta_r, d_skip):
    bsz, seqlen, _ = xs_act.shape
    sb = min(SSD_SEQ_BLOCK, seqlen)
    nblk = seqlen // sb
    cpb = sb // SSD_CHUNK
    n_x = SSD_HPG * SSD_HEAD_DIM
    b_off = 0
    c_off = SSD_GROUPS
    lanes = acs_c.shape[1]
    nrow = acs_r.shape[0]

    def specs(blk):
        xs = pl.BlockSpec((None, sb, n_x), lambda b_, g, i: (b_, blk(i), g))
        bm = pl.BlockSpec((None, sb, SSD_STATE), lambda b_, g, i: (b_, blk(i), b_off + g))
        cm = pl.BlockSpec((None, sb, SSD_STATE), lambda b_, g, i: (b_, blk(i), c_off + g))
        colf = pl.BlockSpec((sb, lanes), lambda b_, g, i: (b_ * nblk + blk(i), 0))
        rowf = pl.BlockSpec((nrow, sb), lambda b_, g, i: (0, b_ * nblk + blk(i)))
        back = pl.BlockSpec((None, None, cpb, SSD_STATE, n_x), lambda b_, g, i: (b_, g, blk(i), 0, 0))
        return xs, bm, cm, colf, rowf, back

    xs, bm, cm, colf, rowf, back = specs(lambda i: nblk - 1 - i)
    assert SSD_CHUNK == 2 * SSD_HEAD_DIM and SSD_HPG == 4
    state = [pltpu.VMEM((SSD_STATE, n_x), F32), pltpu.VMEM((sb, n_x), BF16)]
    back_states = pl.pallas_call(
        _ssd_back_kernel,
        grid=(bsz, SSD_GROUPS, nblk),
        in_specs=[xs, bm, colf, colf],
        out_specs=back,
        out_shape=jax.ShapeDtypeStruct((bsz, SSD_GROUPS, seqlen // SSD_CHUNK, SSD_STATE, n_x), BF16),
        scratch_shapes=state,
        compiler_params=_params("parallel", "parallel", "arbitrary"),
        name="ssd_back",
    )(xs_act, bc_act, acs_c, wend_c)

    xs, bm, cm, colf, rowf, back = specs(lambda i: i)
    return pl.pallas_call(
        _ssd_main_kernel,
        grid=(bsz, SSD_GROUPS, nblk),
        in_specs=[xs, bm, cm, colf, colf, rowf, rowf, back,
                  pl.BlockSpec((None, 1, SSD_HPG), lambda b_, g, i: (g, 0, 0))],
        out_specs=pl.BlockSpec((None, sb, n_x), lambda b_, g, i: (b_, i, g)),
        out_shape=jax.ShapeDtypeStruct((bsz, seqlen, SSD_INNER), F32),
        scratch_shapes=state[:1],
        compiler_params=_params("parallel", "parallel", "arbitrary"),
        name="ssd_main",
    )(xs_act, bc_act, bc_act, acs_c, wend_c, acs_r, delta_r, back_states,
      d_skip.reshape(SSD_GROUPS, 1, SSD_HPG))


def _attn_kernel(q_ref, ko_ref, kp_ref, kn_ref, vo_ref, vp_ref, vn_ref, o_ref, lse_ref, kcat, vcat, bias,
                 *, dil, group):
    tq = q_ref.shape[0]
    rad = ATT_RADIUS
    qs = ATT_QSUB
    kw = qs + 2 * rad
    nsub = tq // qs
    i = pl.program_id(1)
    kcat[0:rad, :] = kp_ref[...].astype(BF16)
    kcat[rad:rad + tq, :] = ko_ref[...].astype(BF16)
    kcat[rad + tq:2 * rad + tq, :] = kn_ref[...].astype(BF16)
    vcat[0:rad, :] = vp_ref[...].astype(BF16)
    vcat[rad:rad + tq, :] = vo_ref[...].astype(BF16)
    vcat[rad + tq:2 * rad + tq, :] = vn_ref[...].astype(BF16)

    row = lax.broadcasted_iota(jnp.int32, (qs, kw), 0)
    col = lax.broadcasted_iota(jnp.int32, (qs, kw), 1)
    dist = jnp.abs(row - col + rad)
    for head in range(ATT_HPG):
        slope = 2.0 ** (-8.0 * (head * ATT_GROUPS + group + 1) / ATT_HEADS)
        bias[head] = jnp.where(dist <= rad, (-slope * dil) * dist.astype(F32), ATT_NEG)
    before_start = (i == 0) & (col < rad)
    past_end = (i == pl.num_programs(1) - 1) & (col >= qs + rad)
    lane = lax.broadcasted_iota(jnp.int32, (qs, 2 * ATT_HEAD_DIM), 1)
    first_half = lane < ATT_HEAD_DIM
    scale = ATT_HEAD_DIM ** -0.5

    for m in range(nsub):
        for pair in range(ATT_HPG // 2):
            lanes = slice(pair * 2 * ATT_HEAD_DIM, (pair + 1) * 2 * ATT_HEAD_DIM)
            q2 = q_ref[m * qs:(m + 1) * qs, lanes]
            k2 = kcat[m * qs:m * qs + kw, lanes]
            v2 = vcat[m * qs:m * qs + kw, lanes]
            for half in range(2):
                head = 2 * pair + half
                qm = jnp.where(first_half if half == 0 else ~first_half, q2, 0.0).astype(BF16)
                s = lax.dot_general(qm, k2, (((1,), (1,)), ((), ())), preferred_element_type=F32)
                s = s * scale + bias[head]
                if m == 0:
                    s = jnp.where(before_start, ATT_NEG, s)
                if m == nsub - 1:
                    s = jnp.where(past_end, ATT_NEG, s)
                mx = jnp.max(s, axis=-1, keepdims=True)
                p = jnp.exp(s - mx)
                zsum = jnp.sum(p, axis=-1, keepdims=True)
                pv = jnp.dot(p.astype(BF16), v2, preferred_element_type=F32)
                own = slice(half * ATT_HEAD_DIM, (half + 1) * ATT_HEAD_DIM)
                out_lanes = slice(lanes.start + own.start, lanes.start + own.stop)
                o_ref[m * qs:(m + 1) * qs, out_lanes] = (pv / zsum)[:, own]
                lse_ref[m * qs:(m + 1) * qs, out_lanes] = jnp.broadcast_to(mx + jnp.log(zsum), (qs, ATT_HEAD_DIM))


def _attention_group(qkv, group, dil):
    nseq, ls, _ = qkv.shape
    tq = min(512, ls)
    nblk = ls // tq
    halo_per_blk = tq // ATT_RADIUS
    n_halo = ls // ATT_RADIUS

    def own(off):
        return pl.BlockSpec((None, tq, ATT_OUT), lambda s, i: (s, i, off))

    def prev(off):
        return pl.BlockSpec((None, ATT_RADIUS, ATT_OUT), lambda s, i: (s, jnp.maximum(i * halo_per_blk - 1, 0), off))

    def nxt(off):
        return pl.BlockSpec((None, ATT_RADIUS, ATT_OUT),
                            lambda s, i: (s, jnp.minimum((i + 1) * halo_per_blk, n_halo - 1), off))

    out_spec = pl.BlockSpec((None, tq, ATT_OUT), lambda s, i: (s, i, 0))
    return pl.pallas_call(
        functools.partial(_attn_kernel, dil=dil, group=group),
        grid=(nseq, nblk),
        in_specs=[own(0), own(1), prev(1), nxt(1), own(2), prev(2), nxt(2)],
        out_specs=[out_spec, out_spec],
        out_shape=[jax.ShapeDtypeStruct((nseq, ls, ATT_OUT), F32)] * 2,
        scratch_shapes=[pltpu.VMEM((tq + 2 * ATT_RADIUS, ATT_OUT), BF16)] * 2
        + [pltpu.VMEM((ATT_HPG, ATT_QSUB, ATT_QSUB + 2 * ATT_RADIUS), F32)],
        compiler_params=_params("parallel", "parallel"),
        name=f"attn_g{group}",
    )(qkv, qkv, qkv, qkv, qkv, qkv, qkv)


def _s5_tables(a_re, a_im, log_step, b_re, b_im, c_re, c_im):
    hp = lax.Precision.HIGHEST
    tc = S5_TC
    ng, width = S5_GROUPS, S5_ROW
    step = jnp.exp(log_step)[..., None]
    ar, ai = a_re, a_im
    mag = jnp.exp(step * ar)
    abr, abi = mag * jnp.cos(step * ai), mag * jnp.sin(step * ai)
    den = ar * ar + ai * ai
    fr = ((abr - 1.0) * ar + abi * ai) / den
    fi = (abi * ar - (abr - 1.0) * ai) / den
    b_re_t, b_im_t = b_re.transpose(0, 2, 1), b_im.transpose(0, 2, 1)
    bbr = fr[:, :, None, :] * b_re_t[None] - fi[:, :, None, :] * b_im_t[None]
    bbi = fr[:, :, None, :] * b_im_t[None] + fi[:, :, None, :] * b_re_t[None]
    ks = jnp.arange(tc + 1, dtype=F32)
    arg_r, arg_i = step * ar, step * ai
    pk_r = jnp.exp(arg_r[:, :, None, :] * ks[:, None]) * jnp.cos(arg_i[:, :, None, :] * ks[:, None])
    pk_i = jnp.exp(arg_r[:, :, None, :] * ks[:, None]) * jnp.sin(arg_i[:, :, None, :] * ks[:, None])
    pn_r = jnp.exp(arg_r[..., None] * ks) * jnp.cos(arg_i[..., None] * ks)
    pn_i = jnp.exp(arg_r[..., None] * ks) * jnp.sin(arg_i[..., None] * ks)
    c_re_t, c_im_t = c_re.transpose(0, 1, 3, 2), c_im.transpose(0, 1, 3, 2)
    cap_r = (c_re_t[:, :, :, None, :] * pn_r[..., None] - c_im_t[:, :, :, None, :] * pn_i[..., None])
    cap_i = (c_re_t[:, :, :, None, :] * pn_i[..., None] + c_im_t[:, :, :, None, :] * pn_r[..., None])
    cap_r = cap_r.reshape(2, ng, S5_STATE, (tc + 1) * S5_CH)
    cap_i = cap_i.reshape(2, ng, S5_STATE, (tc + 1) * S5_CH)
    taps = (jnp.einsum('dgin,dgnm->dgim', bbr, cap_r, precision=hp)
            - jnp.einsum('dgin,dgnm->dgim', bbi, cap_i, precision=hp))[..., :width]
    fwd_src = jnp.pad(taps[0], ((0, 0), (0, 0), (width, 0)))
    bwd_rev = taps[1].reshape(ng, S5_CH, tc, S5_CH)[:, :, ::-1, :].reshape(ng, S5_CH, width)
    bwd_src = jnp.pad(bwd_rev, ((0, 0), (0, 0), (0, width)))
    toep = jnp.stack([fwd_src[:, :, width - S5_CH * s:2 * width - S5_CH * s]
                      + bwd_src[:, :, S5_CH * (tc - 1 - s):S5_CH * (tc - 1 - s) + width] for s in range(tc)], axis=1)
    toep = toep.reshape(ng, width, width)

    def end_state(d, pw_r, pw_i):
        re = pw_r[:, :, None, :] * bbr[d][:, None] - pw_i[:, :, None, :] * bbi[d][:, None]
        im = pw_r[:, :, None, :] * bbi[d][:, None] + pw_i[:, :, None, :] * bbr[d][:, None]
        return [re.reshape(ng, width, S5_STATE), im.reshape(ng, width, S5_STATE)]

    b_end = jnp.concatenate(end_state(0, pk_r[0][:, tc - 1::-1], pk_i[0][:, tc - 1::-1])
                            + end_state(1, pk_r[1][:, :tc], pk_i[1][:, :tc]), axis=-1)

    def lag_reversed(a):
        return a.reshape(ng, S5_STATE, tc, S5_CH)[:, :, ::-1, :].reshape(ng, S5_STATE, width)

    c_out = jnp.concatenate([cap_r[0][..., S5_CH:], -cap_i[0][..., S5_CH:],
                             lag_reversed(cap_r[1][..., S5_CH:]), -lag_reversed(cap_i[1][..., S5_CH:])], axis=1)
    ptr, pti = pk_r[:, :, tc], pk_i[:, :, tc]
    p_mul = jnp.concatenate([ptr[0], ptr[0], ptr[1], ptr[1]], axis=-1)
    q_mul = jnp.concatenate([-pti[0], pti[0], -pti[1], pti[1]], axis=-1)
    return toep, b_end, c_out, p_mul, q_mul


def _granule_transpose(regs, granule):
    n = len(regs)
    d = n // 2
    while d >= 1:
        low = (granule // d) % 2 == 0
        nxt = list(regs)
        for i in range(n):
            if (i // d) % 2 == 0:
                a, b = regs[i], regs[i + d]
                nxt[i] = jnp.where(low, a, pltpu.roll(b, d * S5_CH, 1))
                nxt[i + d] = jnp.where(low, pltpu.roll(a, 128 - d * S5_CH, 1), b)
        regs = nxt
        d //= 2
    return regs


def _s5_rows_to_groups(u_ref, x_scr):
    rr = S5_RELAYOUT_ROWS
    per = S5_LANE_GROUPS
    granule = lax.broadcasted_iota(jnp.int32, (rr, 128), 1) // S5_CH

    def body(rb, carry):
        base = pl.multiple_of(rb * (rr * S5_TC), rr * S5_TC)
        rows = pl.ds(pl.multiple_of(rb * rr, rr), rr)
        for h in range(S5_TC // per):
            steps = [u_ref[pl.ds(base + per * h + t, rr, stride=S5_TC), :] for t in range(per)]
            for gl, block in enumerate(_granule_transpose(steps, granule)):
                x_scr[gl, rows, 128 * h:128 * (h + 1)] = block
        return carry

    lax.fori_loop(0, x_scr.shape[1] // rr, body, 0)


def _s5_in_kernel(u_ref, toep_ref, bend_ref, y_ref, sf_ref, sb_ref, x_scr):
    half = 2 * S5_STATE
    _s5_rows_to_groups(u_ref, x_scr)
    for gl in range(S5_LANE_GROUPS):
        xb = x_scr[gl].astype(BF16)
        y_ref[gl] = jnp.dot(xb, toep_ref[gl], preferred_element_type=F32)
        s = jnp.dot(xb, bend_ref[gl], preferred_element_type=F32)
        sf_ref[gl] = s[:, 0:half]
        sb_ref[gl] = s[:, half:2 * half]


def _s5_scan_kernel(sf_ref, sb_ref, p_ref, q_ref, hf_ref, hb_ref, sf_sw_ref, sb_sw_ref, *, rows_per_seq):
    n_chain = p_ref.shape[0]
    half = 2 * S5_STATE
    pf, qf = p_ref[:, 0:half], q_ref[:, 0:half]
    pb, qb = p_ref[:, half:2 * half], q_ref[:, half:2 * half]
    qf_sw = pltpu.roll(qf, S5_STATE, 1)
    qb_sw = pltpu.roll(qb, S5_STATE, 1)
    zero = jnp.zeros((n_chain, half), F32)
    piece = 256

    def swap(k, carry):
        rows = pl.ds(pl.multiple_of(k * piece, piece), piece)
        sf_sw_ref[rows, :] = pltpu.roll(sf_ref[rows, :], S5_STATE, 1)
        sb_sw_ref[rows, :] = pltpu.roll(sb_ref[rows, :], S5_STATE, 1)
        return carry

    lax.fori_loop(0, sf_ref.shape[0] // piece, swap, 0)

    tile = 8

    def steps(c8, carry):
        hf, hf_sw, hb, hb_sw = carry
        base_f = pl.multiple_of(c8 * tile, tile)
        base_b = pl.multiple_of(rows_per_seq - tile - c8 * tile, tile)
        for k in range(tile):
            rows_f = pl.ds(base_f + k, n_chain, stride=rows_per_seq)
            rows_b = pl.ds(base_b + (tile - 1 - k), n_chain, stride=rows_per_seq)
            hf_ref[rows_f, :] = hf
            hb_ref[rows_b, :] = hb
            sf = sf_ref[rows_f, :]
            sb = sb_ref[rows_b, :]
            sf_sw = sf_sw_ref[rows_f, :]
            sb_sw = sb_sw_ref[rows_b, :]
            hf, hf_sw = pf * hf + qf * hf_sw + sf, pf * hf_sw + qf_sw * hf + sf_sw
            hb, hb_sw = pb * hb + qb * hb_sw + sb, pb * hb_sw + qb_sw * hb + sb_sw
        return hf, hf_sw, hb, hb_sw

    lax.fori_loop(0, rows_per_seq // tile, steps, (zero, zero, zero, zero))


def _s5_out_kernel(hf_ref, hb_ref, cout_ref, y_ref, u_ref, d_ref, o_ref, y_scr):
    half = 2 * S5_STATE
    per = S5_LANE_GROUPS
    for gl in range(per):
        y = y_ref[gl] + jnp.dot(hf_ref[gl].astype(BF16), cout_ref[gl, 0:half, :], preferred_element_type=F32)
        y_scr[gl] = y + jnp.dot(hb_ref[gl].astype(BF16), cout_ref[gl, half:2 * half, :], preferred_element_type=F32)
    rr = S5_RELAYOUT_ROWS
    granule = lax.broadcasted_iota(jnp.int32, (rr, 128), 1) // S5_CH
    d_skip = d_ref[...]

    def body(rb, carry):
        rows = pl.ds(pl.multiple_of(rb * rr, rr), rr)
        base = pl.multiple_of(rb * (rr * S5_TC), rr * S5_TC)
        for h in range(S5_TC // per):
            groups = [y_scr[gl, rows, 128 * h:128 * (h + 1)] for gl in range(per)]
            for t, block in enumerate(_granule_transpose(groups, granule)):
                tok = pl.ds(base + per * h + t, rr, stride=S5_TC)
                o_ref[tok, :] = jax.nn.gelu(block + u_ref[tok, :] * d_skip)
        return carry

    lax.fori_loop(0, y_scr.shape[1] // rr, body, 0)


def _s5_all_tables(a_re, a_im, log_step, b_re, b_im, c_re, c_im):
    toep, b_end, c_out, p_mul, q_mul = jax.vmap(_s5_tables)(a_re, a_im, log_step, b_re, b_im, c_re, c_im)
    return toep.astype(BF16), b_end.astype(BF16), c_out.astype(BF16), p_mul, q_mul


def _s5_mix(u_tiles, seqlen, tables, layer, d_skip):
    ntile, ntok, lanes = u_tiles.shape
    bsz = ntok // seqlen
    rows_per_seq = seqlen // S5_TC
    rows = bsz * rows_per_seq
    rblk = min(S5_ROW_BLOCK, rows)
    per = S5_LANE_GROUPS
    half = 2 * S5_STATE
    toep, b_end, c_out, p_mul, q_mul = tables
    p_mul, q_mul = p_mul[layer], q_mul[layer]
    utile = pl.BlockSpec((None, rblk * S5_TC, lanes), lambda j, i: (j, i, 0))
    ytile = pl.BlockSpec((per, rblk, S5_ROW), lambda j, i: (j, i, 0))
    stile = pl.BlockSpec((per, rblk, half), lambda j, i: (j, i, 0))
    wtile = pl.BlockSpec((None, per, S5_ROW, S5_ROW), lambda j, i: (layer, j, 0, 0))
    state_shape = jax.ShapeDtypeStruct((S5_GROUPS, rows, half), F32)
    grid = (ntile, rows // rblk)
    y_in, s_f, s_b = pl.pallas_call(
        _s5_in_kernel,
        grid=grid,
        in_specs=[utile, wtile, wtile],
        out_specs=[ytile, stile, stile],
        out_shape=[jax.ShapeDtypeStruct((S5_GROUPS, rows, S5_ROW), F32), state_shape, state_shape],
        scratch_shapes=[pltpu.VMEM((per, rblk, S5_ROW), F32)],
        compiler_params=_params("parallel", "parallel"),
        name="s5_in",
    )(u_tiles, toep, b_end)

    gs = S5_SCAN_GROUPS
    n_chain = gs * bsz
    chain_p = jnp.repeat(p_mul, bsz, axis=0)
    chain_q = jnp.repeat(q_mul, bsz, axis=0)
    flat = jax.ShapeDtypeStruct((S5_GROUPS * rows, half), F32)
    srows = pl.BlockSpec((gs * rows, half), lambda t: (t, 0))
    chain = pl.BlockSpec((n_chain, S5_ROW), lambda t: (t, 0))
    h_f, h_b = pl.pallas_call(
        functools.partial(_s5_scan_kernel, rows_per_seq=rows_per_seq),
        grid=(S5_GROUPS // gs,),
        in_specs=[srows, srows, chain, chain],
        out_specs=[srows, srows],
        out_shape=[flat, flat],
        scratch_shapes=[pltpu.VMEM((gs * rows, half), F32)] * 2,
        compiler_params=_params("parallel"),
        name="s5_scan",
    )(s_f.reshape(S5_GROUPS * rows, half), s_b.reshape(S5_GROUPS * rows, half), chain_p, chain_q)

    return pl.pallas_call(
        _s5_out_kernel,
        grid=grid,
        in_specs=[stile, stile, wtile, ytile, utile, pl.BlockSpec((None, 1, lanes), lambda j, i: (j, 0, 0))],
        out_specs=utile,
        out_shape=jax.ShapeDtypeStruct((ntile, ntok, lanes), F32),
        scratch_shapes=[pltpu.VMEM((per, rblk, S5_ROW), F32)],
        compiler_params=_params("parallel", "parallel"),
        name="s5_out",
    )(h_f.reshape(S5_GROUPS, rows, half), h_b.reshape(S5_GROUPS, rows, half), c_out, y_in, u_tiles,
      d_skip.reshape(ntile, 1, lanes))


def _ssd_proj_kernel(y_ref, z_ref, nw_ref, w_ref, gate_ref, bg_ref, o_ref):
    z = z_ref[...]
    y = y_ref[...] * (z * jax.nn.sigmoid(z))
    y = y * lax.rsqrt(jnp.mean(y * y, axis=-1, keepdims=True) + LN_EPS) * nw_ref[...]
    p = jnp.dot(y.astype(BF16), w_ref[...], preferred_element_type=F32)
    o_ref[...] = jax.nn.sigmoid(gate_ref[...] + bg_ref[...]) * p


def _ssd_project(y, z, norm_w, w_br, gates, b_gate, *, tm=512):
    m = y.shape[0]
    wide = pl.BlockSpec((tm, SSD_INNER), lambda i: (i, 0))
    return pl.pallas_call(
        _ssd_proj_kernel,
        grid=(m // tm,),
        in_specs=[wide, wide, pl.BlockSpec((1, SSD_INNER), lambda i: (0, 0)),
                  pl.BlockSpec((SSD_INNER, D_MODEL), lambda i: (0, 0)),
                  pl.BlockSpec((tm, D_MODEL), lambda i: (i, 0)),
                  pl.BlockSpec((1, D_MODEL), lambda i: (0, 0))],
        out_specs=pl.BlockSpec((tm, D_MODEL), lambda i: (i, 0)),
        out_shape=jax.ShapeDtypeStruct((m, D_MODEL), F32),
        compiler_params=_params("parallel"),
        name="ssd_project",
    )(y, z, norm_w.reshape(1, -1), w_br, gates, b_gate[0:1])


def _s5_proj_kernel(h_ref, w1_ref, w2_ref, w_ref, gate_ref, bg_ref, o_ref):
    hb = jnp.concatenate([h_ref[t] for t in range(h_ref.shape[0])], axis=1).astype(BF16)
    a = jnp.dot(hb, w1_ref[...], preferred_element_type=F32)
    b = jnp.dot(hb, w2_ref[...], preferred_element_type=F32)
    p = jnp.dot((a * jax.nn.sigmoid(b)).astype(BF16), w_ref[...], preferred_element_type=F32)
    o_ref[...] = jax.nn.sigmoid(gate_ref[...] + bg_ref[...]) * p


def _s5_project(h_tiles, w1, w2, w_br, gates, b_gate, *, tm=512):
    ntile, m, lanes = h_tiles.shape
    sq = pl.BlockSpec((S5_WIDTH, S5_WIDTH), lambda i: (0, 0))
    return pl.pallas_call(
        _s5_proj_kernel,
        grid=(m // tm,),
        in_specs=[pl.BlockSpec((ntile, tm, lanes), lambda i: (0, i, 0)), sq, sq,
                  pl.BlockSpec((S5_WIDTH, D_MODEL), lambda i: (0, 0)),
                  pl.BlockSpec((tm, D_MODEL), lambda i: (i, 2)),
                  pl.BlockSpec((1, D_MODEL), lambda i: (0, 0))],
        out_specs=pl.BlockSpec((tm, D_MODEL), lambda i: (i, 0)),
        out_shape=jax.ShapeDtypeStruct((m, D_MODEL), F32),
        compiler_params=_params("parallel"),
        name="s5_project",
    )(h_tiles, w1, w2, w_br, gates, b_gate[2:3])


def _layer_norm(v, g, b):
    mu = jnp.mean(v, axis=-1, keepdims=True)
    c = v - mu
    var = jnp.mean(c * c, axis=-1, keepdims=True)
    return c * lax.rsqrt(var + LN_EPS) * g + b


def _merge_kernel(o0_ref, o1_ref, o2_ref, l0_ref, l1_ref, l2_ref, w_att_ref, gate_ref, bg_ref, pssd_ref, ps5_ref,
                  w_out_ref, x_ref, g_ref, b_ref, o_ref):
    l0, l1, l2 = l0_ref[...], l1_ref[...], l2_ref[...]
    mx = jnp.maximum(jnp.maximum(l0, l1), l2)
    e0, e1, e2 = jnp.exp(l0 - mx), jnp.exp(l1 - mx), jnp.exp(l2 - mx)
    den = e0 + e1 + e2
    att = (e0 / den) * o0_ref[...] + (e1 / den) * o1_ref[...] + (e2 / den) * o2_ref[...]
    p_att = jnp.dot(att.astype(BF16), w_att_ref[...], preferred_element_type=F32)
    merged = pssd_ref[...] + jax.nn.sigmoid(gate_ref[...] + bg_ref[...]) * p_att + ps5_ref[...]
    mix = jnp.dot(merged.astype(BF16), w_out_ref[...], preferred_element_type=F32)
    o_ref[...] = _layer_norm(ALPHA * x_ref[...] + mix, g_ref[...], b_ref[...])


def _merge(att_o, att_l, w_att, gates, b_gate, p_ssd, p_s5, w_out, x, ln_g, ln_b, *, tm=512):
    m = x.shape[0]
    att = pl.BlockSpec((tm, ATT_OUT), lambda i: (i, 0))
    row = pl.BlockSpec((tm, D_MODEL), lambda i: (i, 0))
    vec = pl.BlockSpec((1, D_MODEL), lambda i: (0, 0))
    return pl.pallas_call(
        _merge_kernel,
        grid=(m // tm,),
        in_specs=[att] * 6 + [pl.BlockSpec((ATT_OUT, D_MODEL), lambda i: (0, 0)),
                              pl.BlockSpec((tm, D_MODEL), lambda i: (i, 1)), vec, row, row,
                              pl.BlockSpec((D_MODEL, D_MODEL), lambda i: (0, 0)), row, vec, vec],
        out_specs=row,
        out_shape=jax.ShapeDtypeStruct((m, D_MODEL), F32),
        compiler_params=_params("parallel"),
        name="merge_ln1",
    )(*att_o, *att_l, w_att, gates, b_gate[1:2], p_ssd, p_s5, w_out, x, ln_g.reshape(1, -1), ln_b.reshape(1, -1))


def _router_kernel(x_ref, wt_ref, b_ref, idx_ref, gate_ref, rank_ref, count_ref, before_ref):
    tm = x_ref.shape[0]

    @pl.when(pl.program_id(0) == 0)
    def _():
        count_ref[...] = jnp.zeros_like(count_ref)
        earlier = lax.broadcasted_iota(jnp.int32, (tm, tm), 0) < lax.broadcasted_iota(jnp.int32, (tm, tm), 1)
        before_ref[...] = earlier.astype(before_ref.dtype)

    x = x_ref[...]
    wt = wt_ref[...]
    xh = x.astype(BF16)
    xl = (x - xh.astype(F32)).astype(BF16)
    wh = wt.astype(BF16)
    wl = (wt - wh.astype(F32)).astype(BF16)
    nt = (((1,), (1,)), ((), ()))
    logits = (lax.dot_general(wh, xh, nt, preferred_element_type=F32)
              + lax.dot_general(wl, xh, nt, preferred_element_type=F32)
              + lax.dot_general(wh, xl, nt, preferred_element_type=F32)) + b_ref[...]
    eid = lax.broadcasted_iota(jnp.int32, logits.shape, 0)
    vals, idxs = [], []
    for _ in range(MOE_TOP_K):
        mx = jnp.max(logits, axis=0, keepdims=True)
        sel = jnp.min(jnp.where(logits == mx, eid, MOE_EXPERTS), axis=0, keepdims=True)
        vals.append(mx)
        idxs.append(sel)
        logits = jnp.where(eid == sel, -jnp.inf, logits)
    es = [jnp.exp(v - vals[0]) for v in vals]
    den = es[0] + es[1] + es[2] + es[3]
    for k in range(MOE_TOP_K):
        idx_ref[k:k + 1, :] = idxs[k]
        gate_ref[k:k + 1, :] = es[k] / den
    chosen = [eid == idxs[k] for k in range(MOE_TOP_K)]
    stacked = jnp.concatenate([c.astype(before_ref.dtype) for c in chosen], axis=0)
    earlier = jnp.dot(stacked, before_ref[...], preferred_element_type=F32)
    seen = count_ref[...]
    for k in range(MOE_TOP_K):
        pos = earlier[k * MOE_EXPERTS:(k + 1) * MOE_EXPERTS] + seen
        rank_ref[k:k + 1, :] = jnp.sum(jnp.where(chosen[k], pos, 0.0), axis=0, keepdims=True).astype(jnp.int32)
        seen = seen + jnp.sum(chosen[k].astype(F32), axis=1, keepdims=True)
    count_ref[...] = seen


def _router(x, router_w, router_b, *, tm=1024):
    m = x.shape[0]
    out = pl.BlockSpec((MOE_TOP_K, tm), lambda i: (0, i))
    return pl.pallas_call(
        _router_kernel,
        grid=(m // tm,),
        in_specs=[pl.BlockSpec((tm, D_MODEL), lambda i: (i, 0)),
                  pl.BlockSpec((MOE_EXPERTS, D_MODEL), lambda i: (0, 0)),
                  pl.BlockSpec((MOE_EXPERTS, 1), lambda i: (0, 0))],
        out_specs=[out, out, out, pl.BlockSpec((MOE_EXPERTS, 1), lambda i: (0, 0))],
        out_shape=[jax.ShapeDtypeStruct((MOE_TOP_K, m), jnp.int32), jax.ShapeDtypeStruct((MOE_TOP_K, m), F32),
                   jax.ShapeDtypeStruct((MOE_TOP_K, m), jnp.int32), jax.ShapeDtypeStruct((MOE_EXPERTS, 1), F32)],
        scratch_shapes=[pltpu.VMEM((tm, tm), BF16)],
        compiler_params=_params("arbitrary"),
        name="router",
    )(x, router_w.T, router_b.reshape(-1, 1))


def _expert_kernel(be_ref, used_ref, x_ref, wgu_ref, bgu_ref, wdn_ref, bdn_ref, o_ref, wgu_lo, wdn_lo):
    i = pl.program_id(0)
    used = i < used_ref[0]

    @pl.when(used & ((i == 0) | (be_ref[i] != be_ref[jnp.maximum(i - 1, 0)])))
    def _():
        wgu_lo[...] = wgu_ref[...].astype(wgu_lo.dtype)
        wdn_lo[...] = wdn_ref[...].astype(wdn_lo.dtype)

    @pl.when(used)
    def _():
        hu = jnp.dot(x_ref[...].astype(BF16), wgu_lo[...], preferred_element_type=F32) + bgu_ref[...]
        x_glu = jnp.minimum(hu[:, :MOE_FF], SWIGLU_LIMIT)
        x_lin = jnp.clip(hu[:, MOE_FF:], -SWIGLU_LIMIT, SWIGLU_LIMIT)
        act = x_glu * jax.nn.sigmoid(SWIGLU_ALPHA * x_glu) * (x_lin + 1.0)
        o_ref[...] = jnp.dot(act.astype(BF16), wdn_lo[...], preferred_element_type=F32) + bdn_ref[...]

    @pl.when(jnp.logical_not(used))
    def _():
        o_ref[...] = jnp.zeros_like(o_ref)


def _experts(xs, block_expert, n_used, w_gu, b_gu, w_dn, b_dn, layer):
    n_rows = xs.shape[0]
    n_blocks = n_rows // MOE_ROWS
    grid_spec = pltpu.PrefetchScalarGridSpec(
        num_scalar_prefetch=2,
        grid=(n_blocks,),
        in_specs=[
            pl.BlockSpec((MOE_ROWS, D_MODEL), lambda i, be, nu: (i, 0)),
            pl.BlockSpec((None, None, D_MODEL, 2 * MOE_FF), lambda i, be, nu: (layer, be[i], 0, 0)),
            pl.BlockSpec((None, 1, 2 * MOE_FF), lambda i, be, nu: (be[i], 0, 0)),
            pl.BlockSpec((None, None, MOE_FF, D_MODEL), lambda i, be, nu: (layer, be[i], 0, 0)),
            pl.BlockSpec((None, 1, D_MODEL), lambda i, be, nu: (be[i], 0, 0)),
        ],
        out_specs=pl.BlockSpec((MOE_ROWS, D_MODEL), lambda i, be, nu: (i, 0)),
        scratch_shapes=[pltpu.VMEM((D_MODEL, 2 * MOE_FF), BF16), pltpu.VMEM((MOE_FF, D_MODEL), BF16)],
    )
    return pl.pallas_call(
        _expert_kernel,
        grid_spec=grid_spec,
        out_shape=jax.ShapeDtypeStruct((n_rows, D_MODEL), F32),
        compiler_params=_params("arbitrary"),
        name="experts",
    )(block_expert, n_used, xs, w_gu, b_gu.reshape(MOE_EXPERTS, 1, -1), w_dn, b_dn.reshape(MOE_EXPERTS, 1, -1))


def _combine_kernel(x_ref, y0_ref, y1_ref, y2_ref, y3_ref, gate_ref, g_ref, b_ref, o_ref):
    ffn = gate_ref[:, 0:1] * y0_ref[...]
    for k, y_ref in enumerate((y1_ref, y2_ref, y3_ref), start=1):
        ffn = ffn + gate_ref[:, k:k + 1] * y_ref[...]
    o_ref[...] = _layer_norm(ALPHA * x_ref[...] + ffn, g_ref[...], b_ref[...])


def _combine(x, y_rows, gate, ln_g, ln_b, *, tm=512):
    m = x.shape[0]
    row = pl.BlockSpec((tm, D_MODEL), lambda i: (i, 0))
    vec = pl.BlockSpec((1, D_MODEL), lambda i: (0, 0))
    return pl.pallas_call(
        _combine_kernel,
        grid=(m // tm,),
        in_specs=[row] * (1 + MOE_TOP_K) + [pl.BlockSpec((tm, MOE_TOP_K), lambda i: (i, 0)), vec, vec],
        out_specs=row,
        out_shape=jax.ShapeDtypeStruct((m, D_MODEL), F32),
        compiler_params=_params("parallel"),
        name="combine_ln2",
    )(x, *y_rows, gate, ln_g.reshape(1, -1), ln_b.reshape(1, -1))


def _moe(x, router_w, router_b, w_gu, b_gu, w_dn, b_dn, ln_g, ln_b, layer):
    ntok = x.shape[0]
    idx_t, gate_t, rank_t, counts = _router(x, router_w, router_b)
    gate = gate_t.T
    n_assign = ntok * MOE_TOP_K
    counts = counts.reshape(-1).astype(jnp.int32)
    padded = (counts + MOE_ROWS - 1) // MOE_ROWS * MOE_ROWS
    pend = jnp.cumsum(padded)
    pstart = pend - padded
    experts = jnp.arange(MOE_EXPERTS, dtype=jnp.int32)

    def lookup(table, idx):
        return jnp.sum(jnp.where(idx[..., None] == experts, table, 0), axis=-1)

    dest_t = lookup(pstart, idx_t) + rank_t
    n_blocks = n_assign // MOE_ROWS + MOE_EXPERTS
    n_pad = MOE_EXPERTS * MOE_ROWS
    block_start = jnp.arange(n_blocks, dtype=jnp.int32) * MOE_ROWS
    block_expert = jnp.sum((pend[None, :] <= block_start[:, None]).astype(jnp.int32), axis=1)
    block_expert = jnp.minimum(block_expert, MOE_EXPERTS - 1)
    n_used = (pend[-1:] // MOE_ROWS).astype(jnp.int32)
    gap = padded - counts
    gap_end = jnp.cumsum(gap)
    pad_id = jnp.arange(n_pad, dtype=jnp.int32)
    pad_expert = jnp.sum((gap_end[None, :] <= pad_id[:, None]).astype(jnp.int32), axis=1)
    tail_of = lookup(pstart + counts - (gap_end - gap), jnp.minimum(pad_expert, MOE_EXPERTS - 1)) + pad_id
    pad_slot = jnp.where(pad_expert < MOE_EXPERTS, tail_of, pend[-1] + pad_id - gap_end[-1])
    keys = jnp.concatenate([dest_t.reshape(-1), pad_slot])
    toks = jnp.concatenate([jnp.tile(jnp.arange(ntok, dtype=jnp.int32), MOE_TOP_K), jnp.zeros((n_pad,), jnp.int32)])
    _, slot_tok = lax.sort_key_val(keys, toks)
    xs = x[slot_tok]
    ys = _experts(xs, block_expert, n_used, w_gu, b_gu, w_dn, b_dn, layer)
    y_rows = [ys[dest_t[k]] for k in range(MOE_TOP_K)]
    return _combine(x, y_rows, gate, ln_g, ln_b)


def _layer(x, w_in, b_gate, ssd_conv_w, ssd_conv_b, ssd_a_log, ssd_dt_bias, ssd_d, ssd_norm_w,
           s5_d, s5_glu_w1, s5_glu_w2, w_br_ssd, w_br_attn, w_br_s5, w_out, ln1_g, ln1_b,
           router_w, router_b, b_gu, b_dn, ln2_g, ln2_b, *, layer, w_gu, w_dn, s5_tables):
    bsz, seqlen, d = x.shape
    ntok = bsz * seqlen
    xf = x.reshape(ntok, d)
    offs = [0]
    for s in IN_SIZES:
        offs.append(offs[-1] + s)

    def seg(k):
        return w_in[:, offs[k]:offs[k + 1]].astype(BF16)

    xb = xf.astype(BF16)
    gates = _matmul(xb, seg(0), tm=1024, tn=1024, name="proj_gates", out_dtype=BF16)
    z = _matmul(xb, seg(1), tm=1024, tn=1024, name="proj_z")
    u = _matmul_tiles(xb, seg(5), tm=512, name="proj_u")

    w_xbc = seg(2)
    xs_act = _xbc_conv(xb, w_xbc[:, :SSD_INNER], ssd_conv_w[:, :SSD_INNER], ssd_conv_b[:SSD_INNER], seqlen)
    bc_act = _xbc_conv(xb, w_xbc[:, SSD_INNER:], ssd_conv_w[:, SSD_INNER:], ssd_conv_b[SSD_INNER:], seqlen,
                       out_dtype=BF16)
    xs_act = xs_act.reshape(bsz, seqlen, SSD_INNER)
    bc_act = bc_act.reshape(bsz, seqlen, SSD_CONV_CH - SSD_INNER)
    decay = _ssd_decay(xb, seg(3), ssd_a_log, ssd_dt_bias)
    y_ssd = _ssd_scan(xs_act, bc_act, *decay, ssd_d)
    p_ssd = _ssd_project(y_ssd.reshape(ntok, SSD_INNER), z, ssd_norm_w, w_br_ssd.astype(BF16), gates, b_gate)

    w_qkv = w_in[:, offs[4]:offs[5]].reshape(d, 3, ATT_GROUPS, ATT_OUT)
    att_o, att_l = [], []
    for g, (window, dil) in enumerate(ATT_PATTERNS):
        assert (window // 2) // dil == ATT_RADIUS
        ls = seqlen // dil
        x_cls = xb.reshape(bsz, ls, dil, d).transpose(0, 2, 1, 3).reshape(ntok, d)
        qkv_g = _matmul(x_cls, w_qkv[:, :, g].reshape(d, 3 * ATT_OUT).astype(BF16), tm=1024, tn=3 * ATT_OUT,
                        name=f"proj_qkv{g}", out_dtype=BF16)
        o, lse = _attention_group(qkv_g.reshape(bsz * dil, ls, 3 * ATT_OUT), g, dil)
        att_o.append(o.reshape(bsz, dil, ls, ATT_OUT).transpose(0, 2, 1, 3).reshape(ntok, ATT_OUT))
        att_l.append(lse.reshape(bsz, dil, ls, ATT_OUT).transpose(0, 2, 1, 3).reshape(ntok, ATT_OUT))

    h_s5 = _s5_mix(u, seqlen, s5_tables, layer, s5_d)
    p_s5 = _s5_project(h_s5, s5_glu_w1.astype(BF16), s5_glu_w2.astype(BF16), w_br_s5.astype(BF16), gates, b_gate)

    x1 = _merge(att_o, att_l, w_br_attn.astype(BF16), gates, b_gate, p_ssd, p_s5, w_out.astype(BF16),
                xf, ln1_g, ln1_b)
    x2 = _moe(x1, router_w, router_b, w_gu, b_gu, w_dn, b_dn, ln2_g, ln2_b, layer)
    return x2.reshape(bsz, seqlen, d)


def kernel(x, w_in, b_gate, ssd_conv_w, ssd_conv_b, ssd_a_log, ssd_dt_bias, ssd_d, ssd_norm_w, s5_a_re, s5_a_im, s5_log_step, s5_b_re, s5_b_im, s5_c_re, s5_c_im, s5_d, s5_glu_w1, s5_glu_w2, w_br_ssd, w_br_attn, w_br_s5, w_out, ln1_g, ln1_b, router_w, router_b, exp_w_gate_up, exp_b_gate_up, exp_w_down, exp_b_down, ln2_g, ln2_b):
    per_layer = (w_in, b_gate, ssd_conv_w, ssd_conv_b, ssd_a_log, ssd_dt_bias, ssd_d, ssd_norm_w, s5_d, s5_glu_w1,
                 s5_glu_w2, w_br_ssd, w_br_attn, w_br_s5, w_out, ln1_g, ln1_b, router_w, router_b, exp_b_gate_up,
                 exp_b_down, ln2_g, ln2_b)
    s5_tables = _s5_all_tables(s5_a_re, s5_a_im, s5_log_step, s5_b_re, s5_b_im, s5_c_re, s5_c_im)
    for layer in range(DEPTH):
        x = _layer(x, *[p[layer] for p in per_layer], layer=layer, w_gu=exp_w_gate_up, w_dn=exp_w_down,
                   s5_tables=s5_tables)
    return x
```

```python
import functools
import math

import jax
import jax.numpy as jnp
from jax import lax
from jax.experimental import pallas as pl
from jax.experimental.pallas import tpu as pltpu

F32 = jnp.float32
BF16 = jnp.bfloat16

D_MODEL = 1024
DEPTH = 4
ALPHA = (2 * DEPTH) ** 0.25
LN_EPS = 1e-5

SSD_INNER = 2 * D_MODEL
SSD_HEAD_DIM = 64
SSD_GROUPS = 8
SSD_HPG = 4
SSD_HEADS = SSD_GROUPS * SSD_HPG
SSD_STATE = 128
SSD_CONV = 5
SSD_CONV_CH = SSD_INNER + 2 * SSD_GROUPS * SSD_STATE
SSD_CHUNK = 128
SSD_SEQ_BLOCK = 1024
CONV_ROW_PARTS = 2
CONV_HALO_ROWS = 16

ATT_HEAD_DIM = 64
ATT_PATTERNS = ((128, 1), (512, 4), (2048, 16))
ATT_GROUPS = 3
ATT_HPG = 6
ATT_HEADS = ATT_GROUPS * ATT_HPG
ATT_WIDTH = ATT_HEADS * ATT_HEAD_DIM
ATT_OUT = ATT_HPG * ATT_HEAD_DIM
ATT_NEG = -1e30
ATT_RADIUS = 64
ATT_QSUB = 128

S5_CH = 16
S5_STATE = 64
S5_WIDTH = 9 * D_MODEL // 8
S5_GROUPS = S5_WIDTH // S5_CH
S5_TC = 16
S5_ROW = S5_TC * S5_CH
S5_SCAN_GROUPS = 8
S5_LANE_GROUPS = 128 // S5_CH
S5_TILES = S5_GROUPS // S5_LANE_GROUPS
S5_ROW_BLOCK = 256
S5_RELAYOUT_ROWS = 16

N_BRANCHES = 3
IN_SIZES = (N_BRANCHES * D_MODEL, SSD_INNER, SSD_CONV_CH, 2 * SSD_HEADS, 3 * ATT_WIDTH, S5_WIDTH)

MOE_EXPERTS = 32
MOE_TOP_K = 4
MOE_FF = D_MODEL
SWIGLU_LIMIT = 7.0
SWIGLU_ALPHA = 1.702
MOE_ROWS = 256

VMEM_LIMIT = 56 * 1024 * 1024


def _sigmoid(x):
    return 0.5 * jnp.tanh(0.5 * x) + 0.5


def _params(*sem):
    return pltpu.CompilerParams(dimension_semantics=sem, vmem_limit_bytes=VMEM_LIMIT)


def _mm_kernel(x_ref, w_ref, o_ref):
    o_ref[...] = jnp.dot(x_ref[...].astype(BF16), w_ref[...], preferred_element_type=F32).astype(o_ref.dtype)


def _matmul(x, w, *, tm, tn, name, out_dtype=F32):
    m, k = x.shape
    n = w.shape[1]
    return pl.pallas_call(
        _mm_kernel,
        grid=(n // tn, m // tm),
        in_specs=[pl.BlockSpec((tm, k), lambda j, i: (i, 0)), pl.BlockSpec((k, tn), lambda j, i: (0, j))],
        out_specs=pl.BlockSpec((tm, tn), lambda j, i: (i, j)),
        out_shape=jax.ShapeDtypeStruct((m, n), out_dtype),
        compiler_params=_params("parallel", "parallel"),
        name=name,
    )(x, w)


def _mm_tiles_kernel(x_ref, w_ref, o_ref):
    acc = jnp.dot(x_ref[...].astype(BF16), w_ref[...], preferred_element_type=F32)
    for t in range(o_ref.shape[0]):
        o_ref[t] = acc[:, 128 * t:128 * (t + 1)]


def _matmul_tiles(x, w, *, tm, name):
    m, k = x.shape
    n = w.shape[1]
    nt = n // 128
    return pl.pallas_call(
        _mm_tiles_kernel,
        grid=(m // tm,),
        in_specs=[pl.BlockSpec((tm, k), lambda i: (i, 0)), pl.BlockSpec((k, n), lambda i: (0, 0))],
        out_specs=pl.BlockSpec((nt, tm, 128), lambda i: (0, i, 0)),
        out_shape=jax.ShapeDtypeStruct((nt, m, 128), F32),
        compiler_params=_params("parallel"),
        name=name,
    )(x, w)


def _split3(a):
    a1 = a.astype(BF16)
    r1 = a - a1.astype(F32)
    a2 = r1.astype(BF16)
    a3 = (r1 - a2.astype(F32)).astype(BF16)
    return a1, a2, a3


def _decay_kernel(x_ref, w_ref, wt_ref, alog_r_ref, alog_c_ref, bias_r_ref, bias_c_ref,
                  acs_c_ref, wend_c_ref, acs_r_ref, delta_r_ref):
    q = SSD_CHUNK
    nh = SSD_HEADS
    xb = x_ref[...].astype(BF16)
    dt_c = jnp.dot(xb, w_ref[...], preferred_element_type=F32)
    dt_r = lax.dot_general(wt_ref[...], xb, (((1,), (1,)), ((), ())), preferred_element_type=F32)
    delta_c = jax.nn.softplus(dt_c + bias_r_ref[...])
    delta_r = jax.nn.softplus(dt_r + bias_c_ref[...])
    a_c = -jnp.exp(alog_r_ref[...]) * delta_c
    a_r = -jnp.exp(alog_c_ref[...]) * delta_r
    delta_r_ref[...] = delta_r
    row = lax.broadcasted_iota(jnp.int32, (q, q), 0)
    col = lax.broadcasted_iota(jnp.int32, (q, q), 1)
    lower = (col <= row).astype(BF16)
    upper = (col >= row).astype(BF16)
    fwd_lane = lax.broadcasted_iota(jnp.int32, (q, dt_c.shape[1]), 1) < nh
    fwd_row = lax.broadcasted_iota(jnp.int32, (2 * nh, q), 0) < nh
    for c in range(x_ref.shape[0] // q):
        rows = slice(c * q, (c + 1) * q)
        pieces = _split3(a_c[rows])
        prefix = sum(jnp.dot(lower, p, preferred_element_type=F32) for p in pieces)
        suffix = sum(jnp.dot(upper, p, preferred_element_type=F32) for p in pieces)
        acs = jnp.where(fwd_lane, prefix, suffix)
        total = jnp.where(fwd_lane[0:1], prefix[q - 1:q], suffix[0:1])
        acs_c_ref[rows, :] = acs
        wend_c_ref[rows, :] = delta_c[rows] * jnp.exp(total - acs)
        pieces = _split3(a_r[:, rows])
        prefix = sum(jnp.dot(p, upper, preferred_element_type=F32) for p in pieces)
        suffix = sum(jnp.dot(p, lower, preferred_element_type=F32) for p in pieces)
        acs_r_ref[:, rows] = jnp.where(fwd_row, prefix, suffix)


def _ssd_decay(x, w_dt, a_log, dt_bias, *, tm=1024):
    m, k = x.shape
    n = w_dt.shape[1]
    lanes = 128
    pad = lanes - n
    w_pad = jnp.pad(w_dt, ((0, 0), (0, pad)))
    alog = a_log.reshape(-1)
    bias = dt_bias.reshape(-1)
    col = pl.BlockSpec((tm, lanes), lambda i: (i, 0))
    rowf = pl.BlockSpec((n, tm), lambda i: (0, i))
    return pl.pallas_call(
        _decay_kernel,
        grid=(m // tm,),
        in_specs=[pl.BlockSpec((tm, k), lambda i: (i, 0)), pl.BlockSpec((k, lanes), lambda i: (0, 0)),
                  pl.BlockSpec((n, k), lambda i: (0, 0)),
                  pl.BlockSpec((1, lanes), lambda i: (0, 0)), pl.BlockSpec((n, 1), lambda i: (0, 0)),
                  pl.BlockSpec((1, lanes), lambda i: (0, 0)), pl.BlockSpec((n, 1), lambda i: (0, 0))],
        out_specs=[col, col, rowf, rowf],
        out_shape=[jax.ShapeDtypeStruct((m, lanes), F32)] * 2 + [jax.ShapeDtypeStruct((n, m), F32)] * 2,
        compiler_params=_params("parallel"),
        name="ssd_decay",
    )(x, w_pad, w_dt.T, jnp.pad(alog, (0, pad)).reshape(1, lanes), alog.reshape(n, 1),
      jnp.pad(bias, (0, pad)).reshape(1, lanes), bias.reshape(n, 1))


def _xbc_conv_kernel(xm_ref, xp_ref, xn_ref, w_ref, cw_ref, cb_ref, o_ref, buf, *, tiles_per_seq):
    pos = pl.program_id(1) % tiles_per_seq
    ts = xm_ref.shape[0]
    halo = xp_ref.shape[0]
    half = SSD_CONV // 2
    before = jnp.where(pos == 0, jnp.zeros_like(xp_ref[...]), xp_ref[...])
    after = jnp.where(pos == tiles_per_seq - 1, jnp.zeros_like(xn_ref[...]), xn_ref[...])
    rows = jnp.concatenate([before, xm_ref[...], after], axis=0).astype(BF16)
    nparts = CONV_ROW_PARTS
    part = ts // nparts
    edges = [0] + [halo + part * (p + 1) for p in range(nparts - 1)] + [ts + 2 * halo]

    def project(p):
        buf[edges[p]:edges[p + 1], :] = jnp.dot(rows[edges[p]:edges[p + 1]], w_ref[...], preferred_element_type=F32)

    def convolve(p):
        lo = halo - half + part * p
        acc = cb_ref[...] + cw_ref[0:1, :] * buf[lo:lo + part, :]
        for k in range(1, SSD_CONV):
            acc = acc + cw_ref[k:k + 1, :] * buf[lo + k:lo + k + part, :]
        o_ref[part * p:part * (p + 1), :] = (acc * _sigmoid(acc)).astype(o_ref.dtype)

    project(0)
    for p in range(nparts):
        if p + 1 < nparts:
            project(p + 1)
        convolve(p)


def _xbc_conv(x, w, conv_w, conv_b, seqlen, *, tm=512, tn=1024, out_dtype=F32):
    m, k = x.shape
    n = w.shape[1]
    halo = CONV_HALO_ROWS
    nhalo = m // halo
    return pl.pallas_call(
        functools.partial(_xbc_conv_kernel, tiles_per_seq=seqlen // tm),
        grid=(n // tn, m // tm),
        in_specs=[
            pl.BlockSpec((tm, k), lambda j, i: (i, 0)),
            pl.BlockSpec((halo, k), lambda j, i: (jnp.maximum(i * (tm // halo) - 1, 0), 0)),
            pl.BlockSpec((halo, k), lambda j, i: (jnp.minimum((i + 1) * (tm // halo), nhalo - 1), 0)),
            pl.BlockSpec((k, tn), lambda j, i: (0, j)),
            pl.BlockSpec((SSD_CONV, tn), lambda j, i: (0, j)),
            pl.BlockSpec((1, tn), lambda j, i: (0, j)),
        ],
        out_specs=pl.BlockSpec((tm, tn), lambda j, i: (i, j)),
        out_shape=jax.ShapeDtypeStruct((m, n), out_dtype),
        scratch_shapes=[pltpu.VMEM((tm + 2 * halo, tn), F32)],
        compiler_params=_params("parallel", "parallel"),
        name="proj_xbc_conv",
    )(x, x, x, w, conv_w, conv_b.reshape(1, n))


def _per_head(cols):
    rows = cols[0].shape[0]
    first = lax.broadcasted_iota(jnp.int32, (rows, 2 * SSD_HEAD_DIM), 1) < SSD_HEAD_DIM
    return jnp.concatenate([jnp.where(first, cols[0], cols[1]), jnp.where(first, cols[2], cols[3])], axis=1)


def _group_lanes(ref, rows, g):
    lanes = ref.shape[1]
    return pltpu.roll(ref[rows, :], (lanes - SSD_HPG * g) % lanes, 1)


def _group_rows(ref, first, cols, g):
    tile = 8
    per_tile = tile // SSD_HPG
    base = pl.multiple_of(first + (g // per_tile) * tile, tile)
    return pltpu.roll(ref[pl.ds(base, tile), cols], (tile - SSD_HPG * (g % per_tile)) % tile, 0)


def _head_selector(lanes, first):
    width = SSD_HPG * SSD_HEAD_DIM
    src = lax.broadcasted_iota(jnp.int32, (lanes, width), 0)
    dst = lax.broadcasted_iota(jnp.int32, (lanes, width), 1) // SSD_HEAD_DIM
    return (src == dst + first).astype(BF16)


def _expand_heads(cols, selector):
    return sum(jnp.dot(p, selector, preferred_element_type=F32) for p in _split3(cols))


def _weighted_inputs(xs_ref, wend_ref, g, first, xw_scr):
    rows = slice(0, xs_ref.shape[0])
    wend = _group_lanes(wend_ref, rows, g)
    xw_scr[...] = (xs_ref[...] * _expand_heads(wend, _head_selector(wend_ref.shape[1], first))).astype(xw_scr.dtype)


def _ssd_back_kernel(xs_ref, bm_ref, acs_ref, wend_ref, out_ref, carried, xw_scr):
    q = SSD_CHUNK
    nchunk = xs_ref.shape[0] // q
    g = pl.program_id(1)
    bwd = SSD_HEADS

    @pl.when(pl.program_id(2) == 0)
    def _():
        carried[...] = jnp.zeros_like(carried)

    _weighted_inputs(xs_ref, wend_ref, g, bwd, xw_scr)

    def chunk(ci, carry):
        c = nchunk - 1 - ci
        rows = pl.ds(pl.multiple_of(c * q, q), q)
        first_row = pl.ds(pl.multiple_of(c * q, q), 8)
        total = _group_lanes(acs_ref, first_row, g)[0:1]
        states = lax.dot_general(bm_ref[rows, :].astype(BF16), xw_scr[rows, :], (((0,), (0,)), ((), ())),
                                 preferred_element_type=F32)
        prev = carried[...]
        out_ref[c] = prev.astype(out_ref.dtype)
        carried[...] = jnp.exp(_per_head([total[:, bwd + r:bwd + r + 1] for r in range(SSD_HPG)])) * prev + states
        return carry

    lax.fori_loop(0, nchunk, chunk, 0, unroll=4)


def _ssd_main_kernel(xs_ref, bm_ref, cm_ref, acs_c_ref, wend_c_ref, acs_r_ref, delta_r_ref, back_ref,
                     dskip_ref, y_ref, carried):
    q = SSD_CHUNK
    nchunk = xs_ref.shape[0] // q
    g = pl.program_id(1)
    bwd = SSD_HEADS
    hd = SSD_HEAD_DIM

    @pl.when(pl.program_id(2) == 0)
    def _():
        carried[...] = jnp.zeros_like(carried)

    row = lax.broadcasted_iota(jnp.int32, (q, q), 0)
    col = lax.broadcasted_iota(jnp.int32, (q, q), 1)
    lower = col <= row
    upper = col >= row
    d_wide = _per_head([dskip_ref[:, r:r + 1] for r in range(SSD_HPG)])
    selector = _head_selector(wend_c_ref.shape[1], 0)
    first = lax.broadcasted_iota(jnp.int32, (q, 2 * hd), 1) < hd

    def per_head_lanes(blocks):
        return jnp.concatenate([jnp.where(first, blocks[0], blocks[1]), jnp.where(first, blocks[2], blocks[3])], axis=1)

    def chunk(c, carry):
        r0 = pl.multiple_of(c * q, q)
        rows = pl.ds(r0, q)
        xs = xs_ref[rows, :]
        bmat = bm_ref[rows, :].astype(BF16)
        cmat = cm_ref[rows, :].astype(BF16)
        acs = _group_lanes(acs_c_ref, rows, g)
        wend = _group_lanes(wend_c_ref, rows, g)
        acs_f = [jnp.broadcast_to(acs[:, r:r + 1], (q, q)) for r in range(SSD_HPG)]
        acs_b = [jnp.broadcast_to(acs[:, bwd + r:bwd + r + 1], (q, q)) for r in range(SSD_HPG)]
        acs_rf = _group_rows(acs_r_ref, 0, rows, g)
        acs_rb = _group_rows(acs_r_ref, bwd, rows, g)
        del_rf = _group_rows(delta_r_ref, 0, rows, g)
        del_rb = _group_rows(delta_r_ref, bwd, rows, g)
        cb = lax.dot_general(cmat, bmat, (((1,), (1,)), ((), ())), preferred_element_type=F32)
        prev = carried[...]
        states_in = jnp.concatenate([prev.astype(BF16), back_ref[c]], axis=1)
        y_off = jnp.dot(cmat, states_in, preferred_element_type=F32)
        e_f = jnp.exp(per_head_lanes(acs_f))
        e_b = jnp.exp(per_head_lanes(acs_b))
        rest = e_f * y_off[:, 0:SSD_HPG * hd] + e_b * y_off[:, SSD_HPG * hd:] + d_wide * xs
        for pair in range(SSD_HPG // 2):
            lanes = slice(2 * pair * hd, 2 * (pair + 1) * hd)
            x_pair = xs[:, lanes]
            y_pair = rest[:, lanes]
            for half in range(2):
                r = 2 * pair + half
                seg_f = acs_f[r] - acs_rf[r:r + 1, :]
                seg_b = acs_b[r] - acs_rb[r:r + 1, :]
                w = (jnp.exp(jnp.where(lower, seg_f, ATT_NEG)) * del_rf[r:r + 1, :]
                     + jnp.exp(jnp.where(upper, seg_b, ATT_NEG)) * del_rb[r:r + 1, :])
                x_own = jnp.where(first if half == 0 else ~first, x_pair, 0.0).astype(BF16)
                y_pair = y_pair + jnp.dot((cb * w).astype(BF16), x_own, preferred_element_type=F32)
            y_ref[rows, lanes] = y_pair
        xw = xs * _expand_heads(wend, selector)
        states = lax.dot_general(bmat, xw.astype(BF16), (((0,), (0,)), ((), ())), preferred_element_type=F32)
        total = acs[q - 1:q]
        carried[...] = jnp.exp(_per_head([total[:, r:r + 1] for r in range(SSD_HPG)])) * prev + states
        return carry

    lax.fori_loop(0, nchunk, chunk, 0, unroll=8)


def _ssd_scan(xs_act, bc_act, acs_c, wend_c, acs_r, delta_r, d_skip):
    bsz, seqlen, _ = xs_act.shape
    sb = min(SSD_SEQ_BLOCK, seqlen)
    nblk = seqlen // sb
    cpb = sb // SSD_CHUNK
    n_x = SSD_HPG * SSD_HEAD_DIM
    b_off = 0
    c_off = SSD_GROUPS
    lanes = acs_c.shape[1]
    nrow = acs_r.shape[0]

    def specs(blk):
        xs = pl.BlockSpec((None, sb, n_x), lambda b_, g, i: (b_, blk(i), g))
        bm = pl.BlockSpec((None, sb, SSD_STATE), lambda b_, g, i: (b_, blk(i), b_off + g))
        cm = pl.BlockSpec((None, sb, SSD_STATE), lambda b_, g, i: (b_, blk(i), c_off + g))
        colf = pl.BlockSpec((sb, lanes), lambda b_, g, i: (b_ * nblk + blk(i), 0))
        rowf = pl.BlockSpec((nrow, sb), lambda b_, g, i: (0, b_ * nblk + blk(i)))
        back = pl.BlockSpec((None, None, cpb, SSD_STATE, n_x), lambda b_, g, i: (b_, g, blk(i), 0, 0))
        return xs, bm, cm, colf, rowf, back

    xs, bm, cm, colf, rowf, back = specs(lambda i: nblk - 1 - i)
    assert SSD_CHUNK == 2 * SSD_HEAD_DIM and SSD_HPG == 4
    state = [pltpu.VMEM((SSD_STATE, n_x), F32), pltpu.VMEM((sb, n_x), BF16)]
    back_states = pl.pallas_call(
        _ssd_back_kernel,
        grid=(bsz, SSD_GROUPS, nblk),
        in_specs=[xs, bm, colf, colf],
        out_specs=back,
        out_shape=jax.ShapeDtypeStruct((bsz, SSD_GROUPS, seqlen // SSD_CHUNK, SSD_STATE, n_x), BF16),
        scratch_shapes=state,
        compiler_params=_params("parallel", "parallel", "arbitrary"),
        name="ssd_back",
    )(xs_act, bc_act, acs_c, wend_c)

    xs, bm, cm, colf, rowf, back = specs(lambda i: i)
    return pl.pallas_call(
        _ssd_main_kernel,
        grid=(bsz, SSD_GROUPS, nblk),
        in_specs=[xs, bm, cm, colf, colf, rowf, rowf, back,
                  pl.BlockSpec((None, 1, SSD_HPG), lambda b_, g, i: (g, 0, 0))],
        out_specs=pl.BlockSpec((None, sb, n_x), lambda b_, g, i: (b_, i, g)),
        out_shape=jax.ShapeDtypeStruct((bsz, seqlen, SSD_INNER), F32),
        scratch_shapes=state[:1],
        compiler_params=_params("parallel", "parallel", "arbitrary"),
        name="ssd_main",
    )(xs_act, bc_act, bc_act, acs_c, wend_c, acs_r, delta_r, back_states,
      d_skip.reshape(SSD_GROUPS, 1, SSD_HPG))


def _attn_kernel(q_ref, ko_ref, kp_ref, kn_ref, vo_ref, vp_ref, vn_ref, o_ref, lse_ref, kcat, vcat, bias,
                 *, dil, group):
    tq = q_ref.shape[0]
    rad = ATT_RADIUS
    qs = ATT_QSUB
    kw = qs + 2 * rad
    nsub = tq // qs
    i = pl.program_id(1)
    kcat[0:rad, :] = kp_ref[...].astype(BF16)
    kcat[rad:rad + tq, :] = ko_ref[...].astype(BF16)
    kcat[rad + tq:2 * rad + tq, :] = kn_ref[...].astype(BF16)
    vcat[0:rad, :] = vp_ref[...].astype(BF16)
    vcat[rad:rad + tq, :] = vo_ref[...].astype(BF16)
    vcat[rad + tq:2 * rad + tq, :] = vn_ref[...].astype(BF16)

    row = lax.broadcasted_iota(jnp.int32, (qs, kw), 0)
    col = lax.broadcasted_iota(jnp.int32, (qs, kw), 1)
    dist = jnp.abs(row - col + rad)
    for head in range(ATT_HPG):
        slope = 2.0 ** (-8.0 * (head * ATT_GROUPS + group + 1) / ATT_HEADS)
        bias[head] = jnp.where(dist <= rad, (-slope * dil) * dist.astype(F32), ATT_NEG)
    before_start = (i == 0) & (col < rad)
    past_end = (i == pl.num_programs(1) - 1) & (col >= qs + rad)
    lane = lax.broadcasted_iota(jnp.int32, (qs, 2 * ATT_HEAD_DIM), 1)
    first_half = lane < ATT_HEAD_DIM
    scale = ATT_HEAD_DIM ** -0.5

    for m in range(nsub):
        for pair in range(ATT_HPG // 2):
            lanes = slice(pair * 2 * ATT_HEAD_DIM, (pair + 1) * 2 * ATT_HEAD_DIM)
            q2 = q_ref[m * qs:(m + 1) * qs, lanes]
            k2 = kcat[m * qs:m * qs + kw, lanes]
            v2 = vcat[m * qs:m * qs + kw, lanes]
            for half in range(2):
                head = 2 * pair + half
                qm = jnp.where(first_half if half == 0 else ~first_half, q2, 0.0).astype(BF16)
                s = lax.dot_general(qm, k2, (((1,), (1,)), ((), ())), preferred_element_type=F32)
                s = s * scale + bias[head]
                if m == 0:
                    s = jnp.where(before_start, ATT_NEG, s)
                if m == nsub - 1:
                    s = jnp.where(past_end, ATT_NEG, s)
                mx = jnp.max(s, axis=-1, keepdims=True)
                p = jnp.exp(s - mx)
                zsum = jnp.sum(p, axis=-1, keepdims=True)
                pv = jnp.dot(p.astype(BF16), v2, preferred_element_type=F32)
                own = slice(half * ATT_HEAD_DIM, (half + 1) * ATT_HEAD_DIM)
                out_lanes = slice(lanes.start + own.start, lanes.start + own.stop)
                o_ref[m * qs:(m + 1) * qs, out_lanes] = (pv / zsum)[:, own]
                lse_ref[m * qs:(m + 1) * qs, out_lanes] = jnp.broadcast_to(mx + jnp.log(zsum), (qs, ATT_HEAD_DIM))


def _attention_group(qkv, group, dil):
    nseq, ls, _ = qkv.shape
    tq = min(512, ls)
    nblk = ls // tq
    halo_per_blk = tq // ATT_RADIUS
    n_halo = ls // ATT_RADIUS

    def own(off):
        return pl.BlockSpec((None, tq, ATT_OUT), lambda s, i: (s, i, off))

    def prev(off):
        return pl.BlockSpec((None, ATT_RADIUS, ATT_OUT), lambda s, i: (s, jnp.maximum(i * halo_per_blk - 1, 0), off))

    def nxt(off):
        return pl.BlockSpec((None, ATT_RADIUS, ATT_OUT),
                            lambda s, i: (s, jnp.minimum((i + 1) * halo_per_blk, n_halo - 1), off))

    out_spec = pl.BlockSpec((None, tq, ATT_OUT), lambda s, i: (s, i, 0))
    return pl.pallas_call(
        functools.partial(_attn_kernel, dil=dil, group=group),
        grid=(nseq, nblk),
        in_specs=[own(0), own(1), prev(1), nxt(1), own(2), prev(2), nxt(2)],
        out_specs=[out_spec, out_spec],
        out_shape=[jax.ShapeDtypeStruct((nseq, ls, ATT_OUT), F32)] * 2,
        scratch_shapes=[pltpu.VMEM((tq + 2 * ATT_RADIUS, ATT_OUT), BF16)] * 2
        + [pltpu.VMEM((ATT_HPG, ATT_QSUB, ATT_QSUB + 2 * ATT_RADIUS), F32)],
        compiler_params=_params("parallel", "parallel"),
        name=f"attn_g{group}",
    )(qkv, qkv, qkv, qkv, qkv, qkv, qkv)


def _s5_tables(a_re, a_im, log_step, b_re, b_im, c_re, c_im):
    hp = lax.Precision.HIGHEST
    tc = S5_TC
    ng, width = S5_GROUPS, S5_ROW
    step = jnp.exp(log_step)[..., None]
    ar, ai = a_re, a_im
    mag = jnp.exp(step * ar)
    abr, abi = mag * jnp.cos(step * ai), mag * jnp.sin(step * ai)
    den = ar * ar + ai * ai
    fr = ((abr - 1.0) * ar + abi * ai) / den
    fi = (abi * ar - (abr - 1.0) * ai) / den
    b_re_t, b_im_t = b_re.transpose(0, 2, 1), b_im.transpose(0, 2, 1)
    bbr = fr[:, :, None, :] * b_re_t[None] - fi[:, :, None, :] * b_im_t[None]
    bbi = fr[:, :, None, :] * b_im_t[None] + fi[:, :, None, :] * b_re_t[None]
    ks = jnp.arange(tc + 1, dtype=F32)
    arg_r, arg_i = step * ar, step * ai
    pk_r = jnp.exp(arg_r[:, :, None, :] * ks[:, None]) * jnp.cos(arg_i[:, :, None, :] * ks[:, None])
    pk_i = jnp.exp(arg_r[:, :, None, :] * ks[:, None]) * jnp.sin(arg_i[:, :, None, :] * ks[:, None])
    pn_r = jnp.exp(arg_r[..., None] * ks) * jnp.cos(arg_i[..., None] * ks)
    pn_i = jnp.exp(arg_r[..., None] * ks) * jnp.sin(arg_i[..., None] * ks)
    c_re_t, c_im_t = c_re.transpose(0, 1, 3, 2), c_im.transpose(0, 1, 3, 2)
    cap_r = (c_re_t[:, :, :, None, :] * pn_r[..., None] - c_im_t[:, :, :, None, :] * pn_i[..., None])
    cap_i = (c_re_t[:, :, :, None, :] * pn_i[..., None] + c_im_t[:, :, :, None, :] * pn_r[..., None])
    cap_r = cap_r.reshape(2, ng, S5_STATE, (tc + 1) * S5_CH)
    cap_i = cap_i.reshape(2, ng, S5_STATE, (tc + 1) * S5_CH)
    taps = (jnp.einsum('dgin,dgnm->dgim', bbr, cap_r, precision=hp)
            - jnp.einsum('dgin,dgnm->dgim', bbi, cap_i, precision=hp))[..., :width]
    fwd_src = jnp.pad(taps[0], ((0, 0), (0, 0), (width, 0)))
    bwd_rev = taps[1].reshape(ng, S5_CH, tc, S5_CH)[:, :, ::-1, :].reshape(ng, S5_CH, width)
    bwd_src = jnp.pad(bwd_rev, ((0, 0), (0, 0), (0, width)))
    toep = jnp.stack([fwd_src[:, :, width - S5_CH * s:2 * width - S5_CH * s]
                      + bwd_src[:, :, S5_CH * (tc - 1 - s):S5_CH * (tc - 1 - s) + width] for s in range(tc)], axis=1)
    toep = toep.reshape(ng, width, width)

    def end_state(d, pw_r, pw_i):
        re = pw_r[:, :, None, :] * bbr[d][:, None] - pw_i[:, :, None, :] * bbi[d][:, None]
        im = pw_r[:, :, None, :] * bbi[d][:, None] + pw_i[:, :, None, :] * bbr[d][:, None]
        return [re.reshape(ng, width, S5_STATE), im.reshape(ng, width, S5_STATE)]

    b_end = jnp.concatenate(end_state(0, pk_r[0][:, tc - 1::-1], pk_i[0][:, tc - 1::-1])
                            + end_state(1, pk_r[1][:, :tc], pk_i[1][:, :tc]), axis=-1)

    def lag_reversed(a):
        return a.reshape(ng, S5_STATE, tc, S5_CH)[:, :, ::-1, :].reshape(ng, S5_STATE, width)

    c_out = jnp.concatenate([cap_r[0][..., S5_CH:], -cap_i[0][..., S5_CH:],
                             lag_reversed(cap_r[1][..., S5_CH:]), -lag_reversed(cap_i[1][..., S5_CH:])], axis=1)
    ptr, pti = pk_r[:, :, tc], pk_i[:, :, tc]
    p_mul = jnp.concatenate([ptr[0], ptr[0], ptr[1], ptr[1]], axis=-1)
    q_mul = jnp.concatenate([-pti[0], pti[0], -pti[1], pti[1]], axis=-1)
    return toep, b_end, c_out, p_mul, q_mul


def _granule_transpose(regs, granule):
    n = len(regs)
    d = n // 2
    while d >= 1:
        low = (granule // d) % 2 == 0
        nxt = list(regs)
        for i in range(n):
            if (i // d) % 2 == 0:
                a, b = regs[i], regs[i + d]
                nxt[i] = jnp.where(low, a, pltpu.roll(b, d * S5_CH, 1))
                nxt[i + d] = jnp.where(low, pltpu.roll(a, 128 - d * S5_CH, 1), b)
        regs = nxt
        d //= 2
    return regs


def _s5_rows_to_groups(u_ref, x_scr):
    rr = S5_RELAYOUT_ROWS
    per = S5_LANE_GROUPS
    granule = lax.broadcasted_iota(jnp.int32, (rr, 128), 1) // S5_CH

    def body(rb, carry):
        base = pl.multiple_of(rb * (rr * S5_TC), rr * S5_TC)
        rows = pl.ds(pl.multiple_of(rb * rr, rr), rr)
        for h in range(S5_TC // per):
            steps = [u_ref[pl.ds(base + per * h + t, rr, stride=S5_TC), :] for t in range(per)]
            for gl, block in enumerate(_granule_transpose(steps, granule)):
                x_scr[gl, rows, 128 * h:128 * (h + 1)] = block
        return carry

    lax.fori_loop(0, x_scr.shape[1] // rr, body, 0)


def _s5_in_kernel(u_ref, toep_ref, bend_ref, y_ref, sf_ref, sb_ref, x_scr):
    half = 2 * S5_STATE
    _s5_rows_to_groups(u_ref, x_scr)
    for gl in range(S5_LANE_GROUPS):
        xb = x_scr[gl].astype(BF16)
        y_ref[gl] = jnp.dot(xb, toep_ref[gl], preferred_element_type=F32)
        s = jnp.dot(xb, bend_ref[gl], preferred_element_type=F32)
        sf_ref[gl] = s[:, 0:half]
        sb_ref[gl] = s[:, half:2 * half]


def _s5_scan_kernel(sf_ref, sb_ref, p_ref, q_ref, hf_ref, hb_ref, sf_sw_ref, sb_sw_ref, *, rows_per_seq):
    n_chain = p_ref.shape[0]
    half = 2 * S5_STATE
    pf, qf = p_ref[:, 0:half], q_ref[:, 0:half]
    pb, qb = p_ref[:, half:2 * half], q_ref[:, half:2 * half]
    qf_sw = pltpu.roll(qf, S5_STATE, 1)
    qb_sw = pltpu.roll(qb, S5_STATE, 1)
    zero = jnp.zeros((n_chain, half), F32)
    piece = 256

    def swap(k, carry):
        rows = pl.ds(pl.multiple_of(k * piece, piece), piece)
        sf_sw_ref[rows, :] = pltpu.roll(sf_ref[rows, :], S5_STATE, 1)
        sb_sw_ref[rows, :] = pltpu.roll(sb_ref[rows, :], S5_STATE, 1)
        return carry

    lax.fori_loop(0, sf_ref.shape[0] // piece, swap, 0)

    tile = 8

    def steps(c8, carry):
        hf, hf_sw, hb, hb_sw = carry
        base_f = pl.multiple_of(c8 * tile, tile)
        base_b = pl.multiple_of(rows_per_seq - tile - c8 * tile, tile)
        for k in range(tile):
            rows_f = pl.ds(base_f + k, n_chain, stride=rows_per_seq)
            rows_b = pl.ds(base_b + (tile - 1 - k), n_chain, stride=rows_per_seq)
            hf_ref[rows_f, :] = hf
            hb_ref[rows_b, :] = hb
            sf = sf_ref[rows_f, :]
            sb = sb_ref[rows_b, :]
            sf_sw = sf_sw_ref[rows_f, :]
            sb_sw = sb_sw_ref[rows_b, :]
            hf, hf_sw = pf * hf + qf * hf_sw + sf, pf * hf_sw + qf_sw * hf + sf_sw
            hb, hb_sw = pb * hb + qb * hb_sw + sb, pb * hb_sw + qb_sw * hb + sb_sw
        return hf, hf_sw, hb, hb_sw

    lax.fori_loop(0, rows_per_seq // tile, steps, (zero, zero, zero, zero))


def _s5_out_kernel(hf_ref, hb_ref, cout_ref, y_ref, u_ref, d_ref, o_ref, y_scr):
    half = 2 * S5_STATE
    per = S5_LANE_GROUPS
    for gl in range(per):
        y = y_ref[gl] + jnp.dot(hf_ref[gl].astype(BF16), cout_ref[gl, 0:half, :], preferred_element_type=F32)
        y_scr[gl] = y + jnp.dot(hb_ref[gl].astype(BF16), cout_ref[gl, half:2 * half, :], preferred_element_type=F32)
    rr = S5_RELAYOUT_ROWS
    granule = lax.broadcasted_iota(jnp.int32, (rr, 128), 1) // S5_CH
    d_skip = d_ref[...]

    def body(rb, carry):
        rows = pl.ds(pl.multiple_of(rb * rr, rr), rr)
        base = pl.multiple_of(rb * (rr * S5_TC), rr * S5_TC)
        for h in range(S5_TC // per):
            groups = [y_scr[gl, rows, 128 * h:128 * (h + 1)] for gl in range(per)]
            for t, block in enumerate(_granule_transpose(groups, granule)):
                tok = pl.ds(base + per * h + t, rr, stride=S5_TC)
                o_ref[tok, :] = jax.nn.gelu(block + u_ref[tok, :] * d_skip)
        return carry

    lax.fori_loop(0, y_scr.shape[1] // rr, body, 0)


def _s5_all_tables(a_re, a_im, log_step, b_re, b_im, c_re, c_im):
    toep, b_end, c_out, p_mul, q_mul = jax.vmap(_s5_tables)(a_re, a_im, log_step, b_re, b_im, c_re, c_im)
    return toep.astype(BF16), b_end.astype(BF16), c_out.astype(BF16), p_mul, q_mul


def _s5_mix(u_tiles, seqlen, tables, layer, d_skip):
    ntile, ntok, lanes = u_tiles.shape
    bsz = ntok // seqlen
    rows_per_seq = seqlen // S5_TC
    rows = bsz * rows_per_seq
    rblk = min(S5_ROW_BLOCK, rows)
    per = S5_LANE_GROUPS
    half = 2 * S5_STATE
    toep, b_end, c_out, p_mul, q_mul = tables
    p_mul, q_mul = p_mul[layer], q_mul[layer]
    utile = pl.BlockSpec((None, rblk * S5_TC, lanes), lambda j, i: (j, i, 0))
    ytile = pl.BlockSpec((per, rblk, S5_ROW), lambda j, i: (j, i, 0))
    stile = pl.BlockSpec((per, rblk, half), lambda j, i: (j, i, 0))
    wtile = pl.BlockSpec((None, per, S5_ROW, S5_ROW), lambda j, i: (layer, j, 0, 0))
    state_shape = jax.ShapeDtypeStruct((S5_GROUPS, rows, half), F32)
    grid = (ntile, rows // rblk)
    y_in, s_f, s_b = pl.pallas_call(
        _s5_in_kernel,
        grid=grid,
        in_specs=[utile, wtile, wtile],
        out_specs=[ytile, stile, stile],
        out_shape=[jax.ShapeDtypeStruct((S5_GROUPS, rows, S5_ROW), F32), state_shape, state_shape],
        scratch_shapes=[pltpu.VMEM((per, rblk, S5_ROW), F32)],
        compiler_params=_params("parallel", "parallel"),
        name="s5_in",
    )(u_tiles, toep, b_end)

    gs = S5_SCAN_GROUPS
    n_chain = gs * bsz
    chain_p = jnp.repeat(p_mul, bsz, axis=0)
    chain_q = jnp.repeat(q_mul, bsz, axis=0)
    flat = jax.ShapeDtypeStruct((S5_GROUPS * rows, half), F32)
    srows = pl.BlockSpec((gs * rows, half), lambda t: (t, 0))
    chain = pl.BlockSpec((n_chain, S5_ROW), lambda t: (t, 0))
    h_f, h_b = pl.pallas_call(
        functools.partial(_s5_scan_kernel, rows_per_seq=rows_per_seq),
        grid=(S5_GROUPS // gs,),
        in_specs=[srows, srows, chain, chain],
        out_specs=[srows, srows],
        out_shape=[flat, flat],
        scratch_shapes=[pltpu.VMEM((gs * rows, half), F32)] * 2,
        compiler_params=_params("parallel"),
        name="s5_scan",
    )(s_f.reshape(S5_GROUPS * rows, half), s_b.reshape(S5_GROUPS * rows, half), chain_p, chain_q)

    return pl.pallas_call(
        _s5_out_kernel,
        grid=grid,
        in_specs=[stile, stile, wtile, ytile, utile, pl.BlockSpec((None, 1, lanes), lambda j, i: (j, 0, 0))],
        out_specs=utile,
        out_shape=jax.ShapeDtypeStruct((ntile, ntok, lanes), F32),
        scratch_shapes=[pltpu.VMEM((per, rblk, S5_ROW), F32)],
        compiler_params=_params("parallel", "parallel"),
        name="s5_out",
    )(h_f.reshape(S5_GROUPS, rows, half), h_b.reshape(S5_GROUPS, rows, half), c_out, y_in, u_tiles,
      d_skip.reshape(ntile, 1, lanes))


def _ssd_proj_kernel(y_ref, z_ref, nw_ref, w_ref, gate_ref, bg_ref, o_ref):
    z = z_ref[...]
    y = y_ref[...] * (z * _sigmoid(z))
    y = y * lax.rsqrt(jnp.mean(y * y, axis=-1, keepdims=True) + LN_EPS) * nw_ref[...]
    p = jnp.dot(y.astype(BF16), w_ref[...], preferred_element_type=F32)
    o_ref[...] = _sigmoid(gate_ref[...] + bg_ref[...]) * p


def _ssd_project(y, z, norm_w, w_br, gates, b_gate, *, tm=512):
    m = y.shape[0]
    wide = pl.BlockSpec((tm, SSD_INNER), lambda i: (i, 0))
    return pl.pallas_call(
        _ssd_proj_kernel,
        grid=(m // tm,),
        in_specs=[wide, wide, pl.BlockSpec((1, SSD_INNER), lambda i: (0, 0)),
                  pl.BlockSpec((SSD_INNER, D_MODEL), lambda i: (0, 0)),
                  pl.BlockSpec((tm, D_MODEL), lambda i: (i, 0)),
                  pl.BlockSpec((1, D_MODEL), lambda i: (0, 0))],
        out_specs=pl.BlockSpec((tm, D_MODEL), lambda i: (i, 0)),
        out_shape=jax.ShapeDtypeStruct((m, D_MODEL), F32),
        compiler_params=_params("parallel"),
        name="ssd_project",
    )(y, z, norm_w.reshape(1, -1), w_br, gates, b_gate[0:1])


def _s5_proj_kernel(h_ref, w1_ref, w2_ref, w_ref, gate_ref, bg_ref, o_ref):
    hb = jnp.concatenate([h_ref[t] for t in range(h_ref.shape[0])], axis=1).astype(BF16)
    a = jnp.dot(hb, w1_ref[...], preferred_element_type=F32)
    b = jnp.dot(hb, w2_ref[...], preferred_element_type=F32)
    p = jnp.dot((a * _sigmoid(b)).astype(BF16), w_ref[...], preferred_element_type=F32)
    o_ref[...] = _sigmoid(gate_ref[...] + bg_ref[...]) * p


def _s5_project(h_tiles, w1, w2, w_br, gates, b_gate, *, tm=512):
    ntile, m, lanes = h_tiles.shape
    sq = pl.BlockSpec((S5_WIDTH, S5_WIDTH), lambda i: (0, 0))
    return pl.pallas_call(
        _s5_proj_kernel,
        grid=(m // tm,),
        in_specs=[pl.BlockSpec((ntile, tm, lanes), lambda i: (0, i, 0)), sq, sq,
                  pl.BlockSpec((S5_WIDTH, D_MODEL), lambda i: (0, 0)),
                  pl.BlockSpec((tm, D_MODEL), lambda i: (i, 2)),
                  pl.BlockSpec((1, D_MODEL), lambda i: (0, 0))],
        out_specs=pl.BlockSpec((tm, D_MODEL), lambda i: (i, 0)),
        out_shape=jax.ShapeDtypeStruct((m, D_MODEL), F32),
        compiler_params=_params("parallel"),
        name="s5_project",
    )(h_tiles, w1, w2, w_br, gates, b_gate[2:3])


def _layer_norm(v, g, b):
    mu = jnp.mean(v, axis=-1, keepdims=True)
    c = v - mu
    var = jnp.mean(c * c, axis=-1, keepdims=True)
    return c * lax.rsqrt(var + LN_EPS) * g + b


def _merge_kernel(o0_ref, o1_ref, o2_ref, l0_ref, l1_ref, l2_ref, w_att_ref, gate_ref, bg_ref, pssd_ref, ps5_ref,
                  w_out_ref, x_ref, g_ref, b_ref, o_ref):
    l0, l1, l2 = l0_ref[...], l1_ref[...], l2_ref[...]
    mx = jnp.maximum(jnp.maximum(l0, l1), l2)
    e0, e1, e2 = jnp.exp(l0 - mx), jnp.exp(l1 - mx), jnp.exp(l2 - mx)
    den = e0 + e1 + e2
    att = (e0 / den) * o0_ref[...] + (e1 / den) * o1_ref[...] + (e2 / den) * o2_ref[...]
    p_att = jnp.dot(att.astype(BF16), w_att_ref[...], preferred_element_type=F32)
    merged = pssd_ref[...] + _sigmoid(gate_ref[...] + bg_ref[...]) * p_att + ps5_ref[...]
    mix = jnp.dot(merged.astype(BF16), w_out_ref[...], preferred_element_type=F32)
    o_ref[...] = _layer_norm(ALPHA * x_ref[...] + mix, g_ref[...], b_ref[...])


def _merge(att_o, att_l, w_att, gates, b_gate, p_ssd, p_s5, w_out, x, ln_g, ln_b, *, tm=512):
    m = x.shape[0]
    att = pl.BlockSpec((tm, ATT_OUT), lambda i: (i, 0))
    row = pl.BlockSpec((tm, D_MODEL), lambda i: (i, 0))
    vec = pl.BlockSpec((1, D_MODEL), lambda i: (0, 0))
    return pl.pallas_call(
        _merge_kernel,
        grid=(m // tm,),
        in_specs=[att] * 6 + [pl.BlockSpec((ATT_OUT, D_MODEL), lambda i: (0, 0)),
                              pl.BlockSpec((tm, D_MODEL), lambda i: (i, 1)), vec, row, row,
                              pl.BlockSpec((D_MODEL, D_MODEL), lambda i: (0, 0)), row, vec, vec],
        out_specs=row,
        out_shape=jax.ShapeDtypeStruct((m, D_MODEL), F32),
        compiler_params=_params("parallel"),
        name="merge_ln1",
    )(*att_o, *att_l, w_att, gates, b_gate[1:2], p_ssd, p_s5, w_out, x, ln_g.reshape(1, -1), ln_b.reshape(1, -1))


def _router_kernel(x_ref, wt_ref, b_ref, idx_ref, gate_ref, rank_ref, count_ref, before_ref):
    tm = x_ref.shape[0]

    @pl.when(pl.program_id(0) == 0)
    def _():
        count_ref[...] = jnp.zeros_like(count_ref)
        earlier = lax.broadcasted_iota(jnp.int32, (tm, tm), 0) < lax.broadcasted_iota(jnp.int32, (tm, tm), 1)
        before_ref[...] = earlier.astype(before_ref.dtype)

    x = x_ref[...]
    wt = wt_ref[...]
    xh = x.astype(BF16)
    xl = (x - xh.astype(F32)).astype(BF16)
    wh = wt.astype(BF16)
    wl = (wt - wh.astype(F32)).astype(BF16)
    nt = (((1,), (1,)), ((), ()))
    logits = (lax.dot_general(wh, xh, nt, preferred_element_type=F32)
              + lax.dot_general(wl, xh, nt, preferred_element_type=F32)
              + lax.dot_general(wh, xl, nt, preferred_element_type=F32)) + b_ref[...]
    eid = lax.broadcasted_iota(jnp.int32, logits.shape, 0)
    vals, idxs = [], []
    for _ in range(MOE_TOP_K):
        mx = jnp.max(logits, axis=0, keepdims=True)
        sel = jnp.min(jnp.where(logits == mx, eid, MOE_EXPERTS), axis=0, keepdims=True)
        vals.append(mx)
        idxs.append(sel)
        logits = jnp.where(eid == sel, -jnp.inf, logits)
    es = [jnp.exp(v - vals[0]) for v in vals]
    den = es[0] + es[1] + es[2] + es[3]
    for k in range(MOE_TOP_K):
        idx_ref[k:k + 1, :] = idxs[k]
        gate_ref[k:k + 1, :] = es[k] / den
    chosen = [eid == idxs[k] for k in range(MOE_TOP_K)]
    stacked = jnp.concatenate([c.astype(before_ref.dtype) for c in chosen], axis=0)
    earlier = jnp.dot(stacked, before_ref[...], preferred_element_type=F32)
    seen = count_ref[...]
    for k in range(MOE_TOP_K):
        pos = earlier[k * MOE_EXPERTS:(k + 1) * MOE_EXPERTS] + seen
        rank_ref[k:k + 1, :] = jnp.sum(jnp.where(chosen[k], pos, 0.0), axis=0, keepdims=True).astype(jnp.int32)
        seen = seen + jnp.sum(chosen[k].astype(F32), axis=1, keepdims=True)
    count_ref[...] = seen


def _router(x, router_w, router_b, *, tm=1024):
    m = x.shape[0]
    out = pl.BlockSpec((MOE_TOP_K, tm), lambda i: (0, i))
    return pl.pallas_call(
        _router_kernel,
        grid=(m // tm,),
        in_specs=[pl.BlockSpec((tm, D_MODEL), lambda i: (i, 0)),
                  pl.BlockSpec((MOE_EXPERTS, D_MODEL), lambda i: (0, 0)),
                  pl.BlockSpec((MOE_EXPERTS, 1), lambda i: (0, 0))],
        out_specs=[out, out, out, pl.BlockSpec((MOE_EXPERTS, 1), lambda i: (0, 0))],
        out_shape=[jax.ShapeDtypeStruct((MOE_TOP_K, m), jnp.int32), jax.ShapeDtypeStruct((MOE_TOP_K, m), F32),
                   jax.ShapeDtypeStruct((MOE_TOP_K, m), jnp.int32), jax.ShapeDtypeStruct((MOE_EXPERTS, 1), F32)],
        scratch_shapes=[pltpu.VMEM((tm, tm), BF16)],
        compiler_params=_params("arbitrary"),
        name="router",
    )(x, router_w.T, router_b.reshape(-1, 1))


def _expert_kernel(be_ref, used_ref, x_ref, wgu_ref, bgu_ref, wdn_ref, bdn_ref, o_ref, wgu_lo, wdn_lo):
    i = pl.program_id(0)
    used = i < used_ref[0]

    @pl.when(used & ((i == 0) | (be_ref[i] != be_ref[jnp.maximum(i - 1, 0)])))
    def _():
        wgu_lo[...] = wgu_ref[...].astype(wgu_lo.dtype)
        wdn_lo[...] = wdn_ref[...].astype(wdn_lo.dtype)

    @pl.when(used)
    def _():
        hu = jnp.dot(x_ref[...].astype(BF16), wgu_lo[...], preferred_element_type=F32) + bgu_ref[...]
        x_glu = jnp.minimum(hu[:, :MOE_FF], SWIGLU_LIMIT)
        x_lin = jnp.clip(hu[:, MOE_FF:], -SWIGLU_LIMIT, SWIGLU_LIMIT)
        act = x_glu * _sigmoid(SWIGLU_ALPHA * x_glu) * (x_lin + 1.0)
        o_ref[...] = jnp.dot(act.astype(BF16), wdn_lo[...], preferred_element_type=F32) + bdn_ref[...]

    @pl.when(jnp.logical_not(used))
    def _():
        o_ref[...] = jnp.zeros_like(o_ref)


def _experts(xs, block_expert, n_used, w_gu, b_gu, w_dn, b_dn, layer):
    n_rows = xs.shape[0]
    n_blocks = n_rows // MOE_ROWS
    grid_spec = pltpu.PrefetchScalarGridSpec(
        num_scalar_prefetch=2,
        grid=(n_blocks,),
        in_specs=[
            pl.BlockSpec((MOE_ROWS, D_MODEL), lambda i, be, nu: (i, 0)),
            pl.BlockSpec((None, None, D_MODEL, 2 * MOE_FF), lambda i, be, nu: (layer, be[i], 0, 0)),
            pl.BlockSpec((None, 1, 2 * MOE_FF), lambda i, be, nu: (be[i], 0, 0)),
            pl.BlockSpec((None, None, MOE_FF, D_MODEL), lambda i, be, nu: (layer, be[i], 0, 0)),
            pl.BlockSpec((None, 1, D_MODEL), lambda i, be, nu: (be[i], 0, 0)),
        ],
        out_specs=pl.BlockSpec((MOE_ROWS, D_MODEL), lambda i, be, nu: (i, 0)),
        scratch_shapes=[pltpu.VMEM((D_MODEL, 2 * MOE_FF), BF16), pltpu.VMEM((MOE_FF, D_MODEL), BF16)],
    )
    return pl.pallas_call(
        _expert_kernel,
        grid_spec=grid_spec,
        out_shape=jax.ShapeDtypeStruct((n_rows, D_MODEL), F32),
        compiler_params=_params("arbitrary"),
        name="experts",
    )(block_expert, n_used, xs, w_gu, b_gu.reshape(MOE_EXPERTS, 1, -1), w_dn, b_dn.reshape(MOE_EXPERTS, 1, -1))


def _combine_kernel(x_ref, y0_ref, y1_ref, y2_ref, y3_ref, gate_ref, g_ref, b_ref, o_ref):
    ffn = gate_ref[:, 0:1] * y0_ref[...]
    for k, y_ref in enumerate((y1_ref, y2_ref, y3_ref), start=1):
        ffn = ffn + gate_ref[:, k:k + 1] * y_ref[...]
    o_ref[...] = _layer_norm(ALPHA * x_ref[...] + ffn, g_ref[...], b_ref[...])


def _combine(x, y_rows, gate, ln_g, ln_b, *, tm=512):
    m = x.shape[0]
    row = pl.BlockSpec((tm, D_MODEL), lambda i: (i, 0))
    vec = pl.BlockSpec((1, D_MODEL), lambda i: (0, 0))
    return pl.pallas_call(
        _combine_kernel,
        grid=(m // tm,),
        in_specs=[row] + [pl.BlockSpec((tm, D_MODEL), functools.partial(lambda k, i: (k * (m // tm) + i, 0), k))
                          for k in range(MOE_TOP_K)] + [pl.BlockSpec((tm, MOE_TOP_K), lambda i: (i, 0)), vec, vec],
        out_specs=row,
        out_shape=jax.ShapeDtypeStruct((m, D_MODEL), F32),
        compiler_params=_params("parallel"),
        name="combine_ln2",
    )(x, *([y_rows] * MOE_TOP_K), gate, ln_g.reshape(1, -1), ln_b.reshape(1, -1))


def _moe(x, router_w, router_b, w_gu, b_gu, w_dn, b_dn, ln_g, ln_b, layer):
    ntok = x.shape[0]
    idx_t, gate_t, rank_t, counts = _router(x, router_w, router_b)
    gate = gate_t.T
    n_assign = ntok * MOE_TOP_K
    counts = counts.reshape(-1).astype(jnp.int32)
    padded = (counts + MOE_ROWS - 1) // MOE_ROWS * MOE_ROWS
    pend = jnp.cumsum(padded)
    pstart = pend - padded
    experts = jnp.arange(MOE_EXPERTS, dtype=jnp.int32)

    def lookup(table, idx):
        return jnp.sum(jnp.where(idx[..., None] == experts, table, 0), axis=-1)

    dest_t = lookup(pstart, idx_t) + rank_t
    n_blocks = n_assign // MOE_ROWS + MOE_EXPERTS
    n_pad = MOE_EXPERTS * MOE_ROWS
    block_start = jnp.arange(n_blocks, dtype=jnp.int32) * MOE_ROWS
    block_expert = jnp.sum((pend[None, :] <= block_start[:, None]).astype(jnp.int32), axis=1)
    block_expert = jnp.minimum(block_expert, MOE_EXPERTS - 1)
    n_used = (pend[-1:] // MOE_ROWS).astype(jnp.int32)
    gap = padded - counts
    gap_end = jnp.cumsum(gap)
    pad_id = jnp.arange(n_pad, dtype=jnp.int32)
    pad_expert = jnp.sum((gap_end[None, :] <= pad_id[:, None]).astype(jnp.int32), axis=1)
    tail_of = lookup(pstart + counts - (gap_end - gap), jnp.minimum(pad_expert, MOE_EXPERTS - 1)) + pad_id
    pad_slot = jnp.where(pad_expert < MOE_EXPERTS, tail_of, pend[-1] + pad_id - gap_end[-1])
    keys = jnp.concatenate([dest_t.reshape(-1), pad_slot])
    toks = jnp.concatenate([jnp.tile(jnp.arange(ntok, dtype=jnp.int32), MOE_TOP_K), jnp.zeros((n_pad,), jnp.int32)])
    _, slot_tok = lax.sort_key_val(keys, toks)
    xs = x[slot_tok]
    ys = _experts(xs, block_expert, n_used, w_gu, b_gu, w_dn, b_dn, layer)
    y_rows = ys[dest_t.reshape(-1)]
    return _combine(x, y_rows, gate, ln_g, ln_b)


def _layer(x, w_in, b_gate, ssd_conv_w, ssd_conv_b, ssd_a_log, ssd_dt_bias, ssd_d, ssd_norm_w,
           s5_d, s5_glu_w1, s5_glu_w2, w_br_ssd, w_br_attn, w_br_s5, w_out, ln1_g, ln1_b,
           router_w, router_b, b_gu, b_dn, ln2_g, ln2_b, *, layer, w_gu, w_dn, s5_tables):
    bsz, seqlen, d = x.shape
    ntok = bsz * seqlen
    xf = x.reshape(ntok, d)
    offs = [0]
    for s in IN_SIZES:
        offs.append(offs[-1] + s)

    def seg(k):
        return w_in[:, offs[k]:offs[k + 1]].astype(BF16)

    xb = xf.astype(BF16)
    gates = _matmul(xb, seg(0), tm=1024, tn=1024, name="proj_gates", out_dtype=BF16)
    z = _matmul(xb, seg(1), tm=1024, tn=1024, name="proj_z")
    u = _matmul_tiles(xb, seg(5), tm=512, name="proj_u")

    w_xbc = seg(2)
    xs_act = _xbc_conv(xb, w_xbc[:, :SSD_INNER], ssd_conv_w[:, :SSD_INNER], ssd_conv_b[:SSD_INNER], seqlen)
    bc_act = _xbc_conv(xb, w_xbc[:, SSD_INNER:], ssd_conv_w[:, SSD_INNER:], ssd_conv_b[SSD_INNER:], seqlen,
                       out_dtype=BF16)
    xs_act = xs_act.reshape(bsz, seqlen, SSD_INNER)
    bc_act = bc_act.reshape(bsz, seqlen, SSD_CONV_CH - SSD_INNER)
    decay = _ssd_decay(xb, seg(3), ssd_a_log, ssd_dt_bias)
    y_ssd = _ssd_scan(xs_act, bc_act, *decay, ssd_d)
    p_ssd = _ssd_project(y_ssd.reshape(ntok, SSD_INNER), z, ssd_norm_w, w_br_ssd.astype(BF16), gates, b_gate)

    w_qkv = w_in[:, offs[4]:offs[5]].reshape(d, 3, ATT_GROUPS, ATT_OUT)
    att_o, att_l = [], []
    for g, (window, dil) in enumerate(ATT_PATTERNS):
        assert (window // 2) // dil == ATT_RADIUS
        ls = seqlen // dil
        x_cls = xb.reshape(bsz, ls, dil, d).transpose(0, 2, 1, 3).reshape(ntok, d)
        qkv_g = _matmul(x_cls, w_qkv[:, :, g].reshape(d, 3 * ATT_OUT).astype(BF16), tm=1024, tn=3 * ATT_OUT,
                        name=f"proj_qkv{g}", out_dtype=BF16)
        o, lse = _attention_group(qkv_g.reshape(bsz * dil, ls, 3 * ATT_OUT), g, dil)
        att_o.append(o.reshape(bsz, dil, ls, ATT_OUT).transpose(0, 2, 1, 3).reshape(ntok, ATT_OUT))
        att_l.append(lse.reshape(bsz, dil, ls, ATT_OUT).transpose(0, 2, 1, 3).reshape(ntok, ATT_OUT))

    h_s5 = _s5_mix(u, seqlen, s5_tables, layer, s5_d)
    p_s5 = _s5_project(h_s5, s5_glu_w1.astype(BF16), s5_glu_w2.astype(BF16), w_br_s5.astype(BF16), gates, b_gate)

    x1 = _merge(att_o, att_l, w_br_attn.astype(BF16), gates, b_gate, p_ssd, p_s5, w_out.astype(BF16),
                xf, ln1_g, ln1_b)
    x2 = _moe(x1, router_w, router_b, w_gu, b_gu, w_dn, b_dn, ln2_g, ln2_b, layer)
    return x2.reshape(bsz, seqlen, d)


def kernel(x, w_in, b_gate, ssd_conv_w, ssd_conv_b, ssd_a_log, ssd_dt_bias, ssd_d, ssd_norm_w, s5_a_re, s5_a_im, s5_log_step, s5_b_re, s5_b_im, s5_c_re, s5_c_im, s5_d, s5_glu_w1, s5_glu_w2, w_br_ssd, w_br_attn, w_br_s5, w_out, ln1_g, ln1_b, router_w, router_b, exp_w_gate_up, exp_b_gate_up, exp_w_down, exp_b_down, ln2_g, ln2_b):
    per_layer = (w_in, b_gate, ssd_conv_w, ssd_conv_b, ssd_a_log, ssd_dt_bias, ssd_d, ssd_norm_w, s5_d, s5_glu_w1,
                 s5_glu_w2, w_br_ssd, w_br_attn, w_br_s5, w_out, ln1_g, ln1_b, router_w, router_b, exp_b_gate_up,
                 exp_b_down, ln2_g, ln2_b)
    s5_tables = _s5_all_tables(s5_a_re, s5_a_im, s5_log_step, s5_b_re, s5_b_im, s5_c_re, s5_c_im)
    for layer in range(DEPTH):
        x = _layer(x, *[p[layer] for p in per_layer], layer=layer, w_gu=exp_w_gate_up, w_dn=exp_w_down,
                   s5_tables=s5_tables)
    return x
```

```python
import functools
import math

import jax
import jax.numpy as jnp
from jax import lax
from jax.experimental import pallas as pl
from jax.experimental.pallas import tpu as pltpu

F32 = jnp.float32
BF16 = jnp.bfloat16

D_MODEL = 1024
DEPTH = 4
ALPHA = (2 * DEPTH) ** 0.25
LN_EPS = 1e-5

SSD_INNER = 2 * D_MODEL
SSD_HEAD_DIM = 64
SSD_GROUPS = 8
SSD_HPG = 4
SSD_HEADS = SSD_GROUPS * SSD_HPG
SSD_STATE = 128
SSD_CONV = 5
SSD_CONV_CH = SSD_INNER + 2 * SSD_GROUPS * SSD_STATE
SSD_CHUNK = 128
SSD_SEQ_BLOCK = 1024
CONV_ROW_PARTS = 1
CONV_HALO_ROWS = 16

ATT_HEAD_DIM = 64
ATT_PATTERNS = ((128, 1), (512, 4), (2048, 16))
ATT_GROUPS = 3
ATT_HPG = 6
ATT_HEADS = ATT_GROUPS * ATT_HPG
ATT_WIDTH = ATT_HEADS * ATT_HEAD_DIM
ATT_OUT = ATT_HPG * ATT_HEAD_DIM
ATT_NEG = -1e30
ATT_RADIUS = 64
ATT_QSUB = 128

S5_CH = 16
S5_STATE = 64
S5_WIDTH = 9 * D_MODEL // 8
S5_GROUPS = S5_WIDTH // S5_CH
S5_TC = 16
S5_ROW = S5_TC * S5_CH
S5_SCAN_GROUPS = 8
S5_LANE_GROUPS = 128 // S5_CH
S5_TILES = S5_GROUPS // S5_LANE_GROUPS
S5_ROW_BLOCK = 512
S5_RELAYOUT_ROWS = 16

N_BRANCHES = 3
IN_SIZES = (N_BRANCHES * D_MODEL, SSD_INNER, SSD_CONV_CH, 2 * SSD_HEADS, 3 * ATT_WIDTH, S5_WIDTH)

MOE_EXPERTS = 32
MOE_TOP_K = 4
MOE_FF = D_MODEL
SWIGLU_LIMIT = 7.0
SWIGLU_ALPHA = 1.702
MOE_ROWS = 256

VMEM_LIMIT = 56 * 1024 * 1024


def _sigmoid(x):
    return 0.5 * jnp.tanh(0.5 * x) + 0.5


def _params(*sem):
    return pltpu.CompilerParams(dimension_semantics=sem, vmem_limit_bytes=VMEM_LIMIT)


def _mm_kernel(x_ref, w_ref, o_ref):
    o_ref[...] = jnp.dot(x_ref[...].astype(BF16), w_ref[...], preferred_element_type=F32).astype(o_ref.dtype)


def _matmul(x, w, *, tm, tn, name, out_dtype=F32):
    m, k = x.shape
    n = w.shape[1]
    return pl.pallas_call(
        _mm_kernel,
        grid=(n // tn, m // tm),
        in_specs=[pl.BlockSpec((tm, k), lambda j, i: (i, 0)), pl.BlockSpec((k, tn), lambda j, i: (0, j))],
        out_specs=pl.BlockSpec((tm, tn), lambda j, i: (i, j)),
        out_shape=jax.ShapeDtypeStruct((m, n), out_dtype),
        compiler_params=_params("parallel", "parallel"),
        name=name,
    )(x, w)


def _mm_tiles_kernel(x_ref, w_ref, o_ref):
    acc = jnp.dot(x_ref[...].astype(BF16), w_ref[...], preferred_element_type=F32)
    for t in range(o_ref.shape[0]):
        o_ref[t] = acc[:, 128 * t:128 * (t + 1)]


def _matmul_tiles(x, w, *, tm, name):
    m, k = x.shape
    n = w.shape[1]
    nt = n // 128
    return pl.pallas_call(
        _mm_tiles_kernel,
        grid=(m // tm,),
        in_specs=[pl.BlockSpec((tm, k), lambda i: (i, 0)), pl.BlockSpec((k, n), lambda i: (0, 0))],
        out_specs=pl.BlockSpec((nt, tm, 128), lambda i: (0, i, 0)),
        out_shape=jax.ShapeDtypeStruct((nt, m, 128), F32),
        compiler_params=_params("parallel"),
        name=name,
    )(x, w)


def _split3(a):
    a1 = a.astype(BF16)
    r1 = a - a1.astype(F32)
    a2 = r1.astype(BF16)
    a3 = (r1 - a2.astype(F32)).astype(BF16)
    return a1, a2, a3


def _decay_kernel(x_ref, w_ref, wt_ref, alog_r_ref, alog_c_ref, bias_r_ref, bias_c_ref,
                  acs_c_ref, wend_c_ref, acs_r_ref, delta_r_ref):
    q = SSD_CHUNK
    nh = SSD_HEADS
    xb = x_ref[...].astype(BF16)
    dt_c = jnp.dot(xb, w_ref[...], preferred_element_type=F32)
    dt_r = lax.dot_general(wt_ref[...], xb, (((1,), (1,)), ((), ())), preferred_element_type=F32)
    delta_c = jax.nn.softplus(dt_c + bias_r_ref[...])
    delta_r = jax.nn.softplus(dt_r + bias_c_ref[...])
    a_c = -jnp.exp(alog_r_ref[...]) * delta_c
    a_r = -jnp.exp(alog_c_ref[...]) * delta_r
    delta_r_ref[...] = delta_r
    row = lax.broadcasted_iota(jnp.int32, (q, q), 0)
    col = lax.broadcasted_iota(jnp.int32, (q, q), 1)
    lower = (col <= row).astype(BF16)
    upper = (col >= row).astype(BF16)
    fwd_lane = lax.broadcasted_iota(jnp.int32, (q, dt_c.shape[1]), 1) < nh
    fwd_row = lax.broadcasted_iota(jnp.int32, (2 * nh, q), 0) < nh
    for c in range(x_ref.shape[0] // q):
        rows = slice(c * q, (c + 1) * q)
        pieces = _split3(a_c[rows])
        prefix = sum(jnp.dot(lower, p, preferred_element_type=F32) for p in pieces)
        suffix = sum(jnp.dot(upper, p, preferred_element_type=F32) for p in pieces)
        acs = jnp.where(fwd_lane, prefix, suffix)
        total = jnp.where(fwd_lane[0:1], prefix[q - 1:q], suffix[0:1])
        acs_c_ref[rows, :] = acs
        wend_c_ref[rows, :] = delta_c[rows] * jnp.exp(total - acs)
        pieces = _split3(a_r[:, rows])
        prefix = sum(jnp.dot(p, upper, preferred_element_type=F32) for p in pieces)
        suffix = sum(jnp.dot(p, lower, preferred_element_type=F32) for p in pieces)
        acs_r_ref[:, rows] = jnp.where(fwd_row, prefix, suffix)


def _ssd_decay(x, w_dt, a_log, dt_bias, *, tm=1024):
    m, k = x.shape
    n = w_dt.shape[1]
    lanes = 128
    pad = lanes - n
    w_pad = jnp.pad(w_dt, ((0, 0), (0, pad)))
    alog = a_log.reshape(-1)
    bias = dt_bias.reshape(-1)
    col = pl.BlockSpec((tm, lanes), lambda i: (i, 0))
    rowf = pl.BlockSpec((n, tm), lambda i: (0, i))
    return pl.pallas_call(
        _decay_kernel,
        grid=(m // tm,),
        in_specs=[pl.BlockSpec((tm, k), lambda i: (i, 0)), pl.BlockSpec((k, lanes), lambda i: (0, 0)),
                  pl.BlockSpec((n, k), lambda i: (0, 0)),
                  pl.BlockSpec((1, lanes), lambda i: (0, 0)), pl.BlockSpec((n, 1), lambda i: (0, 0)),
                  pl.BlockSpec((1, lanes), lambda i: (0, 0)), pl.BlockSpec((n, 1), lambda i: (0, 0))],
        out_specs=[col, col, rowf, rowf],
        out_shape=[jax.ShapeDtypeStruct((m, lanes), F32)] * 2 + [jax.ShapeDtypeStruct((n, m), F32)] * 2,
        compiler_params=_params("parallel"),
        name="ssd_decay",
    )(x, w_pad, w_dt.T, jnp.pad(alog, (0, pad)).reshape(1, lanes), alog.reshape(n, 1),
      jnp.pad(bias, (0, pad)).reshape(1, lanes), bias.reshape(n, 1))


def _xbc_conv_kernel(xm_ref, xp_ref, xn_ref, w_ref, cw_ref, cb_ref, o_ref, buf, *, tiles_per_seq):
    pos = pl.program_id(1) % tiles_per_seq
    ts = xm_ref.shape[0]
    halo = xp_ref.shape[0]
    half = SSD_CONV // 2
    before = jnp.where(pos == 0, jnp.zeros_like(xp_ref[...]), xp_ref[...])
    after = jnp.where(pos == tiles_per_seq - 1, jnp.zeros_like(xn_ref[...]), xn_ref[...])
    rows = jnp.concatenate([before, xm_ref[...], after], axis=0).astype(BF16)
    nparts = CONV_ROW_PARTS
    part = ts // nparts
    edges = [0] + [halo + part * (p + 1) for p in range(nparts - 1)] + [ts + 2 * halo]

    def project(p):
        buf[edges[p]:edges[p + 1], :] = jnp.dot(rows[edges[p]:edges[p + 1]], w_ref[...], preferred_element_type=F32)

    def convolve(p):
        lo = halo - half + part * p
        acc = cb_ref[...] + cw_ref[0:1, :] * buf[lo:lo + part, :]
        for k in range(1, SSD_CONV):
            acc = acc + cw_ref[k:k + 1, :] * buf[lo + k:lo + k + part, :]
        o_ref[part * p:part * (p + 1), :] = (acc * _sigmoid(acc)).astype(o_ref.dtype)

    project(0)
    for p in range(nparts):
        if p + 1 < nparts:
            project(p + 1)
        convolve(p)


def _xbc_conv(x, w, conv_w, conv_b, seqlen, *, tm=512, tn=1024, out_dtype=F32):
    m, k = x.shape
    n = w.shape[1]
    halo = CONV_HALO_ROWS
    nhalo = m // halo
    return pl.pallas_call(
        functools.partial(_xbc_conv_kernel, tiles_per_seq=seqlen // tm),
        grid=(n // tn, m // tm),
        in_specs=[
            pl.BlockSpec((tm, k), lambda j, i: (i, 0)),
            pl.BlockSpec((halo, k), lambda j, i: (jnp.maximum(i * (tm // halo) - 1, 0), 0)),
            pl.BlockSpec((halo, k), lambda j, i: (jnp.minimum((i + 1) * (tm // halo), nhalo - 1), 0)),
            pl.BlockSpec((k, tn), lambda j, i: (0, j)),
            pl.BlockSpec((SSD_CONV, tn), lambda j, i: (0, j)),
            pl.BlockSpec((1, tn), lambda j, i: (0, j)),
        ],
        out_specs=pl.BlockSpec((tm, tn), lambda j, i: (i, j)),
        out_shape=jax.ShapeDtypeStruct((m, n), out_dtype),
        scratch_shapes=[pltpu.VMEM((tm + 2 * halo, tn), F32)],
        compiler_params=_params("parallel", "parallel"),
        name="proj_xbc_conv",
    )(x, x, x, w, conv_w, conv_b.reshape(1, n))


def _per_head(cols):
    rows = cols[0].shape[0]
    first = lax.broadcasted_iota(jnp.int32, (rows, 2 * SSD_HEAD_DIM), 1) < SSD_HEAD_DIM
    return jnp.concatenate([jnp.where(first, cols[0], cols[1]), jnp.where(first, cols[2], cols[3])], axis=1)


def _group_lanes(ref, rows, g):
    lanes = ref.shape[1]
    return pltpu.roll(ref[rows, :], (lanes - SSD_HPG * g) % lanes, 1)


def _group_rows(ref, first, cols, g):
    tile = 8
    per_tile = tile // SSD_HPG
    base = pl.multiple_of(first + (g // per_tile) * tile, tile)
    return pltpu.roll(ref[pl.ds(base, tile), cols], (tile - SSD_HPG * (g % per_tile)) % tile, 0)


def _head_selector(lanes, first):
    width = SSD_HPG * SSD_HEAD_DIM
    src = lax.broadcasted_iota(jnp.int32, (lanes, width), 0)
    dst = lax.broadcasted_iota(jnp.int32, (lanes, width), 1) // SSD_HEAD_DIM
    return (src == dst + first).astype(BF16)


def _expand_heads(cols, selector):
    return sum(jnp.dot(p, selector, preferred_element_type=F32) for p in _split3(cols))


def _weighted_inputs(xs_ref, wend_ref, g, first, xw_scr):
    rows = slice(0, xs_ref.shape[0])
    wend = _group_lanes(wend_ref, rows, g)
    xw_scr[...] = (xs_ref[...] * _expand_heads(wend, _head_selector(wend_ref.shape[1], first))).astype(xw_scr.dtype)


def _ssd_back_kernel(xs_ref, bm_ref, acs_ref, wend_ref, out_ref, carried, xw_scr):
    q = SSD_CHUNK
    nchunk = xs_ref.shape[0] // q
    g = pl.program_id(1)
    bwd = SSD_HEADS

    @pl.when(pl.program_id(2) == 0)
    def _():
        carried[...] = jnp.zeros_like(carried)

    _weighted_inputs(xs_ref, wend_ref, g, bwd, xw_scr)

    def chunk(ci, carry):
        c = nchunk - 1 - ci
        rows = pl.ds(pl.multiple_of(c * q, q), q)
        first_row = pl.ds(pl.multiple_of(c * q, q), 8)
        total = _group_lanes(acs_ref, first_row, g)[0:1]
        states = lax.dot_general(bm_ref[rows, :].astype(BF16), xw_scr[rows, :], (((0,), (0,)), ((), ())),
                                 preferred_element_type=F32)
        prev = carried[...]
        out_ref[c] = prev.astype(out_ref.dtype)
        carried[...] = jnp.exp(_per_head([total[:, bwd + r:bwd + r + 1] for r in range(SSD_HPG)])) * prev + states
        return carry

    lax.fori_loop(0, nchunk, chunk, 0, unroll=4)


def _ssd_main_kernel(xs_ref, bm_ref, cm_ref, acs_c_ref, wend_c_ref, acs_r_ref, delta_r_ref, back_ref,
                     dskip_ref, y_ref, carried):
    q = SSD_CHUNK
    nchunk = xs_ref.shape[0] // q
    g = pl.program_id(1)
    bwd = SSD_HEADS
    hd = SSD_HEAD_DIM

    @pl.when(pl.program_id(2) == 0)
    def _():
        carried[...] = jnp.zeros_like(carried)

    row = lax.broadcasted_iota(jnp.int32, (q, q), 0)
    col = lax.broadcasted_iota(jnp.int32, (q, q), 1)
    lower = col <= row
    upper = col >= row
    d_wide = _per_head([dskip_ref[:, r:r + 1] for r in range(SSD_HPG)])
    selector = _head_selector(wend_c_ref.shape[1], 0)
    first = lax.broadcasted_iota(jnp.int32, (q, 2 * hd), 1) < hd

    def per_head_lanes(blocks):
        return jnp.concatenate([jnp.where(first, blocks[0], blocks[1]), jnp.where(first, blocks[2], blocks[3])], axis=1)

    def chunk(c, carry):
        r0 = pl.multiple_of(c * q, q)
        rows = pl.ds(r0, q)
        xs = xs_ref[rows, :]
        bmat = bm_ref[rows, :].astype(BF16)
        cmat = cm_ref[rows, :].astype(BF16)
        acs = _group_lanes(acs_c_ref, rows, g)
        wend = _group_lanes(wend_c_ref, rows, g)
        acs_f = [jnp.broadcast_to(acs[:, r:r + 1], (q, q)) for r in range(SSD_HPG)]
        acs_b = [jnp.broadcast_to(acs[:, bwd + r:bwd + r + 1], (q, q)) for r in range(SSD_HPG)]
        acs_rf = _group_rows(acs_r_ref, 0, rows, g)
        acs_rb = _group_rows(acs_r_ref, bwd, rows, g)
        del_rf = _group_rows(delta_r_ref, 0, rows, g)
        del_rb = _group_rows(delta_r_ref, bwd, rows, g)
        cb = lax.dot_general(cmat, bmat, (((1,), (1,)), ((), ())), preferred_element_type=F32)
        prev = carried[...]
        states_in = jnp.concatenate([prev.astype(BF16), back_ref[c]], axis=1)
        y_off = jnp.dot(cmat, states_in, preferred_element_type=F32)
        e_f = jnp.exp(per_head_lanes(acs_f))
        e_b = jnp.exp(per_head_lanes(acs_b))
        rest = e_f * y_off[:, 0:SSD_HPG * hd] + e_b * y_off[:, SSD_HPG * hd:] + d_wide * xs
        for pair in range(SSD_HPG // 2):
            lanes = slice(2 * pair * hd, 2 * (pair + 1) * hd)
            x_pair = xs[:, lanes]
            y_pair = rest[:, lanes]
            for half in range(2):
                r = 2 * pair + half
                seg_f = acs_f[r] - acs_rf[r:r + 1, :]
                seg_b = acs_b[r] - acs_rb[r:r + 1, :]
                w = (jnp.exp(jnp.where(lower, seg_f, ATT_NEG)) * del_rf[r:r + 1, :]
                     + jnp.exp(jnp.where(upper, seg_b, ATT_NEG)) * del_rb[r:r + 1, :])
                x_own = jnp.where(first if half == 0 else ~first, x_pair, 0.0).astype(BF16)
                y_pair = y_pair + jnp.dot((cb * w).astype(BF16), x_own, preferred_element_type=F32)
            y_ref[rows, lanes] = y_pair
        xw = xs * _expand_heads(wend, selector)
        states = lax.dot_general(bmat, xw.astype(BF16), (((0,), (0,)), ((), ())), preferred_element_type=F32)
        total = acs[q - 1:q]
        carried[...] = jnp.exp(_per_head([total[:, r:r + 1] for r in range(SSD_HPG)])) * prev + states
        return carry

    lax.fori_loop(0, nchunk, chunk, 0, unroll=8)


def _ssd_scan(xs_act, bc_act, acs_c, wend_c, acs_r, delta_r, d_skip):
    bsz, seqlen, _ = xs_act.shape
    sb = min(SSD_SEQ_BLOCK, seqlen)
    nblk = seqlen // sb
    cpb = sb // SSD_CHUNK
    n_x = SSD_HPG * SSD_HEAD_DIM
    b_off = 0
    c_off = SSD_GROUPS
    lanes = acs_c.shape[1]
    nrow = acs_r.shape[0]

    def specs(blk):
        xs = pl.BlockSpec((None, sb, n_x), lambda b_, g, i: (b_, blk(i), g))
        bm = pl.BlockSpec((None, sb, SSD_STATE), lambda b_, g, i: (b_, blk(i), b_off + g))
        cm = pl.BlockSpec((None, sb, SSD_STATE), lambda b_, g, i: (b_, blk(i), c_off + g))
        colf = pl.BlockSpec((sb, lanes), lambda b_, g, i: (b_ * nblk + blk(i), 0))
        rowf = pl.BlockSpec((nrow, sb), lambda b_, g, i: (0, b_ * nblk + blk(i)))
        back = pl.BlockSpec((None, None, cpb, SSD_STATE, n_x), lambda b_, g, i: (b_, g, blk(i), 0, 0))
        return xs, bm, cm, colf, rowf, back

    xs, bm, cm, colf, rowf, back = specs(lambda i: nblk - 1 - i)
    assert SSD_CHUNK == 2 * SSD_HEAD_DIM and SSD_HPG == 4
    state = [pltpu.VMEM((SSD_STATE, n_x), F32), pltpu.VMEM((sb, n_x), BF16)]
    back_states = pl.pallas_call(
        _ssd_back_kernel,
        grid=(bsz, SSD_GROUPS, nblk),
        in_specs=[xs, bm, colf, colf],
        out_specs=back,
        out_shape=jax.ShapeDtypeStruct((bsz, SSD_GROUPS, seqlen // SSD_CHUNK, SSD_STATE, n_x), BF16),
        scratch_shapes=state,
        compiler_params=_params("parallel", "parallel", "arbitrary"),
        name="ssd_back",
    )(xs_act, bc_act, acs_c, wend_c)

    xs, bm, cm, colf, rowf, back = specs(lambda i: i)
    return pl.pallas_call(
        _ssd_main_kernel,
        grid=(bsz, SSD_GROUPS, nblk),
        in_specs=[xs, bm, cm, colf, colf, rowf, rowf, back,
                  pl.BlockSpec((None, 1, SSD_HPG), lambda b_, g, i: (g, 0, 0))],
        out_specs=pl.BlockSpec((None, sb, n_x), lambda b_, g, i: (b_, i, g)),
        out_shape=jax.ShapeDtypeStruct((bsz, seqlen, SSD_INNER), F32),
        scratch_shapes=state[:1],
        compiler_params=_params("parallel", "parallel", "arbitrary"),
        name="ssd_main",
    )(xs_act, bc_act, bc_act, acs_c, wend_c, acs_r, delta_r, back_states,
      d_skip.reshape(SSD_GROUPS, 1, SSD_HPG))


def _attn_kernel(q_ref, ko_ref, kp_ref, kn_ref, vo_ref, vp_ref, vn_ref, o_ref, lse_ref, kcat, vcat, bias,
                 *, dil, group):
    tq = q_ref.shape[0]
    rad = ATT_RADIUS
    qs = ATT_QSUB
    kw = qs + 2 * rad
    nsub = tq // qs
    i = pl.program_id(1)
    kcat[0:rad, :] = kp_ref[...].astype(BF16)
    kcat[rad:rad + tq, :] = ko_ref[...].astype(BF16)
    kcat[rad + tq:2 * rad + tq, :] = kn_ref[...].astype(BF16)
    vcat[0:rad, :] = vp_ref[...].astype(BF16)
    vcat[rad:rad + tq, :] = vo_ref[...].astype(BF16)
    vcat[rad + tq:2 * rad + tq, :] = vn_ref[...].astype(BF16)

    row = lax.broadcasted_iota(jnp.int32, (qs, kw), 0)
    col = lax.broadcasted_iota(jnp.int32, (qs, kw), 1)
    dist = jnp.abs(row - col + rad)
    for head in range(ATT_HPG):
        slope = 2.0 ** (-8.0 * (head * ATT_GROUPS + group + 1) / ATT_HEADS)
        bias[head] = jnp.where(dist <= rad, (-slope * dil) * dist.astype(F32), ATT_NEG)
    before_start = (i == 0) & (col < rad)
    past_end = (i == pl.num_programs(1) - 1) & (col >= qs + rad)
    lane = lax.broadcasted_iota(jnp.int32, (qs, 2 * ATT_HEAD_DIM), 1)
    first_half = lane < ATT_HEAD_DIM
    scale = ATT_HEAD_DIM ** -0.5

    for m in range(nsub):
        for pair in range(ATT_HPG // 2):
            lanes = slice(pair * 2 * ATT_HEAD_DIM, (pair + 1) * 2 * ATT_HEAD_DIM)
            q2 = q_ref[m * qs:(m + 1) * qs, lanes]
            k2 = kcat[m * qs:m * qs + kw, lanes]
            v2 = vcat[m * qs:m * qs + kw, lanes]
            for half in range(2):
                head = 2 * pair + half
                qm = jnp.where(first_half if half == 0 else ~first_half, q2, 0.0).astype(BF16)
                s = lax.dot_general(qm, k2, (((1,), (1,)), ((), ())), preferred_element_type=F32)
                s = s * scale + bias[head]
                if m == 0:
                    s = jnp.where(before_start, ATT_NEG, s)
                if m == nsub - 1:
                    s = jnp.where(past_end, ATT_NEG, s)
                mx = jnp.max(s, axis=-1, keepdims=True)
                p = jnp.exp(s - mx)
                zsum = jnp.sum(p, axis=-1, keepdims=True)
                pv = jnp.dot(p.astype(BF16), v2, preferred_element_type=F32)
                own = slice(half * ATT_HEAD_DIM, (half + 1) * ATT_HEAD_DIM)
                out_lanes = slice(lanes.start + own.start, lanes.start + own.stop)
                o_ref[m * qs:(m + 1) * qs, out_lanes] = (pv / zsum)[:, own]
                lse_ref[m * qs:(m + 1) * qs, out_lanes] = jnp.broadcast_to(mx + jnp.log(zsum), (qs, ATT_HEAD_DIM))


def _attention_group(qkv, group, dil):
    nseq, ls, _ = qkv.shape
    tq = min(512, ls)
    nblk = ls // tq
    halo_per_blk = tq // ATT_RADIUS
    n_halo = ls // ATT_RADIUS

    def own(off):
        return pl.BlockSpec((None, tq, ATT_OUT), lambda s, i: (s, i, off))

    def prev(off):
        return pl.BlockSpec((None, ATT_RADIUS, ATT_OUT), lambda s, i: (s, jnp.maximum(i * halo_per_blk - 1, 0), off))

    def nxt(off):
        return pl.BlockSpec((None, ATT_RADIUS, ATT_OUT),
                            lambda s, i: (s, jnp.minimum((i + 1) * halo_per_blk, n_halo - 1), off))

    out_spec = pl.BlockSpec((None, tq, ATT_OUT), lambda s, i: (s, i, 0))
    return pl.pallas_call(
        functools.partial(_attn_kernel, dil=dil, group=group),
        grid=(nseq, nblk),
        in_specs=[own(0), own(1), prev(1), nxt(1), own(2), prev(2), nxt(2)],
        out_specs=[out_spec, out_spec],
        out_shape=[jax.ShapeDtypeStruct((nseq, ls, ATT_OUT), F32)] * 2,
        scratch_shapes=[pltpu.VMEM((tq + 2 * ATT_RADIUS, ATT_OUT), BF16)] * 2
        + [pltpu.VMEM((ATT_HPG, ATT_QSUB, ATT_QSUB + 2 * ATT_RADIUS), F32)],
        compiler_params=_params("parallel", "parallel"),
        name=f"attn_g{group}",
    )(qkv, qkv, qkv, qkv, qkv, qkv, qkv)


def _s5_tables(a_re, a_im, log_step, b_re, b_im, c_re, c_im):
    hp = lax.Precision.HIGHEST
    tc = S5_TC
    ng, width = S5_GROUPS, S5_ROW
    step = jnp.exp(log_step)[..., None]
    ar, ai = a_re, a_im
    mag = jnp.exp(step * ar)
    abr, abi = mag * jnp.cos(step * ai), mag * jnp.sin(step * ai)
    den = ar * ar + ai * ai
    fr = ((abr - 1.0) * ar + abi * ai) / den
    fi = (abi * ar - (abr - 1.0) * ai) / den
    b_re_t, b_im_t = b_re.transpose(0, 2, 1), b_im.transpose(0, 2, 1)
    bbr = fr[:, :, None, :] * b_re_t[None] - fi[:, :, None, :] * b_im_t[None]
    bbi = fr[:, :, None, :] * b_im_t[None] + fi[:, :, None, :] * b_re_t[None]
    ks = jnp.arange(tc + 1, dtype=F32)
    arg_r, arg_i = step * ar, step * ai
    pk_r = jnp.exp(arg_r[:, :, None, :] * ks[:, None]) * jnp.cos(arg_i[:, :, None, :] * ks[:, None])
    pk_i = jnp.exp(arg_r[:, :, None, :] * ks[:, None]) * jnp.sin(arg_i[:, :, None, :] * ks[:, None])
    pn_r = jnp.exp(arg_r[..., None] * ks) * jnp.cos(arg_i[..., None] * ks)
    pn_i = jnp.exp(arg_r[..., None] * ks) * jnp.sin(arg_i[..., None] * ks)
    c_re_t, c_im_t = c_re.transpose(0, 1, 3, 2), c_im.transpose(0, 1, 3, 2)
    cap_r = (c_re_t[:, :, :, None, :] * pn_r[..., None] - c_im_t[:, :, :, None, :] * pn_i[..., None])
    cap_i = (c_re_t[:, :, :, None, :] * pn_i[..., None] + c_im_t[:, :, :, None, :] * pn_r[..., None])
    cap_r = cap_r.reshape(2, ng, S5_STATE, (tc + 1) * S5_CH)
    cap_i = cap_i.reshape(2, ng, S5_STATE, (tc + 1) * S5_CH)
    taps = (jnp.einsum('dgin,dgnm->dgim', bbr, cap_r, precision=hp)
            - jnp.einsum('dgin,dgnm->dgim', bbi, cap_i, precision=hp))[..., :width]
    fwd_src = jnp.pad(taps[0], ((0, 0), (0, 0), (width, 0)))
    bwd_rev = taps[1].reshape(ng, S5_CH, tc, S5_CH)[:, :, ::-1, :].reshape(ng, S5_CH, width)
    bwd_src = jnp.pad(bwd_rev, ((0, 0), (0, 0), (0, width)))
    toep = jnp.stack([fwd_src[:, :, width - S5_CH * s:2 * width - S5_CH * s]
                      + bwd_src[:, :, S5_CH * (tc - 1 - s):S5_CH * (tc - 1 - s) + width] for s in range(tc)], axis=1)
    toep = toep.reshape(ng, width, width)

    def end_state(d, pw_r, pw_i):
        re = pw_r[:, :, None, :] * bbr[d][:, None] - pw_i[:, :, None, :] * bbi[d][:, None]
        im = pw_r[:, :, None, :] * bbi[d][:, None] + pw_i[:, :, None, :] * bbr[d][:, None]
        return [re.reshape(ng, width, S5_STATE), im.reshape(ng, width, S5_STATE)]

    b_end = jnp.concatenate(end_state(0, pk_r[0][:, tc - 1::-1], pk_i[0][:, tc - 1::-1])
                            + end_state(1, pk_r[1][:, :tc], pk_i[1][:, :tc]), axis=-1)

    def lag_reversed(a):
        return a.reshape(ng, S5_STATE, tc, S5_CH)[:, :, ::-1, :].reshape(ng, S5_STATE, width)

    c_out = jnp.concatenate([cap_r[0][..., S5_CH:], -cap_i[0][..., S5_CH:],
                             lag_reversed(cap_r[1][..., S5_CH:]), -lag_reversed(cap_i[1][..., S5_CH:])], axis=1)
    ptr, pti = pk_r[:, :, tc], pk_i[:, :, tc]
    p_mul = jnp.concatenate([ptr[0], ptr[0], ptr[1], ptr[1]], axis=-1)
    q_mul = jnp.concatenate([-pti[0], pti[0], -pti[1], pti[1]], axis=-1)
    return toep, b_end, c_out, p_mul, q_mul


def _granule_transpose(regs, granule):
    n = len(regs)
    d = n // 2
    while d >= 1:
        low = (granule // d) % 2 == 0
        nxt = list(regs)
        for i in range(n):
            if (i // d) % 2 == 0:
                a, b = regs[i], regs[i + d]
                nxt[i] = jnp.where(low, a, pltpu.roll(b, d * S5_CH, 1))
                nxt[i + d] = jnp.where(low, pltpu.roll(a, 128 - d * S5_CH, 1), b)
        regs = nxt
        d //= 2
    return regs


def _s5_rows_to_groups(u_ref, x_scr):
    rr = S5_RELAYOUT_ROWS
    per = S5_LANE_GROUPS
    granule = lax.broadcasted_iota(jnp.int32, (rr, 128), 1) // S5_CH

    def body(rb, carry):
        base = pl.multiple_of(rb * (rr * S5_TC), rr * S5_TC)
        rows = pl.ds(pl.multiple_of(rb * rr, rr), rr)
        for h in range(S5_TC // per):
            steps = [u_ref[pl.ds(base + per * h + t, rr, stride=S5_TC), :] for t in range(per)]
            for gl, block in enumerate(_granule_transpose(steps, granule)):
                x_scr[gl, rows, 128 * h:128 * (h + 1)] = block
        return carry

    lax.fori_loop(0, x_scr.shape[1] // rr, body, 0)


def _s5_in_kernel(u_ref, toep_ref, bend_ref, y_ref, sf_ref, sb_ref, x_scr):
    half = 2 * S5_STATE
    _s5_rows_to_groups(u_ref, x_scr)
    for gl in range(S5_LANE_GROUPS):
        xb = x_scr[gl].astype(BF16)
        y_ref[gl] = jnp.dot(xb, toep_ref[gl], preferred_element_type=F32)
        s = jnp.dot(xb, bend_ref[gl], preferred_element_type=F32)
        sf_ref[gl] = s[:, 0:half]
        sb_ref[gl] = s[:, half:2 * half]


def _s5_scan_kernel(sf_ref, sb_ref, p_ref, q_ref, hf_ref, hb_ref, sf_sw_ref, sb_sw_ref, *, rows_per_seq):
    n_chain = p_ref.shape[0]
    half = 2 * S5_STATE
    pf, qf = p_ref[:, 0:half], q_ref[:, 0:half]
    pb, qb = p_ref[:, half:2 * half], q_ref[:, half:2 * half]
    qf_sw = pltpu.roll(qf, S5_STATE, 1)
    qb_sw = pltpu.roll(qb, S5_STATE, 1)
    zero = jnp.zeros((n_chain, half), F32)
    piece = 256

    def swap(k, carry):
        rows = pl.ds(pl.multiple_of(k * piece, piece), piece)
        sf_sw_ref[rows, :] = pltpu.roll(sf_ref[rows, :], S5_STATE, 1)
        sb_sw_ref[rows, :] = pltpu.roll(sb_ref[rows, :], S5_STATE, 1)
        return carry

    lax.fori_loop(0, sf_ref.shape[0] // piece, swap, 0)

    tile = 8

    def steps(c8, carry):
        hf, hf_sw, hb, hb_sw = carry
        base_f = pl.multiple_of(c8 * tile, tile)
        base_b = pl.multiple_of(rows_per_seq - tile - c8 * tile, tile)
        for k in range(tile):
            rows_f = pl.ds(base_f + k, n_chain, stride=rows_per_seq)
            rows_b = pl.ds(base_b + (tile - 1 - k), n_chain, stride=rows_per_seq)
            hf_ref[rows_f, :] = hf
            hb_ref[rows_b, :] = hb
            sf = sf_ref[rows_f, :]
            sb = sb_ref[rows_b, :]
            sf_sw = sf_sw_ref[rows_f, :]
            sb_sw = sb_sw_ref[rows_b, :]
            hf, hf_sw = pf * hf + qf * hf_sw + sf, pf * hf_sw + qf_sw * hf + sf_sw
            hb, hb_sw = pb * hb + qb * hb_sw + sb, pb * hb_sw + qb_sw * hb + sb_sw
        return hf, hf_sw, hb, hb_sw

    lax.fori_loop(0, rows_per_seq // tile, steps, (zero, zero, zero, zero))


def _s5_out_kernel(hf_ref, hb_ref, cout_ref, y_ref, u_ref, d_ref, o_ref, y_scr):
    half = 2 * S5_STATE
    per = S5_LANE_GROUPS
    for gl in range(per):
        y = y_ref[gl] + jnp.dot(hf_ref[gl].astype(BF16), cout_ref[gl, 0:half, :], preferred_element_type=F32)
        y_scr[gl] = y + jnp.dot(hb_ref[gl].astype(BF16), cout_ref[gl, half:2 * half, :], preferred_element_type=F32)
    rr = S5_RELAYOUT_ROWS
    granule = lax.broadcasted_iota(jnp.int32, (rr, 128), 1) // S5_CH
    d_skip = d_ref[...]

    def body(rb, carry):
        rows = pl.ds(pl.multiple_of(rb * rr, rr), rr)
        base = pl.multiple_of(rb * (rr * S5_TC), rr * S5_TC)
        for h in range(S5_TC // per):
            groups = [y_scr[gl, rows, 128 * h:128 * (h + 1)] for gl in range(per)]
            for t, block in enumerate(_granule_transpose(groups, granule)):
                tok = pl.ds(base + per * h + t, rr, stride=S5_TC)
                o_ref[tok, :] = jax.nn.gelu(block + u_ref[tok, :] * d_skip)
        return carry

    lax.fori_loop(0, y_scr.shape[1] // rr, body, 0)


def _s5_all_tables(a_re, a_im, log_step, b_re, b_im, c_re, c_im):
    toep, b_end, c_out, p_mul, q_mul = jax.vmap(_s5_tables)(a_re, a_im, log_step, b_re, b_im, c_re, c_im)
    return toep.astype(BF16), b_end.astype(BF16), c_out.astype(BF16), p_mul, q_mul


def _s5_mix(u_tiles, seqlen, tables, layer, d_skip):
    ntile, ntok, lanes = u_tiles.shape
    bsz = ntok // seqlen
    rows_per_seq = seqlen // S5_TC
    rows = bsz * rows_per_seq
    rblk = min(S5_ROW_BLOCK, rows)
    per = S5_LANE_GROUPS
    half = 2 * S5_STATE
    toep, b_end, c_out, p_mul, q_mul = tables
    p_mul, q_mul = p_mul[layer], q_mul[layer]
    utile = pl.BlockSpec((None, rblk * S5_TC, lanes), lambda j, i: (j, i, 0))
    ytile = pl.BlockSpec((per, rblk, S5_ROW), lambda j, i: (j, i, 0))
    stile = pl.BlockSpec((per, rblk, half), lambda j, i: (j, i, 0))
    wtile = pl.BlockSpec((None, per, S5_ROW, S5_ROW), lambda j, i: (layer, j, 0, 0))
    state_shape = jax.ShapeDtypeStruct((S5_GROUPS, rows, half), F32)
    grid = (ntile, rows // rblk)
    y_in, s_f, s_b = pl.pallas_call(
        _s5_in_kernel,
        grid=grid,
        in_specs=[utile, wtile, wtile],
        out_specs=[ytile, stile, stile],
        out_shape=[jax.ShapeDtypeStruct((S5_GROUPS, rows, S5_ROW), F32), state_shape, state_shape],
        scratch_shapes=[pltpu.VMEM((per, rblk, S5_ROW), F32)],
        compiler_params=_params("parallel", "parallel"),
        name="s5_in",
    )(u_tiles, toep, b_end)

    gs = S5_SCAN_GROUPS
    n_chain = gs * bsz
    chain_p = jnp.repeat(p_mul, bsz, axis=0)
    chain_q = jnp.repeat(q_mul, bsz, axis=0)
    flat = jax.ShapeDtypeStruct((S5_GROUPS * rows, half), F32)
    srows = pl.BlockSpec((gs * rows, half), lambda t: (t, 0))
    chain = pl.BlockSpec((n_chain, S5_ROW), lambda t: (t, 0))
    h_f, h_b = pl.pallas_call(
        functools.partial(_s5_scan_kernel, rows_per_seq=rows_per_seq),
        grid=(S5_GROUPS // gs,),
        in_specs=[srows, srows, chain, chain],
        out_specs=[srows, srows],
        out_shape=[flat, flat],
        scratch_shapes=[pltpu.VMEM((gs * rows, half), F32)] * 2,
        compiler_params=_params("parallel"),
        name="s5_scan",
    )(s_f.reshape(S5_GROUPS * rows, half), s_b.reshape(S5_GROUPS * rows, half), chain_p, chain_q)

    return pl.pallas_call(
        _s5_out_kernel,
        grid=grid,
        in_specs=[stile, stile, wtile, ytile, utile, pl.BlockSpec((None, 1, lanes), lambda j, i: (j, 0, 0))],
        out_specs=utile,
        out_shape=jax.ShapeDtypeStruct((ntile, ntok, lanes), F32),
        scratch_shapes=[pltpu.VMEM((per, rblk, S5_ROW), F32)],
        compiler_params=_params("parallel", "parallel"),
        name="s5_out",
    )(h_f.reshape(S5_GROUPS, rows, half), h_b.reshape(S5_GROUPS, rows, half), c_out, y_in, u_tiles,
      d_skip.reshape(ntile, 1, lanes))


def _ssd_proj_kernel(y_ref, z_ref, nw_ref, w_ref, gate_ref, bg_ref, o_ref):
    z = z_ref[...]
    y = y_ref[...] * (z * _sigmoid(z))
    y = y * lax.rsqrt(jnp.mean(y * y, axis=-1, keepdims=True) + LN_EPS) * nw_ref[...]
    p = jnp.dot(y.astype(BF16), w_ref[...], preferred_element_type=F32)
    o_ref[...] = _sigmoid(gate_ref[...] + bg_ref[...]) * p


def _ssd_project(y, z, norm_w, w_br, gates, b_gate, *, tm=512):
    m = y.shape[0]
    wide = pl.BlockSpec((tm, SSD_INNER), lambda i: (i, 0))
    return pl.pallas_call(
        _ssd_proj_kernel,
        grid=(m // tm,),
        in_specs=[wide, wide, pl.BlockSpec((1, SSD_INNER), lambda i: (0, 0)),
                  pl.BlockSpec((SSD_INNER, D_MODEL), lambda i: (0, 0)),
                  pl.BlockSpec((tm, D_MODEL), lambda i: (i, 0)),
                  pl.BlockSpec((1, D_MODEL), lambda i: (0, 0))],
        out_specs=pl.BlockSpec((tm, D_MODEL), lambda i: (i, 0)),
        out_shape=jax.ShapeDtypeStruct((m, D_MODEL), F32),
        compiler_params=_params("parallel"),
        name="ssd_project",
    )(y, z, norm_w.reshape(1, -1), w_br, gates, b_gate[0:1])


def _s5_proj_kernel(h_ref, w1_ref, w2_ref, w_ref, gate_ref, bg_ref, o_ref):
    hb = jnp.concatenate([h_ref[t] for t in range(h_ref.shape[0])], axis=1).astype(BF16)
    a = jnp.dot(hb, w1_ref[...], preferred_element_type=F32)
    b = jnp.dot(hb, w2_ref[...], preferred_element_type=F32)
    p = jnp.dot((a * _sigmoid(b)).astype(BF16), w_ref[...], preferred_element_type=F32)
    o_ref[...] = _sigmoid(gate_ref[...] + bg_ref[...]) * p


def _s5_project(h_tiles, w1, w2, w_br, gates, b_gate, *, tm=512):
    ntile, m, lanes = h_tiles.shape
    sq = pl.BlockSpec((S5_WIDTH, S5_WIDTH), lambda i: (0, 0))
    return pl.pallas_call(
        _s5_proj_kernel,
        grid=(m // tm,),
        in_specs=[pl.BlockSpec((ntile, tm, lanes), lambda i: (0, i, 0)), sq, sq,
                  pl.BlockSpec((S5_WIDTH, D_MODEL), lambda i: (0, 0)),
                  pl.BlockSpec((tm, D_MODEL), lambda i: (i, 2)),
                  pl.BlockSpec((1, D_MODEL), lambda i: (0, 0))],
        out_specs=pl.BlockSpec((tm, D_MODEL), lambda i: (i, 0)),
        out_shape=jax.ShapeDtypeStruct((m, D_MODEL), F32),
        compiler_params=_params("parallel"),
        name="s5_project",
    )(h_tiles, w1, w2, w_br, gates, b_gate[2:3])


def _layer_norm(v, g, b):
    mu = jnp.mean(v, axis=-1, keepdims=True)
    c = v - mu
    var = jnp.mean(c * c, axis=-1, keepdims=True)
    return c * lax.rsqrt(var + LN_EPS) * g + b


def _merge_kernel(o0_ref, o1_ref, o2_ref, l0_ref, l1_ref, l2_ref, w_att_ref, gate_ref, bg_ref, pssd_ref, ps5_ref,
                  w_out_ref, x_ref, g_ref, b_ref, o_ref):
    l0, l1, l2 = l0_ref[...], l1_ref[...], l2_ref[...]
    mx = jnp.maximum(jnp.maximum(l0, l1), l2)
    e0, e1, e2 = jnp.exp(l0 - mx), jnp.exp(l1 - mx), jnp.exp(l2 - mx)
    den = e0 + e1 + e2
    att = (e0 / den) * o0_ref[...] + (e1 / den) * o1_ref[...] + (e2 / den) * o2_ref[...]
    p_att = jnp.dot(att.astype(BF16), w_att_ref[...], preferred_element_type=F32)
    merged = pssd_ref[...] + _sigmoid(gate_ref[...] + bg_ref[...]) * p_att + ps5_ref[...]
    mix = jnp.dot(merged.astype(BF16), w_out_ref[...], preferred_element_type=F32)
    o_ref[...] = _layer_norm(ALPHA * x_ref[...] + mix, g_ref[...], b_ref[...])


def _merge(att_o, att_l, w_att, gates, b_gate, p_ssd, p_s5, w_out, x, ln_g, ln_b, *, tm=512):
    m = x.shape[0]
    att = pl.BlockSpec((tm, ATT_OUT), lambda i: (i, 0))
    row = pl.BlockSpec((tm, D_MODEL), lambda i: (i, 0))
    vec = pl.BlockSpec((1, D_MODEL), lambda i: (0, 0))
    return pl.pallas_call(
        _merge_kernel,
        grid=(m // tm,),
        in_specs=[att] * 6 + [pl.BlockSpec((ATT_OUT, D_MODEL), lambda i: (0, 0)),
                              pl.BlockSpec((tm, D_MODEL), lambda i: (i, 1)), vec, row, row,
                              pl.BlockSpec((D_MODEL, D_MODEL), lambda i: (0, 0)), row, vec, vec],
        out_specs=row,
        out_shape=jax.ShapeDtypeStruct((m, D_MODEL), F32),
        compiler_params=_params("parallel"),
        name="merge_ln1",
    )(*att_o, *att_l, w_att, gates, b_gate[1:2], p_ssd, p_s5, w_out, x, ln_g.reshape(1, -1), ln_b.reshape(1, -1))


def _router_kernel(x_ref, wt_ref, b_ref, idx_ref, gate_ref, rank_ref, count_ref, before_ref):
    tm = x_ref.shape[0]

    @pl.when(pl.program_id(0) == 0)
    def _():
        count_ref[...] = jnp.zeros_like(count_ref)
        earlier = lax.broadcasted_iota(jnp.int32, (tm, tm), 0) < lax.broadcasted_iota(jnp.int32, (tm, tm), 1)
        before_ref[...] = earlier.astype(before_ref.dtype)

    x = x_ref[...]
    wt = wt_ref[...]
    xh = x.astype(BF16)
    xl = (x - xh.astype(F32)).astype(BF16)
    wh = wt.astype(BF16)
    wl = (wt - wh.astype(F32)).astype(BF16)
    nt = (((1,), (1,)), ((), ()))
    logits = (lax.dot_general(wh, xh, nt, preferred_element_type=F32)
              + lax.dot_general(wl, xh, nt, preferred_element_type=F32)
              + lax.dot_general(wh, xl, nt, preferred_element_type=F32)) + b_ref[...]
    eid = lax.broadcasted_iota(jnp.int32, logits.shape, 0)
    vals, idxs = [], []
    for _ in range(MOE_TOP_K):
        mx = jnp.max(logits, axis=0, keepdims=True)
        sel = jnp.min(jnp.where(logits == mx, eid, MOE_EXPERTS), axis=0, keepdims=True)
        vals.append(mx)
        idxs.append(sel)
        logits = jnp.where(eid == sel, -jnp.inf, logits)
    es = [jnp.exp(v - vals[0]) for v in vals]
    den = es[0] + es[1] + es[2] + es[3]
    for k in range(MOE_TOP_K):
        idx_ref[k:k + 1, :] = idxs[k]
        gate_ref[k:k + 1, :] = es[k] / den
    chosen = [eid == idxs[k] for k in range(MOE_TOP_K)]
    stacked = jnp.concatenate([c.astype(before_ref.dtype) for c in chosen], axis=0)
    earlier = jnp.dot(stacked, before_ref[...], preferred_element_type=F32)
    seen = count_ref[...]
    for k in range(MOE_TOP_K):
        pos = earlier[k * MOE_EXPERTS:(k + 1) * MOE_EXPERTS] + seen
        rank_ref[k:k + 1, :] = jnp.sum(jnp.where(chosen[k], pos, 0.0), axis=0, keepdims=True).astype(jnp.int32)
        seen = seen + jnp.sum(chosen[k].astype(F32), axis=1, keepdims=True)
    count_ref[...] = seen


def _router(x, router_w, router_b, *, tm=1024):
    m = x.shape[0]
    out = pl.BlockSpec((MOE_TOP_K, tm), lambda i: (0, i))
    return pl.pallas_call(
        _router_kernel,
        grid=(m // tm,),
        in_specs=[pl.BlockSpec((tm, D_MODEL), lambda i: (i, 0)),
                  pl.BlockSpec((MOE_EXPERTS, D_MODEL), lambda i: (0, 0)),
                  pl.BlockSpec((MOE_EXPERTS, 1), lambda i: (0, 0))],
        out_specs=[out, out, out, pl.BlockSpec((MOE_EXPERTS, 1), lambda i: (0, 0))],
        out_shape=[jax.ShapeDtypeStruct((MOE_TOP_K, m), jnp.int32), jax.ShapeDtypeStruct((MOE_TOP_K, m), F32),
                   jax.ShapeDtypeStruct((MOE_TOP_K, m), jnp.int32), jax.ShapeDtypeStruct((MOE_EXPERTS, 1), F32)],
        scratch_shapes=[pltpu.VMEM((tm, tm), BF16)],
        compiler_params=_params("arbitrary"),
        name="router",
    )(x, router_w.T, router_b.reshape(-1, 1))


def _expert_kernel(be_ref, used_ref, x_ref, wgu_ref, bgu_ref, wdn_ref, bdn_ref, o_ref, wgu_lo, wdn_lo):
    i = pl.program_id(0)
    used = i < used_ref[0]

    @pl.when(used & ((i == 0) | (be_ref[i] != be_ref[jnp.maximum(i - 1, 0)])))
    def _():
        wgu_lo[...] = wgu_ref[...].astype(wgu_lo.dtype)
        wdn_lo[...] = wdn_ref[...].astype(wdn_lo.dtype)

    @pl.when(used)
    def _():
        hu = jnp.dot(x_ref[...].astype(BF16), wgu_lo[...], preferred_element_type=F32) + bgu_ref[...]
        x_glu = jnp.minimum(hu[:, :MOE_FF], SWIGLU_LIMIT)
        x_lin = jnp.clip(hu[:, MOE_FF:], -SWIGLU_LIMIT, SWIGLU_LIMIT)
        act = x_glu * _sigmoid(SWIGLU_ALPHA * x_glu) * (x_lin + 1.0)
        o_ref[...] = jnp.dot(act.astype(BF16), wdn_lo[...], preferred_element_type=F32) + bdn_ref[...]

    @pl.when(jnp.logical_not(used))
    def _():
        o_ref[...] = jnp.zeros_like(o_ref)


def _experts(xs, block_expert, n_used, w_gu, b_gu, w_dn, b_dn, layer):
    n_rows = xs.shape[0]
    n_blocks = n_rows // MOE_ROWS
    grid_spec = pltpu.PrefetchScalarGridSpec(
        num_scalar_prefetch=2,
        grid=(n_blocks,),
        in_specs=[
            pl.BlockSpec((MOE_ROWS, D_MODEL), lambda i, be, nu: (i, 0)),
            pl.BlockSpec((None, None, D_MODEL, 2 * MOE_FF), lambda i, be, nu: (layer, be[i], 0, 0)),
            pl.BlockSpec((None, 1, 2 * MOE_FF), lambda i, be, nu: (be[i], 0, 0)),
            pl.BlockSpec((None, None, MOE_FF, D_MODEL), lambda i, be, nu: (layer, be[i], 0, 0)),
            pl.BlockSpec((None, 1, D_MODEL), lambda i, be, nu: (be[i], 0, 0)),
        ],
        out_specs=pl.BlockSpec((MOE_ROWS, D_MODEL), lambda i, be, nu: (i, 0)),
        scratch_shapes=[pltpu.VMEM((D_MODEL, 2 * MOE_FF), BF16), pltpu.VMEM((MOE_FF, D_MODEL), BF16)],
    )
    return pl.pallas_call(
        _expert_kernel,
        grid_spec=grid_spec,
        out_shape=jax.ShapeDtypeStruct((n_rows, D_MODEL), F32),
        compiler_params=_params("arbitrary"),
        name="experts",
    )(block_expert, n_used, xs, w_gu, b_gu.reshape(MOE_EXPERTS, 1, -1), w_dn, b_dn.reshape(MOE_EXPERTS, 1, -1))


def _combine_kernel(x_ref, y0_ref, y1_ref, y2_ref, y3_ref, gate_ref, g_ref, b_ref, o_ref):
    ffn = gate_ref[:, 0:1] * y0_ref[...]
    for k, y_ref in enumerate((y1_ref, y2_ref, y3_ref), start=1):
        ffn = ffn + gate_ref[:, k:k + 1] * y_ref[...]
    o_ref[...] = _layer_norm(ALPHA * x_ref[...] + ffn, g_ref[...], b_ref[...])


def _combine(x, y_rows, gate, ln_g, ln_b, *, tm=512):
    m = x.shape[0]
    row = pl.BlockSpec((tm, D_MODEL), lambda i: (i, 0))
    vec = pl.BlockSpec((1, D_MODEL), lambda i: (0, 0))
    return pl.pallas_call(
        _combine_kernel,
        grid=(m // tm,),
        in_specs=[row] + [pl.BlockSpec((tm, D_MODEL), functools.partial(lambda k, i: (k * (m // tm) + i, 0), k))
                          for k in range(MOE_TOP_K)] + [pl.BlockSpec((tm, MOE_TOP_K), lambda i: (i, 0)), vec, vec],
        out_specs=row,
        out_shape=jax.ShapeDtypeStruct((m, D_MODEL), F32),
        compiler_params=_params("parallel"),
        name="combine_ln2",
    )(x, *([y_rows] * MOE_TOP_K), gate, ln_g.reshape(1, -1), ln_b.reshape(1, -1))


def _moe(x, router_w, router_b, w_gu, b_gu, w_dn, b_dn, ln_g, ln_b, layer):
    ntok = x.shape[0]
    idx_t, gate_t, rank_t, counts = _router(x, router_w, router_b)
    gate = gate_t.T
    n_assign = ntok * MOE_TOP_K
    counts = counts.reshape(-1).astype(jnp.int32)
    padded = (counts + MOE_ROWS - 1) // MOE_ROWS * MOE_ROWS
    pend = jnp.cumsum(padded)
    pstart = pend - padded
    experts = jnp.arange(MOE_EXPERTS, dtype=jnp.int32)

    def lookup(table, idx):
        return jnp.sum(jnp.where(idx[..., None] == experts, table, 0), axis=-1)

    dest_t = lookup(pstart, idx_t) + rank_t
    n_blocks = n_assign // MOE_ROWS + MOE_EXPERTS
    n_pad = MOE_EXPERTS * MOE_ROWS
    block_start = jnp.arange(n_blocks, dtype=jnp.int32) * MOE_ROWS
    block_expert = jnp.sum((pend[None, :] <= block_start[:, None]).astype(jnp.int32), axis=1)
    block_expert = jnp.minimum(block_expert, MOE_EXPERTS - 1)
    n_used = (pend[-1:] // MOE_ROWS).astype(jnp.int32)
    gap = padded - counts
    gap_end = jnp.cumsum(gap)
    pad_id = jnp.arange(n_pad, dtype=jnp.int32)
    pad_expert = jnp.sum((gap_end[None, :] <= pad_id[:, None]).astype(jnp.int32), axis=1)
    tail_of = lookup(pstart + counts - (gap_end - gap), jnp.minimum(pad_expert, MOE_EXPERTS - 1)) + pad_id
    pad_slot = jnp.where(pad_expert < MOE_EXPERTS, tail_of, pend[-1] + pad_id - gap_end[-1])
    keys = jnp.concatenate([dest_t.reshape(-1), pad_slot])
    toks = jnp.concatenate([jnp.tile(jnp.arange(ntok, dtype=jnp.int32), MOE_TOP_K), jnp.zeros((n_pad,), jnp.int32)])
    _, slot_tok = lax.sort_key_val(keys, toks)
    xs = x[slot_tok]
    ys = _experts(xs, block_expert, n_used, w_gu, b_gu, w_dn, b_dn, layer)
    y_rows = ys[dest_t.reshape(-1)]
    return _combine(x, y_rows, gate, ln_g, ln_b)


def _layer(x, w_in, b_gate, ssd_conv_w, ssd_conv_b, ssd_a_log, ssd_dt_bias, ssd_d, ssd_norm_w,
           s5_d, s5_glu_w1, s5_glu_w2, w_br_ssd, w_br_attn, w_br_s5, w_out, ln1_g, ln1_b,
           router_w, router_b, b_gu, b_dn, ln2_g, ln2_b, *, layer, w_gu, w_dn, s5_tables):
    bsz, seqlen, d = x.shape
    ntok = bsz * seqlen
    xf = x.reshape(ntok, d)
    offs = [0]
    for s in IN_SIZES:
        offs.append(offs[-1] + s)

    def seg(k):
        return w_in[:, offs[k]:offs[k + 1]].astype(BF16)

    xb = xf.astype(BF16)
    gates = _matmul(xb, seg(0), tm=1024, tn=1024, name="proj_gates", out_dtype=BF16)
    z = _matmul(xb, seg(1), tm=1024, tn=1024, name="proj_z")
    u = _matmul_tiles(xb, seg(5), tm=512, name="proj_u")

    w_xbc = seg(2)
    xs_act = _xbc_conv(xb, w_xbc[:, :SSD_INNER], ssd_conv_w[:, :SSD_INNER], ssd_conv_b[:SSD_INNER], seqlen)
    bc_act = _xbc_conv(xb, w_xbc[:, SSD_INNER:], ssd_conv_w[:, SSD_INNER:], ssd_conv_b[SSD_INNER:], seqlen,
                       out_dtype=BF16)
    xs_act = xs_act.reshape(bsz, seqlen, SSD_INNER)
    bc_act = bc_act.reshape(bsz, seqlen, SSD_CONV_CH - SSD_INNER)
    decay = _ssd_decay(xb, seg(3), ssd_a_log, ssd_dt_bias)
    y_ssd = _ssd_scan(xs_act, bc_act, *decay, ssd_d)
    p_ssd = _ssd_project(y_ssd.reshape(ntok, SSD_INNER), z, ssd_norm_w, w_br_ssd.astype(BF16), gates, b_gate)

    w_qkv = w_in[:, offs[4]:offs[5]].reshape(d, 3, ATT_GROUPS, ATT_OUT)
    att_o, att_l = [], []
    for g, (window, dil) in enumerate(ATT_PATTERNS):
        assert (window // 2) // dil == ATT_RADIUS
        ls = seqlen // dil
        x_cls = xb.reshape(bsz, ls, dil, d).transpose(0, 2, 1, 3).reshape(ntok, d)
        qkv_g = _matmul(x_cls, w_qkv[:, :, g].reshape(d, 3 * ATT_OUT).astype(BF16), tm=1024, tn=3 * ATT_OUT,
                        name=f"proj_qkv{g}", out_dtype=BF16)
        o, lse = _attention_group(qkv_g.reshape(bsz * dil, ls, 3 * ATT_OUT), g, dil)
        att_o.append(o.reshape(bsz, dil, ls, ATT_OUT).transpose(0, 2, 1, 3).reshape(ntok, ATT_OUT))
        att_l.append(lse.reshape(bsz, dil, ls, ATT_OUT).transpose(0, 2, 1, 3).reshape(ntok, ATT_OUT))

    h_s5 = _s5_mix(u, seqlen, s5_tables, layer, s5_d)
    p_s5 = _s5_project(h_s5, s5_glu_w1.astype(BF16), s5_glu_w2.astype(BF16), w_br_s5.astype(BF16), gates, b_gate)

    x1 = _merge(att_o, att_l, w_br_attn.astype(BF16), gates, b_gate, p_ssd, p_s5, w_out.astype(BF16),
                xf, ln1_g, ln1_b)
    x2 = _moe(x1, router_w, router_b, w_gu, b_gu, w_dn, b_dn, ln2_g, ln2_b, layer)
    return x2.reshape(bsz, seqlen, d)


def kernel(x, w_in, b_gate, ssd_conv_w, ssd_conv_b, ssd_a_log, ssd_dt_bias, ssd_d, ssd_norm_w, s5_a_re, s5_a_im, s5_log_step, s5_b_re, s5_b_im, s5_c_re, s5_c_im, s5_d, s5_glu_w1, s5_glu_w2, w_br_ssd, w_br_attn, w_br_s5, w_out, ln1_g, ln1_b, router_w, router_b, exp_w_gate_up, exp_b_gate_up, exp_w_down, exp_b_down, ln2_g, ln2_b):
    per_layer = (w_in, b_gate, ssd_conv_w, ssd_conv_b, ssd_a_log, ssd_dt_bias, ssd_d, ssd_norm_w, s5_d, s5_glu_w1,
                 s5_glu_w2, w_br_ssd, w_br_attn, w_br_s5, w_out, ln1_g, ln1_b, router_w, router_b, exp_b_gate_up,
                 exp_b_down, ln2_g, ln2_b)
    s5_tables = _s5_all_tables(s5_a_re, s5_a_im, s5_log_step, s5_b_re, s5_b_im, s5_c_re, s5_c_im)
    for layer in range(DEPTH):
        x = _layer(x, *[p[layer] for p in per_layer], layer=layer, w_gu=exp_w_gate_up, w_dn=exp_w_down,
                   s5_tables=s5_tables)
    return x
```

```python
import functools
import math

import jax
import jax.numpy as jnp
from jax import lax
from jax.experimental import pallas as pl
from jax.experimental.pallas import tpu as pltpu

F32 = jnp.float32
BF16 = jnp.bfloat16

D_MODEL = 1024
DEPTH = 4
ALPHA = (2 * DEPTH) ** 0.25
LN_EPS = 1e-5

SSD_INNER = 2 * D_MODEL
SSD_HEAD_DIM = 64
SSD_GROUPS = 8
SSD_HPG = 4
SSD_HEADS = SSD_GROUPS * SSD_HPG
SSD_STATE = 128
SSD_CONV = 5
SSD_CONV_CH = SSD_INNER + 2 * SSD_GROUPS * SSD_STATE
SSD_CHUNK = 128
SSD_SEQ_BLOCK = 1024
CONV_ROW_PARTS = 1
CONV_HALO_ROWS = 16

ATT_HEAD_DIM = 64
ATT_PATTERNS = ((128, 1), (512, 4), (2048, 16))
ATT_GROUPS = 3
ATT_HPG = 6
ATT_HEADS = ATT_GROUPS * ATT_HPG
ATT_WIDTH = ATT_HEADS * ATT_HEAD_DIM
ATT_OUT = ATT_HPG * ATT_HEAD_DIM
ATT_NEG = -1e30
ATT_RADIUS = 64
ATT_QSUB = 128

S5_CH = 16
S5_STATE = 64
S5_WIDTH = 9 * D_MODEL // 8
S5_GROUPS = S5_WIDTH // S5_CH
S5_TC = 16
S5_ROW = S5_TC * S5_CH
S5_SCAN_GROUPS = 8
S5_LANE_GROUPS = 128 // S5_CH
S5_TILES = S5_GROUPS // S5_LANE_GROUPS
S5_ROW_BLOCK = 512
S5_RELAYOUT_ROWS = 16

N_BRANCHES = 3
IN_SIZES = (N_BRANCHES * D_MODEL, SSD_INNER, SSD_CONV_CH, 2 * SSD_HEADS, 3 * ATT_WIDTH, S5_WIDTH)

MOE_EXPERTS = 32
MOE_TOP_K = 4
MOE_FF = D_MODEL
SWIGLU_LIMIT = 7.0
SWIGLU_ALPHA = 1.702
MOE_ROWS = 256

VMEM_LIMIT = 56 * 1024 * 1024


def _sigmoid(x):
    return 0.5 * jnp.tanh(0.5 * x) + 0.5


def _params(*sem):
    return pltpu.CompilerParams(dimension_semantics=sem, vmem_limit_bytes=VMEM_LIMIT)


def _mm_kernel(x_ref, w_ref, o_ref):
    o_ref[...] = jnp.dot(x_ref[...].astype(BF16), w_ref[...], preferred_element_type=F32).astype(o_ref.dtype)


def _matmul(x, w, *, tm, tn, name, out_dtype=F32):
    m, k = x.shape
    n = w.shape[1]
    return pl.pallas_call(
        _mm_kernel,
        grid=(n // tn, m // tm),
        in_specs=[pl.BlockSpec((tm, k), lambda j, i: (i, 0)), pl.BlockSpec((k, tn), lambda j, i: (0, j))],
        out_specs=pl.BlockSpec((tm, tn), lambda j, i: (i, j)),
        out_shape=jax.ShapeDtypeStruct((m, n), out_dtype),
        compiler_params=_params("parallel", "parallel"),
        name=name,
    )(x, w)


def _mm_tiles_kernel(x_ref, w_ref, o_ref):
    acc = jnp.dot(x_ref[...].astype(BF16), w_ref[...], preferred_element_type=F32)
    for t in range(o_ref.shape[0]):
        o_ref[t] = acc[:, 128 * t:128 * (t + 1)]


def _matmul_tiles(x, w, *, tm, name):
    m, k = x.shape
    n = w.shape[1]
    nt = n // 128
    return pl.pallas_call(
        _mm_tiles_kernel,
        grid=(m // tm,),
        in_specs=[pl.BlockSpec((tm, k), lambda i: (i, 0)), pl.BlockSpec((k, n), lambda i: (0, 0))],
        out_specs=pl.BlockSpec((nt, tm, 128), lambda i: (0, i, 0)),
        out_shape=jax.ShapeDtypeStruct((nt, m, 128), F32),
        compiler_params=_params("parallel"),
        name=name,
    )(x, w)


def _split3(a):
    a1 = a.astype(BF16)
    r1 = a - a1.astype(F32)
    a2 = r1.astype(BF16)
    a3 = (r1 - a2.astype(F32)).astype(BF16)
    return a1, a2, a3


def _decay_kernel(x_ref, w_ref, wt_ref, alog_r_ref, alog_c_ref, bias_r_ref, bias_c_ref,
                  acs_c_ref, wend_c_ref, acs_r_ref, delta_r_ref):
    q = SSD_CHUNK
    nh = SSD_HEADS
    xb = x_ref[...].astype(BF16)
    dt_c = jnp.dot(xb, w_ref[...], preferred_element_type=F32)
    dt_r = lax.dot_general(wt_ref[...], xb, (((1,), (1,)), ((), ())), preferred_element_type=F32)
    delta_c = jax.nn.softplus(dt_c + bias_r_ref[...])
    delta_r = jax.nn.softplus(dt_r + bias_c_ref[...])
    a_c = -jnp.exp(alog_r_ref[...]) * delta_c
    a_r = -jnp.exp(alog_c_ref[...]) * delta_r
    delta_r_ref[...] = delta_r
    row = lax.broadcasted_iota(jnp.int32, (q, q), 0)
    col = lax.broadcasted_iota(jnp.int32, (q, q), 1)
    lower = (col <= row).astype(BF16)
    upper = (col >= row).astype(BF16)
    fwd_lane = lax.broadcasted_iota(jnp.int32, (q, dt_c.shape[1]), 1) < nh
    fwd_row = lax.broadcasted_iota(jnp.int32, (2 * nh, q), 0) < nh
    for c in range(x_ref.shape[0] // q):
        rows = slice(c * q, (c + 1) * q)
        pieces = _split3(a_c[rows])
        prefix = sum(jnp.dot(lower, p, preferred_element_type=F32) for p in pieces)
        suffix = sum(jnp.dot(upper, p, preferred_element_type=F32) for p in pieces)
        acs = jnp.where(fwd_lane, prefix, suffix)
        total = jnp.where(fwd_lane[0:1], prefix[q - 1:q], suffix[0:1])
        acs_c_ref[rows, :] = acs
        wend_c_ref[rows, :] = delta_c[rows] * jnp.exp(total - acs)
        pieces = _split3(a_r[:, rows])
        prefix = sum(jnp.dot(p, upper, preferred_element_type=F32) for p in pieces)
        suffix = sum(jnp.dot(p, lower, preferred_element_type=F32) for p in pieces)
        acs_r_ref[:, rows] = jnp.where(fwd_row, prefix, suffix)


def _ssd_decay(x, w_dt, a_log, dt_bias, *, tm=1024):
    m, k = x.shape
    n = w_dt.shape[1]
    lanes = 128
    pad = lanes - n
    w_pad = jnp.pad(w_dt, ((0, 0), (0, pad)))
    alog = a_log.reshape(-1)
    bias = dt_bias.reshape(-1)
    col = pl.BlockSpec((tm, lanes), lambda i: (i, 0))
    rowf = pl.BlockSpec((n, tm), lambda i: (0, i))
    return pl.pallas_call(
        _decay_kernel,
        grid=(m // tm,),
        in_specs=[pl.BlockSpec((tm, k), lambda i: (i, 0)), pl.BlockSpec((k, lanes), lambda i: (0, 0)),
                  pl.BlockSpec((n, k), lambda i: (0, 0)),
                  pl.BlockSpec((1, lanes), lambda i: (0, 0)), pl.BlockSpec((n, 1), lambda i: (0, 0)),
                  pl.BlockSpec((1, lanes), lambda i: (0, 0)), pl.BlockSpec((n, 1), lambda i: (0, 0))],
        out_specs=[col, col, rowf, rowf],
        out_shape=[jax.ShapeDtypeStruct((m, lanes), F32)] * 2 + [jax.ShapeDtypeStruct((n, m), F32)] * 2,
        compiler_params=_params("parallel"),
        name="ssd_decay",
    )(x, w_pad, w_dt.T, jnp.pad(alog, (0, pad)).reshape(1, lanes), alog.reshape(n, 1),
      jnp.pad(bias, (0, pad)).reshape(1, lanes), bias.reshape(n, 1))


def _xbc_conv_kernel(xm_ref, xp_ref, xn_ref, w_ref, cw_ref, cb_ref, o_ref, buf, *, tiles_per_seq):
    pos = pl.program_id(1) % tiles_per_seq
    ts = xm_ref.shape[0]
    halo = xp_ref.shape[0]
    half = SSD_CONV // 2
    before = jnp.where(pos == 0, jnp.zeros_like(xp_ref[...]), xp_ref[...])
    after = jnp.where(pos == tiles_per_seq - 1, jnp.zeros_like(xn_ref[...]), xn_ref[...])
    rows = jnp.concatenate([before, xm_ref[...], after], axis=0).astype(BF16)
    nparts = CONV_ROW_PARTS
    part = ts // nparts
    edges = [0] + [halo + part * (p + 1) for p in range(nparts - 1)] + [ts + 2 * halo]

    def project(p):
        buf[edges[p]:edges[p + 1], :] = jnp.dot(rows[edges[p]:edges[p + 1]], w_ref[...], preferred_element_type=F32)

    def convolve(p):
        lo = halo - half + part * p
        acc = cb_ref[...] + cw_ref[0:1, :] * buf[lo:lo + part, :]
        for k in range(1, SSD_CONV):
            acc = acc + cw_ref[k:k + 1, :] * buf[lo + k:lo + k + part, :]
        o_ref[part * p:part * (p + 1), :] = (acc * _sigmoid(acc)).astype(o_ref.dtype)

    project(0)
    for p in range(nparts):
        if p + 1 < nparts:
            project(p + 1)
        convolve(p)


def _xbc_conv(x, w, conv_w, conv_b, seqlen, *, tm=512, tn=1024, out_dtype=F32):
    m, k = x.shape
    n = w.shape[1]
    halo = CONV_HALO_ROWS
    nhalo = m // halo
    return pl.pallas_call(
        functools.partial(_xbc_conv_kernel, tiles_per_seq=seqlen // tm),
        grid=(n // tn, m // tm),
        in_specs=[
            pl.BlockSpec((tm, k), lambda j, i: (i, 0)),
            pl.BlockSpec((halo, k), lambda j, i: (jnp.maximum(i * (tm // halo) - 1, 0), 0)),
            pl.BlockSpec((halo, k), lambda j, i: (jnp.minimum((i + 1) * (tm // halo), nhalo - 1), 0)),
            pl.BlockSpec((k, tn), lambda j, i: (0, j)),
            pl.BlockSpec((SSD_CONV, tn), lambda j, i: (0, j)),
            pl.BlockSpec((1, tn), lambda j, i: (0, j)),
        ],
        out_specs=pl.BlockSpec((tm, tn), lambda j, i: (i, j)),
        out_shape=jax.ShapeDtypeStruct((m, n), out_dtype),
        scratch_shapes=[pltpu.VMEM((tm + 2 * halo, tn), F32)],
        compiler_params=_params("parallel", "parallel"),
        name="proj_xbc_conv",
    )(x, x, x, w, conv_w, conv_b.reshape(1, n))


def _per_head(cols):
    rows = cols[0].shape[0]
    first = lax.broadcasted_iota(jnp.int32, (rows, 2 * SSD_HEAD_DIM), 1) < SSD_HEAD_DIM
    return jnp.concatenate([jnp.where(first, cols[0], cols[1]), jnp.where(first, cols[2], cols[3])], axis=1)


def _group_lanes(ref, rows, g):
    lanes = ref.shape[1]
    return pltpu.roll(ref[rows, :], (lanes - SSD_HPG * g) % lanes, 1)


def _group_rows(ref, first, cols, g):
    tile = 8
    per_tile = tile // SSD_HPG
    base = pl.multiple_of(first + (g // per_tile) * tile, tile)
    return pltpu.roll(ref[pl.ds(base, tile), cols], (tile - SSD_HPG * (g % per_tile)) % tile, 0)


def _head_selector(lanes, first):
    width = SSD_HPG * SSD_HEAD_DIM
    src = lax.broadcasted_iota(jnp.int32, (lanes, width), 0)
    dst = lax.broadcasted_iota(jnp.int32, (lanes, width), 1) // SSD_HEAD_DIM
    return (src == dst + first).astype(BF16)


def _expand_heads(cols, selector):
    return sum(jnp.dot(p, selector, preferred_element_type=F32) for p in _split3(cols))


def _weighted_inputs(xs_ref, wend_ref, g, first, xw_scr):
    rows = slice(0, xs_ref.shape[0])
    wend = _group_lanes(wend_ref, rows, g)
    xw_scr[...] = (xs_ref[...] * _expand_heads(wend, _head_selector(wend_ref.shape[1], first))).astype(xw_scr.dtype)


def _ssd_back_kernel(xs_ref, bm_ref, acs_ref, wend_ref, out_ref, carried, xw_scr):
    q = SSD_CHUNK
    nchunk = xs_ref.shape[0] // q
    g = pl.program_id(1)
    bwd = SSD_HEADS

    @pl.when(pl.program_id(2) == 0)
    def _():
        carried[...] = jnp.zeros_like(carried)

    _weighted_inputs(xs_ref, wend_ref, g, bwd, xw_scr)

    def chunk(ci, carry):
        c = nchunk - 1 - ci
        rows = pl.ds(pl.multiple_of(c * q, q), q)
        first_row = pl.ds(pl.multiple_of(c * q, q), 8)
        total = _group_lanes(acs_ref, first_row, g)[0:1]
        states = lax.dot_general(bm_ref[rows, :].astype(BF16), xw_scr[rows, :], (((0,), (0,)), ((), ())),
                                 preferred_element_type=F32)
        prev = carried[...]
        out_ref[c] = prev.astype(out_ref.dtype)
        carried[...] = jnp.exp(_per_head([total[:, bwd + r:bwd + r + 1] for r in range(SSD_HPG)])) * prev + states
        return carry

    lax.fori_loop(0, nchunk, chunk, 0, unroll=4)


def _ssd_main_kernel(xs_ref, bm_ref, cm_ref, acs_c_ref, wend_c_ref, acs_r_ref, delta_r_ref, back_ref,
                     dskip_ref, y_ref, carried):
    q = SSD_CHUNK
    nchunk = xs_ref.shape[0] // q
    g = pl.program_id(1)
    bwd = SSD_HEADS
    hd = SSD_HEAD_DIM

    @pl.when(pl.program_id(2) == 0)
    def _():
        carried[...] = jnp.zeros_like(carried)

    row = lax.broadcasted_iota(jnp.int32, (q, q), 0)
    col = lax.broadcasted_iota(jnp.int32, (q, q), 1)
    lower = col <= row
    upper = col >= row
    d_wide = _per_head([dskip_ref[:, r:r + 1] for r in range(SSD_HPG)])
    selector = _head_selector(wend_c_ref.shape[1], 0)
    first = lax.broadcasted_iota(jnp.int32, (q, 2 * hd), 1) < hd

    def per_head_lanes(blocks):
        return jnp.concatenate([jnp.where(first, blocks[0], blocks[1]), jnp.where(first, blocks[2], blocks[3])], axis=1)

    def chunk(c, carry):
        r0 = pl.multiple_of(c * q, q)
        rows = pl.ds(r0, q)
        xs = xs_ref[rows, :]
        bmat = bm_ref[rows, :].astype(BF16)
        cmat = cm_ref[rows, :].astype(BF16)
        acs = _group_lanes(acs_c_ref, rows, g)
        wend = _group_lanes(wend_c_ref, rows, g)
        acs_f = [jnp.broadcast_to(acs[:, r:r + 1], (q, q)) for r in range(SSD_HPG)]
        acs_b = [jnp.broadcast_to(acs[:, bwd + r:bwd + r + 1], (q, q)) for r in range(SSD_HPG)]
        acs_rf = _group_rows(acs_r_ref, 0, rows, g)
        acs_rb = _group_rows(acs_r_ref, bwd, rows, g)
        del_rf = _group_rows(delta_r_ref, 0, rows, g)
        del_rb = _group_rows(delta_r_ref, bwd, rows, g)
        cb = lax.dot_general(cmat, bmat, (((1,), (1,)), ((), ())), preferred_element_type=F32)
        prev = carried[...]
        states_in = jnp.concatenate([prev.astype(BF16), back_ref[c]], axis=1)
        y_off = jnp.dot(cmat, states_in, preferred_element_type=F32)
        e_f = jnp.exp(per_head_lanes(acs_f))
        e_b = jnp.exp(per_head_lanes(acs_b))
        rest = e_f * y_off[:, 0:SSD_HPG * hd] + e_b * y_off[:, SSD_HPG * hd:] + d_wide * xs
        for pair in range(SSD_HPG // 2):
            lanes = slice(2 * pair * hd, 2 * (pair + 1) * hd)
            x_pair = xs[:, lanes]
            y_pair = rest[:, lanes]
            for half in range(2):
                r = 2 * pair + half
                seg_f = acs_f[r] - acs_rf[r:r + 1, :]
                seg_b = acs_b[r] - acs_rb[r:r + 1, :]
                w = (jnp.exp(jnp.where(lower, seg_f, ATT_NEG)) * del_rf[r:r + 1, :]
                     + jnp.exp(jnp.where(upper, seg_b, ATT_NEG)) * del_rb[r:r + 1, :])
                x_own = jnp.where(first if half == 0 else ~first, x_pair, 0.0).astype(BF16)
                y_pair = y_pair + jnp.dot((cb * w).astype(BF16), x_own, preferred_element_type=F32)
            y_ref[rows, lanes] = y_pair
        xw = xs * _expand_heads(wend, selector)
        states = lax.dot_general(bmat, xw.astype(BF16), (((0,), (0,)), ((), ())), preferred_element_type=F32)
        total = acs[q - 1:q]
        carried[...] = jnp.exp(_per_head([total[:, r:r + 1] for r in range(SSD_HPG)])) * prev + states
        return carry

    lax.fori_loop(0, nchunk, chunk, 0, unroll=8)


def _ssd_scan(xs_act, bc_act, acs_c, wend_c, acs_r, delta_r, d_skip):
    bsz, seqlen, _ = xs_act.shape
    sb = min(SSD_SEQ_BLOCK, seqlen)
    nblk = seqlen // sb
    cpb = sb // SSD_CHUNK
    n_x = SSD_HPG * SSD_HEAD_DIM
    b_off = 0
    c_off = SSD_GROUPS
    lanes = acs_c.shape[1]
    nrow = acs_r.shape[0]

    def specs(blk):
        xs = pl.BlockSpec((None, sb, n_x), lambda b_, g, i: (b_, blk(i), g))
        bm = pl.BlockSpec((None, sb, SSD_STATE), lambda b_, g, i: (b_, blk(i), b_off + g))
        cm = pl.BlockSpec((None, sb, SSD_STATE), lambda b_, g, i: (b_, blk(i), c_off + g))
        colf = pl.BlockSpec((sb, lanes), lambda b_, g, i: (b_ * nblk + blk(i), 0))
        rowf = pl.BlockSpec((nrow, sb), lambda b_, g, i: (0, b_ * nblk + blk(i)))
        back = pl.BlockSpec((None, None, cpb, SSD_STATE, n_x), lambda b_, g, i: (b_, g, blk(i), 0, 0))
        return xs, bm, cm, colf, rowf, back

    xs, bm, cm, colf, rowf, back = specs(lambda i: nblk - 1 - i)
    assert SSD_CHUNK == 2 * SSD_HEAD_DIM and SSD_HPG == 4
    state = [pltpu.VMEM((SSD_STATE, n_x), F32), pltpu.VMEM((sb, n_x), BF16)]
    back_states = pl.pallas_call(
        _ssd_back_kernel,
        grid=(bsz, SSD_GROUPS, nblk),
        in_specs=[xs, bm, colf, colf],
        out_specs=back,
        out_shape=jax.ShapeDtypeStruct((bsz, SSD_GROUPS, seqlen // SSD_CHUNK, SSD_STATE, n_x), BF16),
        scratch_shapes=state,
        compiler_params=_params("parallel", "parallel", "arbitrary"),
        name="ssd_back",
    )(xs_act, bc_act, acs_c, wend_c)

    xs, bm, cm, colf, rowf, back = specs(lambda i: i)
    return pl.pallas_call(
        _ssd_main_kernel,
        grid=(bsz, SSD_GROUPS, nblk),
        in_specs=[xs, bm, cm, colf, colf, rowf, rowf, back,
                  pl.BlockSpec((None, 1, SSD_HPG), lambda b_, g, i: (g, 0, 0))],
        out_specs=pl.BlockSpec((None, sb, n_x), lambda b_, g, i: (b_, i, g)),
        out_shape=jax.ShapeDtypeStruct((bsz, seqlen, SSD_INNER), F32),
        scratch_shapes=state[:1],
        compiler_params=_params("parallel", "parallel", "arbitrary"),
        name="ssd_main",
    )(xs_act, bc_act, bc_act, acs_c, wend_c, acs_r, delta_r, back_states,
      d_skip.reshape(SSD_GROUPS, 1, SSD_HPG))


def _attn_kernel(q_ref, ko_ref, kp_ref, kn_ref, vo_ref, vp_ref, vn_ref, o_ref, lse_ref, kcat, vcat, bias,
                 *, dil, group):
    tq = q_ref.shape[0]
    rad = ATT_RADIUS
    qs = ATT_QSUB
    kw = qs + 2 * rad
    nsub = tq // qs
    i = pl.program_id(1)
    kcat[0:rad, :] = kp_ref[...].astype(BF16)
    kcat[rad:rad + tq, :] = ko_ref[...].astype(BF16)
    kcat[rad + tq:2 * rad + tq, :] = kn_ref[...].astype(BF16)
    vcat[0:rad, :] = vp_ref[...].astype(BF16)
    vcat[rad:rad + tq, :] = vo_ref[...].astype(BF16)
    vcat[rad + tq:2 * rad + tq, :] = vn_ref[...].astype(BF16)

    row = lax.broadcasted_iota(jnp.int32, (qs, kw), 0)
    col = lax.broadcasted_iota(jnp.int32, (qs, kw), 1)
    dist = jnp.abs(row - col + rad)
    for head in range(ATT_HPG):
        slope = 2.0 ** (-8.0 * (head * ATT_GROUPS + group + 1) / ATT_HEADS)
        bias[head] = jnp.where(dist <= rad, (-slope * dil) * dist.astype(F32), ATT_NEG)
    before_start = (i == 0) & (col < rad)
    past_end = (i == pl.num_programs(1) - 1) & (col >= qs + rad)
    lane = lax.broadcasted_iota(jnp.int32, (qs, 2 * ATT_HEAD_DIM), 1)
    first_half = lane < ATT_HEAD_DIM
    scale = ATT_HEAD_DIM ** -0.5

    for m in range(nsub):
        for pair in range(ATT_HPG // 2):
            lanes = slice(pair * 2 * ATT_HEAD_DIM, (pair + 1) * 2 * ATT_HEAD_DIM)
            q2 = q_ref[m * qs:(m + 1) * qs, lanes]
            k2 = kcat[m * qs:m * qs + kw, lanes]
            v2 = vcat[m * qs:m * qs + kw, lanes]
            for half in range(2):
                head = 2 * pair + half
                qm = jnp.where(first_half if half == 0 else ~first_half, q2, 0.0).astype(BF16)
                s = lax.dot_general(qm, k2, (((1,), (1,)), ((), ())), preferred_element_type=F32)
                s = s * scale + bias[head]
                if m == 0:
                    s = jnp.where(before_start, ATT_NEG, s)
                if m == nsub - 1:
                    s = jnp.where(past_end, ATT_NEG, s)
                mx = jnp.max(s, axis=-1, keepdims=True)
                p = jnp.exp(s - mx)
                zsum = jnp.sum(p, axis=-1, keepdims=True)
                pv = jnp.dot(p.astype(BF16), v2, preferred_element_type=F32)
                own = slice(half * ATT_HEAD_DIM, (half + 1) * ATT_HEAD_DIM)
                out_lanes = slice(lanes.start + own.start, lanes.start + own.stop)
                o_ref[m * qs:(m + 1) * qs, out_lanes] = (pv / zsum)[:, own]
                lse_ref[m * qs:(m + 1) * qs, out_lanes] = jnp.broadcast_to(mx + jnp.log(zsum), (qs, ATT_HEAD_DIM))


def _attention_group(qkv, group, dil):
    nseq, ls, _ = qkv.shape
    tq = min(512, ls)
    nblk = ls // tq
    halo_per_blk = tq // ATT_RADIUS
    n_halo = ls // ATT_RADIUS

    def own(off):
        return pl.BlockSpec((None, tq, ATT_OUT), lambda s, i: (s, i, off))

    def prev(off):
        return pl.BlockSpec((None, ATT_RADIUS, ATT_OUT), lambda s, i: (s, jnp.maximum(i * halo_per_blk - 1, 0), off))

    def nxt(off):
        return pl.BlockSpec((None, ATT_RADIUS, ATT_OUT),
                            lambda s, i: (s, jnp.minimum((i + 1) * halo_per_blk, n_halo - 1), off))

    out_spec = pl.BlockSpec((None, tq, ATT_OUT), lambda s, i: (s, i, 0))
    return pl.pallas_call(
        functools.partial(_attn_kernel, dil=dil, group=group),
        grid=(nseq, nblk),
        in_specs=[own(0), own(1), prev(1), nxt(1), own(2), prev(2), nxt(2)],
        out_specs=[out_spec, out_spec],
        out_shape=[jax.ShapeDtypeStruct((nseq, ls, ATT_OUT), F32)] * 2,
        scratch_shapes=[pltpu.VMEM((tq + 2 * ATT_RADIUS, ATT_OUT), BF16)] * 2
        + [pltpu.VMEM((ATT_HPG, ATT_QSUB, ATT_QSUB + 2 * ATT_RADIUS), F32)],
        compiler_params=_params("parallel", "parallel"),
        name=f"attn_g{group}",
    )(qkv, qkv, qkv, qkv, qkv, qkv, qkv)


def _s5_tables(a_re, a_im, log_step, b_re, b_im, c_re, c_im):
    hp = lax.Precision.HIGHEST
    tc = S5_TC
    ng, width = S5_GROUPS, S5_ROW
    step = jnp.exp(log_step)[..., None]
    ar, ai = a_re, a_im
    mag = jnp.exp(step * ar)
    abr, abi = mag * jnp.cos(step * ai), mag * jnp.sin(step * ai)
    den = ar * ar + ai * ai
    fr = ((abr - 1.0) * ar + abi * ai) / den
    fi = (abi * ar - (abr - 1.0) * ai) / den
    b_re_t, b_im_t = b_re.transpose(0, 2, 1), b_im.transpose(0, 2, 1)
    bbr = fr[:, :, None, :] * b_re_t[None] - fi[:, :, None, :] * b_im_t[None]
    bbi = fr[:, :, None, :] * b_im_t[None] + fi[:, :, None, :] * b_re_t[None]
    ks = jnp.arange(tc + 1, dtype=F32)
    arg_r, arg_i = step * ar, step * ai
    pk_r = jnp.exp(arg_r[:, :, None, :] * ks[:, None]) * jnp.cos(arg_i[:, :, None, :] * ks[:, None])
    pk_i = jnp.exp(arg_r[:, :, None, :] * ks[:, None]) * jnp.sin(arg_i[:, :, None, :] * ks[:, None])
    pn_r = jnp.exp(arg_r[..., None] * ks) * jnp.cos(arg_i[..., None] * ks)
    pn_i = jnp.exp(arg_r[..., None] * ks) * jnp.sin(arg_i[..., None] * ks)
    c_re_t, c_im_t = c_re.transpose(0, 1, 3, 2), c_im.transpose(0, 1, 3, 2)
    cap_r = (c_re_t[:, :, :, None, :] * pn_r[..., None] - c_im_t[:, :, :, None, :] * pn_i[..., None])
    cap_i = (c_re_t[:, :, :, None, :] * pn_i[..., None] + c_im_t[:, :, :, None, :] * pn_r[..., None])
    cap_r = cap_r.reshape(2, ng, S5_STATE, (tc + 1) * S5_CH)
    cap_i = cap_i.reshape(2, ng, S5_STATE, (tc + 1) * S5_CH)
    taps = (jnp.einsum('dgin,dgnm->dgim', bbr, cap_r, precision=hp)
            - jnp.einsum('dgin,dgnm->dgim', bbi, cap_i, precision=hp))[..., :width]
    fwd_src = jnp.pad(taps[0], ((0, 0), (0, 0), (width, 0)))
    bwd_rev = taps[1].reshape(ng, S5_CH, tc, S5_CH)[:, :, ::-1, :].reshape(ng, S5_CH, width)
    bwd_src = jnp.pad(bwd_rev, ((0, 0), (0, 0), (0, width)))
    toep = jnp.stack([fwd_src[:, :, width - S5_CH * s:2 * width - S5_CH * s]
                      + bwd_src[:, :, S5_CH * (tc - 1 - s):S5_CH * (tc - 1 - s) + width] for s in range(tc)], axis=1)
    toep = toep.reshape(ng, width, width)

    def end_state(d, pw_r, pw_i):
        re = pw_r[:, :, None, :] * bbr[d][:, None] - pw_i[:, :, None, :] * bbi[d][:, None]
        im = pw_r[:, :, None, :] * bbi[d][:, None] + pw_i[:, :, None, :] * bbr[d][:, None]
        return [re.reshape(ng, width, S5_STATE), im.reshape(ng, width, S5_STATE)]

    b_end = jnp.concatenate(end_state(0, pk_r[0][:, tc - 1::-1], pk_i[0][:, tc - 1::-1])
                            + end_state(1, pk_r[1][:, :tc], pk_i[1][:, :tc]), axis=-1)

    def lag_reversed(a):
        return a.reshape(ng, S5_STATE, tc, S5_CH)[:, :, ::-1, :].reshape(ng, S5_STATE, width)

    c_out = jnp.concatenate([cap_r[0][..., S5_CH:], -cap_i[0][..., S5_CH:],
                             lag_reversed(cap_r[1][..., S5_CH:]), -lag_reversed(cap_i[1][..., S5_CH:])], axis=1)
    ptr, pti = pk_r[:, :, tc], pk_i[:, :, tc]
    p_mul = jnp.concatenate([ptr[0], ptr[0], ptr[1], ptr[1]], axis=-1)
    q_mul = jnp.concatenate([-pti[0], pti[0], -pti[1], pti[1]], axis=-1)
    return toep, b_end, c_out, p_mul, q_mul


def _granule_transpose(regs, granule):
    n = len(regs)
    d = n // 2
    while d >= 1:
        low = (granule // d) % 2 == 0
        nxt = list(regs)
        for i in range(n):
            if (i // d) % 2 == 0:
                a, b = regs[i], regs[i + d]
                nxt[i] = jnp.where(low, a, pltpu.roll(b, d * S5_CH, 1))
                nxt[i + d] = jnp.where(low, pltpu.roll(a, 128 - d * S5_CH, 1), b)
        regs = nxt
        d //= 2
    return regs


def _s5_rows_to_groups(u_ref, x_scr):
    rr = S5_RELAYOUT_ROWS
    per = S5_LANE_GROUPS
    granule = lax.broadcasted_iota(jnp.int32, (rr, 128), 1) // S5_CH

    def body(rb, carry):
        base = pl.multiple_of(rb * (rr * S5_TC), rr * S5_TC)
        rows = pl.ds(pl.multiple_of(rb * rr, rr), rr)
        for h in range(S5_TC // per):
            steps = [u_ref[pl.ds(base + per * h + t, rr, stride=S5_TC), :] for t in range(per)]
            for gl, block in enumerate(_granule_transpose(steps, granule)):
                x_scr[gl, rows, 128 * h:128 * (h + 1)] = block
        return carry

    lax.fori_loop(0, x_scr.shape[1] // rr, body, 0)


def _s5_in_kernel(u_ref, toep_ref, bend_ref, y_ref, sf_ref, sb_ref, x_scr):
    half = 2 * S5_STATE
    _s5_rows_to_groups(u_ref, x_scr)
    for gl in range(S5_LANE_GROUPS):
        xb = x_scr[gl].astype(BF16)
        y_ref[gl] = jnp.dot(xb, toep_ref[gl], preferred_element_type=F32)
        s = jnp.dot(xb, bend_ref[gl], preferred_element_type=F32)
        sf_ref[gl] = s[:, 0:half]
        sb_ref[gl] = s[:, half:2 * half]


def _s5_scan_kernel(sf_ref, sb_ref, p_ref, q_ref, hf_ref, hb_ref, sf_sw_ref, sb_sw_ref, *, rows_per_seq):
    n_chain = p_ref.shape[0]
    half = 2 * S5_STATE
    pf, qf = p_ref[:, 0:half], q_ref[:, 0:half]
    pb, qb = p_ref[:, half:2 * half], q_ref[:, half:2 * half]
    qf_sw = pltpu.roll(qf, S5_STATE, 1)
    qb_sw = pltpu.roll(qb, S5_STATE, 1)
    zero = jnp.zeros((n_chain, half), F32)
    piece = 256

    def swap(k, carry):
        rows = pl.ds(pl.multiple_of(k * piece, piece), piece)
        sf_sw_ref[rows, :] = pltpu.roll(sf_ref[rows, :], S5_STATE, 1)
        sb_sw_ref[rows, :] = pltpu.roll(sb_ref[rows, :], S5_STATE, 1)
        return carry

    lax.fori_loop(0, sf_ref.shape[0] // piece, swap, 0)

    tile = 8

    def steps(c8, carry):
        hf, hf_sw, hb, hb_sw = carry
        base_f = pl.multiple_of(c8 * tile, tile)
        base_b = pl.multiple_of(rows_per_seq - tile - c8 * tile, tile)
        for k in range(tile):
            rows_f = pl.ds(base_f + k, n_chain, stride=rows_per_seq)
            rows_b = pl.ds(base_b + (tile - 1 - k), n_chain, stride=rows_per_seq)
            hf_ref[rows_f, :] = hf
            hb_ref[rows_b, :] = hb
            sf = sf_ref[rows_f, :]
            sb = sb_ref[rows_b, :]
            sf_sw = sf_sw_ref[rows_f, :]
            sb_sw = sb_sw_ref[rows_b, :]
            hf, hf_sw = pf * hf + qf * hf_sw + sf, pf * hf_sw + qf_sw * hf + sf_sw
            hb, hb_sw = pb * hb + qb * hb_sw + sb, pb * hb_sw + qb_sw * hb + sb_sw
        return hf, hf_sw, hb, hb_sw

    lax.fori_loop(0, rows_per_seq // tile, steps, (zero, zero, zero, zero))


def _s5_out_kernel(hf_ref, hb_ref, cout_ref, y_ref, u_ref, d_ref, o_ref, y_scr):
    half = 2 * S5_STATE
    per = S5_LANE_GROUPS
    for gl in range(per):
        y = y_ref[gl] + jnp.dot(hf_ref[gl].astype(BF16), cout_ref[gl, 0:half, :], preferred_element_type=F32)
        y_scr[gl] = y + jnp.dot(hb_ref[gl].astype(BF16), cout_ref[gl, half:2 * half, :], preferred_element_type=F32)
    rr = S5_RELAYOUT_ROWS
    granule = lax.broadcasted_iota(jnp.int32, (rr, 128), 1) // S5_CH
    d_skip = d_ref[...]

    def body(rb, carry):
        rows = pl.ds(pl.multiple_of(rb * rr, rr), rr)
        base = pl.multiple_of(rb * (rr * S5_TC), rr * S5_TC)
        for h in range(S5_TC // per):
            groups = [y_scr[gl, rows, 128 * h:128 * (h + 1)] for gl in range(per)]
            for t, block in enumerate(_granule_transpose(groups, granule)):
                tok = pl.ds(base + per * h + t, rr, stride=S5_TC)
                o_ref[tok, :] = jax.nn.gelu(block + u_ref[tok, :] * d_skip)
        return carry

    lax.fori_loop(0, y_scr.shape[1] // rr, body, 0)


def _s5_all_tables(a_re, a_im, log_step, b_re, b_im, c_re, c_im):
    toep, b_end, c_out, p_mul, q_mul = jax.vmap(_s5_tables)(a_re, a_im, log_step, b_re, b_im, c_re, c_im)
    return toep.astype(BF16), b_end.astype(BF16), c_out.astype(BF16), p_mul, q_mul


def _s5_mix(u_tiles, seqlen, tables, layer, d_skip):
    ntile, ntok, lanes = u_tiles.shape
    bsz = ntok // seqlen
    rows_per_seq = seqlen // S5_TC
    rows = bsz * rows_per_seq
    rblk = min(S5_ROW_BLOCK, rows)
    per = S5_LANE_GROUPS
    half = 2 * S5_STATE
    toep, b_end, c_out, p_mul, q_mul = tables
    p_mul, q_mul = p_mul[layer], q_mul[layer]
    utile = pl.BlockSpec((None, rblk * S5_TC, lanes), lambda j, i: (j, i, 0))
    ytile = pl.BlockSpec((per, rblk, S5_ROW), lambda j, i: (j, i, 0))
    stile = pl.BlockSpec((per, rblk, half), lambda j, i: (j, i, 0))
    wtile = pl.BlockSpec((None, per, S5_ROW, S5_ROW), lambda j, i: (layer, j, 0, 0))
    state_shape = jax.ShapeDtypeStruct((S5_GROUPS, rows, half), F32)
    grid = (ntile, rows // rblk)
    y_in, s_f, s_b = pl.pallas_call(
        _s5_in_kernel,
        grid=grid,
        in_specs=[utile, wtile, wtile],
        out_specs=[ytile, stile, stile],
        out_shape=[jax.ShapeDtypeStruct((S5_GROUPS, rows, S5_ROW), F32), state_shape, state_shape],
        scratch_shapes=[pltpu.VMEM((per, rblk, S5_ROW), F32)],
        compiler_params=_params("parallel", "parallel"),
        name="s5_in",
    )(u_tiles, toep, b_end)

    gs = S5_SCAN_GROUPS
    n_chain = gs * bsz
    chain_p = jnp.repeat(p_mul, bsz, axis=0)
    chain_q = jnp.repeat(q_mul, bsz, axis=0)
    flat = jax.ShapeDtypeStruct((S5_GROUPS * rows, half), F32)
    srows = pl.BlockSpec((gs * rows, half), lambda t: (t, 0))
    chain = pl.BlockSpec((n_chain, S5_ROW), lambda t: (t, 0))
    h_f, h_b = pl.pallas_call(
        functools.partial(_s5_scan_kernel, rows_per_seq=rows_per_seq),
        grid=(S5_GROUPS // gs,),
        in_specs=[srows, srows, chain, chain],
        out_specs=[srows, srows],
        out_shape=[flat, flat],
        scratch_shapes=[pltpu.VMEM((gs * rows, half), F32)] * 2,
        compiler_params=_params("parallel"),
        name="s5_scan",
    )(s_f.reshape(S5_GROUPS * rows, half), s_b.reshape(S5_GROUPS * rows, half), chain_p, chain_q)

    return pl.pallas_call(
        _s5_out_kernel,
        grid=grid,
        in_specs=[stile, stile, wtile, ytile, utile, pl.BlockSpec((None, 1, lanes), lambda j, i: (j, 0, 0))],
        out_specs=utile,
        out_shape=jax.ShapeDtypeStruct((ntile, ntok, lanes), F32),
        scratch_shapes=[pltpu.VMEM((per, rblk, S5_ROW), F32)],
        compiler_params=_params("parallel", "parallel"),
        name="s5_out",
    )(h_f.reshape(S5_GROUPS, rows, half), h_b.reshape(S5_GROUPS, rows, half), c_out, y_in, u_tiles,
      d_skip.reshape(ntile, 1, lanes))


def _ssd_proj_kernel(y_ref, z_ref, nw_ref, w_ref, gate_ref, bg_ref, o_ref):
    z = z_ref[...]
    y = y_ref[...] * (z * _sigmoid(z))
    y = y * lax.rsqrt(jnp.mean(y * y, axis=-1, keepdims=True) + LN_EPS) * nw_ref[...]
    p = jnp.dot(y.astype(BF16), w_ref[...], preferred_element_type=F32)
    o_ref[...] = _sigmoid(gate_ref[...] + bg_ref[...]) * p


def _ssd_project(y, z, norm_w, w_br, gates, b_gate, *, tm=512):
    m = y.shape[0]
    wide = pl.BlockSpec((tm, SSD_INNER), lambda i: (i, 0))
    return pl.pallas_call(
        _ssd_proj_kernel,
        grid=(m // tm,),
        in_specs=[wide, wide, pl.BlockSpec((1, SSD_INNER), lambda i: (0, 0)),
                  pl.BlockSpec((SSD_INNER, D_MODEL), lambda i: (0, 0)),
                  pl.BlockSpec((tm, D_MODEL), lambda i: (i, 0)),
                  pl.BlockSpec((1, D_MODEL), lambda i: (0, 0))],
        out_specs=pl.BlockSpec((tm, D_MODEL), lambda i: (i, 0)),
        out_shape=jax.ShapeDtypeStruct((m, D_MODEL), F32),
        compiler_params=_params("parallel"),
        name="ssd_project",
    )(y, z, norm_w.reshape(1, -1), w_br, gates, b_gate[0:1])


def _s5_proj_kernel(h_ref, w1_ref, w2_ref, w_ref, gate_ref, bg_ref, o_ref):
    hb = jnp.concatenate([h_ref[t] for t in range(h_ref.shape[0])], axis=1).astype(BF16)
    a = jnp.dot(hb, w1_ref[...], preferred_element_type=F32)
    b = jnp.dot(hb, w2_ref[...], preferred_element_type=F32)
    p = jnp.dot((a * _sigmoid(b)).astype(BF16), w_ref[...], preferred_element_type=F32)
    o_ref[...] = _sigmoid(gate_ref[...] + bg_ref[...]) * p


def _s5_project(h_tiles, w1, w2, w_br, gates, b_gate, *, tm=512):
    ntile, m, lanes = h_tiles.shape
    sq = pl.BlockSpec((S5_WIDTH, S5_WIDTH), lambda i: (0, 0))
    return pl.pallas_call(
        _s5_proj_kernel,
        grid=(m // tm,),
        in_specs=[pl.BlockSpec((ntile, tm, lanes), lambda i: (0, i, 0)), sq, sq,
                  pl.BlockSpec((S5_WIDTH, D_MODEL), lambda i: (0, 0)),
                  pl.BlockSpec((tm, D_MODEL), lambda i: (i, 2)),
                  pl.BlockSpec((1, D_MODEL), lambda i: (0, 0))],
        out_specs=pl.BlockSpec((tm, D_MODEL), lambda i: (i, 0)),
        out_shape=jax.ShapeDtypeStruct((m, D_MODEL), F32),
        compiler_params=_params("parallel"),
        name="s5_project",
    )(h_tiles, w1, w2, w_br, gates, b_gate[2:3])


def _layer_norm(v, g, b):
    mu = jnp.mean(v, axis=-1, keepdims=True)
    c = v - mu
    var = jnp.mean(c * c, axis=-1, keepdims=True)
    return c * lax.rsqrt(var + LN_EPS) * g + b


def _merge_kernel(o0_ref, o1_ref, o2_ref, l0_ref, l1_ref, l2_ref, w_att_ref, gate_ref, bg_ref, pssd_ref, ps5_ref,
                  w_out_ref, x_ref, g_ref, b_ref, o_ref):
    l0, l1, l2 = l0_ref[...], l1_ref[...], l2_ref[...]
    mx = jnp.maximum(jnp.maximum(l0, l1), l2)
    e0, e1, e2 = jnp.exp(l0 - mx), jnp.exp(l1 - mx), jnp.exp(l2 - mx)
    den = e0 + e1 + e2
    att = (e0 / den) * o0_ref[...] + (e1 / den) * o1_ref[...] + (e2 / den) * o2_ref[...]
    p_att = jnp.dot(att.astype(BF16), w_att_ref[...], preferred_element_type=F32)
    merged = pssd_ref[...] + _sigmoid(gate_ref[...] + bg_ref[...]) * p_att + ps5_ref[...]
    mix = jnp.dot(merged.astype(BF16), w_out_ref[...], preferred_element_type=F32)
    o_ref[...] = _layer_norm(ALPHA * x_ref[...] + mix, g_ref[...], b_ref[...])


def _merge(att_o, att_l, w_att, gates, b_gate, p_ssd, p_s5, w_out, x, ln_g, ln_b, *, tm=512):
    m = x.shape[0]
    att = pl.BlockSpec((tm, ATT_OUT), lambda i: (i, 0))
    row = pl.BlockSpec((tm, D_MODEL), lambda i: (i, 0))
    vec = pl.BlockSpec((1, D_MODEL), lambda i: (0, 0))
    return pl.pallas_call(
        _merge_kernel,
        grid=(m // tm,),
        in_specs=[att] * 6 + [pl.BlockSpec((ATT_OUT, D_MODEL), lambda i: (0, 0)),
                              pl.BlockSpec((tm, D_MODEL), lambda i: (i, 1)), vec, row, row,
                              pl.BlockSpec((D_MODEL, D_MODEL), lambda i: (0, 0)), row, vec, vec],
        out_specs=row,
        out_shape=jax.ShapeDtypeStruct((m, D_MODEL), F32),
        compiler_params=_params("parallel"),
        name="merge_ln1",
    )(*att_o, *att_l, w_att, gates, b_gate[1:2], p_ssd, p_s5, w_out, x, ln_g.reshape(1, -1), ln_b.reshape(1, -1))


def _router_kernel(x_ref, wt_ref, b_ref, idx_ref, gate_ref, rank_ref, count_ref, before_ref):
    tm = x_ref.shape[0]

    @pl.when(pl.program_id(0) == 0)
    def _():
        count_ref[...] = jnp.zeros_like(count_ref)
        earlier = lax.broadcasted_iota(jnp.int32, (tm, tm), 0) < lax.broadcasted_iota(jnp.int32, (tm, tm), 1)
        before_ref[...] = earlier.astype(before_ref.dtype)

    x = x_ref[...]
    wt = wt_ref[...]
    xh = x.astype(BF16)
    xl = (x - xh.astype(F32)).astype(BF16)
    wh = wt.astype(BF16)
    wl = (wt - wh.astype(F32)).astype(BF16)
    nt = (((1,), (1,)), ((), ()))
    logits = (lax.dot_general(wh, xh, nt, preferred_element_type=F32)
              + lax.dot_general(wl, xh, nt, preferred_element_type=F32)
              + lax.dot_general(wh, xl, nt, preferred_element_type=F32)) + b_ref[...]
    eid = lax.broadcasted_iota(jnp.int32, logits.shape, 0)
    vals, idxs = [], []
    for _ in range(MOE_TOP_K):
        mx = jnp.max(logits, axis=0, keepdims=True)
        sel = jnp.min(jnp.where(logits == mx, eid, MOE_EXPERTS), axis=0, keepdims=True)
        vals.append(mx)
        idxs.append(sel)
        logits = jnp.where(eid == sel, -jnp.inf, logits)
    es = [jnp.exp(v - vals[0]) for v in vals]
    den = es[0] + es[1] + es[2] + es[3]
    for k in range(MOE_TOP_K):
        idx_ref[k:k + 1, :] = idxs[k]
        gate_ref[k:k + 1, :] = es[k] / den
    chosen = [eid == idxs[k] for k in range(MOE_TOP_K)]
    stacked = jnp.concatenate([c.astype(before_ref.dtype) for c in chosen], axis=0)
    earlier = jnp.dot(stacked, before_ref[...], preferred_element_type=F32)
    seen = count_ref[...]
    for k in range(MOE_TOP_K):
        pos = earlier[k * MOE_EXPERTS:(k + 1) * MOE_EXPERTS] + seen
        rank_ref[k:k + 1, :] = jnp.sum(jnp.where(chosen[k], pos, 0.0), axis=0, keepdims=True).astype(jnp.int32)
        seen = seen + jnp.sum(chosen[k].astype(F32), axis=1, keepdims=True)
    count_ref[...] = seen


def _router(x, router_w, router_b, *, tm=1024):
    m = x.shape[0]
    out = pl.BlockSpec((MOE_TOP_K, tm), lambda i: (0, i))
    return pl.pallas_call(
        _router_kernel,
        grid=(m // tm,),
        in_specs=[pl.BlockSpec((tm, D_MODEL), lambda i: (i, 0)),
                  pl.BlockSpec((MOE_EXPERTS, D_MODEL), lambda i: (0, 0)),
                  pl.BlockSpec((MOE_EXPERTS, 1), lambda i: (0, 0))],
        out_specs=[out, out, out, pl.BlockSpec((MOE_EXPERTS, 1), lambda i: (0, 0))],
        out_shape=[jax.ShapeDtypeStruct((MOE_TOP_K, m), jnp.int32), jax.ShapeDtypeStruct((MOE_TOP_K, m), F32),
                   jax.ShapeDtypeStruct((MOE_TOP_K, m), jnp.int32), jax.ShapeDtypeStruct((MOE_EXPERTS, 1), F32)],
        scratch_shapes=[pltpu.VMEM((tm, tm), BF16)],
        compiler_params=_params("arbitrary"),
        name="router",
    )(x, router_w.T, router_b.reshape(-1, 1))


def _expert_kernel(be_ref, used_ref, x_ref, wgu_ref, bgu_ref, wdn_ref, bdn_ref, o_ref, wgu_lo, wdn_lo):
    i = pl.program_id(0)
    used = i < used_ref[0]

    @pl.when(used & ((i == 0) | (be_ref[i] != be_ref[jnp.maximum(i - 1, 0)])))
    def _():
        wgu_lo[...] = wgu_ref[...].astype(wgu_lo.dtype)
        wdn_lo[...] = wdn_ref[...].astype(wdn_lo.dtype)

    @pl.when(used)
    def _():
        hu = jnp.dot(x_ref[...].astype(BF16), wgu_lo[...], preferred_element_type=F32) + bgu_ref[...]
        x_glu = jnp.minimum(hu[:, :MOE_FF], SWIGLU_LIMIT)
        x_lin = jnp.clip(hu[:, MOE_FF:], -SWIGLU_LIMIT, SWIGLU_LIMIT)
        act = x_glu * _sigmoid(SWIGLU_ALPHA * x_glu) * (x_lin + 1.0)
        o_ref[...] = jnp.dot(act.astype(BF16), wdn_lo[...], preferred_element_type=F32) + bdn_ref[...]

    @pl.when(jnp.logical_not(used))
    def _():
        o_ref[...] = jnp.zeros_like(o_ref)


def _experts(xs, block_expert, n_used, w_gu, b_gu, w_dn, b_dn, layer):
    n_rows = xs.shape[0]
    n_blocks = n_rows // MOE_ROWS
    grid_spec = pltpu.PrefetchScalarGridSpec(
        num_scalar_prefetch=2,
        grid=(n_blocks,),
        in_specs=[
            pl.BlockSpec((MOE_ROWS, D_MODEL), lambda i, be, nu: (i, 0)),
            pl.BlockSpec((None, None, D_MODEL, 2 * MOE_FF), lambda i, be, nu: (layer, be[i], 0, 0)),
            pl.BlockSpec((None, 1, 2 * MOE_FF), lambda i, be, nu: (be[i], 0, 0)),
            pl.BlockSpec((None, None, MOE_FF, D_MODEL), lambda i, be, nu: (layer, be[i], 0, 0)),
            pl.BlockSpec((None, 1, D_MODEL), lambda i, be, nu: (be[i], 0, 0)),
        ],
        out_specs=pl.BlockSpec((MOE_ROWS, D_MODEL), lambda i, be, nu: (i, 0)),
        scratch_shapes=[pltpu.VMEM((D_MODEL, 2 * MOE_FF), BF16), pltpu.VMEM((MOE_FF, D_MODEL), BF16)],
    )
    return pl.pallas_call(
        _expert_kernel,
        grid_spec=grid_spec,
        out_shape=jax.ShapeDtypeStruct((n_rows, D_MODEL), F32),
        compiler_params=_params("arbitrary"),
        name="experts",
    )(block_expert, n_used, xs, w_gu, b_gu.reshape(MOE_EXPERTS, 1, -1), w_dn, b_dn.reshape(MOE_EXPERTS, 1, -1))


def _combine_kernel(x_ref, y0_ref, y1_ref, y2_ref, y3_ref, gate_ref, g_ref, b_ref, o_ref):
    ffn = gate_ref[:, 0:1] * y0_ref[...]
    for k, y_ref in enumerate((y1_ref, y2_ref, y3_ref), start=1):
        ffn = ffn + gate_ref[:, k:k + 1] * y_ref[...]
    o_ref[...] = _layer_norm(ALPHA * x_ref[...] + ffn, g_ref[...], b_ref[...])


def _combine(x, y_rows, gate, ln_g, ln_b, *, tm=512):
    m = x.shape[0]
    row = pl.BlockSpec((tm, D_MODEL), lambda i: (i, 0))
    vec = pl.BlockSpec((1, D_MODEL), lambda i: (0, 0))
    return pl.pallas_call(
        _combine_kernel,
        grid=(m // tm,),
        in_specs=[row] + [pl.BlockSpec((tm, D_MODEL), functools.partial(lambda k, i: (k * (m // tm) + i, 0), k))
                          for k in range(MOE_TOP_K)] + [pl.BlockSpec((tm, MOE_TOP_K), lambda i: (i, 0)), vec, vec],
        out_specs=row,
        out_shape=jax.ShapeDtypeStruct((m, D_MODEL), F32),
        compiler_params=_params("parallel"),
        name="combine_ln2",
    )(x, *([y_rows] * MOE_TOP_K), gate, ln_g.reshape(1, -1), ln_b.reshape(1, -1))


def _moe(x, router_w, router_b, w_gu, b_gu, w_dn, b_dn, ln_g, ln_b, layer):
    ntok = x.shape[0]
    idx_t, gate_t, rank_t, counts = _router(x, router_w, router_b)
    gate = gate_t.T
    n_assign = ntok * MOE_TOP_K
    counts = counts.reshape(-1).astype(jnp.int32)
    padded = (counts + MOE_ROWS - 1) // MOE_ROWS * MOE_ROWS
    pend = jnp.cumsum(padded)
    pstart = pend - padded
    experts = jnp.arange(MOE_EXPERTS, dtype=jnp.int32)

    def lookup(table, idx):
        return jnp.sum(jnp.where(idx[..., None] == experts, table, 0), axis=-1)

    dest_t = lookup(pstart, idx_t) + rank_t
    n_blocks = n_assign // MOE_ROWS + MOE_EXPERTS
    n_pad = MOE_EXPERTS * MOE_ROWS
    block_start = jnp.arange(n_blocks, dtype=jnp.int32) * MOE_ROWS
    block_expert = jnp.sum((pend[None, :] <= block_start[:, None]).astype(jnp.int32), axis=1)
    block_expert = jnp.minimum(block_expert, MOE_EXPERTS - 1)
    n_used = (pend[-1:] // MOE_ROWS).astype(jnp.int32)
    gap = padded - counts
    gap_end = jnp.cumsum(gap)
    pad_id = jnp.arange(n_pad, dtype=jnp.int32)
    pad_expert = jnp.sum((gap_end[None, :] <= pad_id[:, None]).astype(jnp.int32), axis=1)
    tail_of = lookup(pstart + counts - (gap_end - gap), jnp.minimum(pad_expert, MOE_EXPERTS - 1)) + pad_id
    pad_slot = jnp.where(pad_expert < MOE_EXPERTS, tail_of, pend[-1] + pad_id - gap_end[-1])
    keys = jnp.concatenate([dest_t.reshape(-1), pad_slot])
    toks = jnp.concatenate([jnp.tile(jnp.arange(ntok, dtype=jnp.int32), MOE_TOP_K), jnp.zeros((n_pad,), jnp.int32)])
    _, slot_tok = lax.sort((keys, toks), num_keys=1, is_stable=False)
    xs = x.at[slot_tok].get(mode="promise_in_bounds")
    ys = _experts(xs, block_expert, n_used, w_gu, b_gu, w_dn, b_dn, layer)
    y_rows = ys.at[dest_t.reshape(-1)].get(mode="promise_in_bounds", unique_indices=True)
    return _combine(x, y_rows, gate, ln_g, ln_b)


def _layer(x, w_in, b_gate, ssd_conv_w, ssd_conv_b, ssd_a_log, ssd_dt_bias, ssd_d, ssd_norm_w,
           s5_d, s5_glu_w1, s5_glu_w2, w_br_ssd, w_br_attn, w_br_s5, w_out, ln1_g, ln1_b,
           router_w, router_b, b_gu, b_dn, ln2_g, ln2_b, *, layer, w_gu, w_dn, s5_tables):
    bsz, seqlen, d = x.shape
    ntok = bsz * seqlen
    xf = x.reshape(ntok, d)
    offs = [0]
    for s in IN_SIZES:
        offs.append(offs[-1] + s)

    def seg(k):
        return w_in[:, offs[k]:offs[k + 1]].astype(BF16)

    xb = xf.astype(BF16)
    gates = _matmul(xb, seg(0), tm=1024, tn=1024, name="proj_gates", out_dtype=BF16)
    z = _matmul(xb, seg(1), tm=1024, tn=1024, name="proj_z")
    u = _matmul_tiles(xb, seg(5), tm=512, name="proj_u")

    w_xbc = seg(2)
    xs_act = _xbc_conv(xb, w_xbc[:, :SSD_INNER], ssd_conv_w[:, :SSD_INNER], ssd_conv_b[:SSD_INNER], seqlen)
    bc_act = _xbc_conv(xb, w_xbc[:, SSD_INNER:], ssd_conv_w[:, SSD_INNER:], ssd_conv_b[SSD_INNER:], seqlen,
                       out_dtype=BF16)
    xs_act = xs_act.reshape(bsz, seqlen, SSD_INNER)
    bc_act = bc_act.reshape(bsz, seqlen, SSD_CONV_CH - SSD_INNER)
    decay = _ssd_decay(xb, seg(3), ssd_a_log, ssd_dt_bias)
    y_ssd = _ssd_scan(xs_act, bc_act, *decay, ssd_d)
    p_ssd = _ssd_project(y_ssd.reshape(ntok, SSD_INNER), z, ssd_norm_w, w_br_ssd.astype(BF16), gates, b_gate)

    w_qkv = w_in[:, offs[4]:offs[5]].reshape(d, 3, ATT_GROUPS, ATT_OUT)
    att_o, att_l = [], []
    for g, (window, dil) in enumerate(ATT_PATTERNS):
        assert (window // 2) // dil == ATT_RADIUS
        ls = seqlen // dil
        x_cls = xb.reshape(bsz, ls, dil, d).transpose(0, 2, 1, 3).reshape(ntok, d)
        qkv_g = _matmul(x_cls, w_qkv[:, :, g].reshape(d, 3 * ATT_OUT).astype(BF16), tm=1024, tn=3 * ATT_OUT,
                        name=f"proj_qkv{g}", out_dtype=BF16)
        o, lse = _attention_group(qkv_g.reshape(bsz * dil, ls, 3 * ATT_OUT), g, dil)
        att_o.append(o.reshape(bsz, dil, ls, ATT_OUT).transpose(0, 2, 1, 3).reshape(ntok, ATT_OUT))
        att_l.append(lse.reshape(bsz, dil, ls, ATT_OUT).transpose(0, 2, 1, 3).reshape(ntok, ATT_OUT))

    h_s5 = _s5_mix(u, seqlen, s5_tables, layer, s5_d)
    p_s5 = _s5_project(h_s5, s5_glu_w1.astype(BF16), s5_glu_w2.astype(BF16), w_br_s5.astype(BF16), gates, b_gate)

    x1 = _merge(att_o, att_l, w_br_attn.astype(BF16), gates, b_gate, p_ssd, p_s5, w_out.astype(BF16),
                xf, ln1_g, ln1_b)
    x2 = _moe(x1, router_w, router_b, w_gu, b_gu, w_dn, b_dn, ln2_g, ln2_b, layer)
    return x2.reshape(bsz, seqlen, d)


def kernel(x, w_in, b_gate, ssd_conv_w, ssd_conv_b, ssd_a_log, ssd_dt_bias, ssd_d, ssd_norm_w, s5_a_re, s5_a_im, s5_log_step, s5_b_re, s5_b_im, s5_c_re, s5_c_im, s5_d, s5_glu_w1, s5_glu_w2, w_br_ssd, w_br_attn, w_br_s5, w_out, ln1_g, ln1_b, router_w, router_b, exp_w_gate_up, exp_b_gate_up, exp_w_down, exp_b_down, ln2_g, ln2_b):
    per_layer = (w_in, b_gate, ssd_conv_w, ssd_conv_b, ssd_a_log, ssd_dt_bias, ssd_d, ssd_norm_w, s5_d, s5_glu_w1,
                 s5_glu_w2, w_br_ssd, w_br_attn, w_br_s5, w_out, ln1_g, ln1_b, router_w, router_b, exp_b_gate_up,
                 exp_b_down, ln2_g, ln2_b)
    s5_tables = _s5_all_tables(s5_a_re, s5_a_im, s5_log_step, s5_b_re, s5_b_im, s5_c_re, s5_c_im)
    for layer in range(DEPTH):
        x = _layer(x, *[p[layer] for p in per_layer], layer=layer, w_gu=exp_w_gate_up, w_dn=exp_w_down,
                   s5_tables=s5_tables)
    return x
```

```python
import functools
import math

import jax
import jax.numpy as jnp
from jax import lax
from jax.experimental import pallas as pl
from jax.experimental.pallas import tpu as pltpu

F32 = jnp.float32
BF16 = jnp.bfloat16

D_MODEL = 1024
DEPTH = 4
ALPHA = (2 * DEPTH) ** 0.25
LN_EPS = 1e-5

SSD_INNER = 2 * D_MODEL
SSD_HEAD_DIM = 64
SSD_GROUPS = 8
SSD_HPG = 4
SSD_HEADS = SSD_GROUPS * SSD_HPG
SSD_STATE = 128
SSD_CONV = 5
SSD_CONV_CH = SSD_INNER + 2 * SSD_GROUPS * SSD_STATE
SSD_CHUNK = 128
SSD_SEQ_BLOCK = 1024
CONV_ROW_PARTS = 1
CONV_HALO_ROWS = 16

ATT_HEAD_DIM = 64
ATT_PATTERNS = ((128, 1), (512, 4), (2048, 16))
ATT_GROUPS = 3
ATT_HPG = 6
ATT_HEADS = ATT_GROUPS * ATT_HPG
ATT_WIDTH = ATT_HEADS * ATT_HEAD_DIM
ATT_OUT = ATT_HPG * ATT_HEAD_DIM
ATT_NEG = -1e30
ATT_RADIUS = 64
ATT_QSUB = 128

S5_CH = 16
S5_STATE = 64
S5_WIDTH = 9 * D_MODEL // 8
S5_GROUPS = S5_WIDTH // S5_CH
S5_TC = 16
S5_ROW = S5_TC * S5_CH
S5_SCAN_GROUPS = 8
S5_LANE_GROUPS = 128 // S5_CH
S5_TILES = S5_GROUPS // S5_LANE_GROUPS
S5_ROW_BLOCK = 512
S5_RELAYOUT_ROWS = 16

N_BRANCHES = 3
IN_SIZES = (N_BRANCHES * D_MODEL, SSD_INNER, SSD_CONV_CH, 2 * SSD_HEADS, 3 * ATT_WIDTH, S5_WIDTH)

MOE_EXPERTS = 32
MOE_TOP_K = 4
MOE_FF = D_MODEL
SWIGLU_LIMIT = 7.0
SWIGLU_ALPHA = 1.702
MOE_ROWS = 256

VMEM_LIMIT = 56 * 1024 * 1024


def _sigmoid(x):
    return 0.5 * jnp.tanh(0.5 * x) + 0.5


def _params(*sem):
    return pltpu.CompilerParams(dimension_semantics=sem, vmem_limit_bytes=VMEM_LIMIT)


def _mm_kernel(x_ref, w_ref, o_ref):
    o_ref[...] = jnp.dot(x_ref[...].astype(BF16), w_ref[...], preferred_element_type=F32).astype(o_ref.dtype)


def _matmul(x, w, *, tm, tn, name, out_dtype=F32):
    m, k = x.shape
    n = w.shape[1]
    return pl.pallas_call(
        _mm_kernel,
        grid=(n // tn, m // tm),
        in_specs=[pl.BlockSpec((tm, k), lambda j, i: (i, 0)), pl.BlockSpec((k, tn), lambda j, i: (0, j))],
        out_specs=pl.BlockSpec((tm, tn), lambda j, i: (i, j)),
        out_shape=jax.ShapeDtypeStruct((m, n), out_dtype),
        compiler_params=_params("parallel", "parallel"),
        name=name,
    )(x, w)


def _mm_tiles_kernel(x_ref, w_ref, o_ref):
    acc = jnp.dot(x_ref[...].astype(BF16), w_ref[...], preferred_element_type=F32)
    for t in range(o_ref.shape[0]):
        o_ref[t] = acc[:, 128 * t:128 * (t + 1)]


def _matmul_tiles(x, w, *, tm, name):
    m, k = x.shape
    n = w.shape[1]
    nt = n // 128
    return pl.pallas_call(
        _mm_tiles_kernel,
        grid=(m // tm,),
        in_specs=[pl.BlockSpec((tm, k), lambda i: (i, 0)), pl.BlockSpec((k, n), lambda i: (0, 0))],
        out_specs=pl.BlockSpec((nt, tm, 128), lambda i: (0, i, 0)),
        out_shape=jax.ShapeDtypeStruct((nt, m, 128), F32),
        compiler_params=_params("parallel"),
        name=name,
    )(x, w)


def _split3(a):
    a1 = a.astype(BF16)
    r1 = a - a1.astype(F32)
    a2 = r1.astype(BF16)
    a3 = (r1 - a2.astype(F32)).astype(BF16)
    return a1, a2, a3


def _decay_kernel(x_ref, w_ref, wt_ref, alog_r_ref, alog_c_ref, bias_r_ref, bias_c_ref,
                  acs_c_ref, wend_c_ref, acs_r_ref, delta_r_ref):
    q = SSD_CHUNK
    nh = SSD_HEADS
    xb = x_ref[...].astype(BF16)
    dt_c = jnp.dot(xb, w_ref[...], preferred_element_type=F32)
    dt_r = lax.dot_general(wt_ref[...], xb, (((1,), (1,)), ((), ())), preferred_element_type=F32)
    delta_c = jax.nn.softplus(dt_c + bias_r_ref[...])
    delta_r = jax.nn.softplus(dt_r + bias_c_ref[...])
    a_c = -jnp.exp(alog_r_ref[...]) * delta_c
    a_r = -jnp.exp(alog_c_ref[...]) * delta_r
    delta_r_ref[...] = delta_r
    row = lax.broadcasted_iota(jnp.int32, (q, q), 0)
    col = lax.broadcasted_iota(jnp.int32, (q, q), 1)
    lower = (col <= row).astype(BF16)
    upper = (col >= row).astype(BF16)
    fwd_lane = lax.broadcasted_iota(jnp.int32, (q, dt_c.shape[1]), 1) < nh
    fwd_row = lax.broadcasted_iota(jnp.int32, (2 * nh, q), 0) < nh
    for c in range(x_ref.shape[0] // q):
        rows = slice(c * q, (c + 1) * q)
        pieces = _split3(a_c[rows])
        prefix = sum(jnp.dot(lower, p, preferred_element_type=F32) for p in pieces)
        suffix = sum(jnp.dot(upper, p, preferred_element_type=F32) for p in pieces)
        acs = jnp.where(fwd_lane, prefix, suffix)
        total = jnp.where(fwd_lane[0:1], prefix[q - 1:q], suffix[0:1])
        acs_c_ref[rows, :] = acs
        wend_c_ref[rows, :] = delta_c[rows] * jnp.exp(total - acs)
        pieces = _split3(a_r[:, rows])
        prefix = sum(jnp.dot(p, upper, preferred_element_type=F32) for p in pieces)
        suffix = sum(jnp.dot(p, lower, preferred_element_type=F32) for p in pieces)
        acs_r_ref[:, rows] = jnp.where(fwd_row, prefix, suffix)


def _ssd_decay(x, w_dt, a_log, dt_bias, *, tm=1024):
    m, k = x.shape
    n = w_dt.shape[1]
    lanes = 128
    pad = lanes - n
    w_pad = jnp.pad(w_dt, ((0, 0), (0, pad)))
    alog = a_log.reshape(-1)
    bias = dt_bias.reshape(-1)
    col = pl.BlockSpec((tm, lanes), lambda i: (i, 0))
    rowf = pl.BlockSpec((n, tm), lambda i: (0, i))
    return pl.pallas_call(
        _decay_kernel,
        grid=(m // tm,),
        in_specs=[pl.BlockSpec((tm, k), lambda i: (i, 0)), pl.BlockSpec((k, lanes), lambda i: (0, 0)),
                  pl.BlockSpec((n, k), lambda i: (0, 0)),
                  pl.BlockSpec((1, lanes), lambda i: (0, 0)), pl.BlockSpec((n, 1), lambda i: (0, 0)),
                  pl.BlockSpec((1, lanes), lambda i: (0, 0)), pl.BlockSpec((n, 1), lambda i: (0, 0))],
        out_specs=[col, col, rowf, rowf],
        out_shape=[jax.ShapeDtypeStruct((m, lanes), F32)] * 2 + [jax.ShapeDtypeStruct((n, m), F32)] * 2,
        compiler_params=_params("parallel"),
        name="ssd_decay",
    )(x, w_pad, w_dt.T, jnp.pad(alog, (0, pad)).reshape(1, lanes), alog.reshape(n, 1),
      jnp.pad(bias, (0, pad)).reshape(1, lanes), bias.reshape(n, 1))


def _xbc_conv_kernel(xm_ref, xp_ref, xn_ref, w_ref, cw_ref, cb_ref, o_ref, buf, *, tiles_per_seq):
    pos = pl.program_id(1) % tiles_per_seq
    ts = xm_ref.shape[0]
    halo = xp_ref.shape[0]
    half = SSD_CONV // 2
    before = jnp.where(pos == 0, jnp.zeros_like(xp_ref[...]), xp_ref[...])
    after = jnp.where(pos == tiles_per_seq - 1, jnp.zeros_like(xn_ref[...]), xn_ref[...])
    rows = jnp.concatenate([before, xm_ref[...], after], axis=0).astype(BF16)
    nparts = CONV_ROW_PARTS
    part = ts // nparts
    edges = [0] + [halo + part * (p + 1) for p in range(nparts - 1)] + [ts + 2 * halo]

    def project(p):
        buf[edges[p]:edges[p + 1], :] = jnp.dot(rows[edges[p]:edges[p + 1]], w_ref[...], preferred_element_type=F32)

    def convolve(p):
        lo = halo - half + part * p
        acc = cb_ref[...] + cw_ref[0:1, :] * buf[lo:lo + part, :]
        for k in range(1, SSD_CONV):
            acc = acc + cw_ref[k:k + 1, :] * buf[lo + k:lo + k + part, :]
        o_ref[part * p:part * (p + 1), :] = (acc * _sigmoid(acc)).astype(o_ref.dtype)

    project(0)
    for p in range(nparts):
        if p + 1 < nparts:
            project(p + 1)
        convolve(p)


def _xbc_conv(x, w, conv_w, conv_b, seqlen, *, tm=512, tn=1024, out_dtype=F32):
    m, k = x.shape
    n = w.shape[1]
    halo = CONV_HALO_ROWS
    nhalo = m // halo
    return pl.pallas_call(
        functools.partial(_xbc_conv_kernel, tiles_per_seq=seqlen // tm),
        grid=(n // tn, m // tm),
        in_specs=[
            pl.BlockSpec((tm, k), lambda j, i: (i, 0)),
            pl.BlockSpec((halo, k), lambda j, i: (jnp.maximum(i * (tm // halo) - 1, 0), 0)),
            pl.BlockSpec((halo, k), lambda j, i: (jnp.minimum((i + 1) * (tm // halo), nhalo - 1), 0)),
            pl.BlockSpec((k, tn), lambda j, i: (0, j)),
            pl.BlockSpec((SSD_CONV, tn), lambda j, i: (0, j)),
            pl.BlockSpec((1, tn), lambda j, i: (0, j)),
        ],
        out_specs=pl.BlockSpec((tm, tn), lambda j, i: (i, j)),
        out_shape=jax.ShapeDtypeStruct((m, n), out_dtype),
        scratch_shapes=[pltpu.VMEM((tm + 2 * halo, tn), F32)],
        compiler_params=_params("parallel", "parallel"),
        name="proj_xbc_conv",
    )(x, x, x, w, conv_w, conv_b.reshape(1, n))


def _per_head(cols):
    rows = cols[0].shape[0]
    first = lax.broadcasted_iota(jnp.int32, (rows, 2 * SSD_HEAD_DIM), 1) < SSD_HEAD_DIM
    return jnp.concatenate([jnp.where(first, cols[0], cols[1]), jnp.where(first, cols[2], cols[3])], axis=1)


def _group_lanes(ref, rows, g):
    lanes = ref.shape[1]
    return pltpu.roll(ref[rows, :], (lanes - SSD_HPG * g) % lanes, 1)


def _group_rows(ref, first, cols, g):
    tile = 8
    per_tile = tile // SSD_HPG
    base = pl.multiple_of(first + (g // per_tile) * tile, tile)
    return pltpu.roll(ref[pl.ds(base, tile), cols], (tile - SSD_HPG * (g % per_tile)) % tile, 0)


def _head_selector(lanes, first):
    width = SSD_HPG * SSD_HEAD_DIM
    src = lax.broadcasted_iota(jnp.int32, (lanes, width), 0)
    dst = lax.broadcasted_iota(jnp.int32, (lanes, width), 1) // SSD_HEAD_DIM
    return (src == dst + first).astype(BF16)


def _expand_heads(cols, selector):
    return sum(jnp.dot(p, selector, preferred_element_type=F32) for p in _split3(cols))


def _weighted_inputs(xs_ref, wend_ref, g, first, xw_scr):
    rows = slice(0, xs_ref.shape[0])
    wend = _group_lanes(wend_ref, rows, g)
    xw_scr[...] = (xs_ref[...] * _expand_heads(wend, _head_selector(wend_ref.shape[1], first))).astype(xw_scr.dtype)


def _ssd_back_kernel(xs_ref, bm_ref, acs_ref, wend_ref, out_ref, carried, xw_scr):
    q = SSD_CHUNK
    nchunk = xs_ref.shape[0] // q
    g = pl.program_id(1)
    bwd = SSD_HEADS

    @pl.when(pl.program_id(2) == 0)
    def _():
        carried[...] = jnp.zeros_like(carried)

    _weighted_inputs(xs_ref, wend_ref, g, bwd, xw_scr)

    def chunk(ci, carry):
        c = nchunk - 1 - ci
        rows = pl.ds(pl.multiple_of(c * q, q), q)
        first_row = pl.ds(pl.multiple_of(c * q, q), 8)
        total = _group_lanes(acs_ref, first_row, g)[0:1]
        states = lax.dot_general(bm_ref[rows, :].astype(BF16), xw_scr[rows, :], (((0,), (0,)), ((), ())),
                                 preferred_element_type=F32)
        prev = carried[...]
        out_ref[c] = prev.astype(out_ref.dtype)
        carried[...] = jnp.exp(_per_head([total[:, bwd + r:bwd + r + 1] for r in range(SSD_HPG)])) * prev + states
        return carry

    lax.fori_loop(0, nchunk, chunk, 0, unroll=4)


def _ssd_main_kernel(xs_ref, bm_ref, cm_ref, acs_c_ref, wend_c_ref, acs_r_ref, delta_r_ref, back_ref,
                     dskip_ref, y_ref, carried):
    q = SSD_CHUNK
    nchunk = xs_ref.shape[0] // q
    g = pl.program_id(1)
    bwd = SSD_HEADS
    hd = SSD_HEAD_DIM

    @pl.when(pl.program_id(2) == 0)
    def _():
        carried[...] = jnp.zeros_like(carried)

    row = lax.broadcasted_iota(jnp.int32, (q, q), 0)
    col = lax.broadcasted_iota(jnp.int32, (q, q), 1)
    lower = col <= row
    upper = col >= row
    d_wide = _per_head([dskip_ref[:, r:r + 1] for r in range(SSD_HPG)])
    selector = _head_selector(wend_c_ref.shape[1], 0)
    first = lax.broadcasted_iota(jnp.int32, (q, 2 * hd), 1) < hd

    def per_head_lanes(blocks):
        return jnp.concatenate([jnp.where(first, blocks[0], blocks[1]), jnp.where(first, blocks[2], blocks[3])], axis=1)

    def chunk(c, carry):
        r0 = pl.multiple_of(c * q, q)
        rows = pl.ds(r0, q)
        xs = xs_ref[rows, :]
        bmat = bm_ref[rows, :].astype(BF16)
        cmat = cm_ref[rows, :].astype(BF16)
        acs = _group_lanes(acs_c_ref, rows, g)
        wend = _group_lanes(wend_c_ref, rows, g)
        acs_f = [jnp.broadcast_to(acs[:, r:r + 1], (q, q)) for r in range(SSD_HPG)]
        acs_b = [jnp.broadcast_to(acs[:, bwd + r:bwd + r + 1], (q, q)) for r in range(SSD_HPG)]
        acs_rf = _group_rows(acs_r_ref, 0, rows, g)
        acs_rb = _group_rows(acs_r_ref, bwd, rows, g)
        del_rf = _group_rows(delta_r_ref, 0, rows, g)
        del_rb = _group_rows(delta_r_ref, bwd, rows, g)
        cb = lax.dot_general(cmat, bmat, (((1,), (1,)), ((), ())), preferred_element_type=F32)
        prev = carried[...]
        states_in = jnp.concatenate([prev.astype(BF16), back_ref[c]], axis=1)
        y_off = jnp.dot(cmat, states_in, preferred_element_type=F32)
        e_f = jnp.exp(per_head_lanes(acs_f))
        e_b = jnp.exp(per_head_lanes(acs_b))
        rest = e_f * y_off[:, 0:SSD_HPG * hd] + e_b * y_off[:, SSD_HPG * hd:] + d_wide * xs
        for pair in range(SSD_HPG // 2):
            lanes = slice(2 * pair * hd, 2 * (pair + 1) * hd)
            x_pair = xs[:, lanes]
            y_pair = rest[:, lanes]
            for half in range(2):
                r = 2 * pair + half
                seg_f = acs_f[r] - acs_rf[r:r + 1, :]
                seg_b = acs_b[r] - acs_rb[r:r + 1, :]
                w = (jnp.exp(jnp.where(lower, seg_f, ATT_NEG)) * del_rf[r:r + 1, :]
                     + jnp.exp(jnp.where(upper, seg_b, ATT_NEG)) * del_rb[r:r + 1, :])
                x_own = jnp.where(first if half == 0 else ~first, x_pair, 0.0).astype(BF16)
                y_pair = y_pair + jnp.dot((cb * w).astype(BF16), x_own, preferred_element_type=F32)
            y_ref[rows, lanes] = y_pair
        xw = xs * _expand_heads(wend, selector)
        states = lax.dot_general(bmat, xw.astype(BF16), (((0,), (0,)), ((), ())), preferred_element_type=F32)
        total = acs[q - 1:q]
        carried[...] = jnp.exp(_per_head([total[:, r:r + 1] for r in range(SSD_HPG)])) * prev + states
        return carry

    lax.fori_loop(0, nchunk, chunk, 0, unroll=8)


def _ssd_scan(xs_act, bc_act, acs_c, wend_c, acs_r, delta_r, d_skip):
    bsz, seqlen, _ = xs_act.shape
    sb = min(SSD_SEQ_BLOCK, seqlen)
    nblk = seqlen // sb
    cpb = sb // SSD_CHUNK
    n_x = SSD_HPG * SSD_HEAD_DIM
    b_off = 0
    c_off = SSD_GROUPS
    lanes = acs_c.shape[1]
    nrow = acs_r.shape[0]

    def specs(blk):
        xs = pl.BlockSpec((None, sb, n_x), lambda b_, g, i: (b_, blk(i), g))
        bm = pl.BlockSpec((None, sb, SSD_STATE), lambda b_, g, i: (b_, blk(i), b_off + g))
        cm = pl.BlockSpec((None, sb, SSD_STATE), lambda b_, g, i: (b_, blk(i), c_off + g))
        colf = pl.BlockSpec((sb, lanes), lambda b_, g, i: (b_ * nblk + blk(i), 0))
        rowf = pl.BlockSpec((nrow, sb), lambda b_, g, i: (0, b_ * nblk + blk(i)))
        back = pl.BlockSpec((None, None, cpb, SSD_STATE, n_x), lambda b_, g, i: (b_, g, blk(i), 0, 0))
        return xs, bm, cm, colf, rowf, back

    xs, bm, cm, colf, rowf, back = specs(lambda i: nblk - 1 - i)
    assert SSD_CHUNK == 2 * SSD_HEAD_DIM and SSD_HPG == 4
    state = [pltpu.VMEM((SSD_STATE, n_x), F32), pltpu.VMEM((sb, n_x), BF16)]
    back_states = pl.pallas_call(
        _ssd_back_kernel,
        grid=(bsz, SSD_GROUPS, nblk),
        in_specs=[xs, bm, colf, colf],
        out_specs=back,
        out_shape=jax.ShapeDtypeStruct((bsz, SSD_GROUPS, seqlen // SSD_CHUNK, SSD_STATE, n_x), BF16),
        scratch_shapes=state,
        compiler_params=_params("parallel", "parallel", "arbitrary"),
        name="ssd_back",
    )(xs_act, bc_act, acs_c, wend_c)

    xs, bm, cm, colf, rowf, back = specs(lambda i: i)
    return pl.pallas_call(
        _ssd_main_kernel,
        grid=(bsz, SSD_GROUPS, nblk),
        in_specs=[xs, bm, cm, colf, colf, rowf, rowf, back,
                  pl.BlockSpec((None, 1, SSD_HPG), lambda b_, g, i: (g, 0, 0))],
        out_specs=pl.BlockSpec((None, sb, n_x), lambda b_, g, i: (b_, i, g)),
        out_shape=jax.ShapeDtypeStruct((bsz, seqlen, SSD_INNER), F32),
        scratch_shapes=state[:1],
        compiler_params=_params("parallel", "parallel", "arbitrary"),
        name="ssd_main",
    )(xs_act, bc_act, bc_act, acs_c, wend_c, acs_r, delta_r, back_states,
      d_skip.reshape(SSD_GROUPS, 1, SSD_HPG))


def _attn_kernel(q_ref, ko_ref, kp_ref, kn_ref, vo_ref, vp_ref, vn_ref, o_ref, lse_ref, kcat, vcat, bias,
                 *, dil, group):
    tq = q_ref.shape[0]
    rad = ATT_RADIUS
    qs = ATT_QSUB
    kw = qs + 2 * rad
    nsub = tq // qs
    i = pl.program_id(1)
    kcat[0:rad, :] = kp_ref[...].astype(BF16)
    kcat[rad:rad + tq, :] = ko_ref[...].astype(BF16)
    kcat[rad + tq:2 * rad + tq, :] = kn_ref[...].astype(BF16)
    vcat[0:rad, :] = vp_ref[...].astype(BF16)
    vcat[rad:rad + tq, :] = vo_ref[...].astype(BF16)
    vcat[rad + tq:2 * rad + tq, :] = vn_ref[...].astype(BF16)

    row = lax.broadcasted_iota(jnp.int32, (qs, kw), 0)
    col = lax.broadcasted_iota(jnp.int32, (qs, kw), 1)
    dist = jnp.abs(row - col + rad)
    for head in range(ATT_HPG):
        slope = 2.0 ** (-8.0 * (head * ATT_GROUPS + group + 1) / ATT_HEADS)
        bias[head] = jnp.where(dist <= rad, (-slope * dil) * dist.astype(F32), ATT_NEG)
    before_start = (i == 0) & (col < rad)
    past_end = (i == pl.num_programs(1) - 1) & (col >= qs + rad)
    lane = lax.broadcasted_iota(jnp.int32, (qs, 2 * ATT_HEAD_DIM), 1)
    first_half = lane < ATT_HEAD_DIM
    scale = ATT_HEAD_DIM ** -0.5

    for m in range(nsub):
        for pair in range(ATT_HPG // 2):
            lanes = slice(pair * 2 * ATT_HEAD_DIM, (pair + 1) * 2 * ATT_HEAD_DIM)
            q2 = q_ref[m * qs:(m + 1) * qs, lanes]
            k2 = kcat[m * qs:m * qs + kw, lanes]
            v2 = vcat[m * qs:m * qs + kw, lanes]
            for half in range(2):
                head = 2 * pair + half
                qm = jnp.where(first_half if half == 0 else ~first_half, q2, 0.0).astype(BF16)
                s = lax.dot_general(qm, k2, (((1,), (1,)), ((), ())), preferred_element_type=F32)
                s = s * scale + bias[head]
                if m == 0:
                    s = jnp.where(before_start, ATT_NEG, s)
                if m == nsub - 1:
                    s = jnp.where(past_end, ATT_NEG, s)
                mx = jnp.max(s, axis=-1, keepdims=True)
                p = jnp.exp(s - mx)
                zsum = jnp.sum(p, axis=-1, keepdims=True)
                pv = jnp.dot(p.astype(BF16), v2, preferred_element_type=F32)
                own = slice(half * ATT_HEAD_DIM, (half + 1) * ATT_HEAD_DIM)
                out_lanes = slice(lanes.start + own.start, lanes.start + own.stop)
                o_ref[m * qs:(m + 1) * qs, out_lanes] = (pv / zsum)[:, own]
                lse_ref[m * qs:(m + 1) * qs, out_lanes] = jnp.broadcast_to(mx + jnp.log(zsum), (qs, ATT_HEAD_DIM))


def _attention_group(qkv, group, dil):
    nseq, ls, _ = qkv.shape
    tq = min(512, ls)
    nblk = ls // tq
    halo_per_blk = tq // ATT_RADIUS
    n_halo = ls // ATT_RADIUS

    def own(off):
        return pl.BlockSpec((None, tq, ATT_OUT), lambda s, i: (s, i, off))

    def prev(off):
        return pl.BlockSpec((None, ATT_RADIUS, ATT_OUT), lambda s, i: (s, jnp.maximum(i * halo_per_blk - 1, 0), off))

    def nxt(off):
        return pl.BlockSpec((None, ATT_RADIUS, ATT_OUT),
                            lambda s, i: (s, jnp.minimum((i + 1) * halo_per_blk, n_halo - 1), off))

    out_spec = pl.BlockSpec((None, tq, ATT_OUT), lambda s, i: (s, i, 0))
    return pl.pallas_call(
        functools.partial(_attn_kernel, dil=dil, group=group),
        grid=(nseq, nblk),
        in_specs=[own(0), own(1), prev(1), nxt(1), own(2), prev(2), nxt(2)],
        out_specs=[out_spec, out_spec],
        out_shape=[jax.ShapeDtypeStruct((nseq, ls, ATT_OUT), F32)] * 2,
        scratch_shapes=[pltpu.VMEM((tq + 2 * ATT_RADIUS, ATT_OUT), BF16)] * 2
        + [pltpu.VMEM((ATT_HPG, ATT_QSUB, ATT_QSUB + 2 * ATT_RADIUS), F32)],
        compiler_params=_params("parallel", "parallel"),
        name=f"attn_g{group}",
    )(qkv, qkv, qkv, qkv, qkv, qkv, qkv)


def _s5_tables(a_re, a_im, log_step, b_re, b_im, c_re, c_im):
    hp = lax.Precision.HIGHEST
    tc = S5_TC
    ng, width = S5_GROUPS, S5_ROW
    step = jnp.exp(log_step)[..., None]
    ar, ai = a_re, a_im
    mag = jnp.exp(step * ar)
    abr, abi = mag * jnp.cos(step * ai), mag * jnp.sin(step * ai)
    den = ar * ar + ai * ai
    fr = ((abr - 1.0) * ar + abi * ai) / den
    fi = (abi * ar - (abr - 1.0) * ai) / den
    b_re_t, b_im_t = b_re.transpose(0, 2, 1), b_im.transpose(0, 2, 1)
    bbr = fr[:, :, None, :] * b_re_t[None] - fi[:, :, None, :] * b_im_t[None]
    bbi = fr[:, :, None, :] * b_im_t[None] + fi[:, :, None, :] * b_re_t[None]
    ks = jnp.arange(tc + 1, dtype=F32)
    arg_r, arg_i = step * ar, step * ai
    pk_r = jnp.exp(arg_r[:, :, None, :] * ks[:, None]) * jnp.cos(arg_i[:, :, None, :] * ks[:, None])
    pk_i = jnp.exp(arg_r[:, :, None, :] * ks[:, None]) * jnp.sin(arg_i[:, :, None, :] * ks[:, None])
    pn_r = jnp.exp(arg_r[..., None] * ks) * jnp.cos(arg_i[..., None] * ks)
    pn_i = jnp.exp(arg_r[..., None] * ks) * jnp.sin(arg_i[..., None] * ks)
    c_re_t, c_im_t = c_re.transpose(0, 1, 3, 2), c_im.transpose(0, 1, 3, 2)
    cap_r = (c_re_t[:, :, :, None, :] * pn_r[..., None] - c_im_t[:, :, :, None, :] * pn_i[..., None])
    cap_i = (c_re_t[:, :, :, None, :] * pn_i[..., None] + c_im_t[:, :, :, None, :] * pn_r[..., None])
    cap_r = cap_r.reshape(2, ng, S5_STATE, (tc + 1) * S5_CH)
    cap_i = cap_i.reshape(2, ng, S5_STATE, (tc + 1) * S5_CH)
    taps = (jnp.einsum('dgin,dgnm->dgim', bbr, cap_r, precision=hp)
            - jnp.einsum('dgin,dgnm->dgim', bbi, cap_i, precision=hp))[..., :width]
    fwd_src = jnp.pad(taps[0], ((0, 0), (0, 0), (width, 0)))
    bwd_rev = taps[1].reshape(ng, S5_CH, tc, S5_CH)[:, :, ::-1, :].reshape(ng, S5_CH, width)
    bwd_src = jnp.pad(bwd_rev, ((0, 0), (0, 0), (0, width)))
    toep = jnp.stack([fwd_src[:, :, width - S5_CH * s:2 * width - S5_CH * s]
                      + bwd_src[:, :, S5_CH * (tc - 1 - s):S5_CH * (tc - 1 - s) + width] for s in range(tc)], axis=1)
    toep = toep.reshape(ng, width, width)

    def end_state(d, pw_r, pw_i):
        re = pw_r[:, :, None, :] * bbr[d][:, None] - pw_i[:, :, None, :] * bbi[d][:, None]
        im = pw_r[:, :, None, :] * bbi[d][:, None] + pw_i[:, :, None, :] * bbr[d][:, None]
        return [re.reshape(ng, width, S5_STATE), im.reshape(ng, width, S5_STATE)]

    b_end = jnp.concatenate(end_state(0, pk_r[0][:, tc - 1::-1], pk_i[0][:, tc - 1::-1])
                            + end_state(1, pk_r[1][:, :tc], pk_i[1][:, :tc]), axis=-1)

    def lag_reversed(a):
        return a.reshape(ng, S5_STATE, tc, S5_CH)[:, :, ::-1, :].reshape(ng, S5_STATE, width)

    c_out = jnp.concatenate([cap_r[0][..., S5_CH:], -cap_i[0][..., S5_CH:],
                             lag_reversed(cap_r[1][..., S5_CH:]), -lag_reversed(cap_i[1][..., S5_CH:])], axis=1)
    ptr, pti = pk_r[:, :, tc], pk_i[:, :, tc]
    p_mul = jnp.concatenate([ptr[0], ptr[0], ptr[1], ptr[1]], axis=-1)
    q_mul = jnp.concatenate([-pti[0], pti[0], -pti[1], pti[1]], axis=-1)
    return toep, b_end, c_out, p_mul, q_mul


def _granule_transpose(regs, granule):
    n = len(regs)
    d = n // 2
    while d >= 1:
        low = (granule // d) % 2 == 0
        nxt = list(regs)
        for i in range(n):
            if (i // d) % 2 == 0:
                a, b = regs[i], regs[i + d]
                nxt[i] = jnp.where(low, a, pltpu.roll(b, d * S5_CH, 1))
                nxt[i + d] = jnp.where(low, pltpu.roll(a, 128 - d * S5_CH, 1), b)
        regs = nxt
        d //= 2
    return regs


def _s5_rows_to_groups(u_ref, x_scr):
    rr = S5_RELAYOUT_ROWS
    per = S5_LANE_GROUPS
    granule = lax.broadcasted_iota(jnp.int32, (rr, 128), 1) // S5_CH

    def body(rb, carry):
        base = pl.multiple_of(rb * (rr * S5_TC), rr * S5_TC)
        rows = pl.ds(pl.multiple_of(rb * rr, rr), rr)
        for h in range(S5_TC // per):
            steps = [u_ref[pl.ds(base + per * h + t, rr, stride=S5_TC), :] for t in range(per)]
            for gl, block in enumerate(_granule_transpose(steps, granule)):
                x_scr[gl, rows, 128 * h:128 * (h + 1)] = block
        return carry

    lax.fori_loop(0, x_scr.shape[1] // rr, body, 0)


def _s5_in_kernel(u_ref, toep_ref, bend_ref, y_ref, sf_ref, sb_ref, x_scr):
    half = 2 * S5_STATE
    _s5_rows_to_groups(u_ref, x_scr)
    for gl in range(S5_LANE_GROUPS):
        xb = x_scr[gl].astype(BF16)
        y_ref[gl] = jnp.dot(xb, toep_ref[gl], preferred_element_type=F32)
        s = jnp.dot(xb, bend_ref[gl], preferred_element_type=F32)
        sf_ref[gl] = s[:, 0:half]
        sb_ref[gl] = s[:, half:2 * half]


def _s5_scan_kernel(sf_ref, sb_ref, p_ref, q_ref, hf_ref, hb_ref, sf_sw_ref, sb_sw_ref, *, rows_per_seq):
    n_chain = p_ref.shape[0]
    half = 2 * S5_STATE
    pf, qf = p_ref[:, 0:half], q_ref[:, 0:half]
    pb, qb = p_ref[:, half:2 * half], q_ref[:, half:2 * half]
    qf_sw = pltpu.roll(qf, S5_STATE, 1)
    qb_sw = pltpu.roll(qb, S5_STATE, 1)
    zero = jnp.zeros((n_chain, half), F32)
    piece = 256

    def swap(k, carry):
        rows = pl.ds(pl.multiple_of(k * piece, piece), piece)
        sf_sw_ref[rows, :] = pltpu.roll(sf_ref[rows, :], S5_STATE, 1)
        sb_sw_ref[rows, :] = pltpu.roll(sb_ref[rows, :], S5_STATE, 1)
        return carry

    lax.fori_loop(0, sf_ref.shape[0] // piece, swap, 0)

    tile = 8

    def steps(c8, carry):
        hf, hf_sw, hb, hb_sw = carry
        base_f = pl.multiple_of(c8 * tile, tile)
        base_b = pl.multiple_of(rows_per_seq - tile - c8 * tile, tile)
        for k in range(tile):
            rows_f = pl.ds(base_f + k, n_chain, stride=rows_per_seq)
            rows_b = pl.ds(base_b + (tile - 1 - k), n_chain, stride=rows_per_seq)
            hf_ref[rows_f, :] = hf
            hb_ref[rows_b, :] = hb
            sf = sf_ref[rows_f, :]
            sb = sb_ref[rows_b, :]
            sf_sw = sf_sw_ref[rows_f, :]
            sb_sw = sb_sw_ref[rows_b, :]
            hf, hf_sw = pf * hf + qf * hf_sw + sf, pf * hf_sw + qf_sw * hf + sf_sw
            hb, hb_sw = pb * hb + qb * hb_sw + sb, pb * hb_sw + qb_sw * hb + sb_sw
        return hf, hf_sw, hb, hb_sw

    lax.fori_loop(0, rows_per_seq // tile, steps, (zero, zero, zero, zero))


def _s5_out_kernel(hf_ref, hb_ref, cout_ref, y_ref, u_ref, d_ref, o_ref, y_scr):
    half = 2 * S5_STATE
    per = S5_LANE_GROUPS
    for gl in range(per):
        y = y_ref[gl] + jnp.dot(hf_ref[gl].astype(BF16), cout_ref[gl, 0:half, :], preferred_element_type=F32)
        y_scr[gl] = y + jnp.dot(hb_ref[gl].astype(BF16), cout_ref[gl, half:2 * half, :], preferred_element_type=F32)
    rr = S5_RELAYOUT_ROWS
    granule = lax.broadcasted_iota(jnp.int32, (rr, 128), 1) // S5_CH
    d_skip = d_ref[...]

    def body(rb, carry):
        rows = pl.ds(pl.multiple_of(rb * rr, rr), rr)
        base = pl.multiple_of(rb * (rr * S5_TC), rr * S5_TC)
        for h in range(S5_TC // per):
            groups = [y_scr[gl, rows, 128 * h:128 * (h + 1)] for gl in range(per)]
            for t, block in enumerate(_granule_transpose(groups, granule)):
                tok = pl.ds(base + per * h + t, rr, stride=S5_TC)
                o_ref[tok, :] = jax.nn.gelu(block + u_ref[tok, :] * d_skip)
        return carry

    lax.fori_loop(0, y_scr.shape[1] // rr, body, 0)


def _s5_all_tables(a_re, a_im, log_step, b_re, b_im, c_re, c_im):
    toep, b_end, c_out, p_mul, q_mul = jax.vmap(_s5_tables)(a_re, a_im, log_step, b_re, b_im, c_re, c_im)
    return toep.astype(BF16), b_end.astype(BF16), c_out.astype(BF16), p_mul, q_mul


def _s5_mix(u_tiles, seqlen, tables, layer, d_skip):
    ntile, ntok, lanes = u_tiles.shape
    bsz = ntok // seqlen
    rows_per_seq = seqlen // S5_TC
    rows = bsz * rows_per_seq
    rblk = min(S5_ROW_BLOCK, rows)
    per = S5_LANE_GROUPS
    half = 2 * S5_STATE
    toep, b_end, c_out, p_mul, q_mul = tables
    p_mul, q_mul = p_mul[layer], q_mul[layer]
    utile = pl.BlockSpec((None, rblk * S5_TC, lanes), lambda j, i: (j, i, 0))
    ytile = pl.BlockSpec((per, rblk, S5_ROW), lambda j, i: (j, i, 0))
    stile = pl.BlockSpec((per, rblk, half), lambda j, i: (j, i, 0))
    wtile = pl.BlockSpec((None, per, S5_ROW, S5_ROW), lambda j, i: (layer, j, 0, 0))
    state_shape = jax.ShapeDtypeStruct((S5_GROUPS, rows, half), F32)
    grid = (ntile, rows // rblk)
    y_in, s_f, s_b = pl.pallas_call(
        _s5_in_kernel,
        grid=grid,
        in_specs=[utile, wtile, wtile],
        out_specs=[ytile, stile, stile],
        out_shape=[jax.ShapeDtypeStruct((S5_GROUPS, rows, S5_ROW), F32), state_shape, state_shape],
        scratch_shapes=[pltpu.VMEM((per, rblk, S5_ROW), F32)],
        compiler_params=_params("parallel", "parallel"),
        name="s5_in",
    )(u_tiles, toep, b_end)

    gs = S5_SCAN_GROUPS
    n_chain = gs * bsz
    chain_p = jnp.repeat(p_mul, bsz, axis=0)
    chain_q = jnp.repeat(q_mul, bsz, axis=0)
    flat = jax.ShapeDtypeStruct((S5_GROUPS * rows, half), F32)
    srows = pl.BlockSpec((gs * rows, half), lambda t: (t, 0))
    chain = pl.BlockSpec((n_chain, S5_ROW), lambda t: (t, 0))
    h_f, h_b = pl.pallas_call(
        functools.partial(_s5_scan_kernel, rows_per_seq=rows_per_seq),
        grid=(S5_GROUPS // gs,),
        in_specs=[srows, srows, chain, chain],
        out_specs=[srows, srows],
        out_shape=[flat, flat],
        scratch_shapes=[pltpu.VMEM((gs * rows, half), F32)] * 2,
        compiler_params=_params("parallel"),
        name="s5_scan",
    )(s_f.reshape(S5_GROUPS * rows, half), s_b.reshape(S5_GROUPS * rows, half), chain_p, chain_q)

    return pl.pallas_call(
        _s5_out_kernel,
        grid=grid,
        in_specs=[stile, stile, wtile, ytile, utile, pl.BlockSpec((None, 1, lanes), lambda j, i: (j, 0, 0))],
        out_specs=utile,
        out_shape=jax.ShapeDtypeStruct((ntile, ntok, lanes), F32),
        scratch_shapes=[pltpu.VMEM((per, rblk, S5_ROW), F32)],
        compiler_params=_params("parallel", "parallel"),
        name="s5_out",
    )(h_f.reshape(S5_GROUPS, rows, half), h_b.reshape(S5_GROUPS, rows, half), c_out, y_in, u_tiles,
      d_skip.reshape(ntile, 1, lanes))


def _ssd_proj_kernel(y_ref, z_ref, nw_ref, w_ref, gate_ref, bg_ref, o_ref):
    z = z_ref[...]
    y = y_ref[...] * (z * _sigmoid(z))
    y = y * lax.rsqrt(jnp.mean(y * y, axis=-1, keepdims=True) + LN_EPS) * nw_ref[...]
    p = jnp.dot(y.astype(BF16), w_ref[...], preferred_element_type=F32)
    o_ref[...] = (_sigmoid(gate_ref[...] + bg_ref[...]) * p).astype(o_ref.dtype)


def _ssd_project(y, z, norm_w, w_br, gates, b_gate, *, tm=512):
    m = y.shape[0]
    wide = pl.BlockSpec((tm, SSD_INNER), lambda i: (i, 0))
    return pl.pallas_call(
        _ssd_proj_kernel,
        grid=(m // tm,),
        in_specs=[wide, wide, pl.BlockSpec((1, SSD_INNER), lambda i: (0, 0)),
                  pl.BlockSpec((SSD_INNER, D_MODEL), lambda i: (0, 0)),
                  pl.BlockSpec((tm, D_MODEL), lambda i: (i, 0)),
                  pl.BlockSpec((1, D_MODEL), lambda i: (0, 0))],
        out_specs=pl.BlockSpec((tm, D_MODEL), lambda i: (i, 0)),
        out_shape=jax.ShapeDtypeStruct((m, D_MODEL), BF16),
        compiler_params=_params("parallel"),
        name="ssd_project",
    )(y, z, norm_w.reshape(1, -1), w_br, gates, b_gate[0:1])


def _s5_proj_kernel(h_ref, w1_ref, w2_ref, w_ref, gate_ref, bg_ref, o_ref):
    hb = jnp.concatenate([h_ref[t] for t in range(h_ref.shape[0])], axis=1).astype(BF16)
    a = jnp.dot(hb, w1_ref[...], preferred_element_type=F32)
    b = jnp.dot(hb, w2_ref[...], preferred_element_type=F32)
    p = jnp.dot((a * _sigmoid(b)).astype(BF16), w_ref[...], preferred_element_type=F32)
    o_ref[...] = (_sigmoid(gate_ref[...] + bg_ref[...]) * p).astype(o_ref.dtype)


def _s5_project(h_tiles, w1, w2, w_br, gates, b_gate, *, tm=512):
    ntile, m, lanes = h_tiles.shape
    sq = pl.BlockSpec((S5_WIDTH, S5_WIDTH), lambda i: (0, 0))
    return pl.pallas_call(
        _s5_proj_kernel,
        grid=(m // tm,),
        in_specs=[pl.BlockSpec((ntile, tm, lanes), lambda i: (0, i, 0)), sq, sq,
                  pl.BlockSpec((S5_WIDTH, D_MODEL), lambda i: (0, 0)),
                  pl.BlockSpec((tm, D_MODEL), lambda i: (i, 2)),
                  pl.BlockSpec((1, D_MODEL), lambda i: (0, 0))],
        out_specs=pl.BlockSpec((tm, D_MODEL), lambda i: (i, 0)),
        out_shape=jax.ShapeDtypeStruct((m, D_MODEL), BF16),
        compiler_params=_params("parallel"),
        name="s5_project",
    )(h_tiles, w1, w2, w_br, gates, b_gate[2:3])


def _layer_norm(v, g, b):
    mu = jnp.mean(v, axis=-1, keepdims=True)
    c = v - mu
    var = jnp.mean(c * c, axis=-1, keepdims=True)
    return c * lax.rsqrt(var + LN_EPS) * g + b


def _merge_kernel(o0_ref, o1_ref, o2_ref, l0_ref, l1_ref, l2_ref, w_att_ref, gate_ref, bg_ref, pssd_ref, ps5_ref,
                  w_out_ref, x_ref, g_ref, b_ref, o_ref):
    l0, l1, l2 = l0_ref[...], l1_ref[...], l2_ref[...]
    mx = jnp.maximum(jnp.maximum(l0, l1), l2)
    e0, e1, e2 = jnp.exp(l0 - mx), jnp.exp(l1 - mx), jnp.exp(l2 - mx)
    den = e0 + e1 + e2
    att = (e0 / den) * o0_ref[...] + (e1 / den) * o1_ref[...] + (e2 / den) * o2_ref[...]
    p_att = jnp.dot(att.astype(BF16), w_att_ref[...], preferred_element_type=F32)
    merged = pssd_ref[...] + _sigmoid(gate_ref[...] + bg_ref[...]) * p_att + ps5_ref[...]
    mix = jnp.dot(merged.astype(BF16), w_out_ref[...], preferred_element_type=F32)
    o_ref[...] = _layer_norm(ALPHA * x_ref[...] + mix, g_ref[...], b_ref[...])


def _merge(att_o, att_l, w_att, gates, b_gate, p_ssd, p_s5, w_out, x, ln_g, ln_b, *, tm=512):
    m = x.shape[0]
    att = pl.BlockSpec((tm, ATT_OUT), lambda i: (i, 0))
    row = pl.BlockSpec((tm, D_MODEL), lambda i: (i, 0))
    vec = pl.BlockSpec((1, D_MODEL), lambda i: (0, 0))
    return pl.pallas_call(
        _merge_kernel,
        grid=(m // tm,),
        in_specs=[att] * 6 + [pl.BlockSpec((ATT_OUT, D_MODEL), lambda i: (0, 0)),
                              pl.BlockSpec((tm, D_MODEL), lambda i: (i, 1)), vec, row, row,
                              pl.BlockSpec((D_MODEL, D_MODEL), lambda i: (0, 0)), row, vec, vec],
        out_specs=row,
        out_shape=jax.ShapeDtypeStruct((m, D_MODEL), F32),
        compiler_params=_params("parallel"),
        name="merge_ln1",
    )(*att_o, *att_l, w_att, gates, b_gate[1:2], p_ssd, p_s5, w_out, x, ln_g.reshape(1, -1), ln_b.reshape(1, -1))


def _router_kernel(x_ref, wt_ref, b_ref, idx_ref, gate_ref, rank_ref, count_ref, before_ref):
    tm = x_ref.shape[0]

    @pl.when(pl.program_id(0) == 0)
    def _():
        count_ref[...] = jnp.zeros_like(count_ref)
        earlier = lax.broadcasted_iota(jnp.int32, (tm, tm), 0) < lax.broadcasted_iota(jnp.int32, (tm, tm), 1)
        before_ref[...] = earlier.astype(before_ref.dtype)

    x = x_ref[...]
    wt = wt_ref[...]
    xh = x.astype(BF16)
    xl = (x - xh.astype(F32)).astype(BF16)
    wh = wt.astype(BF16)
    wl = (wt - wh.astype(F32)).astype(BF16)
    nt = (((1,), (1,)), ((), ()))
    logits = (lax.dot_general(wh, xh, nt, preferred_element_type=F32)
              + lax.dot_general(wl, xh, nt, preferred_element_type=F32)
              + lax.dot_general(wh, xl, nt, preferred_element_type=F32)) + b_ref[...]
    eid = lax.broadcasted_iota(jnp.int32, logits.shape, 0)
    vals, idxs = [], []
    for _ in range(MOE_TOP_K):
        mx = jnp.max(logits, axis=0, keepdims=True)
        sel = jnp.min(jnp.where(logits == mx, eid, MOE_EXPERTS), axis=0, keepdims=True)
        vals.append(mx)
        idxs.append(sel)
        logits = jnp.where(eid == sel, -jnp.inf, logits)
    es = [jnp.exp(v - vals[0]) for v in vals]
    den = es[0] + es[1] + es[2] + es[3]
    for k in range(MOE_TOP_K):
        idx_ref[k:k + 1, :] = idxs[k]
        gate_ref[k:k + 1, :] = es[k] / den
    chosen = [eid == idxs[k] for k in range(MOE_TOP_K)]
    stacked = jnp.concatenate([c.astype(before_ref.dtype) for c in chosen], axis=0)
    earlier = jnp.dot(stacked, before_ref[...], preferred_element_type=F32)
    seen = count_ref[...]
    for k in range(MOE_TOP_K):
        pos = earlier[k * MOE_EXPERTS:(k + 1) * MOE_EXPERTS] + seen
        rank_ref[k:k + 1, :] = jnp.sum(jnp.where(chosen[k], pos, 0.0), axis=0, keepdims=True).astype(jnp.int32)
        seen = seen + jnp.sum(chosen[k].astype(F32), axis=1, keepdims=True)
    count_ref[...] = seen


def _router(x, router_w, router_b, *, tm=1024):
    m = x.shape[0]
    out = pl.BlockSpec((MOE_TOP_K, tm), lambda i: (0, i))
    return pl.pallas_call(
        _router_kernel,
        grid=(m // tm,),
        in_specs=[pl.BlockSpec((tm, D_MODEL), lambda i: (i, 0)),
                  pl.BlockSpec((MOE_EXPERTS, D_MODEL), lambda i: (0, 0)),
                  pl.BlockSpec((MOE_EXPERTS, 1), lambda i: (0, 0))],
        out_specs=[out, out, out, pl.BlockSpec((MOE_EXPERTS, 1), lambda i: (0, 0))],
        out_shape=[jax.ShapeDtypeStruct((MOE_TOP_K, m), jnp.int32), jax.ShapeDtypeStruct((MOE_TOP_K, m), F32),
                   jax.ShapeDtypeStruct((MOE_TOP_K, m), jnp.int32), jax.ShapeDtypeStruct((MOE_EXPERTS, 1), F32)],
        scratch_shapes=[pltpu.VMEM((tm, tm), BF16)],
        compiler_params=_params("arbitrary"),
        name="router",
    )(x, router_w.T, router_b.reshape(-1, 1))


def _expert_kernel(be_ref, used_ref, x_ref, wgu_ref, bgu_ref, wdn_ref, bdn_ref, o_ref, wgu_lo, wdn_lo):
    i = pl.program_id(0)
    used = i < used_ref[0]

    @pl.when(used & ((i == 0) | (be_ref[i] != be_ref[jnp.maximum(i - 1, 0)])))
    def _():
        wgu_lo[...] = wgu_ref[...].astype(wgu_lo.dtype)
        wdn_lo[...] = wdn_ref[...].astype(wdn_lo.dtype)

    @pl.when(used)
    def _():
        hu = jnp.dot(x_ref[...].astype(BF16), wgu_lo[...], preferred_element_type=F32) + bgu_ref[...]
        x_glu = jnp.minimum(hu[:, :MOE_FF], SWIGLU_LIMIT)
        x_lin = jnp.clip(hu[:, MOE_FF:], -SWIGLU_LIMIT, SWIGLU_LIMIT)
        act = x_glu * _sigmoid(SWIGLU_ALPHA * x_glu) * (x_lin + 1.0)
        o_ref[...] = jnp.dot(act.astype(BF16), wdn_lo[...], preferred_element_type=F32) + bdn_ref[...]

    @pl.when(jnp.logical_not(used))
    def _():
        o_ref[...] = jnp.zeros_like(o_ref)


def _experts(xs, block_expert, n_used, w_gu, b_gu, w_dn, b_dn, layer):
    n_rows = xs.shape[0]
    n_blocks = n_rows // MOE_ROWS
    grid_spec = pltpu.PrefetchScalarGridSpec(
        num_scalar_prefetch=2,
        grid=(n_blocks,),
        in_specs=[
            pl.BlockSpec((MOE_ROWS, D_MODEL), lambda i, be, nu: (i, 0)),
            pl.BlockSpec((None, None, D_MODEL, 2 * MOE_FF), lambda i, be, nu: (layer, be[i], 0, 0)),
            pl.BlockSpec((None, 1, 2 * MOE_FF), lambda i, be, nu: (be[i], 0, 0)),
            pl.BlockSpec((None, None, MOE_FF, D_MODEL), lambda i, be, nu: (layer, be[i], 0, 0)),
            pl.BlockSpec((None, 1, D_MODEL), lambda i, be, nu: (be[i], 0, 0)),
        ],
        out_specs=pl.BlockSpec((MOE_ROWS, D_MODEL), lambda i, be, nu: (i, 0)),
        scratch_shapes=[pltpu.VMEM((D_MODEL, 2 * MOE_FF), BF16), pltpu.VMEM((MOE_FF, D_MODEL), BF16)],
    )
    return pl.pallas_call(
        _expert_kernel,
        grid_spec=grid_spec,
        out_shape=jax.ShapeDtypeStruct((n_rows, D_MODEL), F32),
        compiler_params=_params("arbitrary"),
        name="experts",
    )(block_expert, n_used, xs, w_gu, b_gu.reshape(MOE_EXPERTS, 1, -1), w_dn, b_dn.reshape(MOE_EXPERTS, 1, -1))


def _combine_kernel(x_ref, y0_ref, y1_ref, y2_ref, y3_ref, gate_ref, g_ref, b_ref, o_ref, ob_ref):
    ffn = gate_ref[:, 0:1] * y0_ref[...]
    for k, y_ref in enumerate((y1_ref, y2_ref, y3_ref), start=1):
        ffn = ffn + gate_ref[:, k:k + 1] * y_ref[...]
    out = _layer_norm(ALPHA * x_ref[...] + ffn, g_ref[...], b_ref[...])
    o_ref[...] = out
    ob_ref[...] = out.astype(ob_ref.dtype)


def _combine(x, y_rows, gate, ln_g, ln_b, *, tm=512):
    m = x.shape[0]
    row = pl.BlockSpec((tm, D_MODEL), lambda i: (i, 0))
    vec = pl.BlockSpec((1, D_MODEL), lambda i: (0, 0))
    return pl.pallas_call(
        _combine_kernel,
        grid=(m // tm,),
        in_specs=[row] + [pl.BlockSpec((tm, D_MODEL), functools.partial(lambda k, i: (k * (m // tm) + i, 0), k))
                          for k in range(MOE_TOP_K)] + [pl.BlockSpec((tm, MOE_TOP_K), lambda i: (i, 0)), vec, vec],
        out_specs=[row, row],
        out_shape=[jax.ShapeDtypeStruct((m, D_MODEL), F32), jax.ShapeDtypeStruct((m, D_MODEL), BF16)],
        compiler_params=_params("parallel"),
        name="combine_ln2",
    )(x, *([y_rows] * MOE_TOP_K), gate, ln_g.reshape(1, -1), ln_b.reshape(1, -1))


def _moe(x, router_w, router_b, w_gu, b_gu, w_dn, b_dn, ln_g, ln_b, layer):
    ntok = x.shape[0]
    idx_t, gate_t, rank_t, counts = _router(x, router_w, router_b)
    gate = gate_t.T
    n_assign = ntok * MOE_TOP_K
    counts = counts.reshape(-1).astype(jnp.int32)
    padded = (counts + MOE_ROWS - 1) // MOE_ROWS * MOE_ROWS
    pend = jnp.cumsum(padded)
    pstart = pend - padded
    experts = jnp.arange(MOE_EXPERTS, dtype=jnp.int32)

    def lookup(table, idx):
        return jnp.sum(jnp.where(idx[..., None] == experts, table, 0), axis=-1)

    dest_t = lookup(pstart, idx_t) + rank_t
    n_blocks = n_assign // MOE_ROWS + MOE_EXPERTS
    n_pad = MOE_EXPERTS * MOE_ROWS
    block_start = jnp.arange(n_blocks, dtype=jnp.int32) * MOE_ROWS
    block_expert = jnp.sum((pend[None, :] <= block_start[:, None]).astype(jnp.int32), axis=1)
    block_expert = jnp.minimum(block_expert, MOE_EXPERTS - 1)
    n_used = (pend[-1:] // MOE_ROWS).astype(jnp.int32)
    gap = padded - counts
    gap_end = jnp.cumsum(gap)
    pad_id = jnp.arange(n_pad, dtype=jnp.int32)
    pad_expert = jnp.sum((gap_end[None, :] <= pad_id[:, None]).astype(jnp.int32), axis=1)
    tail_of = lookup(pstart + counts - (gap_end - gap), jnp.minimum(pad_expert, MOE_EXPERTS - 1)) + pad_id
    pad_slot = jnp.where(pad_expert < MOE_EXPERTS, tail_of, pend[-1] + pad_id - gap_end[-1])
    keys = jnp.concatenate([dest_t.reshape(-1), pad_slot])
    toks = jnp.concatenate([jnp.tile(jnp.arange(ntok, dtype=jnp.int32), MOE_TOP_K), jnp.zeros((n_pad,), jnp.int32)])
    _, slot_tok = lax.sort((keys, toks), num_keys=1, is_stable=False)
    xs = x.at[slot_tok].get(mode="promise_in_bounds")
    ys = _experts(xs, block_expert, n_used, w_gu, b_gu, w_dn, b_dn, layer)
    y_rows = ys.at[dest_t.reshape(-1)].get(mode="promise_in_bounds", unique_indices=True)
    return _combine(x, y_rows, gate, ln_g, ln_b)


def _layer(x, xb, w_in, b_gate, ssd_conv_w, ssd_conv_b, ssd_a_log, ssd_dt_bias, ssd_d, ssd_norm_w,
           s5_d, s5_glu_w1, s5_glu_w2, w_br_ssd, w_br_attn, w_br_s5, w_out, ln1_g, ln1_b,
           router_w, router_b, b_gu, b_dn, ln2_g, ln2_b, *, layer, w_gu, w_dn, s5_tables):
    bsz, seqlen, d = x.shape
    ntok = bsz * seqlen
    xf = x.reshape(ntok, d)
    offs = [0]
    for s in IN_SIZES:
        offs.append(offs[-1] + s)

    def seg(k):
        return w_in[:, offs[k]:offs[k + 1]].astype(BF16)

    gates = _matmul(xb, seg(0), tm=1024, tn=1024, name="proj_gates", out_dtype=BF16)
    z = _matmul(xb, seg(1), tm=1024, tn=1024, name="proj_z")
    u = _matmul_tiles(xb, seg(5), tm=512, name="proj_u")

    w_xbc = seg(2)
    xs_act = _xbc_conv(xb, w_xbc[:, :SSD_INNER], ssd_conv_w[:, :SSD_INNER], ssd_conv_b[:SSD_INNER], seqlen)
    bc_act = _xbc_conv(xb, w_xbc[:, SSD_INNER:], ssd_conv_w[:, SSD_INNER:], ssd_conv_b[SSD_INNER:], seqlen,
                       out_dtype=BF16)
    xs_act = xs_act.reshape(bsz, seqlen, SSD_INNER)
    bc_act = bc_act.reshape(bsz, seqlen, SSD_CONV_CH - SSD_INNER)
    decay = _ssd_decay(xb, seg(3), ssd_a_log, ssd_dt_bias)
    y_ssd = _ssd_scan(xs_act, bc_act, *decay, ssd_d)
    p_ssd = _ssd_project(y_ssd.reshape(ntok, SSD_INNER), z, ssd_norm_w, w_br_ssd.astype(BF16), gates, b_gate)

    w_qkv = w_in[:, offs[4]:offs[5]].reshape(d, 3, ATT_GROUPS, ATT_OUT)
    att_o, att_l = [], []
    for g, (window, dil) in enumerate(ATT_PATTERNS):
        assert (window // 2) // dil == ATT_RADIUS
        ls = seqlen // dil
        x_cls = xb.reshape(bsz, ls, dil, d).transpose(0, 2, 1, 3).reshape(ntok, d)
        qkv_g = _matmul(x_cls, w_qkv[:, :, g].reshape(d, 3 * ATT_OUT).astype(BF16), tm=1024, tn=3 * ATT_OUT,
                        name=f"proj_qkv{g}", out_dtype=BF16)
        o, lse = _attention_group(qkv_g.reshape(bsz * dil, ls, 3 * ATT_OUT), g, dil)
        att_o.append(o.reshape(bsz, dil, ls, ATT_OUT).transpose(0, 2, 1, 3).reshape(ntok, ATT_OUT))
        att_l.append(lse.reshape(bsz, dil, ls, ATT_OUT).transpose(0, 2, 1, 3).reshape(ntok, ATT_OUT))

    h_s5 = _s5_mix(u, seqlen, s5_tables, layer, s5_d)
    p_s5 = _s5_project(h_s5, s5_glu_w1.astype(BF16), s5_glu_w2.astype(BF16), w_br_s5.astype(BF16), gates, b_gate)

    x1 = _merge(att_o, att_l, w_br_attn.astype(BF16), gates, b_gate, p_ssd, p_s5, w_out.astype(BF16),
                xf, ln1_g, ln1_b)
    x2, x2b = _moe(x1, router_w, router_b, w_gu, b_gu, w_dn, b_dn, ln2_g, ln2_b, layer)
    return x2.reshape(bsz, seqlen, d), x2b


def kernel(x, w_in, b_gate, ssd_conv_w, ssd_conv_b, ssd_a_log, ssd_dt_bias, ssd_d, ssd_norm_w, s5_a_re, s5_a_im, s5_log_step, s5_b_re, s5_b_im, s5_c_re, s5_c_im, s5_d, s5_glu_w1, s5_glu_w2, w_br_ssd, w_br_attn, w_br_s5, w_out, ln1_g, ln1_b, router_w, router_b, exp_w_gate_up, exp_b_gate_up, exp_w_down, exp_b_down, ln2_g, ln2_b):
    per_layer = (w_in, b_gate, ssd_conv_w, ssd_conv_b, ssd_a_log, ssd_dt_bias, ssd_d, ssd_norm_w, s5_d, s5_glu_w1,
                 s5_glu_w2, w_br_ssd, w_br_attn, w_br_s5, w_out, ln1_g, ln1_b, router_w, router_b, exp_b_gate_up,
                 exp_b_down, ln2_g, ln2_b)
    s5_tables = _s5_all_tables(s5_a_re, s5_a_im, s5_log_step, s5_b_re, s5_b_im, s5_c_re, s5_c_im)
    xb = x.reshape(-1, x.shape[-1]).astype(BF16)
    for layer in range(DEPTH):
        x, xb = _layer(x, xb, *[p[layer] for p in per_layer], layer=layer, w_gu=exp_w_gate_up, w_dn=exp_w_down,
                       s5_tables=s5_tables)
    return x
```

```python
import functools
import math

import jax
import jax.numpy as jnp
from jax import lax
from jax.experimental import pallas as pl
from jax.experimental.pallas import tpu as pltpu

F32 = jnp.float32
BF16 = jnp.bfloat16

D_MODEL = 1024
DEPTH = 4
ALPHA = (2 * DEPTH) ** 0.25
LN_EPS = 1e-5

SSD_INNER = 2 * D_MODEL
SSD_HEAD_DIM = 64
SSD_GROUPS = 8
SSD_HPG = 4
SSD_HEADS = SSD_GROUPS * SSD_HPG
SSD_STATE = 128
SSD_CONV = 5
SSD_CONV_CH = SSD_INNER + 2 * SSD_GROUPS * SSD_STATE
SSD_CHUNK = 128
SSD_SEQ_BLOCK = 1024
CONV_ROW_PARTS = 1
CONV_HALO_ROWS = 16

ATT_HEAD_DIM = 64
ATT_PATTERNS = ((128, 1), (512, 4), (2048, 16))
ATT_GROUPS = 3
ATT_HPG = 6
ATT_HEADS = ATT_GROUPS * ATT_HPG
ATT_WIDTH = ATT_HEADS * ATT_HEAD_DIM
ATT_OUT = ATT_HPG * ATT_HEAD_DIM
ATT_NEG = -1e30
ATT_RADIUS = 64
ATT_QSUB = 128

S5_CH = 16
S5_STATE = 64
S5_WIDTH = 9 * D_MODEL // 8
S5_GROUPS = S5_WIDTH // S5_CH
S5_TC = 16
S5_ROW = S5_TC * S5_CH
S5_SCAN_GROUPS = 8
S5_LANE_GROUPS = 128 // S5_CH
S5_TILES = S5_GROUPS // S5_LANE_GROUPS
S5_ROW_BLOCK = 512
S5_RELAYOUT_ROWS = 16

N_BRANCHES = 3
IN_SIZES = (N_BRANCHES * D_MODEL, SSD_INNER, SSD_CONV_CH, 2 * SSD_HEADS, 3 * ATT_WIDTH, S5_WIDTH)

MOE_EXPERTS = 32
MOE_TOP_K = 4
MOE_FF = D_MODEL
SWIGLU_LIMIT = 7.0
SWIGLU_ALPHA = 1.702
MOE_ROWS = 256

VMEM_LIMIT = 56 * 1024 * 1024


def _sigmoid(x):
    return 0.5 * jnp.tanh(0.5 * x) + 0.5


def _params(*sem):
    return pltpu.CompilerParams(dimension_semantics=sem, vmem_limit_bytes=VMEM_LIMIT)


def _mm_kernel(x_ref, w_ref, o_ref):
    o_ref[...] = jnp.dot(x_ref[...].astype(BF16), w_ref[...], preferred_element_type=F32).astype(o_ref.dtype)


def _matmul(x, w, *, tm, tn, name, out_dtype=F32):
    m, k = x.shape
    n = w.shape[1]
    return pl.pallas_call(
        _mm_kernel,
        grid=(n // tn, m // tm),
        in_specs=[pl.BlockSpec((tm, k), lambda j, i: (i, 0)), pl.BlockSpec((k, tn), lambda j, i: (0, j))],
        out_specs=pl.BlockSpec((tm, tn), lambda j, i: (i, j)),
        out_shape=jax.ShapeDtypeStruct((m, n), out_dtype),
        compiler_params=_params("parallel", "parallel"),
        name=name,
    )(x, w)


def _mm_tiles_kernel(x_ref, w_ref, o_ref):
    acc = jnp.dot(x_ref[...].astype(BF16), w_ref[...], preferred_element_type=F32)
    for t in range(o_ref.shape[0]):
        o_ref[t] = acc[:, 128 * t:128 * (t + 1)]


def _matmul_tiles(x, w, *, tm, name):
    m, k = x.shape
    n = w.shape[1]
    nt = n // 128
    return pl.pallas_call(
        _mm_tiles_kernel,
        grid=(m // tm,),
        in_specs=[pl.BlockSpec((tm, k), lambda i: (i, 0)), pl.BlockSpec((k, n), lambda i: (0, 0))],
        out_specs=pl.BlockSpec((nt, tm, 128), lambda i: (0, i, 0)),
        out_shape=jax.ShapeDtypeStruct((nt, m, 128), F32),
        compiler_params=_params("parallel"),
        name=name,
    )(x, w)


def _split3(a):
    a1 = a.astype(BF16)
    r1 = a - a1.astype(F32)
    a2 = r1.astype(BF16)
    a3 = (r1 - a2.astype(F32)).astype(BF16)
    return a1, a2, a3


def _decay_kernel(x_ref, w_ref, wt_ref, alog_r_ref, alog_c_ref, bias_r_ref, bias_c_ref,
                  acs_c_ref, wend_c_ref, acs_r_ref, delta_r_ref):
    q = SSD_CHUNK
    nh = SSD_HEADS
    xb = x_ref[...].astype(BF16)
    dt_c = jnp.dot(xb, w_ref[...], preferred_element_type=F32)
    dt_r = lax.dot_general(wt_ref[...], xb, (((1,), (1,)), ((), ())), preferred_element_type=F32)
    delta_c = jax.nn.softplus(dt_c + bias_r_ref[...])
    delta_r = jax.nn.softplus(dt_r + bias_c_ref[...])
    a_c = -jnp.exp(alog_r_ref[...]) * delta_c
    a_r = -jnp.exp(alog_c_ref[...]) * delta_r
    delta_r_ref[...] = delta_r
    row = lax.broadcasted_iota(jnp.int32, (q, q), 0)
    col = lax.broadcasted_iota(jnp.int32, (q, q), 1)
    lower = (col <= row).astype(BF16)
    upper = (col >= row).astype(BF16)
    fwd_lane = lax.broadcasted_iota(jnp.int32, (q, dt_c.shape[1]), 1) < nh
    fwd_row = lax.broadcasted_iota(jnp.int32, (2 * nh, q), 0) < nh
    for c in range(x_ref.shape[0] // q):
        rows = slice(c * q, (c + 1) * q)
        pieces = _split3(a_c[rows])
        prefix = sum(jnp.dot(lower, p, preferred_element_type=F32) for p in pieces)
        suffix = sum(jnp.dot(upper, p, preferred_element_type=F32) for p in pieces)
        acs = jnp.where(fwd_lane, prefix, suffix)
        total = jnp.where(fwd_lane[0:1], prefix[q - 1:q], suffix[0:1])
        acs_c_ref[rows, :] = acs
        wend_c_ref[rows, :] = delta_c[rows] * jnp.exp(total - acs)
        pieces = _split3(a_r[:, rows])
        prefix = sum(jnp.dot(p, upper, preferred_element_type=F32) for p in pieces)
        suffix = sum(jnp.dot(p, lower, preferred_element_type=F32) for p in pieces)
        acs_r_ref[:, rows] = jnp.where(fwd_row, prefix, suffix)


def _ssd_decay(x, w_dt, a_log, dt_bias, *, tm=1024):
    m, k = x.shape
    n = w_dt.shape[1]
    lanes = 128
    pad = lanes - n
    w_pad = jnp.pad(w_dt, ((0, 0), (0, pad)))
    alog = a_log.reshape(-1)
    bias = dt_bias.reshape(-1)
    col = pl.BlockSpec((tm, lanes), lambda i: (i, 0))
    rowf = pl.BlockSpec((n, tm), lambda i: (0, i))
    return pl.pallas_call(
        _decay_kernel,
        grid=(m // tm,),
        in_specs=[pl.BlockSpec((tm, k), lambda i: (i, 0)), pl.BlockSpec((k, lanes), lambda i: (0, 0)),
                  pl.BlockSpec((n, k), lambda i: (0, 0)),
                  pl.BlockSpec((1, lanes), lambda i: (0, 0)), pl.BlockSpec((n, 1), lambda i: (0, 0)),
                  pl.BlockSpec((1, lanes), lambda i: (0, 0)), pl.BlockSpec((n, 1), lambda i: (0, 0))],
        out_specs=[col, col, rowf, rowf],
        out_shape=[jax.ShapeDtypeStruct((m, lanes), F32)] * 2 + [jax.ShapeDtypeStruct((n, m), F32)] * 2,
        compiler_params=_params("parallel"),
        name="ssd_decay",
    )(x, w_pad, w_dt.T, jnp.pad(alog, (0, pad)).reshape(1, lanes), alog.reshape(n, 1),
      jnp.pad(bias, (0, pad)).reshape(1, lanes), bias.reshape(n, 1))


def _xbc_conv_kernel(xm_ref, xp_ref, xn_ref, w_ref, cw_ref, cb_ref, o_ref, buf, *, tiles_per_seq):
    pos = pl.program_id(1) % tiles_per_seq
    ts = xm_ref.shape[0]
    halo = xp_ref.shape[0]
    half = SSD_CONV // 2
    before = jnp.where(pos == 0, jnp.zeros_like(xp_ref[...]), xp_ref[...])
    after = jnp.where(pos == tiles_per_seq - 1, jnp.zeros_like(xn_ref[...]), xn_ref[...])
    rows = jnp.concatenate([before, xm_ref[...], after], axis=0).astype(BF16)
    nparts = CONV_ROW_PARTS
    part = ts // nparts
    edges = [0] + [halo + part * (p + 1) for p in range(nparts - 1)] + [ts + 2 * halo]

    def project(p):
        buf[edges[p]:edges[p + 1], :] = jnp.dot(rows[edges[p]:edges[p + 1]], w_ref[...], preferred_element_type=F32)

    def convolve(p):
        lo = halo - half + part * p
        acc = cb_ref[...] + cw_ref[0:1, :] * buf[lo:lo + part, :]
        for k in range(1, SSD_CONV):
            acc = acc + cw_ref[k:k + 1, :] * buf[lo + k:lo + k + part, :]
        o_ref[part * p:part * (p + 1), :] = (acc * _sigmoid(acc)).astype(o_ref.dtype)

    project(0)
    for p in range(nparts):
        if p + 1 < nparts:
            project(p + 1)
        convolve(p)


def _xbc_conv(x, w, conv_w, conv_b, seqlen, *, tm=512, tn=1024, out_dtype=F32):
    m, k = x.shape
    n = w.shape[1]
    halo = CONV_HALO_ROWS
    nhalo = m // halo
    return pl.pallas_call(
        functools.partial(_xbc_conv_kernel, tiles_per_seq=seqlen // tm),
        grid=(n // tn, m // tm),
        in_specs=[
            pl.BlockSpec((tm, k), lambda j, i: (i, 0)),
            pl.BlockSpec((halo, k), lambda j, i: (jnp.maximum(i * (tm // halo) - 1, 0), 0)),
            pl.BlockSpec((halo, k), lambda j, i: (jnp.minimum((i + 1) * (tm // halo), nhalo - 1), 0)),
            pl.BlockSpec((k, tn), lambda j, i: (0, j)),
            pl.BlockSpec((SSD_CONV, tn), lambda j, i: (0, j)),
            pl.BlockSpec((1, tn), lambda j, i: (0, j)),
        ],
        out_specs=pl.BlockSpec((tm, tn), lambda j, i: (i, j)),
        out_shape=jax.ShapeDtypeStruct((m, n), out_dtype),
        scratch_shapes=[pltpu.VMEM((tm + 2 * halo, tn), F32)],
        compiler_params=_params("parallel", "parallel"),
        name="proj_xbc_conv",
    )(x, x, x, w, conv_w, conv_b.reshape(1, n))


def _per_head(cols):
    rows = cols[0].shape[0]
    first = lax.broadcasted_iota(jnp.int32, (rows, 2 * SSD_HEAD_DIM), 1) < SSD_HEAD_DIM
    return jnp.concatenate([jnp.where(first, cols[0], cols[1]), jnp.where(first, cols[2], cols[3])], axis=1)


def _group_lanes(ref, rows, g):
    lanes = ref.shape[1]
    return pltpu.roll(ref[rows, :], (lanes - SSD_HPG * g) % lanes, 1)


def _group_rows(ref, first, cols, g):
    tile = 8
    per_tile = tile // SSD_HPG
    base = pl.multiple_of(first + (g // per_tile) * tile, tile)
    return pltpu.roll(ref[pl.ds(base, tile), cols], (tile - SSD_HPG * (g % per_tile)) % tile, 0)


def _head_selector(lanes, first):
    width = SSD_HPG * SSD_HEAD_DIM
    src = lax.broadcasted_iota(jnp.int32, (lanes, width), 0)
    dst = lax.broadcasted_iota(jnp.int32, (lanes, width), 1) // SSD_HEAD_DIM
    return (src == dst + first).astype(BF16)


def _expand_heads(cols, selector):
    return sum(jnp.dot(p, selector, preferred_element_type=F32) for p in _split3(cols))


def _weighted_inputs(xs_ref, wend_ref, g, first, xw_scr):
    rows = slice(0, xs_ref.shape[0])
    wend = _group_lanes(wend_ref, rows, g)
    xw_scr[...] = (xs_ref[...] * _expand_heads(wend, _head_selector(wend_ref.shape[1], first))).astype(xw_scr.dtype)


def _ssd_back_kernel(xs_ref, bm_ref, acs_ref, wend_ref, out_ref, carried, xw_scr):
    q = SSD_CHUNK
    nchunk = xs_ref.shape[0] // q
    g = pl.program_id(1)
    bwd = SSD_HEADS

    @pl.when(pl.program_id(2) == 0)
    def _():
        carried[...] = jnp.zeros_like(carried)

    _weighted_inputs(xs_ref, wend_ref, g, bwd, xw_scr)

    def chunk(ci, carry):
        c = nchunk - 1 - ci
        rows = pl.ds(pl.multiple_of(c * q, q), q)
        first_row = pl.ds(pl.multiple_of(c * q, q), 8)
        total = _group_lanes(acs_ref, first_row, g)[0:1]
        states = lax.dot_general(bm_ref[rows, :].astype(BF16), xw_scr[rows, :], (((0,), (0,)), ((), ())),
                                 preferred_element_type=F32)
        prev = carried[...]
        out_ref[c] = prev.astype(out_ref.dtype)
        carried[...] = jnp.exp(_per_head([total[:, bwd + r:bwd + r + 1] for r in range(SSD_HPG)])) * prev + states
        return carry

    lax.fori_loop(0, nchunk, chunk, 0, unroll=4)


def _ssd_main_kernel(xs_ref, bm_ref, cm_ref, acs_c_ref, wend_c_ref, acs_r_ref, delta_r_ref, back_ref,
                     dskip_ref, y_ref, carried):
    q = SSD_CHUNK
    nchunk = xs_ref.shape[0] // q
    g = pl.program_id(1)
    bwd = SSD_HEADS
    hd = SSD_HEAD_DIM

    @pl.when(pl.program_id(2) == 0)
    def _():
        carried[...] = jnp.zeros_like(carried)

    row = lax.broadcasted_iota(jnp.int32, (q, q), 0)
    col = lax.broadcasted_iota(jnp.int32, (q, q), 1)
    lower = col <= row
    upper = col >= row
    d_wide = _per_head([dskip_ref[:, r:r + 1] for r in range(SSD_HPG)])
    selector = _head_selector(wend_c_ref.shape[1], 0)
    first = lax.broadcasted_iota(jnp.int32, (q, 2 * hd), 1) < hd

    def per_head_lanes(blocks):
        return jnp.concatenate([jnp.where(first, blocks[0], blocks[1]), jnp.where(first, blocks[2], blocks[3])], axis=1)

    def chunk(c, carry):
        r0 = pl.multiple_of(c * q, q)
        rows = pl.ds(r0, q)
        xs = xs_ref[rows, :]
        bmat = bm_ref[rows, :].astype(BF16)
        cmat = cm_ref[rows, :].astype(BF16)
        acs = _group_lanes(acs_c_ref, rows, g)
        wend = _group_lanes(wend_c_ref, rows, g)
        acs_f = [jnp.broadcast_to(acs[:, r:r + 1], (q, q)) for r in range(SSD_HPG)]
        acs_b = [jnp.broadcast_to(acs[:, bwd + r:bwd + r + 1], (q, q)) for r in range(SSD_HPG)]
        acs_rf = _group_rows(acs_r_ref, 0, rows, g)
        acs_rb = _group_rows(acs_r_ref, bwd, rows, g)
        del_rf = _group_rows(delta_r_ref, 0, rows, g)
        del_rb = _group_rows(delta_r_ref, bwd, rows, g)
        cb = lax.dot_general(cmat, bmat, (((1,), (1,)), ((), ())), preferred_element_type=F32)
        prev = carried[...]
        states_in = jnp.concatenate([prev.astype(BF16), back_ref[c]], axis=1)
        y_off = jnp.dot(cmat, states_in, preferred_element_type=F32)
        e_f = jnp.exp(per_head_lanes(acs_f))
        e_b = jnp.exp(per_head_lanes(acs_b))
        rest = e_f * y_off[:, 0:SSD_HPG * hd] + e_b * y_off[:, SSD_HPG * hd:] + d_wide * xs
        for pair in range(SSD_HPG // 2):
            lanes = slice(2 * pair * hd, 2 * (pair + 1) * hd)
            x_pair = xs[:, lanes]
            y_pair = rest[:, lanes]
            for half in range(2):
                r = 2 * pair + half
                seg_f = acs_f[r] - acs_rf[r:r + 1, :]
                seg_b = acs_b[r] - acs_rb[r:r + 1, :]
                w = (jnp.exp(jnp.where(lower, seg_f, ATT_NEG)) * del_rf[r:r + 1, :]
                     + jnp.exp(jnp.where(upper, seg_b, ATT_NEG)) * del_rb[r:r + 1, :])
                x_own = jnp.where(first if half == 0 else ~first, x_pair, 0.0).astype(BF16)
                y_pair = y_pair + jnp.dot((cb * w).astype(BF16), x_own, preferred_element_type=F32)
            y_ref[rows, lanes] = y_pair.astype(y_ref.dtype)
        xw = xs * _expand_heads(wend, selector)
        states = lax.dot_general(bmat, xw.astype(BF16), (((0,), (0,)), ((), ())), preferred_element_type=F32)
        total = acs[q - 1:q]
        carried[...] = jnp.exp(_per_head([total[:, r:r + 1] for r in range(SSD_HPG)])) * prev + states
        return carry

    lax.fori_loop(0, nchunk, chunk, 0, unroll=8)


def _ssd_scan(xs_act, bc_act, acs_c, wend_c, acs_r, delta_r, d_skip):
    bsz, seqlen, _ = xs_act.shape
    sb = min(SSD_SEQ_BLOCK, seqlen)
    nblk = seqlen // sb
    cpb = sb // SSD_CHUNK
    n_x = SSD_HPG * SSD_HEAD_DIM
    b_off = 0
    c_off = SSD_GROUPS
    lanes = acs_c.shape[1]
    nrow = acs_r.shape[0]

    def specs(blk):
        xs = pl.BlockSpec((None, sb, n_x), lambda b_, g, i: (b_, blk(i), g))
        bm = pl.BlockSpec((None, sb, SSD_STATE), lambda b_, g, i: (b_, blk(i), b_off + g))
        cm = pl.BlockSpec((None, sb, SSD_STATE), lambda b_, g, i: (b_, blk(i), c_off + g))
        colf = pl.BlockSpec((sb, lanes), lambda b_, g, i: (b_ * nblk + blk(i), 0))
        rowf = pl.BlockSpec((nrow, sb), lambda b_, g, i: (0, b_ * nblk + blk(i)))
        back = pl.BlockSpec((None, None, cpb, SSD_STATE, n_x), lambda b_, g, i: (b_, g, blk(i), 0, 0))
        return xs, bm, cm, colf, rowf, back

    xs, bm, cm, colf, rowf, back = specs(lambda i: nblk - 1 - i)
    assert SSD_CHUNK == 2 * SSD_HEAD_DIM and SSD_HPG == 4
    state = [pltpu.VMEM((SSD_STATE, n_x), F32), pltpu.VMEM((sb, n_x), BF16)]
    back_states = pl.pallas_call(
        _ssd_back_kernel,
        grid=(bsz, SSD_GROUPS, nblk),
        in_specs=[xs, bm, colf, colf],
        out_specs=back,
        out_shape=jax.ShapeDtypeStruct((bsz, SSD_GROUPS, seqlen // SSD_CHUNK, SSD_STATE, n_x), BF16),
        scratch_shapes=state,
        compiler_params=_params("parallel", "parallel", "arbitrary"),
        name="ssd_back",
    )(xs_act, bc_act, acs_c, wend_c)

    xs, bm, cm, colf, rowf, back = specs(lambda i: i)
    return pl.pallas_call(
        _ssd_main_kernel,
        grid=(bsz, SSD_GROUPS, nblk),
        in_specs=[xs, bm, cm, colf, colf, rowf, rowf, back,
                  pl.BlockSpec((None, 1, SSD_HPG), lambda b_, g, i: (g, 0, 0))],
        out_specs=pl.BlockSpec((None, sb, n_x), lambda b_, g, i: (b_, i, g)),
        out_shape=jax.ShapeDtypeStruct((bsz, seqlen, SSD_INNER), BF16),
        scratch_shapes=state[:1],
        compiler_params=_params("parallel", "parallel", "arbitrary"),
        name="ssd_main",
    )(xs_act, bc_act, bc_act, acs_c, wend_c, acs_r, delta_r, back_states,
      d_skip.reshape(SSD_GROUPS, 1, SSD_HPG))


def _attn_kernel(q_ref, ko_ref, kp_ref, kn_ref, vo_ref, vp_ref, vn_ref, o_ref, lse_ref, kcat, vcat, bias,
                 *, dil, group):
    tq = q_ref.shape[0]
    rad = ATT_RADIUS
    qs = ATT_QSUB
    kw = qs + 2 * rad
    nsub = tq // qs
    i = pl.program_id(1)
    kcat[0:rad, :] = kp_ref[...].astype(BF16)
    kcat[rad:rad + tq, :] = ko_ref[...].astype(BF16)
    kcat[rad + tq:2 * rad + tq, :] = kn_ref[...].astype(BF16)
    vcat[0:rad, :] = vp_ref[...].astype(BF16)
    vcat[rad:rad + tq, :] = vo_ref[...].astype(BF16)
    vcat[rad + tq:2 * rad + tq, :] = vn_ref[...].astype(BF16)

    row = lax.broadcasted_iota(jnp.int32, (qs, kw), 0)
    col = lax.broadcasted_iota(jnp.int32, (qs, kw), 1)
    dist = jnp.abs(row - col + rad)
    for head in range(ATT_HPG):
        slope = 2.0 ** (-8.0 * (head * ATT_GROUPS + group + 1) / ATT_HEADS)
        bias[head] = jnp.where(dist <= rad, (-slope * dil) * dist.astype(F32), ATT_NEG)
    before_start = (i == 0) & (col < rad)
    past_end = (i == pl.num_programs(1) - 1) & (col >= qs + rad)
    lane = lax.broadcasted_iota(jnp.int32, (qs, 2 * ATT_HEAD_DIM), 1)
    first_half = lane < ATT_HEAD_DIM
    scale = ATT_HEAD_DIM ** -0.5

    for m in range(nsub):
        for pair in range(ATT_HPG // 2):
            lanes = slice(pair * 2 * ATT_HEAD_DIM, (pair + 1) * 2 * ATT_HEAD_DIM)
            q2 = q_ref[m * qs:(m + 1) * qs, lanes]
            k2 = kcat[m * qs:m * qs + kw, lanes]
            v2 = vcat[m * qs:m * qs + kw, lanes]
            for half in range(2):
                head = 2 * pair + half
                qm = jnp.where(first_half if half == 0 else ~first_half, q2, 0.0).astype(BF16)
                s = lax.dot_general(qm, k2, (((1,), (1,)), ((), ())), preferred_element_type=F32)
                s = s * scale + bias[head]
                if m == 0:
                    s = jnp.where(before_start, ATT_NEG, s)
                if m == nsub - 1:
                    s = jnp.where(past_end, ATT_NEG, s)
                mx = jnp.max(s, axis=-1, keepdims=True)
                p = jnp.exp(s - mx)
                zsum = jnp.sum(p, axis=-1, keepdims=True)
                pv = jnp.dot(p.astype(BF16), v2, preferred_element_type=F32)
                own = slice(half * ATT_HEAD_DIM, (half + 1) * ATT_HEAD_DIM)
                out_lanes = slice(lanes.start + own.start, lanes.start + own.stop)
                o_ref[m * qs:(m + 1) * qs, out_lanes] = (pv / zsum)[:, own]
                lse_ref[m * qs:(m + 1) * qs, out_lanes] = jnp.broadcast_to(mx + jnp.log(zsum), (qs, ATT_HEAD_DIM))


def _attention_group(qkv, group, dil):
    nseq, ls, _ = qkv.shape
    tq = min(512, ls)
    nblk = ls // tq
    halo_per_blk = tq // ATT_RADIUS
    n_halo = ls // ATT_RADIUS

    def own(off):
        return pl.BlockSpec((None, tq, ATT_OUT), lambda s, i: (s, i, off))

    def prev(off):
        return pl.BlockSpec((None, ATT_RADIUS, ATT_OUT), lambda s, i: (s, jnp.maximum(i * halo_per_blk - 1, 0), off))

    def nxt(off):
        return pl.BlockSpec((None, ATT_RADIUS, ATT_OUT),
                            lambda s, i: (s, jnp.minimum((i + 1) * halo_per_blk, n_halo - 1), off))

    out_spec = pl.BlockSpec((None, tq, ATT_OUT), lambda s, i: (s, i, 0))
    return pl.pallas_call(
        functools.partial(_attn_kernel, dil=dil, group=group),
        grid=(nseq, nblk),
        in_specs=[own(0), own(1), prev(1), nxt(1), own(2), prev(2), nxt(2)],
        out_specs=[out_spec, out_spec],
        out_shape=[jax.ShapeDtypeStruct((nseq, ls, ATT_OUT), F32)] * 2,
        scratch_shapes=[pltpu.VMEM((tq + 2 * ATT_RADIUS, ATT_OUT), BF16)] * 2
        + [pltpu.VMEM((ATT_HPG, ATT_QSUB, ATT_QSUB + 2 * ATT_RADIUS), F32)],
        compiler_params=_params("parallel", "parallel"),
        name=f"attn_g{group}",
    )(qkv, qkv, qkv, qkv, qkv, qkv, qkv)


def _s5_tables(a_re, a_im, log_step, b_re, b_im, c_re, c_im):
    hp = lax.Precision.HIGHEST
    tc = S5_TC
    ng, width = S5_GROUPS, S5_ROW
    step = jnp.exp(log_step)[..., None]
    ar, ai = a_re, a_im
    mag = jnp.exp(step * ar)
    abr, abi = mag * jnp.cos(step * ai), mag * jnp.sin(step * ai)
    den = ar * ar + ai * ai
    fr = ((abr - 1.0) * ar + abi * ai) / den
    fi = (abi * ar - (abr - 1.0) * ai) / den
    b_re_t, b_im_t = b_re.transpose(0, 2, 1), b_im.transpose(0, 2, 1)
    bbr = fr[:, :, None, :] * b_re_t[None] - fi[:, :, None, :] * b_im_t[None]
    bbi = fr[:, :, None, :] * b_im_t[None] + fi[:, :, None, :] * b_re_t[None]
    ks = jnp.arange(tc + 1, dtype=F32)
    arg_r, arg_i = step * ar, step * ai
    pk_r = jnp.exp(arg_r[:, :, None, :] * ks[:, None]) * jnp.cos(arg_i[:, :, None, :] * ks[:, None])
    pk_i = jnp.exp(arg_r[:, :, None, :] * ks[:, None]) * jnp.sin(arg_i[:, :, None, :] * ks[:, None])
    pn_r = jnp.exp(arg_r[..., None] * ks) * jnp.cos(arg_i[..., None] * ks)
    pn_i = jnp.exp(arg_r[..., None] * ks) * jnp.sin(arg_i[..., None] * ks)
    c_re_t, c_im_t = c_re.transpose(0, 1, 3, 2), c_im.transpose(0, 1, 3, 2)
    cap_r = (c_re_t[:, :, :, None, :] * pn_r[..., None] - c_im_t[:, :, :, None, :] * pn_i[..., None])
    cap_i = (c_re_t[:, :, :, None, :] * pn_i[..., None] + c_im_t[:, :, :, None, :] * pn_r[..., None])
    cap_r = cap_r.reshape(2, ng, S5_STATE, (tc + 1) * S5_CH)
    cap_i = cap_i.reshape(2, ng, S5_STATE, (tc + 1) * S5_CH)
    taps = (jnp.einsum('dgin,dgnm->dgim', bbr, cap_r, precision=hp)
            - jnp.einsum('dgin,dgnm->dgim', bbi, cap_i, precision=hp))[..., :width]
    fwd_src = jnp.pad(taps[0], ((0, 0), (0, 0), (width, 0)))
    bwd_rev = taps[1].reshape(ng, S5_CH, tc, S5_CH)[:, :, ::-1, :].reshape(ng, S5_CH, width)
    bwd_src = jnp.pad(bwd_rev, ((0, 0), (0, 0), (0, width)))
    toep = jnp.stack([fwd_src[:, :, width - S5_CH * s:2 * width - S5_CH * s]
                      + bwd_src[:, :, S5_CH * (tc - 1 - s):S5_CH * (tc - 1 - s) + width] for s in range(tc)], axis=1)
    toep = toep.reshape(ng, width, width)

    def end_state(d, pw_r, pw_i):
        re = pw_r[:, :, None, :] * bbr[d][:, None] - pw_i[:, :, None, :] * bbi[d][:, None]
        im = pw_r[:, :, None, :] * bbi[d][:, None] + pw_i[:, :, None, :] * bbr[d][:, None]
        return [re.reshape(ng, width, S5_STATE), im.reshape(ng, width, S5_STATE)]

    b_end = jnp.concatenate(end_state(0, pk_r[0][:, tc - 1::-1], pk_i[0][:, tc - 1::-1])
                            + end_state(1, pk_r[1][:, :tc], pk_i[1][:, :tc]), axis=-1)

    def lag_reversed(a):
        return a.reshape(ng, S5_STATE, tc, S5_CH)[:, :, ::-1, :].reshape(ng, S5_STATE, width)

    c_out = jnp.concatenate([cap_r[0][..., S5_CH:], -cap_i[0][..., S5_CH:],
                             lag_reversed(cap_r[1][..., S5_CH:]), -lag_reversed(cap_i[1][..., S5_CH:])], axis=1)
    ptr, pti = pk_r[:, :, tc], pk_i[:, :, tc]
    p_mul = jnp.concatenate([ptr[0], ptr[0], ptr[1], ptr[1]], axis=-1)
    q_mul = jnp.concatenate([-pti[0], pti[0], -pti[1], pti[1]], axis=-1)
    return toep, b_end, c_out, p_mul, q_mul


def _granule_transpose(regs, granule):
    n = len(regs)
    d = n // 2
    while d >= 1:
        low = (granule // d) % 2 == 0
        nxt = list(regs)
        for i in range(n):
            if (i // d) % 2 == 0:
                a, b = regs[i], regs[i + d]
                nxt[i] = jnp.where(low, a, pltpu.roll(b, d * S5_CH, 1))
                nxt[i + d] = jnp.where(low, pltpu.roll(a, 128 - d * S5_CH, 1), b)
        regs = nxt
        d //= 2
    return regs


def _s5_rows_to_groups(u_ref, x_scr):
    rr = S5_RELAYOUT_ROWS
    per = S5_LANE_GROUPS
    granule = lax.broadcasted_iota(jnp.int32, (rr, 128), 1) // S5_CH

    def body(rb, carry):
        base = pl.multiple_of(rb * (rr * S5_TC), rr * S5_TC)
        rows = pl.ds(pl.multiple_of(rb * rr, rr), rr)
        for h in range(S5_TC // per):
            steps = [u_ref[pl.ds(base + per * h + t, rr, stride=S5_TC), :] for t in range(per)]
            for gl, block in enumerate(_granule_transpose(steps, granule)):
                x_scr[gl, rows, 128 * h:128 * (h + 1)] = block
        return carry

    lax.fori_loop(0, x_scr.shape[1] // rr, body, 0)


def _s5_in_kernel(u_ref, toep_ref, bend_ref, y_ref, sf_ref, sb_ref, x_scr):
    half = 2 * S5_STATE
    _s5_rows_to_groups(u_ref, x_scr)
    for gl in range(S5_LANE_GROUPS):
        xb = x_scr[gl].astype(BF16)
        y_ref[gl] = jnp.dot(xb, toep_ref[gl], preferred_element_type=F32)
        s = jnp.dot(xb, bend_ref[gl], preferred_element_type=F32)
        sf_ref[gl] = s[:, 0:half]
        sb_ref[gl] = s[:, half:2 * half]


def _s5_scan_kernel(sf_ref, sb_ref, p_ref, q_ref, hf_ref, hb_ref, sf_sw_ref, sb_sw_ref, *, rows_per_seq):
    n_chain = p_ref.shape[0]
    half = 2 * S5_STATE
    pf, qf = p_ref[:, 0:half], q_ref[:, 0:half]
    pb, qb = p_ref[:, half:2 * half], q_ref[:, half:2 * half]
    qf_sw = pltpu.roll(qf, S5_STATE, 1)
    qb_sw = pltpu.roll(qb, S5_STATE, 1)
    zero = jnp.zeros((n_chain, half), F32)
    piece = 256

    def swap(k, carry):
        rows = pl.ds(pl.multiple_of(k * piece, piece), piece)
        sf_sw_ref[rows, :] = pltpu.roll(sf_ref[rows, :], S5_STATE, 1)
        sb_sw_ref[rows, :] = pltpu.roll(sb_ref[rows, :], S5_STATE, 1)
        return carry

    lax.fori_loop(0, sf_ref.shape[0] // piece, swap, 0)

    tile = 8

    def steps(c8, carry):
        hf, hf_sw, hb, hb_sw = carry
        base_f = pl.multiple_of(c8 * tile, tile)
        base_b = pl.multiple_of(rows_per_seq - tile - c8 * tile, tile)
        for k in range(tile):
            rows_f = pl.ds(base_f + k, n_chain, stride=rows_per_seq)
            rows_b = pl.ds(base_b + (tile - 1 - k), n_chain, stride=rows_per_seq)
            hf_ref[rows_f, :] = hf
            hb_ref[rows_b, :] = hb
            sf = sf_ref[rows_f, :]
            sb = sb_ref[rows_b, :]
            sf_sw = sf_sw_ref[rows_f, :]
            sb_sw = sb_sw_ref[rows_b, :]
            hf, hf_sw = pf * hf + qf * hf_sw + sf, pf * hf_sw + qf_sw * hf + sf_sw
            hb, hb_sw = pb * hb + qb * hb_sw + sb, pb * hb_sw + qb_sw * hb + sb_sw
        return hf, hf_sw, hb, hb_sw

    lax.fori_loop(0, rows_per_seq // tile, steps, (zero, zero, zero, zero))


def _s5_out_kernel(hf_ref, hb_ref, cout_ref, y_ref, u_ref, d_ref, o_ref, y_scr):
    half = 2 * S5_STATE
    per = S5_LANE_GROUPS
    for gl in range(per):
        y = y_ref[gl] + jnp.dot(hf_ref[gl].astype(BF16), cout_ref[gl, 0:half, :], preferred_element_type=F32)
        y_scr[gl] = y + jnp.dot(hb_ref[gl].astype(BF16), cout_ref[gl, half:2 * half, :], preferred_element_type=F32)
    rr = S5_RELAYOUT_ROWS
    granule = lax.broadcasted_iota(jnp.int32, (rr, 128), 1) // S5_CH
    d_skip = d_ref[...]

    def body(rb, carry):
        rows = pl.ds(pl.multiple_of(rb * rr, rr), rr)
        base = pl.multiple_of(rb * (rr * S5_TC), rr * S5_TC)
        for h in range(S5_TC // per):
            groups = [y_scr[gl, rows, 128 * h:128 * (h + 1)] for gl in range(per)]
            for t, block in enumerate(_granule_transpose(groups, granule)):
                tok = pl.ds(base + per * h + t, rr, stride=S5_TC)
                o_ref[tok, :] = jax.nn.gelu(block + u_ref[tok, :] * d_skip)
        return carry

    lax.fori_loop(0, y_scr.shape[1] // rr, body, 0)


def _s5_all_tables(a_re, a_im, log_step, b_re, b_im, c_re, c_im):
    toep, b_end, c_out, p_mul, q_mul = jax.vmap(_s5_tables)(a_re, a_im, log_step, b_re, b_im, c_re, c_im)
    return toep.astype(BF16), b_end.astype(BF16), c_out.astype(BF16), p_mul, q_mul


def _s5_mix(u_tiles, seqlen, tables, layer, d_skip):
    ntile, ntok, lanes = u_tiles.shape
    bsz = ntok // seqlen
    rows_per_seq = seqlen // S5_TC
    rows = bsz * rows_per_seq
    rblk = min(S5_ROW_BLOCK, rows)
    per = S5_LANE_GROUPS
    half = 2 * S5_STATE
    toep, b_end, c_out, p_mul, q_mul = tables
    p_mul, q_mul = p_mul[layer], q_mul[layer]
    utile = pl.BlockSpec((None, rblk * S5_TC, lanes), lambda j, i: (j, i, 0))
    ytile = pl.BlockSpec((per, rblk, S5_ROW), lambda j, i: (j, i, 0))
    stile = pl.BlockSpec((per, rblk, half), lambda j, i: (j, i, 0))
    wtile = pl.BlockSpec((None, per, S5_ROW, S5_ROW), lambda j, i: (layer, j, 0, 0))
    state_shape = jax.ShapeDtypeStruct((S5_GROUPS, rows, half), F32)
    grid = (ntile, rows // rblk)
    y_in, s_f, s_b = pl.pallas_call(
        _s5_in_kernel,
        grid=grid,
        in_specs=[utile, wtile, wtile],
        out_specs=[ytile, stile, stile],
        out_shape=[jax.ShapeDtypeStruct((S5_GROUPS, rows, S5_ROW), F32), state_shape, state_shape],
        scratch_shapes=[pltpu.VMEM((per, rblk, S5_ROW), F32)],
        compiler_params=_params("parallel", "parallel"),
        name="s5_in",
    )(u_tiles, toep, b_end)

    gs = S5_SCAN_GROUPS
    n_chain = gs * bsz
    chain_p = jnp.repeat(p_mul, bsz, axis=0)
    chain_q = jnp.repeat(q_mul, bsz, axis=0)
    flat = jax.ShapeDtypeStruct((S5_GROUPS * rows, half), F32)
    srows = pl.BlockSpec((gs * rows, half), lambda t: (t, 0))
    chain = pl.BlockSpec((n_chain, S5_ROW), lambda t: (t, 0))
    h_f, h_b = pl.pallas_call(
        functools.partial(_s5_scan_kernel, rows_per_seq=rows_per_seq),
        grid=(S5_GROUPS // gs,),
        in_specs=[srows, srows, chain, chain],
        out_specs=[srows, srows],
        out_shape=[flat, flat],
        scratch_shapes=[pltpu.VMEM((gs * rows, half), F32)] * 2,
        compiler_params=_params("parallel"),
        name="s5_scan",
    )(s_f.reshape(S5_GROUPS * rows, half), s_b.reshape(S5_GROUPS * rows, half), chain_p, chain_q)

    return pl.pallas_call(
        _s5_out_kernel,
        grid=grid,
        in_specs=[stile, stile, wtile, ytile, utile, pl.BlockSpec((None, 1, lanes), lambda j, i: (j, 0, 0))],
        out_specs=utile,
        out_shape=jax.ShapeDtypeStruct((ntile, ntok, lanes), F32),
        scratch_shapes=[pltpu.VMEM((per, rblk, S5_ROW), F32)],
        compiler_params=_params("parallel", "parallel"),
        name="s5_out",
    )(h_f.reshape(S5_GROUPS, rows, half), h_b.reshape(S5_GROUPS, rows, half), c_out, y_in, u_tiles,
      d_skip.reshape(ntile, 1, lanes))


def _ssd_proj_kernel(y_ref, z_ref, nw_ref, w_ref, gate_ref, bg_ref, o_ref):
    z = z_ref[...].astype(F32)
    y = y_ref[...].astype(F32) * (z * _sigmoid(z))
    y = y * lax.rsqrt(jnp.mean(y * y, axis=-1, keepdims=True) + LN_EPS) * nw_ref[...]
    p = jnp.dot(y.astype(BF16), w_ref[...], preferred_element_type=F32)
    o_ref[...] = (_sigmoid(gate_ref[...] + bg_ref[...]) * p).astype(o_ref.dtype)


def _ssd_project(y, z, norm_w, w_br, gates, b_gate, *, tm=512):
    m = y.shape[0]
    wide = pl.BlockSpec((tm, SSD_INNER), lambda i: (i, 0))
    return pl.pallas_call(
        _ssd_proj_kernel,
        grid=(m // tm,),
        in_specs=[wide, wide, pl.BlockSpec((1, SSD_INNER), lambda i: (0, 0)),
                  pl.BlockSpec((SSD_INNER, D_MODEL), lambda i: (0, 0)),
                  pl.BlockSpec((tm, D_MODEL), lambda i: (i, 0)),
                  pl.BlockSpec((1, D_MODEL), lambda i: (0, 0))],
        out_specs=pl.BlockSpec((tm, D_MODEL), lambda i: (i, 0)),
        out_shape=jax.ShapeDtypeStruct((m, D_MODEL), BF16),
        compiler_params=_params("parallel"),
        name="ssd_project",
    )(y, z, norm_w.reshape(1, -1), w_br, gates, b_gate[0:1])


def _s5_proj_kernel(h_ref, w1_ref, w2_ref, w_ref, gate_ref, bg_ref, o_ref):
    hb = jnp.concatenate([h_ref[t] for t in range(h_ref.shape[0])], axis=1).astype(BF16)
    a = jnp.dot(hb, w1_ref[...], preferred_element_type=F32)
    b = jnp.dot(hb, w2_ref[...], preferred_element_type=F32)
    p = jnp.dot((a * _sigmoid(b)).astype(BF16), w_ref[...], preferred_element_type=F32)
    o_ref[...] = (_sigmoid(gate_ref[...] + bg_ref[...]) * p).astype(o_ref.dtype)


def _s5_project(h_tiles, w1, w2, w_br, gates, b_gate, *, tm=512):
    ntile, m, lanes = h_tiles.shape
    sq = pl.BlockSpec((S5_WIDTH, S5_WIDTH), lambda i: (0, 0))
    return pl.pallas_call(
        _s5_proj_kernel,
        grid=(m // tm,),
        in_specs=[pl.BlockSpec((ntile, tm, lanes), lambda i: (0, i, 0)), sq, sq,
                  pl.BlockSpec((S5_WIDTH, D_MODEL), lambda i: (0, 0)),
                  pl.BlockSpec((tm, D_MODEL), lambda i: (i, 2)),
                  pl.BlockSpec((1, D_MODEL), lambda i: (0, 0))],
        out_specs=pl.BlockSpec((tm, D_MODEL), lambda i: (i, 0)),
        out_shape=jax.ShapeDtypeStruct((m, D_MODEL), BF16),
        compiler_params=_params("parallel"),
        name="s5_project",
    )(h_tiles, w1, w2, w_br, gates, b_gate[2:3])


def _layer_norm(v, g, b):
    mu = jnp.mean(v, axis=-1, keepdims=True)
    c = v - mu
    var = jnp.mean(c * c, axis=-1, keepdims=True)
    return c * lax.rsqrt(var + LN_EPS) * g + b


def _merge_kernel(o0_ref, o1_ref, o2_ref, l0_ref, l1_ref, l2_ref, w_att_ref, gate_ref, bg_ref, pssd_ref, ps5_ref,
                  w_out_ref, x_ref, g_ref, b_ref, o_ref):
    l0, l1, l2 = l0_ref[...], l1_ref[...], l2_ref[...]
    mx = jnp.maximum(jnp.maximum(l0, l1), l2)
    e0, e1, e2 = jnp.exp(l0 - mx), jnp.exp(l1 - mx), jnp.exp(l2 - mx)
    den = e0 + e1 + e2
    att = (e0 / den) * o0_ref[...] + (e1 / den) * o1_ref[...] + (e2 / den) * o2_ref[...]
    p_att = jnp.dot(att.astype(BF16), w_att_ref[...], preferred_element_type=F32)
    merged = pssd_ref[...] + _sigmoid(gate_ref[...] + bg_ref[...]) * p_att + ps5_ref[...]
    mix = jnp.dot(merged.astype(BF16), w_out_ref[...], preferred_element_type=F32)
    o_ref[...] = _layer_norm(ALPHA * x_ref[...] + mix, g_ref[...], b_ref[...])


def _merge(att_o, att_l, w_att, gates, b_gate, p_ssd, p_s5, w_out, x, ln_g, ln_b, *, tm=512):
    m = x.shape[0]
    att = pl.BlockSpec((tm, ATT_OUT), lambda i: (i, 0))
    row = pl.BlockSpec((tm, D_MODEL), lambda i: (i, 0))
    vec = pl.BlockSpec((1, D_MODEL), lambda i: (0, 0))
    return pl.pallas_call(
        _merge_kernel,
        grid=(m // tm,),
        in_specs=[att] * 6 + [pl.BlockSpec((ATT_OUT, D_MODEL), lambda i: (0, 0)),
                              pl.BlockSpec((tm, D_MODEL), lambda i: (i, 1)), vec, row, row,
                              pl.BlockSpec((D_MODEL, D_MODEL), lambda i: (0, 0)), row, vec, vec],
        out_specs=row,
        out_shape=jax.ShapeDtypeStruct((m, D_MODEL), F32),
        compiler_params=_params("parallel"),
        name="merge_ln1",
    )(*att_o, *att_l, w_att, gates, b_gate[1:2], p_ssd, p_s5, w_out, x, ln_g.reshape(1, -1), ln_b.reshape(1, -1))


def _router_kernel(x_ref, wt_ref, b_ref, idx_ref, gate_ref, rank_ref, count_ref, before_ref):
    tm = x_ref.shape[0]

    @pl.when(pl.program_id(0) == 0)
    def _():
        count_ref[...] = jnp.zeros_like(count_ref)
        earlier = lax.broadcasted_iota(jnp.int32, (tm, tm), 0) < lax.broadcasted_iota(jnp.int32, (tm, tm), 1)
        before_ref[...] = earlier.astype(before_ref.dtype)

    x = x_ref[...]
    wt = wt_ref[...]
    xh = x.astype(BF16)
    xl = (x - xh.astype(F32)).astype(BF16)
    wh = wt.astype(BF16)
    wl = (wt - wh.astype(F32)).astype(BF16)
    nt = (((1,), (1,)), ((), ()))
    logits = (lax.dot_general(wh, xh, nt, preferred_element_type=F32)
              + lax.dot_general(wl, xh, nt, preferred_element_type=F32)
              + lax.dot_general(wh, xl, nt, preferred_element_type=F32)) + b_ref[...]
    eid = lax.broadcasted_iota(jnp.int32, logits.shape, 0)
    vals, idxs = [], []
    for _ in range(MOE_TOP_K):
        mx = jnp.max(logits, axis=0, keepdims=True)
        sel = jnp.min(jnp.where(logits == mx, eid, MOE_EXPERTS), axis=0, keepdims=True)
        vals.append(mx)
        idxs.append(sel)
        logits = jnp.where(eid == sel, -jnp.inf, logits)
    es = [jnp.exp(v - vals[0]) for v in vals]
    den = es[0] + es[1] + es[2] + es[3]
    for k in range(MOE_TOP_K):
        idx_ref[k:k + 1, :] = idxs[k]
        gate_ref[k:k + 1, :] = es[k] / den
    chosen = [eid == idxs[k] for k in range(MOE_TOP_K)]
    stacked = jnp.concatenate([c.astype(before_ref.dtype) for c in chosen], axis=0)
    earlier = jnp.dot(stacked, before_ref[...], preferred_element_type=F32)
    seen = count_ref[...]
    for k in range(MOE_TOP_K):
        pos = earlier[k * MOE_EXPERTS:(k + 1) * MOE_EXPERTS] + seen
        rank_ref[k:k + 1, :] = jnp.sum(jnp.where(chosen[k], pos, 0.0), axis=0, keepdims=True).astype(jnp.int32)
        seen = seen + jnp.sum(chosen[k].astype(F32), axis=1, keepdims=True)
    count_ref[...] = seen


def _router(x, router_w, router_b, *, tm=1024):
    m = x.shape[0]
    out = pl.BlockSpec((MOE_TOP_K, tm), lambda i: (0, i))
    return pl.pallas_call(
        _router_kernel,
        grid=(m // tm,),
        in_specs=[pl.BlockSpec((tm, D_MODEL), lambda i: (i, 0)),
                  pl.BlockSpec((MOE_EXPERTS, D_MODEL), lambda i: (0, 0)),
                  pl.BlockSpec((MOE_EXPERTS, 1), lambda i: (0, 0))],
        out_specs=[out, out, out, pl.BlockSpec((MOE_EXPERTS, 1), lambda i: (0, 0))],
        out_shape=[jax.ShapeDtypeStruct((MOE_TOP_K, m), jnp.int32), jax.ShapeDtypeStruct((MOE_TOP_K, m), F32),
                   jax.ShapeDtypeStruct((MOE_TOP_K, m), jnp.int32), jax.ShapeDtypeStruct((MOE_EXPERTS, 1), F32)],
        scratch_shapes=[pltpu.VMEM((tm, tm), BF16)],
        compiler_params=_params("arbitrary"),
        name="router",
    )(x, router_w.T, router_b.reshape(-1, 1))


def _expert_kernel(be_ref, used_ref, x_ref, wgu_ref, bgu_ref, wdn_ref, bdn_ref, o_ref, wgu_lo, wdn_lo):
    i = pl.program_id(0)
    used = i < used_ref[0]

    @pl.when(used & ((i == 0) | (be_ref[i] != be_ref[jnp.maximum(i - 1, 0)])))
    def _():
        wgu_lo[...] = wgu_ref[...].astype(wgu_lo.dtype)
        wdn_lo[...] = wdn_ref[...].astype(wdn_lo.dtype)

    @pl.when(used)
    def _():
        hu = jnp.dot(x_ref[...].astype(BF16), wgu_lo[...], preferred_element_type=F32) + bgu_ref[...]
        x_glu = jnp.minimum(hu[:, :MOE_FF], SWIGLU_LIMIT)
        x_lin = jnp.clip(hu[:, MOE_FF:], -SWIGLU_LIMIT, SWIGLU_LIMIT)
        act = x_glu * _sigmoid(SWIGLU_ALPHA * x_glu) * (x_lin + 1.0)
        o_ref[...] = jnp.dot(act.astype(BF16), wdn_lo[...], preferred_element_type=F32) + bdn_ref[...]

    @pl.when(jnp.logical_not(used))
    def _():
        o_ref[...] = jnp.zeros_like(o_ref)


def _experts(xs, block_expert, n_used, w_gu, b_gu, w_dn, b_dn, layer):
    n_rows = xs.shape[0]
    n_blocks = n_rows // MOE_ROWS
    grid_spec = pltpu.PrefetchScalarGridSpec(
        num_scalar_prefetch=2,
        grid=(n_blocks,),
        in_specs=[
            pl.BlockSpec((MOE_ROWS, D_MODEL), lambda i, be, nu: (i, 0)),
            pl.BlockSpec((None, None, D_MODEL, 2 * MOE_FF), lambda i, be, nu: (layer, be[i], 0, 0)),
            pl.BlockSpec((None, 1, 2 * MOE_FF), lambda i, be, nu: (be[i], 0, 0)),
            pl.BlockSpec((None, None, MOE_FF, D_MODEL), lambda i, be, nu: (layer, be[i], 0, 0)),
            pl.BlockSpec((None, 1, D_MODEL), lambda i, be, nu: (be[i], 0, 0)),
        ],
        out_specs=pl.BlockSpec((MOE_ROWS, D_MODEL), lambda i, be, nu: (i, 0)),
        scratch_shapes=[pltpu.VMEM((D_MODEL, 2 * MOE_FF), BF16), pltpu.VMEM((MOE_FF, D_MODEL), BF16)],
    )
    return pl.pallas_call(
        _expert_kernel,
        grid_spec=grid_spec,
        out_shape=jax.ShapeDtypeStruct((n_rows, D_MODEL), F32),
        compiler_params=_params("arbitrary"),
        name="experts",
    )(block_expert, n_used, xs, w_gu, b_gu.reshape(MOE_EXPERTS, 1, -1), w_dn, b_dn.reshape(MOE_EXPERTS, 1, -1))


def _combine_kernel(x_ref, y0_ref, y1_ref, y2_ref, y3_ref, gate_ref, g_ref, b_ref, o_ref, ob_ref):
    ffn = gate_ref[:, 0:1] * y0_ref[...]
    for k, y_ref in enumerate((y1_ref, y2_ref, y3_ref), start=1):
        ffn = ffn + gate_ref[:, k:k + 1] * y_ref[...]
    out = _layer_norm(ALPHA * x_ref[...] + ffn, g_ref[...], b_ref[...])
    o_ref[...] = out
    ob_ref[...] = out.astype(ob_ref.dtype)


def _combine(x, y_rows, gate, ln_g, ln_b, *, tm=512):
    m = x.shape[0]
    row = pl.BlockSpec((tm, D_MODEL), lambda i: (i, 0))
    vec = pl.BlockSpec((1, D_MODEL), lambda i: (0, 0))
    return pl.pallas_call(
        _combine_kernel,
        grid=(m // tm,),
        in_specs=[row] + [pl.BlockSpec((tm, D_MODEL), functools.partial(lambda k, i: (k * (m // tm) + i, 0), k))
                          for k in range(MOE_TOP_K)] + [pl.BlockSpec((tm, MOE_TOP_K), lambda i: (i, 0)), vec, vec],
        out_specs=[row, row],
        out_shape=[jax.ShapeDtypeStruct((m, D_MODEL), F32), jax.ShapeDtypeStruct((m, D_MODEL), BF16)],
        compiler_params=_params("parallel"),
        name="combine_ln2",
    )(x, *([y_rows] * MOE_TOP_K), gate, ln_g.reshape(1, -1), ln_b.reshape(1, -1))


def _moe(x, router_w, router_b, w_gu, b_gu, w_dn, b_dn, ln_g, ln_b, layer):
    ntok = x.shape[0]
    idx_t, gate_t, rank_t, counts = _router(x, router_w, router_b)
    gate = gate_t.T
    n_assign = ntok * MOE_TOP_K
    counts = counts.reshape(-1).astype(jnp.int32)
    padded = (counts + MOE_ROWS - 1) // MOE_ROWS * MOE_ROWS
    pend = jnp.cumsum(padded)
    pstart = pend - padded
    experts = jnp.arange(MOE_EXPERTS, dtype=jnp.int32)

    def lookup(table, idx):
        return jnp.sum(jnp.where(idx[..., None] == experts, table, 0), axis=-1)

    dest_t = lookup(pstart, idx_t) + rank_t
    n_blocks = n_assign // MOE_ROWS + MOE_EXPERTS
    n_pad = MOE_EXPERTS * MOE_ROWS
    block_start = jnp.arange(n_blocks, dtype=jnp.int32) * MOE_ROWS
    block_expert = jnp.sum((pend[None, :] <= block_start[:, None]).astype(jnp.int32), axis=1)
    block_expert = jnp.minimum(block_expert, MOE_EXPERTS - 1)
    n_used = (pend[-1:] // MOE_ROWS).astype(jnp.int32)
    gap = padded - counts
    gap_end = jnp.cumsum(gap)
    pad_id = jnp.arange(n_pad, dtype=jnp.int32)
    pad_expert = jnp.sum((gap_end[None, :] <= pad_id[:, None]).astype(jnp.int32), axis=1)
    tail_of = lookup(pstart + counts - (gap_end - gap), jnp.minimum(pad_expert, MOE_EXPERTS - 1)) + pad_id
    pad_slot = jnp.where(pad_expert < MOE_EXPERTS, tail_of, pend[-1] + pad_id - gap_end[-1])
    keys = jnp.concatenate([dest_t.reshape(-1), pad_slot])
    toks = jnp.concatenate([jnp.tile(jnp.arange(ntok, dtype=jnp.int32), MOE_TOP_K), jnp.zeros((n_pad,), jnp.int32)])
    _, slot_tok = lax.sort((keys, toks), num_keys=1, is_stable=False)
    xs = x.at[slot_tok].get(mode="promise_in_bounds")
    ys = _experts(xs, block_expert, n_used, w_gu, b_gu, w_dn, b_dn, layer)
    y_rows = ys.at[dest_t.reshape(-1)].get(mode="promise_in_bounds", unique_indices=True)
    return _combine(x, y_rows, gate, ln_g, ln_b)


def _layer(x, xb, w_in, b_gate, ssd_conv_w, ssd_conv_b, ssd_a_log, ssd_dt_bias, ssd_d, ssd_norm_w,
           s5_d, s5_glu_w1, s5_glu_w2, w_br_ssd, w_br_attn, w_br_s5, w_out, ln1_g, ln1_b,
           router_w, router_b, b_gu, b_dn, ln2_g, ln2_b, *, layer, w_gu, w_dn, s5_tables):
    bsz, seqlen, d = x.shape
    ntok = bsz * seqlen
    xf = x.reshape(ntok, d)
    offs = [0]
    for s in IN_SIZES:
        offs.append(offs[-1] + s)

    def seg(k):
        return w_in[:, offs[k]:offs[k + 1]].astype(BF16)

    gates = _matmul(xb, seg(0), tm=1024, tn=1024, name="proj_gates", out_dtype=BF16)
    z = _matmul(xb, seg(1), tm=1024, tn=1024, name="proj_z", out_dtype=BF16)
    u = _matmul_tiles(xb, seg(5), tm=512, name="proj_u")

    w_xbc = seg(2)
    xs_act = _xbc_conv(xb, w_xbc[:, :SSD_INNER], ssd_conv_w[:, :SSD_INNER], ssd_conv_b[:SSD_INNER], seqlen)
    bc_act = _xbc_conv(xb, w_xbc[:, SSD_INNER:], ssd_conv_w[:, SSD_INNER:], ssd_conv_b[SSD_INNER:], seqlen,
                       out_dtype=BF16)
    xs_act = xs_act.reshape(bsz, seqlen, SSD_INNER)
    bc_act = bc_act.reshape(bsz, seqlen, SSD_CONV_CH - SSD_INNER)
    decay = _ssd_decay(xb, seg(3), ssd_a_log, ssd_dt_bias)
    y_ssd = _ssd_scan(xs_act, bc_act, *decay, ssd_d)
    p_ssd = _ssd_project(y_ssd.reshape(ntok, SSD_INNER), z, ssd_norm_w, w_br_ssd.astype(BF16), gates, b_gate)

    w_qkv = w_in[:, offs[4]:offs[5]].reshape(d, 3, ATT_GROUPS, ATT_OUT)
    att_o, att_l = [], []
    for g, (window, dil) in enumerate(ATT_PATTERNS):
        assert (window // 2) // dil == ATT_RADIUS
        ls = seqlen // dil
        x_cls = xb.reshape(bsz, ls, dil, d).transpose(0, 2, 1, 3).reshape(ntok, d)
        qkv_g = _matmul(x_cls, w_qkv[:, :, g].reshape(d, 3 * ATT_OUT).astype(BF16), tm=1024, tn=3 * ATT_OUT,
                        name=f"proj_qkv{g}", out_dtype=BF16)
        o, lse = _attention_group(qkv_g.reshape(bsz * dil, ls, 3 * ATT_OUT), g, dil)
        att_o.append(o.reshape(bsz, dil, ls, ATT_OUT).transpose(0, 2, 1, 3).reshape(ntok, ATT_OUT))
        att_l.append(lse.reshape(bsz, dil, ls, ATT_OUT).transpose(0, 2, 1, 3).reshape(ntok, ATT_OUT))

    h_s5 = _s5_mix(u, seqlen, s5_tables, layer, s5_d)
    p_s5 = _s5_project(h_s5, s5_glu_w1.astype(BF16), s5_glu_w2.astype(BF16), w_br_s5.astype(BF16), gates, b_gate)

    x1 = _merge(att_o, att_l, w_br_attn.astype(BF16), gates, b_gate, p_ssd, p_s5, w_out.astype(BF16),
                xf, ln1_g, ln1_b)
    x2, x2b = _moe(x1, router_w, router_b, w_gu, b_gu, w_dn, b_dn, ln2_g, ln2_b, layer)
    return x2.reshape(bsz, seqlen, d), x2b


def kernel(x, w_in, b_gate, ssd_conv_w, ssd_conv_b, ssd_a_log, ssd_dt_bias, ssd_d, ssd_norm_w, s5_a_re, s5_a_im, s5_log_step, s5_b_re, s5_b_im, s5_c_re, s5_c_im, s5_d, s5_glu_w1, s5_glu_w2, w_br_ssd, w_br_attn, w_br_s5, w_out, ln1_g, ln1_b, router_w, router_b, exp_w_gate_up, exp_b_gate_up, exp_w_down, exp_b_down, ln2_g, ln2_b):
    per_layer = (w_in, b_gate, ssd_conv_w, ssd_conv_b, ssd_a_log, ssd_dt_bias, ssd_d, ssd_norm_w, s5_d, s5_glu_w1,
                 s5_glu_w2, w_br_ssd, w_br_attn, w_br_s5, w_out, ln1_g, ln1_b, router_w, router_b, exp_b_gate_up,
                 exp_b_down, ln2_g, ln2_b)
    s5_tables = _s5_all_tables(s5_a_re, s5_a_im, s5_log_step, s5_b_re, s5_b_im, s5_c_re, s5_c_im)
    xb = x.reshape(-1, x.shape[-1]).astype(BF16)
    for layer in range(DEPTH):
        x, xb = _layer(x, xb, *[p[layer] for p in per_layer], layer=layer, w_gu=exp_w_gate_up, w_dn=exp_w_down,
                       s5_tables=s5_tables)
    return x
```

```python
import functools
import math

import jax
import jax.numpy as jnp
from jax import lax
from jax.experimental import pallas as pl
from jax.experimental.pallas import tpu as pltpu

F32 = jnp.float32
BF16 = jnp.bfloat16

D_MODEL = 1024
DEPTH = 4
ALPHA = (2 * DEPTH) ** 0.25
LN_EPS = 1e-5

SSD_INNER = 2 * D_MODEL
SSD_HEAD_DIM = 64
SSD_GROUPS = 8
SSD_HPG = 4
SSD_HEADS = SSD_GROUPS * SSD_HPG
SSD_STATE = 128
SSD_CONV = 5
SSD_CONV_CH = SSD_INNER + 2 * SSD_GROUPS * SSD_STATE
SSD_CHUNK = 128
SSD_SEQ_BLOCK = 1024
CONV_ROW_PARTS = 1
CONV_HALO_ROWS = 16

ATT_HEAD_DIM = 64
ATT_PATTERNS = ((128, 1), (512, 4), (2048, 16))
ATT_GROUPS = 3
ATT_HPG = 6
ATT_HEADS = ATT_GROUPS * ATT_HPG
ATT_WIDTH = ATT_HEADS * ATT_HEAD_DIM
ATT_OUT = ATT_HPG * ATT_HEAD_DIM
ATT_NEG = -1e30
ATT_RADIUS = 64
ATT_QSUB = 128

S5_CH = 16
S5_STATE = 64
S5_WIDTH = 9 * D_MODEL // 8
S5_GROUPS = S5_WIDTH // S5_CH
S5_TC = 16
S5_ROW = S5_TC * S5_CH
S5_SCAN_GROUPS = 8
S5_LANE_GROUPS = 128 // S5_CH
S5_TILES = S5_GROUPS // S5_LANE_GROUPS
S5_ROW_BLOCK = 512
S5_RELAYOUT_ROWS = 16

N_BRANCHES = 3
IN_SIZES = (N_BRANCHES * D_MODEL, SSD_INNER, SSD_CONV_CH, 2 * SSD_HEADS, 3 * ATT_WIDTH, S5_WIDTH)

MOE_EXPERTS = 32
MOE_TOP_K = 4
MOE_FF = D_MODEL
SWIGLU_LIMIT = 7.0
SWIGLU_ALPHA = 1.702
MOE_ROWS = 512

VMEM_LIMIT = 56 * 1024 * 1024


def _sigmoid(x):
    return 0.5 * jnp.tanh(0.5 * x) + 0.5


def _params(*sem):
    return pltpu.CompilerParams(dimension_semantics=sem, vmem_limit_bytes=VMEM_LIMIT)


def _mm_kernel(x_ref, w_ref, o_ref):
    o_ref[...] = jnp.dot(x_ref[...].astype(BF16), w_ref[...], preferred_element_type=F32).astype(o_ref.dtype)


def _matmul(x, w, *, tm, tn, name, out_dtype=F32):
    m, k = x.shape
    n = w.shape[1]
    return pl.pallas_call(
        _mm_kernel,
        grid=(n // tn, m // tm),
        in_specs=[pl.BlockSpec((tm, k), lambda j, i: (i, 0)), pl.BlockSpec((k, tn), lambda j, i: (0, j))],
        out_specs=pl.BlockSpec((tm, tn), lambda j, i: (i, j)),
        out_shape=jax.ShapeDtypeStruct((m, n), out_dtype),
        compiler_params=_params("parallel", "parallel"),
        name=name,
    )(x, w)


def _mm_tiles_kernel(x_ref, w_ref, o_ref):
    acc = jnp.dot(x_ref[...].astype(BF16), w_ref[...], preferred_element_type=F32)
    for t in range(o_ref.shape[0]):
        o_ref[t] = acc[:, 128 * t:128 * (t + 1)]


def _matmul_tiles(x, w, *, tm, name):
    m, k = x.shape
    n = w.shape[1]
    nt = n // 128
    return pl.pallas_call(
        _mm_tiles_kernel,
        grid=(m // tm,),
        in_specs=[pl.BlockSpec((tm, k), lambda i: (i, 0)), pl.BlockSpec((k, n), lambda i: (0, 0))],
        out_specs=pl.BlockSpec((nt, tm, 128), lambda i: (0, i, 0)),
        out_shape=jax.ShapeDtypeStruct((nt, m, 128), F32),
        compiler_params=_params("parallel"),
        name=name,
    )(x, w)


def _split3(a):
    a1 = a.astype(BF16)
    r1 = a - a1.astype(F32)
    a2 = r1.astype(BF16)
    a3 = (r1 - a2.astype(F32)).astype(BF16)
    return a1, a2, a3


def _decay_kernel(x_ref, w_ref, wt_ref, alog_r_ref, alog_c_ref, bias_r_ref, bias_c_ref,
                  acs_c_ref, wend_c_ref, acs_r_ref, delta_r_ref):
    q = SSD_CHUNK
    nh = SSD_HEADS
    xb = x_ref[...].astype(BF16)
    dt_c = jnp.dot(xb, w_ref[...], preferred_element_type=F32)
    dt_r = lax.dot_general(wt_ref[...], xb, (((1,), (1,)), ((), ())), preferred_element_type=F32)
    delta_c = jax.nn.softplus(dt_c + bias_r_ref[...])
    delta_r = jax.nn.softplus(dt_r + bias_c_ref[...])
    a_c = -jnp.exp(alog_r_ref[...]) * delta_c
    a_r = -jnp.exp(alog_c_ref[...]) * delta_r
    delta_r_ref[...] = delta_r
    row = lax.broadcasted_iota(jnp.int32, (q, q), 0)
    col = lax.broadcasted_iota(jnp.int32, (q, q), 1)
    lower = (col <= row).astype(BF16)
    upper = (col >= row).astype(BF16)
    fwd_lane = lax.broadcasted_iota(jnp.int32, (q, dt_c.shape[1]), 1) < nh
    fwd_row = lax.broadcasted_iota(jnp.int32, (2 * nh, q), 0) < nh
    for c in range(x_ref.shape[0] // q):
        rows = slice(c * q, (c + 1) * q)
        pieces = _split3(a_c[rows])
        prefix = sum(jnp.dot(lower, p, preferred_element_type=F32) for p in pieces)
        suffix = sum(jnp.dot(upper, p, preferred_element_type=F32) for p in pieces)
        acs = jnp.where(fwd_lane, prefix, suffix)
        total = jnp.where(fwd_lane[0:1], prefix[q - 1:q], suffix[0:1])
        acs_c_ref[rows, :] = acs
        wend_c_ref[rows, :] = delta_c[rows] * jnp.exp(total - acs)
        pieces = _split3(a_r[:, rows])
        prefix = sum(jnp.dot(p, upper, preferred_element_type=F32) for p in pieces)
        suffix = sum(jnp.dot(p, lower, preferred_element_type=F32) for p in pieces)
        acs_r_ref[:, rows] = jnp.where(fwd_row, prefix, suffix)


def _ssd_decay(x, w_dt, a_log, dt_bias, *, tm=1024):
    m, k = x.shape
    n = w_dt.shape[1]
    lanes = 128
    pad = lanes - n
    w_pad = jnp.pad(w_dt, ((0, 0), (0, pad)))
    alog = a_log.reshape(-1)
    bias = dt_bias.reshape(-1)
    col = pl.BlockSpec((tm, lanes), lambda i: (i, 0))
    rowf = pl.BlockSpec((n, tm), lambda i: (0, i))
    return pl.pallas_call(
        _decay_kernel,
        grid=(m // tm,),
        in_specs=[pl.BlockSpec((tm, k), lambda i: (i, 0)), pl.BlockSpec((k, lanes), lambda i: (0, 0)),
                  pl.BlockSpec((n, k), lambda i: (0, 0)),
                  pl.BlockSpec((1, lanes), lambda i: (0, 0)), pl.BlockSpec((n, 1), lambda i: (0, 0)),
                  pl.BlockSpec((1, lanes), lambda i: (0, 0)), pl.BlockSpec((n, 1), lambda i: (0, 0))],
        out_specs=[col, col, rowf, rowf],
        out_shape=[jax.ShapeDtypeStruct((m, lanes), F32)] * 2 + [jax.ShapeDtypeStruct((n, m), F32)] * 2,
        compiler_params=_params("parallel"),
        name="ssd_decay",
    )(x, w_pad, w_dt.T, jnp.pad(alog, (0, pad)).reshape(1, lanes), alog.reshape(n, 1),
      jnp.pad(bias, (0, pad)).reshape(1, lanes), bias.reshape(n, 1))


def _xbc_conv_kernel(xm_ref, xp_ref, xn_ref, w_ref, cw_ref, cb_ref, o_ref, buf, *, tiles_per_seq):
    pos = pl.program_id(1) % tiles_per_seq
    ts = xm_ref.shape[0]
    halo = xp_ref.shape[0]
    half = SSD_CONV // 2
    before = jnp.where(pos == 0, jnp.zeros_like(xp_ref[...]), xp_ref[...])
    after = jnp.where(pos == tiles_per_seq - 1, jnp.zeros_like(xn_ref[...]), xn_ref[...])
    rows = jnp.concatenate([before, xm_ref[...], after], axis=0).astype(BF16)
    nparts = CONV_ROW_PARTS
    part = ts // nparts
    edges = [0] + [halo + part * (p + 1) for p in range(nparts - 1)] + [ts + 2 * halo]

    def project(p):
        buf[edges[p]:edges[p + 1], :] = jnp.dot(rows[edges[p]:edges[p + 1]], w_ref[...], preferred_element_type=F32)

    def convolve(p):
        lo = halo - half + part * p
        acc = cb_ref[...] + cw_ref[0:1, :] * buf[lo:lo + part, :]
        for k in range(1, SSD_CONV):
            acc = acc + cw_ref[k:k + 1, :] * buf[lo + k:lo + k + part, :]
        o_ref[part * p:part * (p + 1), :] = (acc * _sigmoid(acc)).astype(o_ref.dtype)

    project(0)
    for p in range(nparts):
        if p + 1 < nparts:
            project(p + 1)
        convolve(p)


def _xbc_conv(x, w, conv_w, conv_b, seqlen, *, tm=512, tn=1024, out_dtype=F32):
    m, k = x.shape
    n = w.shape[1]
    halo = CONV_HALO_ROWS
    nhalo = m // halo
    return pl.pallas_call(
        functools.partial(_xbc_conv_kernel, tiles_per_seq=seqlen // tm),
        grid=(n // tn, m // tm),
        in_specs=[
            pl.BlockSpec((tm, k), lambda j, i: (i, 0)),
            pl.BlockSpec((halo, k), lambda j, i: (jnp.maximum(i * (tm // halo) - 1, 0), 0)),
            pl.BlockSpec((halo, k), lambda j, i: (jnp.minimum((i + 1) * (tm // halo), nhalo - 1), 0)),
            pl.BlockSpec((k, tn), lambda j, i: (0, j)),
            pl.BlockSpec((SSD_CONV, tn), lambda j, i: (0, j)),
            pl.BlockSpec((1, tn), lambda j, i: (0, j)),
        ],
        out_specs=pl.BlockSpec((tm, tn), lambda j, i: (i, j)),
        out_shape=jax.ShapeDtypeStruct((m, n), out_dtype),
        scratch_shapes=[pltpu.VMEM((tm + 2 * halo, tn), F32)],
        compiler_params=_params("parallel", "parallel"),
        name="proj_xbc_conv",
    )(x, x, x, w, conv_w, conv_b.reshape(1, n))


def _per_head(cols):
    rows = cols[0].shape[0]
    first = lax.broadcasted_iota(jnp.int32, (rows, 2 * SSD_HEAD_DIM), 1) < SSD_HEAD_DIM
    return jnp.concatenate([jnp.where(first, cols[0], cols[1]), jnp.where(first, cols[2], cols[3])], axis=1)


def _group_lanes(ref, rows, g):
    lanes = ref.shape[1]
    return pltpu.roll(ref[rows, :], (lanes - SSD_HPG * g) % lanes, 1)


def _group_rows(ref, first, cols, g):
    tile = 8
    per_tile = tile // SSD_HPG
    base = pl.multiple_of(first + (g // per_tile) * tile, tile)
    return pltpu.roll(ref[pl.ds(base, tile), cols], (tile - SSD_HPG * (g % per_tile)) % tile, 0)


def _head_selector(lanes, first):
    width = SSD_HPG * SSD_HEAD_DIM
    src = lax.broadcasted_iota(jnp.int32, (lanes, width), 0)
    dst = lax.broadcasted_iota(jnp.int32, (lanes, width), 1) // SSD_HEAD_DIM
    return (src == dst + first).astype(BF16)


def _expand_heads(cols, selector):
    return sum(jnp.dot(p, selector, preferred_element_type=F32) for p in _split3(cols))


def _weighted_inputs(xs_ref, wend_ref, g, first, xw_scr):
    rows = slice(0, xs_ref.shape[0])
    wend = _group_lanes(wend_ref, rows, g)
    xw_scr[...] = (xs_ref[...] * _expand_heads(wend, _head_selector(wend_ref.shape[1], first))).astype(xw_scr.dtype)


def _ssd_back_kernel(xs_ref, bm_ref, acs_ref, wend_ref, out_ref, carried, xw_scr):
    q = SSD_CHUNK
    nchunk = xs_ref.shape[0] // q
    g = pl.program_id(1)
    bwd = SSD_HEADS

    @pl.when(pl.program_id(2) == 0)
    def _():
        carried[...] = jnp.zeros_like(carried)

    _weighted_inputs(xs_ref, wend_ref, g, bwd, xw_scr)

    def chunk(ci, carry):
        c = nchunk - 1 - ci
        rows = pl.ds(pl.multiple_of(c * q, q), q)
        first_row = pl.ds(pl.multiple_of(c * q, q), 8)
        total = _group_lanes(acs_ref, first_row, g)[0:1]
        states = lax.dot_general(bm_ref[rows, :].astype(BF16), xw_scr[rows, :], (((0,), (0,)), ((), ())),
                                 preferred_element_type=F32)
        prev = carried[...]
        out_ref[c] = prev.astype(out_ref.dtype)
        carried[...] = jnp.exp(_per_head([total[:, bwd + r:bwd + r + 1] for r in range(SSD_HPG)])) * prev + states
        return carry

    lax.fori_loop(0, nchunk, chunk, 0, unroll=4)


def _ssd_main_kernel(xs_ref, bm_ref, cm_ref, acs_c_ref, wend_c_ref, acs_r_ref, delta_r_ref, back_ref,
                     dskip_ref, y_ref, carried):
    q = SSD_CHUNK
    nchunk = xs_ref.shape[0] // q
    g = pl.program_id(1)
    bwd = SSD_HEADS
    hd = SSD_HEAD_DIM

    @pl.when(pl.program_id(2) == 0)
    def _():
        carried[...] = jnp.zeros_like(carried)

    row = lax.broadcasted_iota(jnp.int32, (q, q), 0)
    col = lax.broadcasted_iota(jnp.int32, (q, q), 1)
    lower = col <= row
    upper = col >= row
    d_wide = _per_head([dskip_ref[:, r:r + 1] for r in range(SSD_HPG)])
    selector = _head_selector(wend_c_ref.shape[1], 0)
    first = lax.broadcasted_iota(jnp.int32, (q, 2 * hd), 1) < hd

    def per_head_lanes(blocks):
        return jnp.concatenate([jnp.where(first, blocks[0], blocks[1]), jnp.where(first, blocks[2], blocks[3])], axis=1)

    def chunk(c, carry):
        r0 = pl.multiple_of(c * q, q)
        rows = pl.ds(r0, q)
        xs = xs_ref[rows, :]
        bmat = bm_ref[rows, :].astype(BF16)
        cmat = cm_ref[rows, :].astype(BF16)
        acs = _group_lanes(acs_c_ref, rows, g)
        wend = _group_lanes(wend_c_ref, rows, g)
        acs_f = [jnp.broadcast_to(acs[:, r:r + 1], (q, q)) for r in range(SSD_HPG)]
        acs_b = [jnp.broadcast_to(acs[:, bwd + r:bwd + r + 1], (q, q)) for r in range(SSD_HPG)]
        acs_rf = _group_rows(acs_r_ref, 0, rows, g)
        acs_rb = _group_rows(acs_r_ref, bwd, rows, g)
        del_rf = _group_rows(delta_r_ref, 0, rows, g)
        del_rb = _group_rows(delta_r_ref, bwd, rows, g)
        cb = lax.dot_general(cmat, bmat, (((1,), (1,)), ((), ())), preferred_element_type=F32)
        prev = carried[...]
        states_in = jnp.concatenate([prev.astype(BF16), back_ref[c]], axis=1)
        y_off = jnp.dot(cmat, states_in, preferred_element_type=F32)
        e_f = jnp.exp(per_head_lanes(acs_f))
        e_b = jnp.exp(per_head_lanes(acs_b))
        rest = e_f * y_off[:, 0:SSD_HPG * hd] + e_b * y_off[:, SSD_HPG * hd:] + d_wide * xs
        for pair in range(SSD_HPG // 2):
            lanes = slice(2 * pair * hd, 2 * (pair + 1) * hd)
            x_pair = xs[:, lanes]
            y_pair = rest[:, lanes]
            for half in range(2):
                r = 2 * pair + half
                seg_f = acs_f[r] - acs_rf[r:r + 1, :]
                seg_b = acs_b[r] - acs_rb[r:r + 1, :]
                w = (jnp.exp(jnp.where(lower, seg_f, ATT_NEG)) * del_rf[r:r + 1, :]
                     + jnp.exp(jnp.where(upper, seg_b, ATT_NEG)) * del_rb[r:r + 1, :])
                x_own = jnp.where(first if half == 0 else ~first, x_pair, 0.0).astype(BF16)
                y_pair = y_pair + jnp.dot((cb * w).astype(BF16), x_own, preferred_element_type=F32)
            y_ref[rows, lanes] = y_pair.astype(y_ref.dtype)
        xw = xs * _expand_heads(wend, selector)
        states = lax.dot_general(bmat, xw.astype(BF16), (((0,), (0,)), ((), ())), preferred_element_type=F32)
        total = acs[q - 1:q]
        carried[...] = jnp.exp(_per_head([total[:, r:r + 1] for r in range(SSD_HPG)])) * prev + states
        return carry

    lax.fori_loop(0, nchunk, chunk, 0, unroll=8)


def _ssd_scan(xs_act, bc_act, acs_c, wend_c, acs_r, delta_r, d_skip):
    bsz, seqlen, _ = xs_act.shape
    sb = min(SSD_SEQ_BLOCK, seqlen)
    nblk = seqlen // sb
    cpb = sb // SSD_CHUNK
    n_x = SSD_HPG * SSD_HEAD_DIM
    b_off = 0
    c_off = SSD_GROUPS
    lanes = acs_c.shape[1]
    nrow = acs_r.shape[0]

    def specs(blk):
        xs = pl.BlockSpec((None, sb, n_x), lambda b_, g, i: (b_, blk(i), g))
        bm = pl.BlockSpec((None, sb, SSD_STATE), lambda b_, g, i: (b_, blk(i), b_off + g))
        cm = pl.BlockSpec((None, sb, SSD_STATE), lambda b_, g, i: (b_, blk(i), c_off + g))
        colf = pl.BlockSpec((sb, lanes), lambda b_, g, i: (b_ * nblk + blk(i), 0))
        rowf = pl.BlockSpec((nrow, sb), lambda b_, g, i: (0, b_ * nblk + blk(i)))
        back = pl.BlockSpec((None, None, cpb, SSD_STATE, n_x), lambda b_, g, i: (b_, g, blk(i), 0, 0))
        return xs, bm, cm, colf, rowf, back

    xs, bm, cm, colf, rowf, back = specs(lambda i: nblk - 1 - i)
    assert SSD_CHUNK == 2 * SSD_HEAD_DIM and SSD_HPG == 4
    state = [pltpu.VMEM((SSD_STATE, n_x), F32), pltpu.VMEM((sb, n_x), BF16)]
    back_states = pl.pallas_call(
        _ssd_back_kernel,
        grid=(bsz, SSD_GROUPS, nblk),
        in_specs=[xs, bm, colf, colf],
        out_specs=back,
        out_shape=jax.ShapeDtypeStruct((bsz, SSD_GROUPS, seqlen // SSD_CHUNK, SSD_STATE, n_x), BF16),
        scratch_shapes=state,
        compiler_params=_params("parallel", "parallel", "arbitrary"),
        name="ssd_back",
    )(xs_act, bc_act, acs_c, wend_c)

    xs, bm, cm, colf, rowf, back = specs(lambda i: i)
    return pl.pallas_call(
        _ssd_main_kernel,
        grid=(bsz, SSD_GROUPS, nblk),
        in_specs=[xs, bm, cm, colf, colf, rowf, rowf, back,
                  pl.BlockSpec((None, 1, SSD_HPG), lambda b_, g, i: (g, 0, 0))],
        out_specs=pl.BlockSpec((None, sb, n_x), lambda b_, g, i: (b_, i, g)),
        out_shape=jax.ShapeDtypeStruct((bsz, seqlen, SSD_INNER), BF16),
        scratch_shapes=state[:1],
        compiler_params=_params("parallel", "parallel", "arbitrary"),
        name="ssd_main",
    )(xs_act, bc_act, bc_act, acs_c, wend_c, acs_r, delta_r, back_states,
      d_skip.reshape(SSD_GROUPS, 1, SSD_HPG))


def _attn_kernel(q_ref, ko_ref, kp_ref, kn_ref, vo_ref, vp_ref, vn_ref, o_ref, lse_ref, kcat, vcat, bias,
                 *, dil, group):
    tq = q_ref.shape[0]
    rad = ATT_RADIUS
    qs = ATT_QSUB
    kw = qs + 2 * rad
    nsub = tq // qs
    i = pl.program_id(1)
    kcat[0:rad, :] = kp_ref[...].astype(BF16)
    kcat[rad:rad + tq, :] = ko_ref[...].astype(BF16)
    kcat[rad + tq:2 * rad + tq, :] = kn_ref[...].astype(BF16)
    vcat[0:rad, :] = vp_ref[...].astype(BF16)
    vcat[rad:rad + tq, :] = vo_ref[...].astype(BF16)
    vcat[rad + tq:2 * rad + tq, :] = vn_ref[...].astype(BF16)

    row = lax.broadcasted_iota(jnp.int32, (qs, kw), 0)
    col = lax.broadcasted_iota(jnp.int32, (qs, kw), 1)
    dist = jnp.abs(row - col + rad)
    for head in range(ATT_HPG):
        slope = 2.0 ** (-8.0 * (head * ATT_GROUPS + group + 1) / ATT_HEADS)
        bias[head] = jnp.where(dist <= rad, (-slope * dil) * dist.astype(F32), ATT_NEG)
    before_start = (i == 0) & (col < rad)
    past_end = (i == pl.num_programs(1) - 1) & (col >= qs + rad)
    lane = lax.broadcasted_iota(jnp.int32, (qs, 2 * ATT_HEAD_DIM), 1)
    first_half = lane < ATT_HEAD_DIM
    scale = ATT_HEAD_DIM ** -0.5

    for m in range(nsub):
        for pair in range(ATT_HPG // 2):
            lanes = slice(pair * 2 * ATT_HEAD_DIM, (pair + 1) * 2 * ATT_HEAD_DIM)
            q2 = q_ref[m * qs:(m + 1) * qs, lanes]
            k2 = kcat[m * qs:m * qs + kw, lanes]
            v2 = vcat[m * qs:m * qs + kw, lanes]
            for half in range(2):
                head = 2 * pair + half
                qm = jnp.where(first_half if half == 0 else ~first_half, q2, 0.0).astype(BF16)
                s = lax.dot_general(qm, k2, (((1,), (1,)), ((), ())), preferred_element_type=F32)
                s = s * scale + bias[head]
                if m == 0:
                    s = jnp.where(before_start, ATT_NEG, s)
                if m == nsub - 1:
                    s = jnp.where(past_end, ATT_NEG, s)
                mx = jnp.max(s, axis=-1, keepdims=True)
                p = jnp.exp(s - mx)
                zsum = jnp.sum(p, axis=-1, keepdims=True)
                pv = jnp.dot(p.astype(BF16), v2, preferred_element_type=F32)
                own = slice(half * ATT_HEAD_DIM, (half + 1) * ATT_HEAD_DIM)
                out_lanes = slice(lanes.start + own.start, lanes.start + own.stop)
                o_ref[m * qs:(m + 1) * qs, out_lanes] = (pv / zsum)[:, own]
                lse_ref[m * qs:(m + 1) * qs, out_lanes] = jnp.broadcast_to(mx + jnp.log(zsum), (qs, ATT_HEAD_DIM))


def _attention_group(qkv, group, dil):
    nseq, ls, _ = qkv.shape
    tq = min(512, ls)
    nblk = ls // tq
    halo_per_blk = tq // ATT_RADIUS
    n_halo = ls // ATT_RADIUS

    def own(off):
        return pl.BlockSpec((None, tq, ATT_OUT), lambda s, i: (s, i, off))

    def prev(off):
        return pl.BlockSpec((None, ATT_RADIUS, ATT_OUT), lambda s, i: (s, jnp.maximum(i * halo_per_blk - 1, 0), off))

    def nxt(off):
        return pl.BlockSpec((None, ATT_RADIUS, ATT_OUT),
                            lambda s, i: (s, jnp.minimum((i + 1) * halo_per_blk, n_halo - 1), off))

    out_spec = pl.BlockSpec((None, tq, ATT_OUT), lambda s, i: (s, i, 0))
    return pl.pallas_call(
        functools.partial(_attn_kernel, dil=dil, group=group),
        grid=(nseq, nblk),
        in_specs=[own(0), own(1), prev(1), nxt(1), own(2), prev(2), nxt(2)],
        out_specs=[out_spec, out_spec],
        out_shape=[jax.ShapeDtypeStruct((nseq, ls, ATT_OUT), F32)] * 2,
        scratch_shapes=[pltpu.VMEM((tq + 2 * ATT_RADIUS, ATT_OUT), BF16)] * 2
        + [pltpu.VMEM((ATT_HPG, ATT_QSUB, ATT_QSUB + 2 * ATT_RADIUS), F32)],
        compiler_params=_params("parallel", "parallel"),
        name=f"attn_g{group}",
    )(qkv, qkv, qkv, qkv, qkv, qkv, qkv)


def _s5_tables(a_re, a_im, log_step, b_re, b_im, c_re, c_im):
    hp = lax.Precision.HIGHEST
    tc = S5_TC
    ng, width = S5_GROUPS, S5_ROW
    step = jnp.exp(log_step)[..., None]
    ar, ai = a_re, a_im
    mag = jnp.exp(step * ar)
    abr, abi = mag * jnp.cos(step * ai), mag * jnp.sin(step * ai)
    den = ar * ar + ai * ai
    fr = ((abr - 1.0) * ar + abi * ai) / den
    fi = (abi * ar - (abr - 1.0) * ai) / den
    b_re_t, b_im_t = b_re.transpose(0, 2, 1), b_im.transpose(0, 2, 1)
    bbr = fr[:, :, None, :] * b_re_t[None] - fi[:, :, None, :] * b_im_t[None]
    bbi = fr[:, :, None, :] * b_im_t[None] + fi[:, :, None, :] * b_re_t[None]
    ks = jnp.arange(tc + 1, dtype=F32)
    arg_r, arg_i = step * ar, step * ai
    pk_r = jnp.exp(arg_r[:, :, None, :] * ks[:, None]) * jnp.cos(arg_i[:, :, None, :] * ks[:, None])
    pk_i = jnp.exp(arg_r[:, :, None, :] * ks[:, None]) * jnp.sin(arg_i[:, :, None, :] * ks[:, None])
    pn_r = jnp.exp(arg_r[..., None] * ks) * jnp.cos(arg_i[..., None] * ks)
    pn_i = jnp.exp(arg_r[..., None] * ks) * jnp.sin(arg_i[..., None] * ks)
    c_re_t, c_im_t = c_re.transpose(0, 1, 3, 2), c_im.transpose(0, 1, 3, 2)
    cap_r = (c_re_t[:, :, :, None, :] * pn_r[..., None] - c_im_t[:, :, :, None, :] * pn_i[..., None])
    cap_i = (c_re_t[:, :, :, None, :] * pn_i[..., None] + c_im_t[:, :, :, None, :] * pn_r[..., None])
    cap_r = cap_r.reshape(2, ng, S5_STATE, (tc + 1) * S5_CH)
    cap_i = cap_i.reshape(2, ng, S5_STATE, (tc + 1) * S5_CH)
    taps = (jnp.einsum('dgin,dgnm->dgim', bbr, cap_r, precision=hp)
            - jnp.einsum('dgin,dgnm->dgim', bbi, cap_i, precision=hp))[..., :width]
    fwd_src = jnp.pad(taps[0], ((0, 0), (0, 0), (width, 0)))
    bwd_rev = taps[1].reshape(ng, S5_CH, tc, S5_CH)[:, :, ::-1, :].reshape(ng, S5_CH, width)
    bwd_src = jnp.pad(bwd_rev, ((0, 0), (0, 0), (0, width)))
    toep = jnp.stack([fwd_src[:, :, width - S5_CH * s:2 * width - S5_CH * s]
                      + bwd_src[:, :, S5_CH * (tc - 1 - s):S5_CH * (tc - 1 - s) + width] for s in range(tc)], axis=1)
    toep = toep.reshape(ng, width, width)

    def end_state(d, pw_r, pw_i):
        re = pw_r[:, :, None, :] * bbr[d][:, None] - pw_i[:, :, None, :] * bbi[d][:, None]
        im = pw_r[:, :, None, :] * bbi[d][:, None] + pw_i[:, :, None, :] * bbr[d][:, None]
        return [re.reshape(ng, width, S5_STATE), im.reshape(ng, width, S5_STATE)]

    b_end = jnp.concatenate(end_state(0, pk_r[0][:, tc - 1::-1], pk_i[0][:, tc - 1::-1])
                            + end_state(1, pk_r[1][:, :tc], pk_i[1][:, :tc]), axis=-1)

    def lag_reversed(a):
        return a.reshape(ng, S5_STATE, tc, S5_CH)[:, :, ::-1, :].reshape(ng, S5_STATE, width)

    c_out = jnp.concatenate([cap_r[0][..., S5_CH:], -cap_i[0][..., S5_CH:],
                             lag_reversed(cap_r[1][..., S5_CH:]), -lag_reversed(cap_i[1][..., S5_CH:])], axis=1)
    ptr, pti = pk_r[:, :, tc], pk_i[:, :, tc]
    p_mul = jnp.concatenate([ptr[0], ptr[0], ptr[1], ptr[1]], axis=-1)
    q_mul = jnp.concatenate([-pti[0], pti[0], -pti[1], pti[1]], axis=-1)
    return toep, b_end, c_out, p_mul, q_mul


def _granule_transpose(regs, granule):
    n = len(regs)
    d = n // 2
    while d >= 1:
        low = (granule // d) % 2 == 0
        nxt = list(regs)
        for i in range(n):
            if (i // d) % 2 == 0:
                a, b = regs[i], regs[i + d]
                nxt[i] = jnp.where(low, a, pltpu.roll(b, d * S5_CH, 1))
                nxt[i + d] = jnp.where(low, pltpu.roll(a, 128 - d * S5_CH, 1), b)
        regs = nxt
        d //= 2
    return regs


def _s5_rows_to_groups(u_ref, x_scr):
    rr = S5_RELAYOUT_ROWS
    per = S5_LANE_GROUPS
    granule = lax.broadcasted_iota(jnp.int32, (rr, 128), 1) // S5_CH

    def body(rb, carry):
        base = pl.multiple_of(rb * (rr * S5_TC), rr * S5_TC)
        rows = pl.ds(pl.multiple_of(rb * rr, rr), rr)
        for h in range(S5_TC // per):
            steps = [u_ref[pl.ds(base + per * h + t, rr, stride=S5_TC), :] for t in range(per)]
            for gl, block in enumerate(_granule_transpose(steps, granule)):
                x_scr[gl, rows, 128 * h:128 * (h + 1)] = block
        return carry

    lax.fori_loop(0, x_scr.shape[1] // rr, body, 0)


def _s5_in_kernel(u_ref, toep_ref, bend_ref, y_ref, sf_ref, sb_ref, x_scr):
    half = 2 * S5_STATE
    _s5_rows_to_groups(u_ref, x_scr)
    for gl in range(S5_LANE_GROUPS):
        xb = x_scr[gl].astype(BF16)
        y_ref[gl] = jnp.dot(xb, toep_ref[gl], preferred_element_type=F32)
        s = jnp.dot(xb, bend_ref[gl], preferred_element_type=F32)
        sf_ref[gl] = s[:, 0:half]
        sb_ref[gl] = s[:, half:2 * half]


def _s5_scan_kernel(sf_ref, sb_ref, p_ref, q_ref, hf_ref, hb_ref, sf_sw_ref, sb_sw_ref, *, rows_per_seq):
    n_chain = p_ref.shape[0]
    half = 2 * S5_STATE
    pf, qf = p_ref[:, 0:half], q_ref[:, 0:half]
    pb, qb = p_ref[:, half:2 * half], q_ref[:, half:2 * half]
    qf_sw = pltpu.roll(qf, S5_STATE, 1)
    qb_sw = pltpu.roll(qb, S5_STATE, 1)
    zero = jnp.zeros((n_chain, half), F32)
    piece = 256

    def swap(k, carry):
        rows = pl.ds(pl.multiple_of(k * piece, piece), piece)
        sf_sw_ref[rows, :] = pltpu.roll(sf_ref[rows, :], S5_STATE, 1)
        sb_sw_ref[rows, :] = pltpu.roll(sb_ref[rows, :], S5_STATE, 1)
        return carry

    lax.fori_loop(0, sf_ref.shape[0] // piece, swap, 0)

    tile = 8

    def steps(c8, carry):
        hf, hf_sw, hb, hb_sw = carry
        base_f = pl.multiple_of(c8 * tile, tile)
        base_b = pl.multiple_of(rows_per_seq - tile - c8 * tile, tile)
        for k in range(tile):
            rows_f = pl.ds(base_f + k, n_chain, stride=rows_per_seq)
            rows_b = pl.ds(base_b + (tile - 1 - k), n_chain, stride=rows_per_seq)
            hf_ref[rows_f, :] = hf
            hb_ref[rows_b, :] = hb
            sf = sf_ref[rows_f, :]
            sb = sb_ref[rows_b, :]
            sf_sw = sf_sw_ref[rows_f, :]
            sb_sw = sb_sw_ref[rows_b, :]
            hf, hf_sw = pf * hf + qf * hf_sw + sf, pf * hf_sw + qf_sw * hf + sf_sw
            hb, hb_sw = pb * hb + qb * hb_sw + sb, pb * hb_sw + qb_sw * hb + sb_sw
        return hf, hf_sw, hb, hb_sw

    lax.fori_loop(0, rows_per_seq // tile, steps, (zero, zero, zero, zero))


def _s5_out_kernel(hf_ref, hb_ref, cout_ref, y_ref, u_ref, d_ref, o_ref, y_scr):
    half = 2 * S5_STATE
    per = S5_LANE_GROUPS
    for gl in range(per):
        y = y_ref[gl] + jnp.dot(hf_ref[gl].astype(BF16), cout_ref[gl, 0:half, :], preferred_element_type=F32)
        y_scr[gl] = y + jnp.dot(hb_ref[gl].astype(BF16), cout_ref[gl, half:2 * half, :], preferred_element_type=F32)
    rr = S5_RELAYOUT_ROWS
    granule = lax.broadcasted_iota(jnp.int32, (rr, 128), 1) // S5_CH
    d_skip = d_ref[...]

    def body(rb, carry):
        rows = pl.ds(pl.multiple_of(rb * rr, rr), rr)
        base = pl.multiple_of(rb * (rr * S5_TC), rr * S5_TC)
        for h in range(S5_TC // per):
            groups = [y_scr[gl, rows, 128 * h:128 * (h + 1)] for gl in range(per)]
            for t, block in enumerate(_granule_transpose(groups, granule)):
                tok = pl.ds(base + per * h + t, rr, stride=S5_TC)
                o_ref[tok, :] = jax.nn.gelu(block + u_ref[tok, :] * d_skip)
        return carry

    lax.fori_loop(0, y_scr.shape[1] // rr, body, 0)


def _s5_all_tables(a_re, a_im, log_step, b_re, b_im, c_re, c_im):
    toep, b_end, c_out, p_mul, q_mul = jax.vmap(_s5_tables)(a_re, a_im, log_step, b_re, b_im, c_re, c_im)
    return toep.astype(BF16), b_end.astype(BF16), c_out.astype(BF16), p_mul, q_mul


def _s5_mix(u_tiles, seqlen, tables, layer, d_skip):
    ntile, ntok, lanes = u_tiles.shape
    bsz = ntok // seqlen
    rows_per_seq = seqlen // S5_TC
    rows = bsz * rows_per_seq
    rblk = min(S5_ROW_BLOCK, rows)
    per = S5_LANE_GROUPS
    half = 2 * S5_STATE
    toep, b_end, c_out, p_mul, q_mul = tables
    p_mul, q_mul = p_mul[layer], q_mul[layer]
    utile = pl.BlockSpec((None, rblk * S5_TC, lanes), lambda j, i: (j, i, 0))
    ytile = pl.BlockSpec((per, rblk, S5_ROW), lambda j, i: (j, i, 0))
    stile = pl.BlockSpec((per, rblk, half), lambda j, i: (j, i, 0))
    wtile = pl.BlockSpec((None, per, S5_ROW, S5_ROW), lambda j, i: (layer, j, 0, 0))
    state_shape = jax.ShapeDtypeStruct((S5_GROUPS, rows, half), F32)
    grid = (ntile, rows // rblk)
    y_in, s_f, s_b = pl.pallas_call(
        _s5_in_kernel,
        grid=grid,
        in_specs=[utile, wtile, wtile],
        out_specs=[ytile, stile, stile],
        out_shape=[jax.ShapeDtypeStruct((S5_GROUPS, rows, S5_ROW), F32), state_shape, state_shape],
        scratch_shapes=[pltpu.VMEM((per, rblk, S5_ROW), F32)],
        compiler_params=_params("parallel", "parallel"),
        name="s5_in",
    )(u_tiles, toep, b_end)

    gs = S5_SCAN_GROUPS
    n_chain = gs * bsz
    chain_p = jnp.repeat(p_mul, bsz, axis=0)
    chain_q = jnp.repeat(q_mul, bsz, axis=0)
    flat = jax.ShapeDtypeStruct((S5_GROUPS * rows, half), F32)
    srows = pl.BlockSpec((gs * rows, half), lambda t: (t, 0))
    chain = pl.BlockSpec((n_chain, S5_ROW), lambda t: (t, 0))
    h_f, h_b = pl.pallas_call(
        functools.partial(_s5_scan_kernel, rows_per_seq=rows_per_seq),
        grid=(S5_GROUPS // gs,),
        in_specs=[srows, srows, chain, chain],
        out_specs=[srows, srows],
        out_shape=[flat, flat],
        scratch_shapes=[pltpu.VMEM((gs * rows, half), F32)] * 2,
        compiler_params=_params("parallel"),
        name="s5_scan",
    )(s_f.reshape(S5_GROUPS * rows, half), s_b.reshape(S5_GROUPS * rows, half), chain_p, chain_q)

    return pl.pallas_call(
        _s5_out_kernel,
        grid=grid,
        in_specs=[stile, stile, wtile, ytile, utile, pl.BlockSpec((None, 1, lanes), lambda j, i: (j, 0, 0))],
        out_specs=utile,
        out_shape=jax.ShapeDtypeStruct((ntile, ntok, lanes), F32),
        scratch_shapes=[pltpu.VMEM((per, rblk, S5_ROW), F32)],
        compiler_params=_params("parallel", "parallel"),
        name="s5_out",
    )(h_f.reshape(S5_GROUPS, rows, half), h_b.reshape(S5_GROUPS, rows, half), c_out, y_in, u_tiles,
      d_skip.reshape(ntile, 1, lanes))


def _ssd_proj_kernel(y_ref, z_ref, nw_ref, w_ref, gate_ref, bg_ref, o_ref):
    z = z_ref[...].astype(F32)
    y = y_ref[...].astype(F32) * (z * _sigmoid(z))
    y = y * lax.rsqrt(jnp.mean(y * y, axis=-1, keepdims=True) + LN_EPS) * nw_ref[...]
    p = jnp.dot(y.astype(BF16), w_ref[...], preferred_element_type=F32)
    o_ref[...] = (_sigmoid(gate_ref[...] + bg_ref[...]) * p).astype(o_ref.dtype)


def _ssd_project(y, z, norm_w, w_br, gates, b_gate, *, tm=512):
    m = y.shape[0]
    wide = pl.BlockSpec((tm, SSD_INNER), lambda i: (i, 0))
    return pl.pallas_call(
        _ssd_proj_kernel,
        grid=(m // tm,),
        in_specs=[wide, wide, pl.BlockSpec((1, SSD_INNER), lambda i: (0, 0)),
                  pl.BlockSpec((SSD_INNER, D_MODEL), lambda i: (0, 0)),
                  pl.BlockSpec((tm, D_MODEL), lambda i: (i, 0)),
                  pl.BlockSpec((1, D_MODEL), lambda i: (0, 0))],
        out_specs=pl.BlockSpec((tm, D_MODEL), lambda i: (i, 0)),
        out_shape=jax.ShapeDtypeStruct((m, D_MODEL), BF16),
        compiler_params=_params("parallel"),
        name="ssd_project",
    )(y, z, norm_w.reshape(1, -1), w_br, gates, b_gate[0:1])


def _s5_proj_kernel(h_ref, w1_ref, w2_ref, w_ref, gate_ref, bg_ref, o_ref):
    hb = jnp.concatenate([h_ref[t] for t in range(h_ref.shape[0])], axis=1).astype(BF16)
    a = jnp.dot(hb, w1_ref[...], preferred_element_type=F32)
    b = jnp.dot(hb, w2_ref[...], preferred_element_type=F32)
    p = jnp.dot((a * _sigmoid(b)).astype(BF16), w_ref[...], preferred_element_type=F32)
    o_ref[...] = (_sigmoid(gate_ref[...] + bg_ref[...]) * p).astype(o_ref.dtype)


def _s5_project(h_tiles, w1, w2, w_br, gates, b_gate, *, tm=512):
    ntile, m, lanes = h_tiles.shape
    sq = pl.BlockSpec((S5_WIDTH, S5_WIDTH), lambda i: (0, 0))
    return pl.pallas_call(
        _s5_proj_kernel,
        grid=(m // tm,),
        in_specs=[pl.BlockSpec((ntile, tm, lanes), lambda i: (0, i, 0)), sq, sq,
                  pl.BlockSpec((S5_WIDTH, D_MODEL), lambda i: (0, 0)),
                  pl.BlockSpec((tm, D_MODEL), lambda i: (i, 2)),
                  pl.BlockSpec((1, D_MODEL), lambda i: (0, 0))],
        out_specs=pl.BlockSpec((tm, D_MODEL), lambda i: (i, 0)),
        out_shape=jax.ShapeDtypeStruct((m, D_MODEL), BF16),
        compiler_params=_params("parallel"),
        name="s5_project",
    )(h_tiles, w1, w2, w_br, gates, b_gate[2:3])


def _layer_norm(v, g, b):
    mu = jnp.mean(v, axis=-1, keepdims=True)
    c = v - mu
    var = jnp.mean(c * c, axis=-1, keepdims=True)
    return c * lax.rsqrt(var + LN_EPS) * g + b


def _merge_kernel(o0_ref, o1_ref, o2_ref, l0_ref, l1_ref, l2_ref, w_att_ref, gate_ref, bg_ref, pssd_ref, ps5_ref,
                  w_out_ref, x_ref, g_ref, b_ref, o_ref):
    l0, l1, l2 = l0_ref[...], l1_ref[...], l2_ref[...]
    mx = jnp.maximum(jnp.maximum(l0, l1), l2)
    e0, e1, e2 = jnp.exp(l0 - mx), jnp.exp(l1 - mx), jnp.exp(l2 - mx)
    den = e0 + e1 + e2
    att = (e0 / den) * o0_ref[...] + (e1 / den) * o1_ref[...] + (e2 / den) * o2_ref[...]
    p_att = jnp.dot(att.astype(BF16), w_att_ref[...], preferred_element_type=F32)
    merged = pssd_ref[...] + _sigmoid(gate_ref[...] + bg_ref[...]) * p_att + ps5_ref[...]
    mix = jnp.dot(merged.astype(BF16), w_out_ref[...], preferred_element_type=F32)
    o_ref[...] = _layer_norm(ALPHA * x_ref[...] + mix, g_ref[...], b_ref[...])


def _merge(att_o, att_l, w_att, gates, b_gate, p_ssd, p_s5, w_out, x, ln_g, ln_b, *, tm=512):
    m = x.shape[0]
    att = pl.BlockSpec((tm, ATT_OUT), lambda i: (i, 0))
    row = pl.BlockSpec((tm, D_MODEL), lambda i: (i, 0))
    vec = pl.BlockSpec((1, D_MODEL), lambda i: (0, 0))
    return pl.pallas_call(
        _merge_kernel,
        grid=(m // tm,),
        in_specs=[att] * 6 + [pl.BlockSpec((ATT_OUT, D_MODEL), lambda i: (0, 0)),
                              pl.BlockSpec((tm, D_MODEL), lambda i: (i, 1)), vec, row, row,
                              pl.BlockSpec((D_MODEL, D_MODEL), lambda i: (0, 0)), row, vec, vec],
        out_specs=row,
        out_shape=jax.ShapeDtypeStruct((m, D_MODEL), F32),
        compiler_params=_params("parallel"),
        name="merge_ln1",
    )(*att_o, *att_l, w_att, gates, b_gate[1:2], p_ssd, p_s5, w_out, x, ln_g.reshape(1, -1), ln_b.reshape(1, -1))


def _router_kernel(x_ref, wt_ref, b_ref, idx_ref, gate_ref, rank_ref, count_ref, before_ref):
    tm = x_ref.shape[0]

    @pl.when(pl.program_id(0) == 0)
    def _():
        count_ref[...] = jnp.zeros_like(count_ref)
        earlier = lax.broadcasted_iota(jnp.int32, (tm, tm), 0) < lax.broadcasted_iota(jnp.int32, (tm, tm), 1)
        before_ref[...] = earlier.astype(before_ref.dtype)

    x = x_ref[...]
    wt = wt_ref[...]
    xh = x.astype(BF16)
    xl = (x - xh.astype(F32)).astype(BF16)
    wh = wt.astype(BF16)
    wl = (wt - wh.astype(F32)).astype(BF16)
    nt = (((1,), (1,)), ((), ()))
    logits = (lax.dot_general(wh, xh, nt, preferred_element_type=F32)
              + lax.dot_general(wl, xh, nt, preferred_element_type=F32)
              + lax.dot_general(wh, xl, nt, preferred_element_type=F32)) + b_ref[...]
    eid = lax.broadcasted_iota(jnp.int32, logits.shape, 0)
    vals, idxs = [], []
    for _ in range(MOE_TOP_K):
        mx = jnp.max(logits, axis=0, keepdims=True)
        sel = jnp.min(jnp.where(logits == mx, eid, MOE_EXPERTS), axis=0, keepdims=True)
        vals.append(mx)
        idxs.append(sel)
        logits = jnp.where(eid == sel, -jnp.inf, logits)
    es = [jnp.exp(v - vals[0]) for v in vals]
    den = es[0] + es[1] + es[2] + es[3]
    for k in range(MOE_TOP_K):
        idx_ref[k:k + 1, :] = idxs[k]
        gate_ref[k:k + 1, :] = es[k] / den
    chosen = [eid == idxs[k] for k in range(MOE_TOP_K)]
    stacked = jnp.concatenate([c.astype(before_ref.dtype) for c in chosen], axis=0)
    earlier = jnp.dot(stacked, before_ref[...], preferred_element_type=F32)
    seen = count_ref[...]
    for k in range(MOE_TOP_K):
        pos = earlier[k * MOE_EXPERTS:(k + 1) * MOE_EXPERTS] + seen
        rank_ref[k:k + 1, :] = jnp.sum(jnp.where(chosen[k], pos, 0.0), axis=0, keepdims=True).astype(jnp.int32)
        seen = seen + jnp.sum(chosen[k].astype(F32), axis=1, keepdims=True)
    count_ref[...] = seen


def _router(x, router_w, router_b, *, tm=1024):
    m = x.shape[0]
    out = pl.BlockSpec((MOE_TOP_K, tm), lambda i: (0, i))
    return pl.pallas_call(
        _router_kernel,
        grid=(m // tm,),
        in_specs=[pl.BlockSpec((tm, D_MODEL), lambda i: (i, 0)),
                  pl.BlockSpec((MOE_EXPERTS, D_MODEL), lambda i: (0, 0)),
                  pl.BlockSpec((MOE_EXPERTS, 1), lambda i: (0, 0))],
        out_specs=[out, out, out, pl.BlockSpec((MOE_EXPERTS, 1), lambda i: (0, 0))],
        out_shape=[jax.ShapeDtypeStruct((MOE_TOP_K, m), jnp.int32), jax.ShapeDtypeStruct((MOE_TOP_K, m), F32),
                   jax.ShapeDtypeStruct((MOE_TOP_K, m), jnp.int32), jax.ShapeDtypeStruct((MOE_EXPERTS, 1), F32)],
        scratch_shapes=[pltpu.VMEM((tm, tm), BF16)],
        compiler_params=_params("arbitrary"),
        name="router",
    )(x, router_w.T, router_b.reshape(-1, 1))


def _expert_kernel(be_ref, used_ref, x_ref, wgu_ref, bgu_ref, wdn_ref, bdn_ref, o_ref, wgu_lo, wdn_lo):
    i = pl.program_id(0)
    used = i < used_ref[0]

    @pl.when(used & ((i == 0) | (be_ref[i] != be_ref[jnp.maximum(i - 1, 0)])))
    def _():
        wgu_lo[...] = wgu_ref[...].astype(wgu_lo.dtype)
        wdn_lo[...] = wdn_ref[...].astype(wdn_lo.dtype)

    @pl.when(used)
    def _():
        hu = jnp.dot(x_ref[...].astype(BF16), wgu_lo[...], preferred_element_type=F32) + bgu_ref[...]
        x_glu = jnp.minimum(hu[:, :MOE_FF], SWIGLU_LIMIT)
        x_lin = jnp.clip(hu[:, MOE_FF:], -SWIGLU_LIMIT, SWIGLU_LIMIT)
        act = x_glu * _sigmoid(SWIGLU_ALPHA * x_glu) * (x_lin + 1.0)
        o_ref[...] = jnp.dot(act.astype(BF16), wdn_lo[...], preferred_element_type=F32) + bdn_ref[...]

    @pl.when(jnp.logical_not(used))
    def _():
        o_ref[...] = jnp.zeros_like(o_ref)


def _experts(xs, block_expert, n_used, w_gu, b_gu, w_dn, b_dn, layer):
    n_rows = xs.shape[0]
    n_blocks = n_rows // MOE_ROWS
    grid_spec = pltpu.PrefetchScalarGridSpec(
        num_scalar_prefetch=2,
        grid=(n_blocks,),
        in_specs=[
            pl.BlockSpec((MOE_ROWS, D_MODEL), lambda i, be, nu: (i, 0)),
            pl.BlockSpec((None, None, D_MODEL, 2 * MOE_FF), lambda i, be, nu: (layer, be[i], 0, 0)),
            pl.BlockSpec((None, 1, 2 * MOE_FF), lambda i, be, nu: (be[i], 0, 0)),
            pl.BlockSpec((None, None, MOE_FF, D_MODEL), lambda i, be, nu: (layer, be[i], 0, 0)),
            pl.BlockSpec((None, 1, D_MODEL), lambda i, be, nu: (be[i], 0, 0)),
        ],
        out_specs=pl.BlockSpec((MOE_ROWS, D_MODEL), lambda i, be, nu: (i, 0)),
        scratch_shapes=[pltpu.VMEM((D_MODEL, 2 * MOE_FF), BF16), pltpu.VMEM((MOE_FF, D_MODEL), BF16)],
    )
    return pl.pallas_call(
        _expert_kernel,
        grid_spec=grid_spec,
        out_shape=jax.ShapeDtypeStruct((n_rows, D_MODEL), F32),
        compiler_params=_params("arbitrary"),
        name="experts",
    )(block_expert, n_used, xs, w_gu, b_gu.reshape(MOE_EXPERTS, 1, -1), w_dn, b_dn.reshape(MOE_EXPERTS, 1, -1))


def _combine_kernel(x_ref, y0_ref, y1_ref, y2_ref, y3_ref, gate_ref, g_ref, b_ref, o_ref, ob_ref):
    ffn = gate_ref[:, 0:1] * y0_ref[...]
    for k, y_ref in enumerate((y1_ref, y2_ref, y3_ref), start=1):
        ffn = ffn + gate_ref[:, k:k + 1] * y_ref[...]
    out = _layer_norm(ALPHA * x_ref[...] + ffn, g_ref[...], b_ref[...])
    o_ref[...] = out
    ob_ref[...] = out.astype(ob_ref.dtype)


def _combine(x, y_rows, gate, ln_g, ln_b, *, tm=512):
    m = x.shape[0]
    row = pl.BlockSpec((tm, D_MODEL), lambda i: (i, 0))
    vec = pl.BlockSpec((1, D_MODEL), lambda i: (0, 0))
    return pl.pallas_call(
        _combine_kernel,
        grid=(m // tm,),
        in_specs=[row] + [pl.BlockSpec((tm, D_MODEL), functools.partial(lambda k, i: (k * (m // tm) + i, 0), k))
                          for k in range(MOE_TOP_K)] + [pl.BlockSpec((tm, MOE_TOP_K), lambda i: (i, 0)), vec, vec],
        out_specs=[row, row],
        out_shape=[jax.ShapeDtypeStruct((m, D_MODEL), F32), jax.ShapeDtypeStruct((m, D_MODEL), BF16)],
        compiler_params=_params("parallel"),
        name="combine_ln2",
    )(x, *([y_rows] * MOE_TOP_K), gate, ln_g.reshape(1, -1), ln_b.reshape(1, -1))


def _moe(x, router_w, router_b, w_gu, b_gu, w_dn, b_dn, ln_g, ln_b, layer):
    ntok = x.shape[0]
    idx_t, gate_t, rank_t, counts = _router(x, router_w, router_b)
    gate = gate_t.T
    n_assign = ntok * MOE_TOP_K
    counts = counts.reshape(-1).astype(jnp.int32)
    padded = (counts + MOE_ROWS - 1) // MOE_ROWS * MOE_ROWS
    pend = jnp.cumsum(padded)
    pstart = pend - padded
    experts = jnp.arange(MOE_EXPERTS, dtype=jnp.int32)

    def lookup(table, idx):
        return jnp.sum(jnp.where(idx[..., None] == experts, table, 0), axis=-1)

    dest_t = lookup(pstart, idx_t) + rank_t
    n_blocks = n_assign // MOE_ROWS + MOE_EXPERTS
    n_pad = MOE_EXPERTS * MOE_ROWS
    block_start = jnp.arange(n_blocks, dtype=jnp.int32) * MOE_ROWS
    block_expert = jnp.sum((pend[None, :] <= block_start[:, None]).astype(jnp.int32), axis=1)
    block_expert = jnp.minimum(block_expert, MOE_EXPERTS - 1)
    n_used = (pend[-1:] // MOE_ROWS).astype(jnp.int32)
    gap = padded - counts
    gap_end = jnp.cumsum(gap)
    pad_id = jnp.arange(n_pad, dtype=jnp.int32)
    pad_expert = jnp.sum((gap_end[None, :] <= pad_id[:, None]).astype(jnp.int32), axis=1)
    tail_of = lookup(pstart + counts - (gap_end - gap), jnp.minimum(pad_expert, MOE_EXPERTS - 1)) + pad_id
    pad_slot = jnp.where(pad_expert < MOE_EXPERTS, tail_of, pend[-1] + pad_id - gap_end[-1])
    keys = jnp.concatenate([dest_t.reshape(-1), pad_slot])
    toks = jnp.concatenate([jnp.tile(jnp.arange(ntok, dtype=jnp.int32), MOE_TOP_K), jnp.zeros((n_pad,), jnp.int32)])
    _, slot_tok = lax.sort((keys, toks), num_keys=1, is_stable=False)
    xs = x.at[slot_tok].get(mode="promise_in_bounds")
    ys = _experts(xs, block_expert, n_used, w_gu, b_gu, w_dn, b_dn, layer)
    y_rows = ys.at[dest_t.reshape(-1)].get(mode="promise_in_bounds", unique_indices=True)
    return _combine(x, y_rows, gate, ln_g, ln_b)


def _layer(x, xb, w_in, b_gate, ssd_conv_w, ssd_conv_b, ssd_a_log, ssd_dt_bias, ssd_d, ssd_norm_w,
           s5_d, s5_glu_w1, s5_glu_w2, w_br_ssd, w_br_attn, w_br_s5, w_out, ln1_g, ln1_b,
           router_w, router_b, b_gu, b_dn, ln2_g, ln2_b, *, layer, w_gu, w_dn, s5_tables):
    bsz, seqlen, d = x.shape
    ntok = bsz * seqlen
    xf = x.reshape(ntok, d)
    offs = [0]
    for s in IN_SIZES:
        offs.append(offs[-1] + s)

    def seg(k):
        return w_in[:, offs[k]:offs[k + 1]].astype(BF16)

    gates = _matmul(xb, seg(0), tm=1024, tn=1024, name="proj_gates", out_dtype=BF16)
    z = _matmul(xb, seg(1), tm=1024, tn=1024, name="proj_z", out_dtype=BF16)
    u = _matmul_tiles(xb, seg(5), tm=512, name="proj_u")

    w_xbc = seg(2)
    xs_act = _xbc_conv(xb, w_xbc[:, :SSD_INNER], ssd_conv_w[:, :SSD_INNER], ssd_conv_b[:SSD_INNER], seqlen)
    bc_act = _xbc_conv(xb, w_xbc[:, SSD_INNER:], ssd_conv_w[:, SSD_INNER:], ssd_conv_b[SSD_INNER:], seqlen,
                       out_dtype=BF16)
    xs_act = xs_act.reshape(bsz, seqlen, SSD_INNER)
    bc_act = bc_act.reshape(bsz, seqlen, SSD_CONV_CH - SSD_INNER)
    decay = _ssd_decay(xb, seg(3), ssd_a_log, ssd_dt_bias)
    y_ssd = _ssd_scan(xs_act, bc_act, *decay, ssd_d)
    p_ssd = _ssd_project(y_ssd.reshape(ntok, SSD_INNER), z, ssd_norm_w, w_br_ssd.astype(BF16), gates, b_gate)

    w_qkv = w_in[:, offs[4]:offs[5]].reshape(d, 3, ATT_GROUPS, ATT_OUT)
    att_o, att_l = [], []
    for g, (window, dil) in enumerate(ATT_PATTERNS):
        assert (window // 2) // dil == ATT_RADIUS
        ls = seqlen // dil
        x_cls = xb.reshape(bsz, ls, dil, d).transpose(0, 2, 1, 3).reshape(ntok, d)
        qkv_g = _matmul(x_cls, w_qkv[:, :, g].reshape(d, 3 * ATT_OUT).astype(BF16), tm=1024, tn=3 * ATT_OUT,
                        name=f"proj_qkv{g}", out_dtype=BF16)
        o, lse = _attention_group(qkv_g.reshape(bsz * dil, ls, 3 * ATT_OUT), g, dil)
        att_o.append(o.reshape(bsz, dil, ls, ATT_OUT).transpose(0, 2, 1, 3).reshape(ntok, ATT_OUT))
        att_l.append(lse.reshape(bsz, dil, ls, ATT_OUT).transpose(0, 2, 1, 3).reshape(ntok, ATT_OUT))

    h_s5 = _s5_mix(u, seqlen, s5_tables, layer, s5_d)
    p_s5 = _s5_project(h_s5, s5_glu_w1.astype(BF16), s5_glu_w2.astype(BF16), w_br_s5.astype(BF16), gates, b_gate)

    x1 = _merge(att_o, att_l, w_br_attn.astype(BF16), gates, b_gate, p_ssd, p_s5, w_out.astype(BF16),
                xf, ln1_g, ln1_b)
    x2, x2b = _moe(x1, router_w, router_b, w_gu, b_gu, w_dn, b_dn, ln2_g, ln2_b, layer)
    return x2.reshape(bsz, seqlen, d), x2b


def kernel(x, w_in, b_gate, ssd_conv_w, ssd_conv_b, ssd_a_log, ssd_dt_bias, ssd_d, ssd_norm_w, s5_a_re, s5_a_im, s5_log_step, s5_b_re, s5_b_im, s5_c_re, s5_c_im, s5_d, s5_glu_w1, s5_glu_w2, w_br_ssd, w_br_attn, w_br_s5, w_out, ln1_g, ln1_b, router_w, router_b, exp_w_gate_up, exp_b_gate_up, exp_w_down, exp_b_down, ln2_g, ln2_b):
    per_layer = (w_in, b_gate, ssd_conv_w, ssd_conv_b, ssd_a_log, ssd_dt_bias, ssd_d, ssd_norm_w, s5_d, s5_glu_w1,
                 s5_glu_w2, w_br_ssd, w_br_attn, w_br_s5, w_out, ln1_g, ln1_b, router_w, router_b, exp_b_gate_up,
                 exp_b_down, ln2_g, ln2_b)
    s5_tables = _s5_all_tables(s5_a_re, s5_a_im, s5_log_step, s5_b_re, s5_b_im, s5_c_re, s5_c_im)
    xb = x.reshape(-1, x.shape[-1]).astype(BF16)
    for layer in range(DEPTH):
        x, xb = _layer(x, xb, *[p[layer] for p in per_layer], layer=layer, w_gu=exp_w_gate_up, w_dn=exp_w_down,
                       s5_tables=s5_tables)
    return x
```

```python
import functools
import math

import jax
import jax.numpy as jnp
from jax import lax
from jax.experimental import pallas as pl
from jax.experimental.pallas import tpu as pltpu

F32 = jnp.float32
BF16 = jnp.bfloat16

D_MODEL = 1024
DEPTH = 4
ALPHA = (2 * DEPTH) ** 0.25
LN_EPS = 1e-5

SSD_INNER = 2 * D_MODEL
SSD_HEAD_DIM = 64
SSD_GROUPS = 8
SSD_HPG = 4
SSD_HEADS = SSD_GROUPS * SSD_HPG
SSD_STATE = 128
SSD_CONV = 5
SSD_CONV_CH = SSD_INNER + 2 * SSD_GROUPS * SSD_STATE
SSD_CHUNK = 128
SSD_SEQ_BLOCK = 2048
CONV_ROW_PARTS = 1
CONV_HALO_ROWS = 16

ATT_HEAD_DIM = 64
ATT_PATTERNS = ((128, 1), (512, 4), (2048, 16))
ATT_GROUPS = 3
ATT_HPG = 6
ATT_HEADS = ATT_GROUPS * ATT_HPG
ATT_WIDTH = ATT_HEADS * ATT_HEAD_DIM
ATT_OUT = ATT_HPG * ATT_HEAD_DIM
ATT_NEG = -1e30
ATT_RADIUS = 64
ATT_QSUB = 128

S5_CH = 16
S5_STATE = 64
S5_WIDTH = 9 * D_MODEL // 8
S5_GROUPS = S5_WIDTH // S5_CH
S5_TC = 16
S5_ROW = S5_TC * S5_CH
S5_SCAN_GROUPS = 8
S5_LANE_GROUPS = 128 // S5_CH
S5_TILES = S5_GROUPS // S5_LANE_GROUPS
S5_ROW_BLOCK = 512
S5_RELAYOUT_ROWS = 16

N_BRANCHES = 3
IN_SIZES = (N_BRANCHES * D_MODEL, SSD_INNER, SSD_CONV_CH, 2 * SSD_HEADS, 3 * ATT_WIDTH, S5_WIDTH)

MOE_EXPERTS = 32
MOE_TOP_K = 4
MOE_FF = D_MODEL
SWIGLU_LIMIT = 7.0
SWIGLU_ALPHA = 1.702
MOE_ROWS = 512

VMEM_LIMIT = 56 * 1024 * 1024


def _sigmoid(x):
    return 0.5 * jnp.tanh(0.5 * x) + 0.5


def _params(*sem):
    return pltpu.CompilerParams(dimension_semantics=sem, vmem_limit_bytes=VMEM_LIMIT)


def _mm_kernel(x_ref, w_ref, o_ref):
    o_ref[...] = jnp.dot(x_ref[...].astype(BF16), w_ref[...], preferred_element_type=F32).astype(o_ref.dtype)


def _matmul(x, w, *, tm, tn, name, out_dtype=F32):
    m, k = x.shape
    n = w.shape[1]
    return pl.pallas_call(
        _mm_kernel,
        grid=(n // tn, m // tm),
        in_specs=[pl.BlockSpec((tm, k), lambda j, i: (i, 0)), pl.BlockSpec((k, tn), lambda j, i: (0, j))],
        out_specs=pl.BlockSpec((tm, tn), lambda j, i: (i, j)),
        out_shape=jax.ShapeDtypeStruct((m, n), out_dtype),
        compiler_params=_params("parallel", "parallel"),
        name=name,
    )(x, w)


def _mm_tiles_kernel(x_ref, w_ref, o_ref):
    acc = jnp.dot(x_ref[...].astype(BF16), w_ref[...], preferred_element_type=F32)
    for t in range(o_ref.shape[0]):
        o_ref[t] = acc[:, 128 * t:128 * (t + 1)]


def _matmul_tiles(x, w, *, tm, name):
    m, k = x.shape
    n = w.shape[1]
    nt = n // 128
    return pl.pallas_call(
        _mm_tiles_kernel,
        grid=(m // tm,),
        in_specs=[pl.BlockSpec((tm, k), lambda i: (i, 0)), pl.BlockSpec((k, n), lambda i: (0, 0))],
        out_specs=pl.BlockSpec((nt, tm, 128), lambda i: (0, i, 0)),
        out_shape=jax.ShapeDtypeStruct((nt, m, 128), F32),
        compiler_params=_params("parallel"),
        name=name,
    )(x, w)


def _split3(a):
    a1 = a.astype(BF16)
    r1 = a - a1.astype(F32)
    a2 = r1.astype(BF16)
    a3 = (r1 - a2.astype(F32)).astype(BF16)
    return a1, a2, a3


def _decay_kernel(x_ref, w_ref, wt_ref, alog_r_ref, alog_c_ref, bias_r_ref, bias_c_ref,
                  acs_c_ref, wend_c_ref, acs_r_ref, delta_r_ref):
    q = SSD_CHUNK
    nh = SSD_HEADS
    xb = x_ref[...].astype(BF16)
    dt_c = jnp.dot(xb, w_ref[...], preferred_element_type=F32)
    dt_r = lax.dot_general(wt_ref[...], xb, (((1,), (1,)), ((), ())), preferred_element_type=F32)
    delta_c = jax.nn.softplus(dt_c + bias_r_ref[...])
    delta_r = jax.nn.softplus(dt_r + bias_c_ref[...])
    a_c = -jnp.exp(alog_r_ref[...]) * delta_c
    a_r = -jnp.exp(alog_c_ref[...]) * delta_r
    delta_r_ref[...] = delta_r
    row = lax.broadcasted_iota(jnp.int32, (q, q), 0)
    col = lax.broadcasted_iota(jnp.int32, (q, q), 1)
    lower = (col <= row).astype(BF16)
    upper = (col >= row).astype(BF16)
    fwd_lane = lax.broadcasted_iota(jnp.int32, (q, dt_c.shape[1]), 1) < nh
    fwd_row = lax.broadcasted_iota(jnp.int32, (2 * nh, q), 0) < nh
    for c in range(x_ref.shape[0] // q):
        rows = slice(c * q, (c + 1) * q)
        pieces = _split3(a_c[rows])
        prefix = sum(jnp.dot(lower, p, preferred_element_type=F32) for p in pieces)
        suffix = sum(jnp.dot(upper, p, preferred_element_type=F32) for p in pieces)
        acs = jnp.where(fwd_lane, prefix, suffix)
        total = jnp.where(fwd_lane[0:1], prefix[q - 1:q], suffix[0:1])
        acs_c_ref[rows, :] = acs
        wend_c_ref[rows, :] = delta_c[rows] * jnp.exp(total - acs)
        pieces = _split3(a_r[:, rows])
        prefix = sum(jnp.dot(p, upper, preferred_element_type=F32) for p in pieces)
        suffix = sum(jnp.dot(p, lower, preferred_element_type=F32) for p in pieces)
        acs_r_ref[:, rows] = jnp.where(fwd_row, prefix, suffix)


def _ssd_decay(x, w_dt, a_log, dt_bias, *, tm=1024):
    m, k = x.shape
    n = w_dt.shape[1]
    lanes = 128
    pad = lanes - n
    w_pad = jnp.pad(w_dt, ((0, 0), (0, pad)))
    alog = a_log.reshape(-1)
    bias = dt_bias.reshape(-1)
    col = pl.BlockSpec((tm, lanes), lambda i: (i, 0))
    rowf = pl.BlockSpec((n, tm), lambda i: (0, i))
    return pl.pallas_call(
        _decay_kernel,
        grid=(m // tm,),
        in_specs=[pl.BlockSpec((tm, k), lambda i: (i, 0)), pl.BlockSpec((k, lanes), lambda i: (0, 0)),
                  pl.BlockSpec((n, k), lambda i: (0, 0)),
                  pl.BlockSpec((1, lanes), lambda i: (0, 0)), pl.BlockSpec((n, 1), lambda i: (0, 0)),
                  pl.BlockSpec((1, lanes), lambda i: (0, 0)), pl.BlockSpec((n, 1), lambda i: (0, 0))],
        out_specs=[col, col, rowf, rowf],
        out_shape=[jax.ShapeDtypeStruct((m, lanes), F32)] * 2 + [jax.ShapeDtypeStruct((n, m), F32)] * 2,
        compiler_params=_params("parallel"),
        name="ssd_decay",
    )(x, w_pad, w_dt.T, jnp.pad(alog, (0, pad)).reshape(1, lanes), alog.reshape(n, 1),
      jnp.pad(bias, (0, pad)).reshape(1, lanes), bias.reshape(n, 1))


def _xbc_conv_kernel(xm_ref, xp_ref, xn_ref, w_ref, cw_ref, cb_ref, o_ref, buf, *, tiles_per_seq):
    pos = pl.program_id(1) % tiles_per_seq
    ts = xm_ref.shape[0]
    halo = xp_ref.shape[0]
    half = SSD_CONV // 2
    before = jnp.where(pos == 0, jnp.zeros_like(xp_ref[...]), xp_ref[...])
    after = jnp.where(pos == tiles_per_seq - 1, jnp.zeros_like(xn_ref[...]), xn_ref[...])
    rows = jnp.concatenate([before, xm_ref[...], after], axis=0).astype(BF16)
    nparts = CONV_ROW_PARTS
    part = ts // nparts
    edges = [0] + [halo + part * (p + 1) for p in range(nparts - 1)] + [ts + 2 * halo]

    def project(p):
        buf[edges[p]:edges[p + 1], :] = jnp.dot(rows[edges[p]:edges[p + 1]], w_ref[...], preferred_element_type=F32)

    def convolve(p):
        lo = halo - half + part * p
        acc = cb_ref[...] + cw_ref[0:1, :] * buf[lo:lo + part, :]
        for k in range(1, SSD_CONV):
            acc = acc + cw_ref[k:k + 1, :] * buf[lo + k:lo + k + part, :]
        o_ref[part * p:part * (p + 1), :] = (acc * _sigmoid(acc)).astype(o_ref.dtype)

    project(0)
    for p in range(nparts):
        if p + 1 < nparts:
            project(p + 1)
        convolve(p)


def _xbc_conv(x, w, conv_w, conv_b, seqlen, *, tm=512, tn=1024, out_dtype=F32):
    m, k = x.shape
    n = w.shape[1]
    halo = CONV_HALO_ROWS
    nhalo = m // halo
    return pl.pallas_call(
        functools.partial(_xbc_conv_kernel, tiles_per_seq=seqlen // tm),
        grid=(n // tn, m // tm),
        in_specs=[
            pl.BlockSpec((tm, k), lambda j, i: (i, 0)),
            pl.BlockSpec((halo, k), lambda j, i: (jnp.maximum(i * (tm // halo) - 1, 0), 0)),
            pl.BlockSpec((halo, k), lambda j, i: (jnp.minimum((i + 1) * (tm // halo), nhalo - 1), 0)),
            pl.BlockSpec((k, tn), lambda j, i: (0, j)),
            pl.BlockSpec((SSD_CONV, tn), lambda j, i: (0, j)),
            pl.BlockSpec((1, tn), lambda j, i: (0, j)),
        ],
        out_specs=pl.BlockSpec((tm, tn), lambda j, i: (i, j)),
        out_shape=jax.ShapeDtypeStruct((m, n), out_dtype),
        scratch_shapes=[pltpu.VMEM((tm + 2 * halo, tn), F32)],
        compiler_params=_params("parallel", "parallel"),
        name="proj_xbc_conv",
    )(x, x, x, w, conv_w, conv_b.reshape(1, n))


def _per_head(cols):
    rows = cols[0].shape[0]
    first = lax.broadcasted_iota(jnp.int32, (rows, 2 * SSD_HEAD_DIM), 1) < SSD_HEAD_DIM
    return jnp.concatenate([jnp.where(first, cols[0], cols[1]), jnp.where(first, cols[2], cols[3])], axis=1)


def _group_lanes(ref, rows, g):
    lanes = ref.shape[1]
    return pltpu.roll(ref[rows, :], (lanes - SSD_HPG * g) % lanes, 1)


def _group_rows(ref, first, cols, g):
    tile = 8
    per_tile = tile // SSD_HPG
    base = pl.multiple_of(first + (g // per_tile) * tile, tile)
    return pltpu.roll(ref[pl.ds(base, tile), cols], (tile - SSD_HPG * (g % per_tile)) % tile, 0)


def _head_selector(lanes, first):
    width = SSD_HPG * SSD_HEAD_DIM
    src = lax.broadcasted_iota(jnp.int32, (lanes, width), 0)
    dst = lax.broadcasted_iota(jnp.int32, (lanes, width), 1) // SSD_HEAD_DIM
    return (src == dst + first).astype(BF16)


def _expand_heads(cols, selector):
    return sum(jnp.dot(p, selector, preferred_element_type=F32) for p in _split3(cols))


def _weighted_inputs(xs_ref, wend_ref, g, first, xw_scr):
    rows = slice(0, xs_ref.shape[0])
    wend = _group_lanes(wend_ref, rows, g)
    xw_scr[...] = (xs_ref[...] * _expand_heads(wend, _head_selector(wend_ref.shape[1], first))).astype(xw_scr.dtype)


def _ssd_back_kernel(xs_ref, bm_ref, acs_ref, wend_ref, out_ref, carried, xw_scr):
    q = SSD_CHUNK
    nchunk = xs_ref.shape[0] // q
    g = pl.program_id(1)
    bwd = SSD_HEADS

    @pl.when(pl.program_id(2) == 0)
    def _():
        carried[...] = jnp.zeros_like(carried)

    _weighted_inputs(xs_ref, wend_ref, g, bwd, xw_scr)

    def chunk(ci, carry):
        c = nchunk - 1 - ci
        rows = pl.ds(pl.multiple_of(c * q, q), q)
        first_row = pl.ds(pl.multiple_of(c * q, q), 8)
        total = _group_lanes(acs_ref, first_row, g)[0:1]
        states = lax.dot_general(bm_ref[rows, :].astype(BF16), xw_scr[rows, :], (((0,), (0,)), ((), ())),
                                 preferred_element_type=F32)
        prev = carried[...]
        out_ref[c] = prev.astype(out_ref.dtype)
        carried[...] = jnp.exp(_per_head([total[:, bwd + r:bwd + r + 1] for r in range(SSD_HPG)])) * prev + states
        return carry

    lax.fori_loop(0, nchunk, chunk, 0, unroll=4)


def _ssd_main_kernel(xs_ref, bm_ref, cm_ref, acs_c_ref, wend_c_ref, acs_r_ref, delta_r_ref, back_ref,
                     dskip_ref, y_ref, carried):
    q = SSD_CHUNK
    nchunk = xs_ref.shape[0] // q
    g = pl.program_id(1)
    bwd = SSD_HEADS
    hd = SSD_HEAD_DIM

    @pl.when(pl.program_id(2) == 0)
    def _():
        carried[...] = jnp.zeros_like(carried)

    row = lax.broadcasted_iota(jnp.int32, (q, q), 0)
    col = lax.broadcasted_iota(jnp.int32, (q, q), 1)
    lower = col <= row
    upper = col >= row
    d_wide = _per_head([dskip_ref[:, r:r + 1] for r in range(SSD_HPG)])
    selector = _head_selector(wend_c_ref.shape[1], 0)
    first = lax.broadcasted_iota(jnp.int32, (q, 2 * hd), 1) < hd

    def per_head_lanes(blocks):
        return jnp.concatenate([jnp.where(first, blocks[0], blocks[1]), jnp.where(first, blocks[2], blocks[3])], axis=1)

    def chunk(c, carry):
        r0 = pl.multiple_of(c * q, q)
        rows = pl.ds(r0, q)
        xs = xs_ref[rows, :]
        bmat = bm_ref[rows, :].astype(BF16)
        cmat = cm_ref[rows, :].astype(BF16)
        acs = _group_lanes(acs_c_ref, rows, g)
        wend = _group_lanes(wend_c_ref, rows, g)
        acs_f = [jnp.broadcast_to(acs[:, r:r + 1], (q, q)) for r in range(SSD_HPG)]
        acs_b = [jnp.broadcast_to(acs[:, bwd + r:bwd + r + 1], (q, q)) for r in range(SSD_HPG)]
        acs_rf = _group_rows(acs_r_ref, 0, rows, g)
        acs_rb = _group_rows(acs_r_ref, bwd, rows, g)
        del_rf = _group_rows(delta_r_ref, 0, rows, g)
        del_rb = _group_rows(delta_r_ref, bwd, rows, g)
        cb = lax.dot_general(cmat, bmat, (((1,), (1,)), ((), ())), preferred_element_type=F32)
        prev = carried[...]
        states_in = jnp.concatenate([prev.astype(BF16), back_ref[c]], axis=1)
        y_off = jnp.dot(cmat, states_in, preferred_element_type=F32)
        e_f = jnp.exp(per_head_lanes(acs_f))
        e_b = jnp.exp(per_head_lanes(acs_b))
        rest = e_f * y_off[:, 0:SSD_HPG * hd] + e_b * y_off[:, SSD_HPG * hd:] + d_wide * xs
        for pair in range(SSD_HPG // 2):
            lanes = slice(2 * pair * hd, 2 * (pair + 1) * hd)
            x_pair = xs[:, lanes]
            y_pair = rest[:, lanes]
            for half in range(2):
                r = 2 * pair + half
                seg_f = acs_f[r] - acs_rf[r:r + 1, :]
                seg_b = acs_b[r] - acs_rb[r:r + 1, :]
                w = (jnp.exp(jnp.where(lower, seg_f, ATT_NEG)) * del_rf[r:r + 1, :]
                     + jnp.exp(jnp.where(upper, seg_b, ATT_NEG)) * del_rb[r:r + 1, :])
                x_own = jnp.where(first if half == 0 else ~first, x_pair, 0.0).astype(BF16)
                y_pair = y_pair + jnp.dot((cb * w).astype(BF16), x_own, preferred_element_type=F32)
            y_ref[rows, lanes] = y_pair.astype(y_ref.dtype)
        xw = xs * _expand_heads(wend, selector)
        states = lax.dot_general(bmat, xw.astype(BF16), (((0,), (0,)), ((), ())), preferred_element_type=F32)
        total = acs[q - 1:q]
        carried[...] = jnp.exp(_per_head([total[:, r:r + 1] for r in range(SSD_HPG)])) * prev + states
        return carry

    lax.fori_loop(0, nchunk, chunk, 0, unroll=8)


def _ssd_scan(xs_act, bc_act, acs_c, wend_c, acs_r, delta_r, d_skip):
    bsz, seqlen, _ = xs_act.shape
    sb = min(SSD_SEQ_BLOCK, seqlen)
    nblk = seqlen // sb
    cpb = sb // SSD_CHUNK
    n_x = SSD_HPG * SSD_HEAD_DIM
    b_off = 0
    c_off = SSD_GROUPS
    lanes = acs_c.shape[1]
    nrow = acs_r.shape[0]

    def specs(blk):
        xs = pl.BlockSpec((None, sb, n_x), lambda b_, g, i: (b_, blk(i), g))
        bm = pl.BlockSpec((None, sb, SSD_STATE), lambda b_, g, i: (b_, blk(i), b_off + g))
        cm = pl.BlockSpec((None, sb, SSD_STATE), lambda b_, g, i: (b_, blk(i), c_off + g))
        colf = pl.BlockSpec((sb, lanes), lambda b_, g, i: (b_ * nblk + blk(i), 0))
        rowf = pl.BlockSpec((nrow, sb), lambda b_, g, i: (0, b_ * nblk + blk(i)))
        back = pl.BlockSpec((None, None, cpb, SSD_STATE, n_x), lambda b_, g, i: (b_, g, blk(i), 0, 0))
        return xs, bm, cm, colf, rowf, back

    xs, bm, cm, colf, rowf, back = specs(lambda i: nblk - 1 - i)
    assert SSD_CHUNK == 2 * SSD_HEAD_DIM and SSD_HPG == 4
    state = [pltpu.VMEM((SSD_STATE, n_x), F32), pltpu.VMEM((sb, n_x), BF16)]
    back_states = pl.pallas_call(
        _ssd_back_kernel,
        grid=(bsz, SSD_GROUPS, nblk),
        in_specs=[xs, bm, colf, colf],
        out_specs=back,
        out_shape=jax.ShapeDtypeStruct((bsz, SSD_GROUPS, seqlen // SSD_CHUNK, SSD_STATE, n_x), BF16),
        scratch_shapes=state,
        compiler_params=_params("parallel", "parallel", "arbitrary"),
        name="ssd_back",
    )(xs_act, bc_act, acs_c, wend_c)

    xs, bm, cm, colf, rowf, back = specs(lambda i: i)
    return pl.pallas_call(
        _ssd_main_kernel,
        grid=(bsz, SSD_GROUPS, nblk),
        in_specs=[xs, bm, cm, colf, colf, rowf, rowf, back,
                  pl.BlockSpec((None, 1, SSD_HPG), lambda b_, g, i: (g, 0, 0))],
        out_specs=pl.BlockSpec((None, sb, n_x), lambda b_, g, i: (b_, i, g)),
        out_shape=jax.ShapeDtypeStruct((bsz, seqlen, SSD_INNER), BF16),
        scratch_shapes=state[:1],
        compiler_params=_params("parallel", "parallel", "arbitrary"),
        name="ssd_main",
    )(xs_act, bc_act, bc_act, acs_c, wend_c, acs_r, delta_r, back_states,
      d_skip.reshape(SSD_GROUPS, 1, SSD_HPG))


def _attn_kernel(q_ref, ko_ref, kp_ref, kn_ref, vo_ref, vp_ref, vn_ref, o_ref, lse_ref, kcat, vcat, bias,
                 *, dil, group):
    tq = q_ref.shape[0]
    rad = ATT_RADIUS
    qs = ATT_QSUB
    kw = qs + 2 * rad
    nsub = tq // qs
    i = pl.program_id(1)
    kcat[0:rad, :] = kp_ref[...].astype(BF16)
    kcat[rad:rad + tq, :] = ko_ref[...].astype(BF16)
    kcat[rad + tq:2 * rad + tq, :] = kn_ref[...].astype(BF16)
    vcat[0:rad, :] = vp_ref[...].astype(BF16)
    vcat[rad:rad + tq, :] = vo_ref[...].astype(BF16)
    vcat[rad + tq:2 * rad + tq, :] = vn_ref[...].astype(BF16)

    row = lax.broadcasted_iota(jnp.int32, (qs, kw), 0)
    col = lax.broadcasted_iota(jnp.int32, (qs, kw), 1)
    dist = jnp.abs(row - col + rad)
    for head in range(ATT_HPG):
        slope = 2.0 ** (-8.0 * (head * ATT_GROUPS + group + 1) / ATT_HEADS)
        bias[head] = jnp.where(dist <= rad, (-slope * dil) * dist.astype(F32), ATT_NEG)
    before_start = (i == 0) & (col < rad)
    past_end = (i == pl.num_programs(1) - 1) & (col >= qs + rad)
    lane = lax.broadcasted_iota(jnp.int32, (qs, 2 * ATT_HEAD_DIM), 1)
    first_half = lane < ATT_HEAD_DIM
    scale = ATT_HEAD_DIM ** -0.5

    for m in range(nsub):
        for pair in range(ATT_HPG // 2):
            lanes = slice(pair * 2 * ATT_HEAD_DIM, (pair + 1) * 2 * ATT_HEAD_DIM)
            q2 = q_ref[m * qs:(m + 1) * qs, lanes]
            k2 = kcat[m * qs:m * qs + kw, lanes]
            v2 = vcat[m * qs:m * qs + kw, lanes]
            for half in range(2):
                head = 2 * pair + half
                qm = jnp.where(first_half if half == 0 else ~first_half, q2, 0.0).astype(BF16)
                s = lax.dot_general(qm, k2, (((1,), (1,)), ((), ())), preferred_element_type=F32)
                s = s * scale + bias[head]
                if m == 0:
                    s = jnp.where(before_start, ATT_NEG, s)
                if m == nsub - 1:
                    s = jnp.where(past_end, ATT_NEG, s)
                mx = jnp.max(s, axis=-1, keepdims=True)
                p = jnp.exp(s - mx)
                zsum = jnp.sum(p, axis=-1, keepdims=True)
                pv = jnp.dot(p.astype(BF16), v2, preferred_element_type=F32)
                own = slice(half * ATT_HEAD_DIM, (half + 1) * ATT_HEAD_DIM)
                out_lanes = slice(lanes.start + own.start, lanes.start + own.stop)
                o_ref[m * qs:(m + 1) * qs, out_lanes] = (pv / zsum)[:, own]
                lse_ref[m * qs:(m + 1) * qs, out_lanes] = jnp.broadcast_to(mx + jnp.log(zsum), (qs, ATT_HEAD_DIM))


def _attention_group(qkv, group, dil):
    nseq, ls, _ = qkv.shape
    tq = min(512, ls)
    nblk = ls // tq
    halo_per_blk = tq // ATT_RADIUS
    n_halo = ls // ATT_RADIUS

    def own(off):
        return pl.BlockSpec((None, tq, ATT_OUT), lambda s, i: (s, i, off))

    def prev(off):
        return pl.BlockSpec((None, ATT_RADIUS, ATT_OUT), lambda s, i: (s, jnp.maximum(i * halo_per_blk - 1, 0), off))

    def nxt(off):
        return pl.BlockSpec((None, ATT_RADIUS, ATT_OUT),
                            lambda s, i: (s, jnp.minimum((i + 1) * halo_per_blk, n_halo - 1), off))

    out_spec = pl.BlockSpec((None, tq, ATT_OUT), lambda s, i: (s, i, 0))
    return pl.pallas_call(
        functools.partial(_attn_kernel, dil=dil, group=group),
        grid=(nseq, nblk),
        in_specs=[own(0), own(1), prev(1), nxt(1), own(2), prev(2), nxt(2)],
        out_specs=[out_spec, out_spec],
        out_shape=[jax.ShapeDtypeStruct((nseq, ls, ATT_OUT), F32)] * 2,
        scratch_shapes=[pltpu.VMEM((tq + 2 * ATT_RADIUS, ATT_OUT), BF16)] * 2
        + [pltpu.VMEM((ATT_HPG, ATT_QSUB, ATT_QSUB + 2 * ATT_RADIUS), F32)],
        compiler_params=_params("parallel", "parallel"),
        name=f"attn_g{group}",
    )(qkv, qkv, qkv, qkv, qkv, qkv, qkv)


def _s5_tables(a_re, a_im, log_step, b_re, b_im, c_re, c_im):
    hp = lax.Precision.HIGHEST
    tc = S5_TC
    ng, width = S5_GROUPS, S5_ROW
    step = jnp.exp(log_step)[..., None]
    ar, ai = a_re, a_im
    mag = jnp.exp(step * ar)
    abr, abi = mag * jnp.cos(step * ai), mag * jnp.sin(step * ai)
    den = ar * ar + ai * ai
    fr = ((abr - 1.0) * ar + abi * ai) / den
    fi = (abi * ar - (abr - 1.0) * ai) / den
    b_re_t, b_im_t = b_re.transpose(0, 2, 1), b_im.transpose(0, 2, 1)
    bbr = fr[:, :, None, :] * b_re_t[None] - fi[:, :, None, :] * b_im_t[None]
    bbi = fr[:, :, None, :] * b_im_t[None] + fi[:, :, None, :] * b_re_t[None]
    ks = jnp.arange(tc + 1, dtype=F32)
    arg_r, arg_i = step * ar, step * ai
    pk_r = jnp.exp(arg_r[:, :, None, :] * ks[:, None]) * jnp.cos(arg_i[:, :, None, :] * ks[:, None])
    pk_i = jnp.exp(arg_r[:, :, None, :] * ks[:, None]) * jnp.sin(arg_i[:, :, None, :] * ks[:, None])
    pn_r = jnp.exp(arg_r[..., None] * ks) * jnp.cos(arg_i[..., None] * ks)
    pn_i = jnp.exp(arg_r[..., None] * ks) * jnp.sin(arg_i[..., None] * ks)
    c_re_t, c_im_t = c_re.transpose(0, 1, 3, 2), c_im.transpose(0, 1, 3, 2)
    cap_r = (c_re_t[:, :, :, None, :] * pn_r[..., None] - c_im_t[:, :, :, None, :] * pn_i[..., None])
    cap_i = (c_re_t[:, :, :, None, :] * pn_i[..., None] + c_im_t[:, :, :, None, :] * pn_r[..., None])
    cap_r = cap_r.reshape(2, ng, S5_STATE, (tc + 1) * S5_CH)
    cap_i = cap_i.reshape(2, ng, S5_STATE, (tc + 1) * S5_CH)
    taps = (jnp.einsum('dgin,dgnm->dgim', bbr, cap_r, precision=hp)
            - jnp.einsum('dgin,dgnm->dgim', bbi, cap_i, precision=hp))[..., :width]
    fwd_src = jnp.pad(taps[0], ((0, 0), (0, 0), (width, 0)))
    bwd_rev = taps[1].reshape(ng, S5_CH, tc, S5_CH)[:, :, ::-1, :].reshape(ng, S5_CH, width)
    bwd_src = jnp.pad(bwd_rev, ((0, 0), (0, 0), (0, width)))
    toep = jnp.stack([fwd_src[:, :, width - S5_CH * s:2 * width - S5_CH * s]
                      + bwd_src[:, :, S5_CH * (tc - 1 - s):S5_CH * (tc - 1 - s) + width] for s in range(tc)], axis=1)
    toep = toep.reshape(ng, width, width)

    def end_state(d, pw_r, pw_i):
        re = pw_r[:, :, None, :] * bbr[d][:, None] - pw_i[:, :, None, :] * bbi[d][:, None]
        im = pw_r[:, :, None, :] * bbi[d][:, None] + pw_i[:, :, None, :] * bbr[d][:, None]
        return [re.reshape(ng, width, S5_STATE), im.reshape(ng, width, S5_STATE)]

    b_end = jnp.concatenate(end_state(0, pk_r[0][:, tc - 1::-1], pk_i[0][:, tc - 1::-1])
                            + end_state(1, pk_r[1][:, :tc], pk_i[1][:, :tc]), axis=-1)

    def lag_reversed(a):
        return a.reshape(ng, S5_STATE, tc, S5_CH)[:, :, ::-1, :].reshape(ng, S5_STATE, width)

    c_out = jnp.concatenate([cap_r[0][..., S5_CH:], -cap_i[0][..., S5_CH:],
                             lag_reversed(cap_r[1][..., S5_CH:]), -lag_reversed(cap_i[1][..., S5_CH:])], axis=1)
    ptr, pti = pk_r[:, :, tc], pk_i[:, :, tc]
    p_mul = jnp.concatenate([ptr[0], ptr[0], ptr[1], ptr[1]], axis=-1)
    q_mul = jnp.concatenate([-pti[0], pti[0], -pti[1], pti[1]], axis=-1)
    return toep, b_end, c_out, p_mul, q_mul


def _granule_transpose(regs, granule):
    n = len(regs)
    d = n // 2
    while d >= 1:
        low = (granule // d) % 2 == 0
        nxt = list(regs)
        for i in range(n):
            if (i // d) % 2 == 0:
                a, b = regs[i], regs[i + d]
                nxt[i] = jnp.where(low, a, pltpu.roll(b, d * S5_CH, 1))
                nxt[i + d] = jnp.where(low, pltpu.roll(a, 128 - d * S5_CH, 1), b)
        regs = nxt
        d //= 2
    return regs


def _s5_rows_to_groups(u_ref, x_scr):
    rr = S5_RELAYOUT_ROWS
    per = S5_LANE_GROUPS
    granule = lax.broadcasted_iota(jnp.int32, (rr, 128), 1) // S5_CH

    def body(rb, carry):
        base = pl.multiple_of(rb * (rr * S5_TC), rr * S5_TC)
        rows = pl.ds(pl.multiple_of(rb * rr, rr), rr)
        for h in range(S5_TC // per):
            steps = [u_ref[pl.ds(base + per * h + t, rr, stride=S5_TC), :] for t in range(per)]
            for gl, block in enumerate(_granule_transpose(steps, granule)):
                x_scr[gl, rows, 128 * h:128 * (h + 1)] = block
        return carry

    lax.fori_loop(0, x_scr.shape[1] // rr, body, 0)


def _s5_in_kernel(u_ref, toep_ref, bend_ref, y_ref, sf_ref, sb_ref, x_scr):
    half = 2 * S5_STATE
    _s5_rows_to_groups(u_ref, x_scr)
    for gl in range(S5_LANE_GROUPS):
        xb = x_scr[gl].astype(BF16)
        y_ref[gl] = jnp.dot(xb, toep_ref[gl], preferred_element_type=F32)
        s = jnp.dot(xb, bend_ref[gl], preferred_element_type=F32)
        sf_ref[gl] = s[:, 0:half]
        sb_ref[gl] = s[:, half:2 * half]


def _s5_scan_kernel(sf_ref, sb_ref, p_ref, q_ref, hf_ref, hb_ref, sf_sw_ref, sb_sw_ref, *, rows_per_seq):
    n_chain = p_ref.shape[0]
    half = 2 * S5_STATE
    pf, qf = p_ref[:, 0:half], q_ref[:, 0:half]
    pb, qb = p_ref[:, half:2 * half], q_ref[:, half:2 * half]
    qf_sw = pltpu.roll(qf, S5_STATE, 1)
    qb_sw = pltpu.roll(qb, S5_STATE, 1)
    zero = jnp.zeros((n_chain, half), F32)
    piece = 256

    def swap(k, carry):
        rows = pl.ds(pl.multiple_of(k * piece, piece), piece)
        sf_sw_ref[rows, :] = pltpu.roll(sf_ref[rows, :], S5_STATE, 1)
        sb_sw_ref[rows, :] = pltpu.roll(sb_ref[rows, :], S5_STATE, 1)
        return carry

    lax.fori_loop(0, sf_ref.shape[0] // piece, swap, 0)

    tile = 8

    def steps(c8, carry):
        hf, hf_sw, hb, hb_sw = carry
        base_f = pl.multiple_of(c8 * tile, tile)
        base_b = pl.multiple_of(rows_per_seq - tile - c8 * tile, tile)
        for k in range(tile):
            rows_f = pl.ds(base_f + k, n_chain, stride=rows_per_seq)
            rows_b = pl.ds(base_b + (tile - 1 - k), n_chain, stride=rows_per_seq)
            hf_ref[rows_f, :] = hf
            hb_ref[rows_b, :] = hb
            sf = sf_ref[rows_f, :]
            sb = sb_ref[rows_b, :]
            sf_sw = sf_sw_ref[rows_f, :]
            sb_sw = sb_sw_ref[rows_b, :]
            hf, hf_sw = pf * hf + qf * hf_sw + sf, pf * hf_sw + qf_sw * hf + sf_sw
            hb, hb_sw = pb * hb + qb * hb_sw + sb, pb * hb_sw + qb_sw * hb + sb_sw
        return hf, hf_sw, hb, hb_sw

    lax.fori_loop(0, rows_per_seq // tile, steps, (zero, zero, zero, zero))


def _s5_out_kernel(hf_ref, hb_ref, cout_ref, y_ref, u_ref, d_ref, o_ref, y_scr):
    half = 2 * S5_STATE
    per = S5_LANE_GROUPS
    for gl in range(per):
        y = y_ref[gl] + jnp.dot(hf_ref[gl].astype(BF16), cout_ref[gl, 0:half, :], preferred_element_type=F32)
        y_scr[gl] = y + jnp.dot(hb_ref[gl].astype(BF16), cout_ref[gl, half:2 * half, :], preferred_element_type=F32)
    rr = S5_RELAYOUT_ROWS
    granule = lax.broadcasted_iota(jnp.int32, (rr, 128), 1) // S5_CH
    d_skip = d_ref[...]

    def body(rb, carry):
        rows = pl.ds(pl.multiple_of(rb * rr, rr), rr)
        base = pl.multiple_of(rb * (rr * S5_TC), rr * S5_TC)
        for h in range(S5_TC // per):
            groups = [y_scr[gl, rows, 128 * h:128 * (h + 1)] for gl in range(per)]
            for t, block in enumerate(_granule_transpose(groups, granule)):
                tok = pl.ds(base + per * h + t, rr, stride=S5_TC)
                o_ref[tok, :] = jax.nn.gelu(block + u_ref[tok, :] * d_skip)
        return carry

    lax.fori_loop(0, y_scr.shape[1] // rr, body, 0)


def _s5_all_tables(a_re, a_im, log_step, b_re, b_im, c_re, c_im):
    toep, b_end, c_out, p_mul, q_mul = jax.vmap(_s5_tables)(a_re, a_im, log_step, b_re, b_im, c_re, c_im)
    return toep.astype(BF16), b_end.astype(BF16), c_out.astype(BF16), p_mul, q_mul


def _s5_mix(u_tiles, seqlen, tables, layer, d_skip):
    ntile, ntok, lanes = u_tiles.shape
    bsz = ntok // seqlen
    rows_per_seq = seqlen // S5_TC
    rows = bsz * rows_per_seq
    rblk = min(S5_ROW_BLOCK, rows)
    per = S5_LANE_GROUPS
    half = 2 * S5_STATE
    toep, b_end, c_out, p_mul, q_mul = tables
    p_mul, q_mul = p_mul[layer], q_mul[layer]
    utile = pl.BlockSpec((None, rblk * S5_TC, lanes), lambda j, i: (j, i, 0))
    ytile = pl.BlockSpec((per, rblk, S5_ROW), lambda j, i: (j, i, 0))
    stile = pl.BlockSpec((per, rblk, half), lambda j, i: (j, i, 0))
    wtile = pl.BlockSpec((None, per, S5_ROW, S5_ROW), lambda j, i: (layer, j, 0, 0))
    state_shape = jax.ShapeDtypeStruct((S5_GROUPS, rows, half), F32)
    grid = (ntile, rows // rblk)
    y_in, s_f, s_b = pl.pallas_call(
        _s5_in_kernel,
        grid=grid,
        in_specs=[utile, wtile, wtile],
        out_specs=[ytile, stile, stile],
        out_shape=[jax.ShapeDtypeStruct((S5_GROUPS, rows, S5_ROW), F32), state_shape, state_shape],
        scratch_shapes=[pltpu.VMEM((per, rblk, S5_ROW), F32)],
        compiler_params=_params("parallel", "parallel"),
        name="s5_in",
    )(u_tiles, toep, b_end)

    gs = S5_SCAN_GROUPS
    n_chain = gs * bsz
    chain_p = jnp.repeat(p_mul, bsz, axis=0)
    chain_q = jnp.repeat(q_mul, bsz, axis=0)
    flat = jax.ShapeDtypeStruct((S5_GROUPS * rows, half), F32)
    srows = pl.BlockSpec((gs * rows, half), lambda t: (t, 0))
    chain = pl.BlockSpec((n_chain, S5_ROW), lambda t: (t, 0))
    h_f, h_b = pl.pallas_call(
        functools.partial(_s5_scan_kernel, rows_per_seq=rows_per_seq),
        grid=(S5_GROUPS // gs,),
        in_specs=[srows, srows, chain, chain],
        out_specs=[srows, srows],
        out_shape=[flat, flat],
        scratch_shapes=[pltpu.VMEM((gs * rows, half), F32)] * 2,
        compiler_params=_params("parallel"),
        name="s5_scan",
    )(s_f.reshape(S5_GROUPS * rows, half), s_b.reshape(S5_GROUPS * rows, half), chain_p, chain_q)

    return pl.pallas_call(
        _s5_out_kernel,
        grid=grid,
        in_specs=[stile, stile, wtile, ytile, utile, pl.BlockSpec((None, 1, lanes), lambda j, i: (j, 0, 0))],
        out_specs=utile,
        out_shape=jax.ShapeDtypeStruct((ntile, ntok, lanes), F32),
        scratch_shapes=[pltpu.VMEM((per, rblk, S5_ROW), F32)],
        compiler_params=_params("parallel", "parallel"),
        name="s5_out",
    )(h_f.reshape(S5_GROUPS, rows, half), h_b.reshape(S5_GROUPS, rows, half), c_out, y_in, u_tiles,
      d_skip.reshape(ntile, 1, lanes))


def _ssd_proj_kernel(y_ref, z_ref, nw_ref, w_ref, gate_ref, bg_ref, o_ref):
    z = z_ref[...].astype(F32)
    y = y_ref[...].astype(F32) * (z * _sigmoid(z))
    y = y * lax.rsqrt(jnp.mean(y * y, axis=-1, keepdims=True) + LN_EPS) * nw_ref[...]
    p = jnp.dot(y.astype(BF16), w_ref[...], preferred_element_type=F32)
    o_ref[...] = (_sigmoid(gate_ref[...] + bg_ref[...]) * p).astype(o_ref.dtype)


def _ssd_project(y, z, norm_w, w_br, gates, b_gate, *, tm=512):
    m = y.shape[0]
    wide = pl.BlockSpec((tm, SSD_INNER), lambda i: (i, 0))
    return pl.pallas_call(
        _ssd_proj_kernel,
        grid=(m // tm,),
        in_specs=[wide, wide, pl.BlockSpec((1, SSD_INNER), lambda i: (0, 0)),
                  pl.BlockSpec((SSD_INNER, D_MODEL), lambda i: (0, 0)),
                  pl.BlockSpec((tm, D_MODEL), lambda i: (i, 0)),
                  pl.BlockSpec((1, D_MODEL), lambda i: (0, 0))],
        out_specs=pl.BlockSpec((tm, D_MODEL), lambda i: (i, 0)),
        out_shape=jax.ShapeDtypeStruct((m, D_MODEL), BF16),
        compiler_params=_params("parallel"),
        name="ssd_project",
    )(y, z, norm_w.reshape(1, -1), w_br, gates, b_gate[0:1])


def _s5_proj_kernel(h_ref, w1_ref, w2_ref, w_ref, gate_ref, bg_ref, o_ref):
    hb = jnp.concatenate([h_ref[t] for t in range(h_ref.shape[0])], axis=1).astype(BF16)
    a = jnp.dot(hb, w1_ref[...], preferred_element_type=F32)
    b = jnp.dot(hb, w2_ref[...], preferred_element_type=F32)
    p = jnp.dot((a * _sigmoid(b)).astype(BF16), w_ref[...], preferred_element_type=F32)
    o_ref[...] = (_sigmoid(gate_ref[...] + bg_ref[...]) * p).astype(o_ref.dtype)


def _s5_project(h_tiles, w1, w2, w_br, gates, b_gate, *, tm=512):
    ntile, m, lanes = h_tiles.shape
    sq = pl.BlockSpec((S5_WIDTH, S5_WIDTH), lambda i: (0, 0))
    return pl.pallas_call(
        _s5_proj_kernel,
        grid=(m // tm,),
        in_specs=[pl.BlockSpec((ntile, tm, lanes), lambda i: (0, i, 0)), sq, sq,
                  pl.BlockSpec((S5_WIDTH, D_MODEL), lambda i: (0, 0)),
                  pl.BlockSpec((tm, D_MODEL), lambda i: (i, 2)),
                  pl.BlockSpec((1, D_MODEL), lambda i: (0, 0))],
        out_specs=pl.BlockSpec((tm, D_MODEL), lambda i: (i, 0)),
        out_shape=jax.ShapeDtypeStruct((m, D_MODEL), BF16),
        compiler_params=_params("parallel"),
        name="s5_project",
    )(h_tiles, w1, w2, w_br, gates, b_gate[2:3])


def _layer_norm(v, g, b):
    mu = jnp.mean(v, axis=-1, keepdims=True)
    c = v - mu
    var = jnp.mean(c * c, axis=-1, keepdims=True)
    return c * lax.rsqrt(var + LN_EPS) * g + b


def _merge_kernel(o0_ref, o1_ref, o2_ref, l0_ref, l1_ref, l2_ref, w_att_ref, gate_ref, bg_ref, pssd_ref, ps5_ref,
                  w_out_ref, x_ref, g_ref, b_ref, o_ref):
    l0, l1, l2 = l0_ref[...], l1_ref[...], l2_ref[...]
    mx = jnp.maximum(jnp.maximum(l0, l1), l2)
    e0, e1, e2 = jnp.exp(l0 - mx), jnp.exp(l1 - mx), jnp.exp(l2 - mx)
    den = e0 + e1 + e2
    att = (e0 / den) * o0_ref[...] + (e1 / den) * o1_ref[...] + (e2 / den) * o2_ref[...]
    p_att = jnp.dot(att.astype(BF16), w_att_ref[...], preferred_element_type=F32)
    merged = pssd_ref[...] + _sigmoid(gate_ref[...] + bg_ref[...]) * p_att + ps5_ref[...]
    mix = jnp.dot(merged.astype(BF16), w_out_ref[...], preferred_element_type=F32)
    o_ref[...] = _layer_norm(ALPHA * x_ref[...] + mix, g_ref[...], b_ref[...])


def _merge(att_o, att_l, w_att, gates, b_gate, p_ssd, p_s5, w_out, x, ln_g, ln_b, *, tm=512):
    m = x.shape[0]
    att = pl.BlockSpec((tm, ATT_OUT), lambda i: (i, 0))
    row = pl.BlockSpec((tm, D_MODEL), lambda i: (i, 0))
    vec = pl.BlockSpec((1, D_MODEL), lambda i: (0, 0))
    return pl.pallas_call(
        _merge_kernel,
        grid=(m // tm,),
        in_specs=[att] * 6 + [pl.BlockSpec((ATT_OUT, D_MODEL), lambda i: (0, 0)),
                              pl.BlockSpec((tm, D_MODEL), lambda i: (i, 1)), vec, row, row,
                              pl.BlockSpec((D_MODEL, D_MODEL), lambda i: (0, 0)), row, vec, vec],
        out_specs=row,
        out_shape=jax.ShapeDtypeStruct((m, D_MODEL), F32),
        compiler_params=_params("parallel"),
        name="merge_ln1",
    )(*att_o, *att_l, w_att, gates, b_gate[1:2], p_ssd, p_s5, w_out, x, ln_g.reshape(1, -1), ln_b.reshape(1, -1))


def _router_kernel(x_ref, wt_ref, b_ref, idx_ref, gate_ref, rank_ref, count_ref, before_ref):
    tm = x_ref.shape[0]

    @pl.when(pl.program_id(0) == 0)
    def _():
        count_ref[...] = jnp.zeros_like(count_ref)
        earlier = lax.broadcasted_iota(jnp.int32, (tm, tm), 0) < lax.broadcasted_iota(jnp.int32, (tm, tm), 1)
        before_ref[...] = earlier.astype(before_ref.dtype)

    x = x_ref[...]
    wt = wt_ref[...]
    xh = x.astype(BF16)
    xl = (x - xh.astype(F32)).astype(BF16)
    wh = wt.astype(BF16)
    wl = (wt - wh.astype(F32)).astype(BF16)
    nt = (((1,), (1,)), ((), ()))
    logits = (lax.dot_general(wh, xh, nt, preferred_element_type=F32)
              + lax.dot_general(wl, xh, nt, preferred_element_type=F32)
              + lax.dot_general(wh, xl, nt, preferred_element_type=F32)) + b_ref[...]
    eid = lax.broadcasted_iota(jnp.int32, logits.shape, 0)
    vals, idxs = [], []
    for _ in range(MOE_TOP_K):
        mx = jnp.max(logits, axis=0, keepdims=True)
        sel = jnp.min(jnp.where(logits == mx, eid, MOE_EXPERTS), axis=0, keepdims=True)
        vals.append(mx)
        idxs.append(sel)
        logits = jnp.where(eid == sel, -jnp.inf, logits)
    es = [jnp.exp(v - vals[0]) for v in vals]
    den = es[0] + es[1] + es[2] + es[3]
    for k in range(MOE_TOP_K):
        idx_ref[k:k + 1, :] = idxs[k]
        gate_ref[k:k + 1, :] = es[k] / den
    chosen = [eid == idxs[k] for k in range(MOE_TOP_K)]
    stacked = jnp.concatenate([c.astype(before_ref.dtype) for c in chosen], axis=0)
    earlier = jnp.dot(stacked, before_ref[...], preferred_element_type=F32)
    seen = count_ref[...]
    for k in range(MOE_TOP_K):
        pos = earlier[k * MOE_EXPERTS:(k + 1) * MOE_EXPERTS] + seen
        rank_ref[k:k + 1, :] = jnp.sum(jnp.where(chosen[k], pos, 0.0), axis=0, keepdims=True).astype(jnp.int32)
        seen = seen + jnp.sum(chosen[k].astype(F32), axis=1, keepdims=True)
    count_ref[...] = seen


def _router(x, router_w, router_b, *, tm=1024):
    m = x.shape[0]
    out = pl.BlockSpec((MOE_TOP_K, tm), lambda i: (0, i))
    return pl.pallas_call(
        _router_kernel,
        grid=(m // tm,),
        in_specs=[pl.BlockSpec((tm, D_MODEL), lambda i: (i, 0)),
                  pl.BlockSpec((MOE_EXPERTS, D_MODEL), lambda i: (0, 0)),
                  pl.BlockSpec((MOE_EXPERTS, 1), lambda i: (0, 0))],
        out_specs=[out, out, out, pl.BlockSpec((MOE_EXPERTS, 1), lambda i: (0, 0))],
        out_shape=[jax.ShapeDtypeStruct((MOE_TOP_K, m), jnp.int32), jax.ShapeDtypeStruct((MOE_TOP_K, m), F32),
                   jax.ShapeDtypeStruct((MOE_TOP_K, m), jnp.int32), jax.ShapeDtypeStruct((MOE_EXPERTS, 1), F32)],
        scratch_shapes=[pltpu.VMEM((tm, tm), BF16)],
        compiler_params=_params("arbitrary"),
        name="router",
    )(x, router_w.T, router_b.reshape(-1, 1))


def _expert_kernel(be_ref, used_ref, x_ref, wgu_ref, bgu_ref, wdn_ref, bdn_ref, o_ref, wgu_lo, wdn_lo):
    i = pl.program_id(0)
    used = i < used_ref[0]

    @pl.when(used & ((i == 0) | (be_ref[i] != be_ref[jnp.maximum(i - 1, 0)])))
    def _():
        wgu_lo[...] = wgu_ref[...].astype(wgu_lo.dtype)
        wdn_lo[...] = wdn_ref[...].astype(wdn_lo.dtype)

    @pl.when(used)
    def _():
        hu = jnp.dot(x_ref[...].astype(BF16), wgu_lo[...], preferred_element_type=F32) + bgu_ref[...]
        x_glu = jnp.minimum(hu[:, :MOE_FF], SWIGLU_LIMIT)
        x_lin = jnp.clip(hu[:, MOE_FF:], -SWIGLU_LIMIT, SWIGLU_LIMIT)
        act = x_glu * _sigmoid(SWIGLU_ALPHA * x_glu) * (x_lin + 1.0)
        o_ref[...] = jnp.dot(act.astype(BF16), wdn_lo[...], preferred_element_type=F32) + bdn_ref[...]

    @pl.when(jnp.logical_not(used))
    def _():
        o_ref[...] = jnp.zeros_like(o_ref)


def _experts(xs, block_expert, n_used, w_gu, b_gu, w_dn, b_dn, layer):
    n_rows = xs.shape[0]
    n_blocks = n_rows // MOE_ROWS
    grid_spec = pltpu.PrefetchScalarGridSpec(
        num_scalar_prefetch=2,
        grid=(n_blocks,),
        in_specs=[
            pl.BlockSpec((MOE_ROWS, D_MODEL), lambda i, be, nu: (i, 0)),
            pl.BlockSpec((None, None, D_MODEL, 2 * MOE_FF), lambda i, be, nu: (layer, be[i], 0, 0)),
            pl.BlockSpec((None, 1, 2 * MOE_FF), lambda i, be, nu: (be[i], 0, 0)),
            pl.BlockSpec((None, None, MOE_FF, D_MODEL), lambda i, be, nu: (layer, be[i], 0, 0)),
            pl.BlockSpec((None, 1, D_MODEL), lambda i, be, nu: (be[i], 0, 0)),
        ],
        out_specs=pl.BlockSpec((MOE_ROWS, D_MODEL), lambda i, be, nu: (i, 0)),
        scratch_shapes=[pltpu.VMEM((D_MODEL, 2 * MOE_FF), BF16), pltpu.VMEM((MOE_FF, D_MODEL), BF16)],
    )
    return pl.pallas_call(
        _expert_kernel,
        grid_spec=grid_spec,
        out_shape=jax.ShapeDtypeStruct((n_rows, D_MODEL), F32),
        compiler_params=_params("arbitrary"),
        name="experts",
    )(block_expert, n_used, xs, w_gu, b_gu.reshape(MOE_EXPERTS, 1, -1), w_dn, b_dn.reshape(MOE_EXPERTS, 1, -1))


def _combine_kernel(x_ref, y0_ref, y1_ref, y2_ref, y3_ref, gate_ref, g_ref, b_ref, o_ref, ob_ref):
    ffn = gate_ref[:, 0:1] * y0_ref[...]
    for k, y_ref in enumerate((y1_ref, y2_ref, y3_ref), start=1):
        ffn = ffn + gate_ref[:, k:k + 1] * y_ref[...]
    out = _layer_norm(ALPHA * x_ref[...] + ffn, g_ref[...], b_ref[...])
    o_ref[...] = out
    ob_ref[...] = out.astype(ob_ref.dtype)


def _combine(x, y_rows, gate, ln_g, ln_b, *, tm=512):
    m = x.shape[0]
    row = pl.BlockSpec((tm, D_MODEL), lambda i: (i, 0))
    vec = pl.BlockSpec((1, D_MODEL), lambda i: (0, 0))
    return pl.pallas_call(
        _combine_kernel,
        grid=(m // tm,),
        in_specs=[row] + [pl.BlockSpec((tm, D_MODEL), functools.partial(lambda k, i: (k * (m // tm) + i, 0), k))
                          for k in range(MOE_TOP_K)] + [pl.BlockSpec((tm, MOE_TOP_K), lambda i: (i, 0)), vec, vec],
        out_specs=[row, row],
        out_shape=[jax.ShapeDtypeStruct((m, D_MODEL), F32), jax.ShapeDtypeStruct((m, D_MODEL), BF16)],
        compiler_params=_params("parallel"),
        name="combine_ln2",
    )(x, *([y_rows] * MOE_TOP_K), gate, ln_g.reshape(1, -1), ln_b.reshape(1, -1))


def _moe(x, router_w, router_b, w_gu, b_gu, w_dn, b_dn, ln_g, ln_b, layer):
    ntok = x.shape[0]
    idx_t, gate_t, rank_t, counts = _router(x, router_w, router_b)
    gate = gate_t.T
    n_assign = ntok * MOE_TOP_K
    counts = counts.reshape(-1).astype(jnp.int32)
    padded = (counts + MOE_ROWS - 1) // MOE_ROWS * MOE_ROWS
    pend = jnp.cumsum(padded)
    pstart = pend - padded
    experts = jnp.arange(MOE_EXPERTS, dtype=jnp.int32)

    def lookup(table, idx):
        return jnp.sum(jnp.where(idx[..., None] == experts, table, 0), axis=-1)

    dest_t = lookup(pstart, idx_t) + rank_t
    n_blocks = n_assign // MOE_ROWS + MOE_EXPERTS
    n_pad = MOE_EXPERTS * MOE_ROWS
    block_start = jnp.arange(n_blocks, dtype=jnp.int32) * MOE_ROWS
    block_expert = jnp.sum((pend[None, :] <= block_start[:, None]).astype(jnp.int32), axis=1)
    block_expert = jnp.minimum(block_expert, MOE_EXPERTS - 1)
    n_used = (pend[-1:] // MOE_ROWS).astype(jnp.int32)
    gap = padded - counts
    gap_end = jnp.cumsum(gap)
    pad_id = jnp.arange(n_pad, dtype=jnp.int32)
    pad_expert = jnp.sum((gap_end[None, :] <= pad_id[:, None]).astype(jnp.int32), axis=1)
    tail_of = lookup(pstart + counts - (gap_end - gap), jnp.minimum(pad_expert, MOE_EXPERTS - 1)) + pad_id
    pad_slot = jnp.where(pad_expert < MOE_EXPERTS, tail_of, pend[-1] + pad_id - gap_end[-1])
    keys = jnp.concatenate([dest_t.reshape(-1), pad_slot])
    toks = jnp.concatenate([jnp.tile(jnp.arange(ntok, dtype=jnp.int32), MOE_TOP_K), jnp.zeros((n_pad,), jnp.int32)])
    _, slot_tok = lax.sort((keys, toks), num_keys=1, is_stable=False)
    xs = x.at[slot_tok].get(mode="promise_in_bounds")
    ys = _experts(xs, block_expert, n_used, w_gu, b_gu, w_dn, b_dn, layer)
    y_rows = ys.at[dest_t.reshape(-1)].get(mode="promise_in_bounds", unique_indices=True)
    return _combine(x, y_rows, gate, ln_g, ln_b)


def _layer(x, xb, w_in, b_gate, ssd_conv_w, ssd_conv_b, ssd_a_log, ssd_dt_bias, ssd_d, ssd_norm_w,
           s5_d, s5_glu_w1, s5_glu_w2, w_br_ssd, w_br_attn, w_br_s5, w_out, ln1_g, ln1_b,
           router_w, router_b, b_gu, b_dn, ln2_g, ln2_b, *, layer, w_gu, w_dn, s5_tables):
    bsz, seqlen, d = x.shape
    ntok = bsz * seqlen
    xf = x.reshape(ntok, d)
    offs = [0]
    for s in IN_SIZES:
        offs.append(offs[-1] + s)

    def seg(k):
        return w_in[:, offs[k]:offs[k + 1]].astype(BF16)

    gates = _matmul(xb, seg(0), tm=1024, tn=1024, name="proj_gates", out_dtype=BF16)
    z = _matmul(xb, seg(1), tm=1024, tn=1024, name="proj_z", out_dtype=BF16)
    u = _matmul_tiles(xb, seg(5), tm=512, name="proj_u")

    w_xbc = seg(2)
    xs_act = _xbc_conv(xb, w_xbc[:, :SSD_INNER], ssd_conv_w[:, :SSD_INNER], ssd_conv_b[:SSD_INNER], seqlen)
    bc_act = _xbc_conv(xb, w_xbc[:, SSD_INNER:], ssd_conv_w[:, SSD_INNER:], ssd_conv_b[SSD_INNER:], seqlen,
                       out_dtype=BF16)
    xs_act = xs_act.reshape(bsz, seqlen, SSD_INNER)
    bc_act = bc_act.reshape(bsz, seqlen, SSD_CONV_CH - SSD_INNER)
    decay = _ssd_decay(xb, seg(3), ssd_a_log, ssd_dt_bias)
    y_ssd = _ssd_scan(xs_act, bc_act, *decay, ssd_d)
    p_ssd = _ssd_project(y_ssd.reshape(ntok, SSD_INNER), z, ssd_norm_w, w_br_ssd.astype(BF16), gates, b_gate)

    w_qkv = w_in[:, offs[4]:offs[5]].reshape(d, 3, ATT_GROUPS, ATT_OUT)
    att_o, att_l = [], []
    for g, (window, dil) in enumerate(ATT_PATTERNS):
        assert (window // 2) // dil == ATT_RADIUS
        ls = seqlen // dil
        x_cls = xb.reshape(bsz, ls, dil, d).transpose(0, 2, 1, 3).reshape(ntok, d)
        qkv_g = _matmul(x_cls, w_qkv[:, :, g].reshape(d, 3 * ATT_OUT).astype(BF16), tm=1024, tn=3 * ATT_OUT,
                        name=f"proj_qkv{g}", out_dtype=BF16)
        o, lse = _attention_group(qkv_g.reshape(bsz * dil, ls, 3 * ATT_OUT), g, dil)
        att_o.append(o.reshape(bsz, dil, ls, ATT_OUT).transpose(0, 2, 1, 3).reshape(ntok, ATT_OUT))
        att_l.append(lse.reshape(bsz, dil, ls, ATT_OUT).transpose(0, 2, 1, 3).reshape(ntok, ATT_OUT))

    h_s5 = _s5_mix(u, seqlen, s5_tables, layer, s5_d)
    p_s5 = _s5_project(h_s5, s5_glu_w1.astype(BF16), s5_glu_w2.astype(BF16), w_br_s5.astype(BF16), gates, b_gate)

    x1 = _merge(att_o, att_l, w_br_attn.astype(BF16), gates, b_gate, p_ssd, p_s5, w_out.astype(BF16),
                xf, ln1_g, ln1_b)
    x2, x2b = _moe(x1, router_w, router_b, w_gu, b_gu, w_dn, b_dn, ln2_g, ln2_b, layer)
    return x2.reshape(bsz, seqlen, d), x2b


def kernel(x, w_in, b_gate, ssd_conv_w, ssd_conv_b, ssd_a_log, ssd_dt_bias, ssd_d, ssd_norm_w, s5_a_re, s5_a_im, s5_log_step, s5_b_re, s5_b_im, s5_c_re, s5_c_im, s5_d, s5_glu_w1, s5_glu_w2, w_br_ssd, w_br_attn, w_br_s5, w_out, ln1_g, ln1_b, router_w, router_b, exp_w_gate_up, exp_b_gate_up, exp_w_down, exp_b_down, ln2_g, ln2_b):
    per_layer = (w_in, b_gate, ssd_conv_w, ssd_conv_b, ssd_a_log, ssd_dt_bias, ssd_d, ssd_norm_w, s5_d, s5_glu_w1,
                 s5_glu_w2, w_br_ssd, w_br_attn, w_br_s5, w_out, ln1_g, ln1_b, router_w, router_b, exp_b_gate_up,
                 exp_b_down, ln2_g, ln2_b)
    s5_tables = _s5_all_tables(s5_a_re, s5_a_im, s5_log_step, s5_b_re, s5_b_im, s5_c_re, s5_c_im)
    xb = x.reshape(-1, x.shape[-1]).astype(BF16)
    for layer in range(DEPTH):
        x, xb = _layer(x, xb, *[p[layer] for p in per_layer], layer=layer, w_gu=exp_w_gate_up, w_dn=exp_w_down,
                       s5_tables=s5_tables)
    return x
```
